```python
import jax, jax.numpy as jnp
from jax import lax
import numpy as np

D_MODEL = 1024
BATCH = 4
SEQ = 4096
DEPTH = 2

GRID_W = 64
CTX_LEN = 256
EPS = 1e-6

F_GROUPS = 4
F_GDIM = 64
F_WIDTH = F_GROUPS * F_GDIM
NA_HEADS = 8
HEAD_DIM = 64
NA_WIDTH = NA_HEADS * HEAD_DIM
NA_WIN_R = 8
NA_WIN_C = 16
ATTN_SCALE = HEAD_DIM ** -0.5
ROPE_BASE = 10000.0
ROPE_HALF = HEAD_DIM // 2
ROPE_PER_AXIS = HEAD_DIM // 4
CONV_WIDTH = 256
CONV_K = 3
N_BRANCH = 3
COL_F = 0
COL_Q = COL_F + F_WIDTH
COL_K = COL_Q + NA_WIDTH
COL_V = COL_K + NA_WIDTH
COL_CX = COL_V + NA_WIDTH
COL_CB = COL_CX + CONV_WIDTH
COL_CC = COL_CB + CONV_WIDTH
COL_G = COL_CC + CONV_WIDTH
IN_WIDTH = COL_G + N_BRANCH * D_MODEL
N_EXPERTS = 16
N_GROUPS = 4
EXPERTS_PER_GROUP = N_EXPERTS // N_GROUPS
TOPK_GROUPS = 1
TOP_K = 2
D_EXPERT = 512
MOE_BLOCK = 128

kernel_name = 'hybrid_fourier_na_conv_grouped_moe_dit'


def rmsnorm(x, g):
    xf = x.astype(jnp.float32)
    y = xf * lax.rsqrt(jnp.mean(xf * xf, axis=-1, keepdims=True) + EPS)
    return (y * g.astype(jnp.float32)).astype(x.dtype)


def modulate(h, shift, scale):
    return h * (1.0 + scale) + shift


def axial_rope_tables(L, dtype):
    t = jnp.arange(L)
    row = (t // GRID_W).astype(jnp.float32)
    col = (t % GRID_W).astype(jnp.float32)
    inv = jnp.power(ROPE_BASE, -jnp.arange(ROPE_PER_AXIS, dtype=jnp.float32) / ROPE_PER_AXIS)
    ang = jnp.concatenate([row[:, None] * inv, col[:, None] * inv], axis=-1)
    return jnp.cos(ang)[:, None, :].astype(dtype), jnp.sin(ang)[:, None, :].astype(dtype)


def apply_rope(x, cos, sin):
    x1, x2 = x[..., :ROPE_HALF], x[..., ROPE_HALF:]
    return jnp.concatenate([x1 * cos - x2 * sin, x2 * cos + x1 * sin], axis=-1)


def q_heads(pq, qn_g):
    bsz, L, _ = pq.shape
    return rmsnorm(pq.reshape(bsz, L, NA_HEADS, HEAD_DIM), qn_g)


def kv_heads(pkv, kn_g):
    bsz, L, _ = pkv.shape
    k = rmsnorm(pkv[..., :NA_WIDTH].reshape(bsz, L, NA_HEADS, HEAD_DIM), kn_g)
    v = pkv[..., NA_WIDTH:].reshape(bsz, L, NA_HEADS, HEAD_DIM)
    return k, v


def fourier_mix(u):
    bsz, L, _ = u.shape
    ug = u.reshape(bsz, L, F_GROUPS, F_GDIM).astype(jnp.float32)
    f = jnp.fft.fft2(ug, axes=(1, 3), norm='ortho').real
    return f.reshape(bsz, L, F_WIDTH).astype(u.dtype)


def short_conv(xc, bg, cg, conv_w):
    u = cg * xc
    up = jnp.pad(u, ((0, 0), (1, 1), (0, 0)))
    y = conv_w[0] * up[:, :-2] + conv_w[1] * up[:, 1:-1] + conv_w[2] * up[:, 2:]
    return bg * y


def context_attention(q, k, v):
    bsz, L = q.shape[:2]
    s = jnp.einsum('bqhd,bkhd->bhqk', q, k).astype(jnp.float32) * ATTN_SCALE
    p = jax.nn.softmax(s, axis=-1).astype(v.dtype)
    return jnp.einsum('bhqk,bkhd->bqhd', p, v).reshape(bsz, L, NA_WIDTH)


def neighbourhood_attention(q, k, v, k_ctx, v_ctx, rpb):
    bsz, L, H, Dh = q.shape
    rows = L // GRID_W
    wr = min(NA_WIN_R, rows)
    qg = q.reshape(bsz, rows, GRID_W, H, Dh)
    kg = k.reshape(bsz, rows, GRID_W, H, Dh)
    vg = v.reshape(bsz, rows, GRID_W, H, Dh)
    r = jnp.arange(rows)
    row_start = jnp.clip(r - wr // 2, 0, rows - wr)
    row_idx = row_start[:, None] + jnp.arange(wr)[None, :]
    kb = kg[:, row_idx]
    vb = vg[:, row_idx]
    col = jnp.arange(GRID_W)
    col_start = jnp.clip(col - NA_WIN_C // 2, 0, GRID_W - NA_WIN_C)
    col_mask = (col[None, :] >= col_start[:, None]) & (col[None, :] < col_start[:, None] + NA_WIN_C)
    dr = row_idx - r[:, None] + (NA_WIN_R - 1)
    dc = jnp.clip(col[None, :] - col[:, None] + (NA_WIN_C - 1), 0, 2 * NA_WIN_C - 2)
    bias = rpb[:, dr][:, :, :, dc].transpose(1, 3, 0, 2, 4)
    s_loc = jnp.einsum('brqhd,brkwhd->brqhkw', qg, kb).astype(jnp.float32) * ATTN_SCALE
    s_loc = s_loc + bias.astype(jnp.float32)
    s_loc = jnp.where(col_mask[:, None, None, :], s_loc, -jnp.inf)
    s_ctx = jnp.einsum('brqhd,bkhd->brqhk', qg, k_ctx).astype(jnp.float32) * ATTN_SCALE
    n_loc = wr * GRID_W
    s = jnp.concatenate([s_loc.reshape(bsz, rows, GRID_W, H, n_loc), s_ctx], axis=-1)
    p = jax.nn.softmax(s, axis=-1).astype(v.dtype)
    p_loc = p[..., :n_loc].reshape(bsz, rows, GRID_W, H, wr, GRID_W)
    p_ctx = p[..., n_loc:]
    out = (jnp.einsum('brqhkw,brkwhd->brqhd', p_loc, vb)
           + jnp.einsum('brqhk,bkhd->brqhd', p_ctx, v_ctx))
    return out.reshape(bsz, L, NA_WIDTH)


def merge_branches(P, attn, w_f, w_na, w_cv, conv_w, w_o):
    f = fourier_mix(P[..., COL_F:COL_Q])
    y_cv = short_conv(P[..., COL_CX:COL_CB], P[..., COL_CB:COL_CC], P[..., COL_CC:COL_G], conv_w)
    g = jax.nn.sigmoid(P[..., COL_G:])
    g_f = g[..., :D_MODEL]
    g_na = g[..., D_MODEL:2 * D_MODEL]
    g_cv = g[..., 2 * D_MODEL:]
    merged = g_f * (f @ w_f) + g_na * (attn @ w_na) + g_cv * (y_cv @ w_cv)
    return merged @ w_o


def route(h, router_w, router_b):
    n = h.shape[0]
    s = jax.nn.sigmoid((h @ router_w).astype(jnp.float32))
    sb = s + router_b.astype(jnp.float32)
    gscore = lax.top_k(sb.reshape(n, N_GROUPS, EXPERTS_PER_GROUP), 2)[0].sum(-1)
    _, gidx = lax.top_k(gscore, TOPK_GROUPS)
    gmask = jnp.any(gidx[..., None] == jnp.arange(N_GROUPS), axis=1)
    emask = jnp.repeat(gmask, EXPERTS_PER_GROUP, axis=-1)
    _, eidx = lax.top_k(jnp.where(emask, sb, -jnp.inf), TOP_K)
    w = jnp.take_along_axis(s, eidx, axis=-1)
    w = w / jnp.sum(w, axis=-1, keepdims=True)
    return eidx, w


def moe_ffn(h, router_w, router_b, w1, w3, w2):
    n, d = h.shape
    eidx, wts = route(h, router_w, router_b)
    a = n * TOP_K
    flat_e = eidx.reshape(a)
    flat_t = jnp.repeat(jnp.arange(n, dtype=jnp.int32), TOP_K)
    flat_w = wts.reshape(a)
    order = jnp.argsort(flat_e)
    e_s, t_s, w_s = flat_e[order], flat_t[order], flat_w[order]
    counts = jnp.bincount(flat_e, length=N_EXPERTS)
    starts = jnp.cumsum(counts) - counts
    padded = (counts + MOE_BLOCK - 1) // MOE_BLOCK * MOE_BLOCK
    pends = jnp.cumsum(padded)
    pstarts = pends - padded
    dest = pstarts[e_s] + (jnp.arange(a) - starts[e_s])
    n_blk = (a + N_EXPERTS * (MOE_BLOCK - 1) + MOE_BLOCK - 1) // MOE_BLOCK
    p_len = n_blk * MOE_BLOCK
    tok_buf = jnp.full((p_len,), n, dtype=jnp.int32).at[dest].set(t_s)
    w_buf = jnp.zeros((p_len,), h.dtype).at[dest].set(w_s.astype(h.dtype))
    blk_e = jnp.minimum(jnp.searchsorted(pends, jnp.arange(n_blk) * MOE_BLOCK, side='right'), N_EXPERTS - 1)
    h_pad = jnp.concatenate([h, jnp.zeros((1, d), h.dtype)], axis=0)
    xb = h_pad[tok_buf].reshape(n_blk, MOE_BLOCK, d)

    def expert_block(args):
        xblk, e = args
        return (jax.nn.silu(xblk @ w1[e]) * (xblk @ w3[e])) @ w2[e]

    yb = lax.map(expert_block, (xb, blk_e)).reshape(p_len, d)
    out = jnp.zeros((n + 1, d), h.dtype).at[tok_buf].add(yb * w_buf[:, None])
    return out[:n]


def setup_inputs(seed: int = 0) -> dict:
    key = jax.random.key(seed)
    ks = jax.random.split(key, 24)
    D = D_MODEL

    def nrm(k, shape, scale):
        return jax.random.normal(k, shape, jnp.float32) * scale

    return {
        'x': nrm(ks[0], (BATCH, SEQ, D), 1.0),
        'c': nrm(ks[1], (BATCH, D), 1.0),
        'ctx': nrm(ks[2], (BATCH, CTX_LEN, D), 1.0),
        'c_ctx': nrm(ks[3], (D,), 1.0),
        'ada_w': nrm(ks[4], (DEPTH, D, 6 * D), 0.5 * D ** -0.5),
        'ada_b': nrm(ks[5], (DEPTH, 6 * D), 0.02),
        'norm1_g': 1.0 + nrm(ks[6], (DEPTH, D), 0.05),
        'w_in': nrm(ks[7], (DEPTH, D, IN_WIDTH), D ** -0.5),
        'qn_g': 1.0 + nrm(ks[8], (DEPTH, HEAD_DIM), 0.05),
        'kn_g': 1.0 + nrm(ks[9], (DEPTH, HEAD_DIM), 0.05),
        'rpb': nrm(ks[10], (DEPTH, NA_HEADS, 2 * NA_WIN_R - 1, 2 * NA_WIN_C - 1), 0.2),
        'conv_w': nrm(ks[11], (DEPTH, CONV_K, CONV_WIDTH), CONV_K ** -0.5),
        'w_f': nrm(ks[12], (DEPTH, F_WIDTH, D), F_WIDTH ** -0.5),
        'w_na': nrm(ks[13], (DEPTH, NA_WIDTH, D), NA_WIDTH ** -0.5),
        'w_cv': nrm(ks[14], (DEPTH, CONV_WIDTH, D), CONV_WIDTH ** -0.5),
        'w_o': nrm(ks[15], (DEPTH, D, D), D ** -0.5),
        'norm2_g': 1.0 + nrm(ks[16], (DEPTH, D), 0.05),
        'router_w': nrm(ks[17], (D, N_EXPERTS), D ** -0.5),
        'router_b': nrm(ks[18], (N_EXPERTS,), 0.01),
        'w1': nrm(ks[19], (DEPTH, N_EXPERTS, D, D_EXPERT), D ** -0.5),
        'w3': nrm(ks[20], (DEPTH, N_EXPERTS, D, D_EXPERT), D ** -0.5),
        'w2': nrm(ks[21], (DEPTH, N_EXPERTS, D_EXPERT, D), D_EXPERT ** -0.5),
    }


def reference(x, c, ctx, c_ctx, ada_w, ada_b, norm1_g, w_in, qn_g, kn_g, rpb, conv_w, w_f, w_na, w_cv, w_o,
              norm2_g, router_w, router_b, w1, w3, w2):
    bsz, L, d = x.shape
    cos, sin = axial_rope_tables(L, x.dtype)
    sc = jax.nn.silu(c)
    scc = jax.nn.silu(c_ctx)
    for l in range(DEPTH):
        last = l == DEPTH - 1
        mod = (sc @ ada_w[l] + ada_b[l])[:, None, :]
        mod_c = scc @ ada_w[l] + ada_b[l]
        sh1, sc1, g1, sh2, sc2, g2 = jnp.split(mod, 6, axis=-1)
        csh1, csc1, cg1, csh2, csc2, cg2 = jnp.split(mod_c, 6, axis=-1)

        hc = modulate(rmsnorm(ctx, norm1_g[l]), csh1, csc1)
        if last:
            k_c, v_c = kv_heads(hc @ w_in[l][:, COL_K:COL_CX], kn_g[l])
        else:
            Pc = hc @ w_in[l]
            q_c = q_heads(Pc[..., COL_Q:COL_K], qn_g[l])
            k_c, v_c = kv_heads(Pc[..., COL_K:COL_CX], kn_g[l])
            attn_c = context_attention(q_c, k_c, v_c)
            ctx_new = ctx + cg1 * merge_branches(Pc, attn_c, w_f[l], w_na[l], w_cv[l], conv_w[l], w_o[l])

        h = modulate(rmsnorm(x, norm1_g[l]), sh1, sc1)
        P = h @ w_in[l]
        q = apply_rope(q_heads(P[..., COL_Q:COL_K], qn_g[l]), cos, sin)
        k, v = kv_heads(P[..., COL_K:COL_CX], kn_g[l])
        k = apply_rope(k, cos, sin)
        attn = neighbourhood_attention(q, k, v, k_c, v_c, rpb[l])
        x = x + g1 * merge_branches(P, attn, w_f[l], w_na[l], w_cv[l], conv_w[l], w_o[l])

        h2 = modulate(rmsnorm(x, norm2_g[l]), sh2, sc2).reshape(bsz * L, d)
        if last:
            y = moe_ffn(h2, router_w, router_b, w1[l], w3[l], w2[l])
            x = x + g2 * y.reshape(bsz, L, d)
        else:
            h2c = modulate(rmsnorm(ctx_new, norm2_g[l]), csh2, csc2).reshape(bsz * CTX_LEN, d)
            y = moe_ffn(jnp.concatenate([h2, h2c], axis=0), router_w, router_b, w1[l], w3[l], w2[l])
            x = x + g2 * y[:bsz * L].reshape(bsz, L, d)
            ctx = ctx_new + cg2 * y[bsz * L:].reshape(bsz, CTX_LEN, d)
    return x
```

```python
import functools
import math

import numpy as np
import jax
import jax.numpy as jnp
from jax import lax
from jax.experimental import pallas as pl
from jax.experimental.pallas import tpu as pltpu

F32 = jnp.float32
BF16 = jnp.bfloat16
I32 = jnp.int32
HIGHEST = lax.Precision.HIGHEST

D_MODEL = 1024
DEPTH = 2
GRID_W = 64
EPS = 1e-6
F_GROUPS = 4
F_GDIM = 64
F_WIDTH = 256
NA_HEADS = 8
HEAD_DIM = 64
NA_WIDTH = 512
NA_WIN_R = 8
NA_WIN_C = 16
ATTN_SCALE = HEAD_DIM ** -0.5
ROPE_BASE = 10000.0
ROPE_PER_AXIS = HEAD_DIM // 4
CONV_WIDTH = 256
COL_Q = 256
COL_K = 768
COL_V = 1280
COL_CX = 1792
COL_CB = 2048
COL_CC = 2304
COL_G = 2560
N_EXPERTS = 16
N_GROUPS = 4
EXPERTS_PER_GROUP = 4
D_EXPERT = 512

V7X_LANES = 128
V7X_SUBLANES = 8
V7X_MXU_DIM = 256

TOKEN_TILE = 256
ROUTE_TILE = 512
EXPERT_BLOCK = 256
HEADS_PER_GROUP = V7X_MXU_DIM // HEAD_DIM
NEG_BIG = -1e30
MODS_ROWS = 8
VMEM_LIMIT = 48 * 1024 * 1024


def _cparams(sem):
    return pltpu.CompilerParams(dimension_semantics=sem, vmem_limit_bytes=VMEM_LIMIT)


def _mods_kernel(c_ref, w_ref, b_ref, o_ref):
    c = c_ref[...]
    sc = c * jax.nn.sigmoid(c)
    o_ref[...] = jnp.dot(sc, w_ref[...], precision=HIGHEST, preferred_element_type=F32) + b_ref[...]


def _mods(c8, ada_w, ada_b):
    nb = 1536
    return pl.pallas_call(
        _mods_kernel,
        grid=(DEPTH, 6 * D_MODEL // nb),
        in_specs=[
            pl.BlockSpec((MODS_ROWS, D_MODEL), lambda l, j: (0, 0)),
            pl.BlockSpec((None, D_MODEL, nb), lambda l, j: (l, 0, j)),
            pl.BlockSpec((None, 1, nb), lambda l, j: (l, 0, j)),
        ],
        out_specs=pl.BlockSpec((None, MODS_ROWS, nb), lambda l, j: (l, 0, j)),
        out_shape=jax.ShapeDtypeStruct((DEPTH, MODS_ROWS, 6 * D_MODEL), F32),
        compiler_params=_cparams(("arbitrary", "arbitrary")),
        name="mods",
    )(c8, ada_w, ada_b.reshape(DEPTH, 1, 6 * D_MODEL))


def _norm_mod(x, g, shift, scale):
    ms = jnp.mean(x * x, axis=-1, keepdims=True)
    return (x * lax.rsqrt(ms + EPS) * g) * (1.0 + scale) + shift


def _mod_row(mods_ref, tile, tiles_per_seq, fixed_row):
    row = fixed_row if fixed_row is not None else tile // tiles_per_seq
    return mods_ref[pl.ds(row, 1), :]


def _proj_kernel(*refs, tiles_per_seq, fixed_row, rope):
    if rope:
        (x_ref, mods_ref, g_ref, w_ref, qg_ref, kg_ref, mavg_ref, cbd_ref, sbd_ref, cos_ref, sin_ref,
         a_ref, b_ref, q_ref, k_ref, v_ref, u_ref, bg_ref) = refs
    else:
        (x_ref, mods_ref, g_ref, w_ref, qg_ref, kg_ref, mavg_ref, cbd_ref, sbd_ref,
         a_ref, b_ref, q_ref, k_ref, v_ref, u_ref, bg_ref) = refs
    m = _mod_row(mods_ref, pl.program_id(0), tiles_per_seq, fixed_row)
    h = _norm_mod(x_ref[...], g_ref[...], m[:, 0:D_MODEL], m[:, D_MODEL:2 * D_MODEL])
    p = jnp.dot(h.astype(BF16), w_ref[...], preferred_element_type=F32)

    uf = p[:, 0:COL_Q].astype(BF16)
    a_ref[...] = jnp.dot(uf, cbd_ref[...], preferred_element_type=F32).astype(BF16)
    b_ref[...] = jnp.dot(uf, sbd_ref[...], preferred_element_type=F32).astype(BF16)

    def head_norm(t, g):
        ms = jnp.dot((t * t).astype(BF16), mavg_ref[...], preferred_element_type=F32)
        return t * lax.rsqrt(ms + EPS) * g

    def rotate(t):
        n = t.shape[-1]
        lane = lax.broadcasted_iota(I32, t.shape, 1)
        first_half = (lane % HEAD_DIM) < (HEAD_DIM // 2)
        swapped = jnp.where(first_half, pltpu.roll(t, n - HEAD_DIM // 2, 1), pltpu.roll(t, HEAD_DIM // 2, 1))
        return t * cos_ref[...] + swapped * sin_ref[...]

    q = head_norm(p[:, COL_Q:COL_K], qg_ref[...])
    k = head_norm(p[:, COL_K:COL_V], kg_ref[...])
    if rope:
        q = rotate(q)
        k = rotate(k)
    q_ref[...] = (q * ATTN_SCALE).astype(BF16)
    k_ref[...] = k.astype(BF16)
    v_ref[...] = p[:, COL_V:COL_CX].astype(BF16)
    u_ref[...] = p[:, COL_CC:COL_G] * p[:, COL_CX:COL_CB]
    bg_ref[...] = p[:, COL_CB:COL_CC]


def _proj(x2, mods_l, norm_g, w_proj, qg, kg, mavg, cbd, sbd, cos_t, sin_t, *, seq_len, n_seq, fixed_row):
    n_tok = x2.shape[0]
    tm = TOKEN_TILE
    tps = seq_len // tm
    rope = cos_t is not None
    const = lambda i: (0, 0)
    in_specs = [
        pl.BlockSpec((tm, D_MODEL), lambda i: (i, 0)),
        pl.BlockSpec((MODS_ROWS, 6 * D_MODEL), const),
        pl.BlockSpec((1, D_MODEL), const),
        pl.BlockSpec((D_MODEL, COL_G), const),
        pl.BlockSpec((1, NA_WIDTH), const),
        pl.BlockSpec((1, NA_WIDTH), const),
        pl.BlockSpec((NA_WIDTH, NA_WIDTH), const),
        pl.BlockSpec((F_WIDTH, F_WIDTH), const),
        pl.BlockSpec((F_WIDTH, F_WIDTH), const),
    ]
    args = [x2, mods_l, norm_g, w_proj, qg, kg, mavg, cbd, sbd]
    if rope:
        in_specs += [pl.BlockSpec((tm, NA_WIDTH), lambda i: (i % tps, 0))] * 2
        args += [cos_t, sin_t]
    tok = lambda w: pl.BlockSpec((tm, w), lambda i: (i, 0))
    fmap = pl.BlockSpec((tm, F_WIDTH), lambda i: (i % tps, i // tps))
    out_specs = [fmap, fmap, tok(NA_WIDTH), tok(NA_WIDTH), tok(NA_WIDTH), tok(CONV_WIDTH), tok(CONV_WIDTH)]
    out_shape = [
        jax.ShapeDtypeStruct((seq_len, n_seq * F_WIDTH), BF16),
        jax.ShapeDtypeStruct((seq_len, n_seq * F_WIDTH), BF16),
        jax.ShapeDtypeStruct((n_tok, NA_WIDTH), BF16),
        jax.ShapeDtypeStruct((n_tok, NA_WIDTH), BF16),
        jax.ShapeDtypeStruct((n_tok, NA_WIDTH), BF16),
        jax.ShapeDtypeStruct((n_tok, CONV_WIDTH), F32),
        jax.ShapeDtypeStruct((n_tok, CONV_WIDTH), F32),
    ]
    return pl.pallas_call(
        functools.partial(_proj_kernel, tiles_per_seq=tps, fixed_row=fixed_row, rope=rope),
        grid=(n_tok // tm,),
        in_specs=in_specs,
        out_specs=out_specs,
        out_shape=out_shape,
        compiler_params=_cparams(("arbitrary",)),
        name="proj",
    )(*args)


def _fourier_kernel(c_ref, s_ref, a_ref, b_ref, o_ref):
    o = (jnp.dot(c_ref[...], a_ref[...], preferred_element_type=F32)
         - jnp.dot(s_ref[...], b_ref[...], preferred_element_type=F32))
    o_ref[...] = o.astype(BF16)


def _fourier(c_tab, s_tab, a, b):
    seq_len, width = a.shape
    tk = min(seq_len, 256)
    full = lambda i: (0, 0)
    return pl.pallas_call(
        _fourier_kernel,
        grid=(seq_len // tk,),
        in_specs=[
            pl.BlockSpec((tk, seq_len), lambda i: (i, 0)),
            pl.BlockSpec((tk, seq_len), lambda i: (i, 0)),
            pl.BlockSpec((seq_len, width), full, pipeline_mode=pl.Buffered(1)),
            pl.BlockSpec((seq_len, width), full, pipeline_mode=pl.Buffered(1)),
        ],
        out_specs=pl.BlockSpec((tk, width), lambda i: (i, 0)),
        out_shape=jax.ShapeDtypeStruct((seq_len, width), BF16),
        compiler_params=_cparams(("arbitrary",)),
        name="fourier",
    )(c_tab, s_tab, a, b)


def _stack_heads(qg):
    lane_head = lax.broadcasted_iota(I32, qg.shape, 1) // HEAD_DIM
    zero = jnp.zeros_like(qg)
    return jnp.concatenate([jnp.where(lane_head == h, qg, zero) for h in range(HEADS_PER_GROUP)], axis=0)


def _unstack_heads(o, rows):
    lane_head = lax.broadcasted_iota(I32, (rows, o.shape[1]), 1) // HEAD_DIM
    acc = jnp.zeros((rows, o.shape[1]), F32)
    for h in range(HEADS_PER_GROUP):
        acc = acc + jnp.where(lane_head == h, o[h * rows:(h + 1) * rows, :], 0.0)
    return acc


_NT = (((1,), (1,)), ((), ()))


def _attn_kernel(q_ref, k_ref, v_ref, kc_ref, vc_ref, bias_ref, o_ref, *, rows):
    r = pl.program_id(1)
    rs = jnp.clip(r - NA_WIN_R // 2, 0, rows - NA_WIN_R)
    start = pl.multiple_of(rs * GRID_W, GRID_W)
    n_loc = NA_WIN_R * GRID_W
    kwin = k_ref[pl.ds(start, n_loc), :]
    vwin = v_ref[pl.ds(start, n_loc), :]
    q = q_ref[...]
    outs = []
    for g in range(NA_HEADS // HEADS_PER_GROUP):
        sl = slice(g * V7X_MXU_DIM, (g + 1) * V7X_MXU_DIM)
        qs = _stack_heads(q[:, sl])
        s_loc = lax.dot_general(qs, kwin[:, sl], _NT, preferred_element_type=F32)
        bias = bias_ref[g * HEADS_PER_GROUP:(g + 1) * HEADS_PER_GROUP].reshape(HEADS_PER_GROUP * GRID_W, n_loc)
        s_loc = s_loc + bias
        s_ctx = lax.dot_general(qs, kc_ref[:, sl], _NT, preferred_element_type=F32)
        m = jnp.maximum(jnp.max(s_loc, axis=-1, keepdims=True), jnp.max(s_ctx, axis=-1, keepdims=True))
        p_loc = jnp.exp(s_loc - m)
        p_ctx = jnp.exp(s_ctx - m)
        denom = jnp.sum(p_loc, axis=-1, keepdims=True) + jnp.sum(p_ctx, axis=-1, keepdims=True)
        o = (jnp.dot(p_loc.astype(BF16), vwin[:, sl], preferred_element_type=F32)
             + jnp.dot(p_ctx.astype(BF16), vc_ref[:, sl], preferred_element_type=F32))
        outs.append(_unstack_heads(o / denom, GRID_W))
    o_ref[...] = jnp.concatenate(outs, axis=1).astype(BF16)


def _attn(q, k, v, kc, vc, bias_tab, *, n_seq, seq_len, ctx_len):
    rows = seq_len // GRID_W

    def bias_map(b, r):
        rs = jnp.clip(r - NA_WIN_R // 2, 0, rows - NA_WIN_R)
        return (rs - r + NA_WIN_R - 1, 0, 0, 0)

    return pl.pallas_call(
        functools.partial(_attn_kernel, rows=rows),
        grid=(n_seq, rows),
        in_specs=[
            pl.BlockSpec((GRID_W, NA_WIDTH), lambda b, r: (b * rows + r, 0)),
            pl.BlockSpec((seq_len, NA_WIDTH), lambda b, r: (b, 0)),
            pl.BlockSpec((seq_len, NA_WIDTH), lambda b, r: (b, 0)),
            pl.BlockSpec((ctx_len, NA_WIDTH), lambda b, r: (b, 0)),
            pl.BlockSpec((ctx_len, NA_WIDTH), lambda b, r: (b, 0)),
            pl.BlockSpec((None, NA_HEADS, GRID_W, NA_WIN_R * GRID_W), bias_map),
        ],
        out_specs=pl.BlockSpec((GRID_W, NA_WIDTH), lambda b, r: (b * rows + r, 0)),
        out_shape=jax.ShapeDtypeStruct((n_seq * seq_len, NA_WIDTH), BF16),
        compiler_params=_cparams(("arbitrary", "arbitrary")),
        name="attn",
    )(q, k, v, kc, vc, bias_tab)


def _ctx_attn_kernel(q_ref, k_ref, v_ref, o_ref):
    q = q_ref[...]
    n = q.shape[0]
    outs = []
    for g in range(NA_HEADS // HEADS_PER_GROUP):
        sl = slice(g * V7X_MXU_DIM, (g + 1) * V7X_MXU_DIM)
        qs = _stack_heads(q[:, sl])
        s = lax.dot_general(qs, k_ref[:, sl], _NT, preferred_element_type=F32)
        m = jnp.max(s, axis=-1, keepdims=True)
        p = jnp.exp(s - m)
        denom = jnp.sum(p, axis=-1, keepdims=True)
        o = jnp.dot(p.astype(BF16), v_ref[:, sl], preferred_element_type=F32)
        outs.append(_unstack_heads(o / denom, n))
    o_ref[...] = jnp.concatenate(outs, axis=1).astype(BF16)


def _ctx_attn(q, k, v, *, n_seq, ctx_len):
    spec = pl.BlockSpec((ctx_len, NA_WIDTH), lambda b: (b, 0))
    return pl.pallas_call(
        _ctx_attn_kernel,
        grid=(n_seq,),
        in_specs=[spec, spec, spec],
        out_specs=spec,
        out_shape=jax.ShapeDtypeStruct((n_seq * ctx_len, NA_WIDTH), BF16),
        compiler_params=_cparams(("arbitrary",)),
        name="ctx_attn",
    )(q, k, v)


def _merge_kernel(x_ref, mods_ref, n1_ref, n2_ref, f_ref, at_ref, u_ref, up_ref, un_ref, bg_ref, cw_ref,
                  wg_ref, wf_ref, wna_ref, wcv_ref, wo_ref, rwt_ref,
                  xo_ref, h2_ref, lg_ref, *, tiles_per_seq, fixed_row):
    i = pl.program_id(0)
    m = _mod_row(mods_ref, i, tiles_per_seq, fixed_row)
    dm = D_MODEL
    x = x_ref[...]
    h = _norm_mod(x, n1_ref[...], m[:, 0:dm], m[:, dm:2 * dm]).astype(BF16)
    gates = jax.nn.sigmoid(jnp.dot(h, wg_ref[...], preferred_element_type=F32))

    y_f = jnp.dot(f_ref[...], wf_ref[...], preferred_element_type=F32)
    y_na = jnp.dot(at_ref[...], wna_ref[...], preferred_element_type=F32)

    u = u_ref[...]
    t = u.shape[0]
    ti = i % tiles_per_seq
    row = lax.broadcasted_iota(I32, u.shape, 0)
    prev_row = jnp.where(ti == 0, 0.0, up_ref[V7X_SUBLANES - 1:V7X_SUBLANES, :])
    next_row = jnp.where(ti == tiles_per_seq - 1, 0.0, un_ref[0:1, :])
    u_prev = jnp.where(row == 0, prev_row, pltpu.roll(u, 1, 0))
    u_next = jnp.where(row == t - 1, next_row, pltpu.roll(u, t - 1, 0))
    y_cv = bg_ref[...] * (cw_ref[0:1, :] * u_prev + cw_ref[1:2, :] * u + cw_ref[2:3, :] * u_next)
    y_cv = jnp.dot(y_cv.astype(BF16), wcv_ref[...], preferred_element_type=F32)

    merged = gates[:, 0:dm] * y_f + gates[:, dm:2 * dm] * y_na + gates[:, 2 * dm:3 * dm] * y_cv
    mixed = jnp.dot(merged.astype(BF16), wo_ref[...], preferred_element_type=F32)
    x_new = x + m[:, 2 * dm:3 * dm] * mixed
    xo_ref[...] = x_new

    h2 = _norm_mod(x_new, n2_ref[...], m[:, 3 * dm:4 * dm], m[:, 4 * dm:5 * dm])
    h2_ref[...] = h2
    lg_ref[...] = lax.dot_general(rwt_ref[...], h2, _NT, precision=HIGHEST, preferred_element_type=F32)


def _merge(x2, mods_l, n1, n2, f_all, attn, u, bg, conv_w, w_gate, w_f, w_na, w_cv, w_o, rwt,
           *, seq_len, fixed_row):
    n_tok = x2.shape[0]
    tm = TOKEN_TILE
    tps = seq_len // tm
    const = lambda i: (0, 0)
    halo = tm // V7X_SUBLANES
    n_halo = n_tok // V7X_SUBLANES
    in_specs = [
        pl.BlockSpec((tm, D_MODEL), lambda i: (i, 0)),
        pl.BlockSpec((MODS_ROWS, 6 * D_MODEL), const),
        pl.BlockSpec((1, D_MODEL), const),
        pl.BlockSpec((1, D_MODEL), const),
        pl.BlockSpec((tm, F_WIDTH), lambda i: (i % tps, i // tps)),
        pl.BlockSpec((tm, NA_WIDTH), lambda i: (i, 0)),
        pl.BlockSpec((tm, CONV_WIDTH), lambda i: (i, 0)),
        pl.BlockSpec((V7X_SUBLANES, CONV_WIDTH), lambda i: (jnp.maximum(i * halo - 1, 0), 0)),
        pl.BlockSpec((V7X_SUBLANES, CONV_WIDTH), lambda i: (jnp.minimum((i + 1) * halo, n_halo - 1), 0)),
        pl.BlockSpec((tm, CONV_WIDTH), lambda i: (i, 0)),
        pl.BlockSpec((3, CONV_WIDTH), const),
        pl.BlockSpec((D_MODEL, 3 * D_MODEL), const),
        pl.BlockSpec((F_WIDTH, D_MODEL), const),
        pl.BlockSpec((NA_WIDTH, D_MODEL), const),
        pl.BlockSpec((CONV_WIDTH, D_MODEL), const),
        pl.BlockSpec((D_MODEL, D_MODEL), const),
        pl.BlockSpec((N_EXPERTS, D_MODEL), const),
    ]
    out_specs = [
        pl.BlockSpec((tm, D_MODEL), lambda i: (i, 0)),
        pl.BlockSpec((tm, D_MODEL), lambda i: (i, 0)),
        pl.BlockSpec((N_EXPERTS, tm), lambda i: (0, i)),
    ]
    out_shape = [
        jax.ShapeDtypeStruct((n_tok, D_MODEL), F32),
        jax.ShapeDtypeStruct((n_tok, D_MODEL), F32),
        jax.ShapeDtypeStruct((N_EXPERTS, n_tok), F32),
    ]
    return pl.pallas_call(
        functools.partial(_merge_kernel, tiles_per_seq=tps, fixed_row=fixed_row),
        grid=(n_tok // tm,),
        in_specs=in_specs,
        out_specs=out_specs,
        out_shape=out_shape,
        compiler_params=_cparams(("arbitrary",)),
        name="merge",
    )(x2, mods_l, n1, n2, f_all, attn, u, u, u, bg, conv_w, w_gate, w_f, w_na, w_cv, w_o, rwt)


def _first_max(vals):
    best = vals[0]
    idx = jnp.zeros(best.shape, I32)
    for j in range(1, len(vals)):
        better = vals[j] > best
        idx = jnp.where(better, j, idx)
        best = jnp.where(better, vals[j], best)
    return best, idx


def _select(idx, vals):
    out = vals[-1]
    for j in range(len(vals) - 2, -1, -1):
        out = jnp.where(idx == j, vals[j], out)
    return out


def _route_kernel(lg_ref, rb_ref, ids_ref, wts_ref, cnt_ref, run_ref):
    step = pl.program_id(0)

    @pl.when(step == 0)
    def _():
        run_ref[...] = jnp.zeros_like(run_ref)

    s = jax.nn.sigmoid(lg_ref[...])
    sb = s + rb_ref[...]
    t = s.shape[1]
    s_rows = [s[e:e + 1, :] for e in range(N_EXPERTS)]
    b_rows = [sb[e:e + 1, :] for e in range(N_EXPERTS)]
    epg = EXPERTS_PER_GROUP
    gscore = []
    for g in range(N_GROUPS):
        v = b_rows[g * epg:(g + 1) * epg]
        pair = None
        for a in range(epg):
            for b in range(a + 1, epg):
                pair = v[a] + v[b] if pair is None else jnp.maximum(pair, v[a] + v[b])
        gscore.append(pair)
    _, gi = _first_max(gscore)
    bv = [_select(gi, [b_rows[g * epg + j] for g in range(N_GROUPS)]) for j in range(epg)]
    sv = [_select(gi, [s_rows[g * epg + j] for g in range(N_GROUPS)]) for j in range(epg)]
    _, i1 = _first_max(bv)
    _, i2 = _first_max([jnp.where(i1 == j, -jnp.inf, bv[j]) for j in range(epg)])
    s1 = _select(i1, sv)
    s2 = _select(i2, sv)
    tot = s1 + s2
    e1 = gi * epg + i1
    e2 = gi * epg + i2

    eid = lax.broadcasted_iota(I32, (N_EXPERTS, t), 0)
    hit1 = eid == e1
    hit2 = eid == e2
    onehot = jnp.where(hit1 | hit2, 1.0, 0.0)
    before = (lax.broadcasted_iota(I32, (t, t), 0) < lax.broadcasted_iota(I32, (t, t), 1))
    prefix = jnp.dot(onehot.astype(BF16), jnp.where(before, 1.0, 0.0).astype(BF16),
                     preferred_element_type=F32) + run_ref[...]
    r1 = jnp.sum(jnp.where(hit1, prefix, 0.0), axis=0, keepdims=True)
    r2 = jnp.sum(jnp.where(hit2, prefix, 0.0), axis=0, keepdims=True)
    run = run_ref[...] + jnp.sum(onehot, axis=1, keepdims=True)
    run_ref[...] = run
    cnt_ref[...] = jnp.broadcast_to(run, cnt_ref.shape)

    zi = jnp.zeros((V7X_SUBLANES - 4, t), I32)
    ids_ref[...] = jnp.concatenate([e1, e2, r1.astype(I32), r2.astype(I32), zi], axis=0)
    zf = jnp.zeros((V7X_SUBLANES - 2, t), F32)
    wts_ref[...] = jnp.concatenate([s1 / tot, s2 / tot, zf], axis=0)


def _route(logits_t, router_b):
    n_tok = logits_t.shape[1]
    tr = ROUTE_TILE
    return pl.pallas_call(
        _route_kernel,
        grid=(n_tok // tr,),
        in_specs=[
            pl.BlockSpec((N_EXPERTS, tr), lambda i: (0, i)),
            pl.BlockSpec((N_EXPERTS, 1), lambda i: (0, 0)),
        ],
        out_specs=[
            pl.BlockSpec((V7X_SUBLANES, tr), lambda i: (0, i)),
            pl.BlockSpec((V7X_SUBLANES, tr), lambda i: (0, i)),
            pl.BlockSpec((N_EXPERTS, V7X_LANES), lambda i: (0, 0)),
        ],
        out_shape=[
            jax.ShapeDtypeStruct((V7X_SUBLANES, n_tok), I32),
            jax.ShapeDtypeStruct((V7X_SUBLANES, n_tok), F32),
            jax.ShapeDtypeStruct((N_EXPERTS, V7X_LANES), F32),
        ],
        scratch_shapes=[pltpu.VMEM((N_EXPERTS, 1), F32)],
        compiler_params=_cparams(("arbitrary",)),
        name="route",
    )(logits_t, router_b.reshape(N_EXPERTS, 1))


def _slots_kernel(ids_ref, cnt_ref, dst_ref, blk_ref, *, n_blocks):
    cnt = cnt_ref[...][:, 0:1]
    blk = float(EXPERT_BLOCK)
    padded = jnp.floor((cnt + (blk - 1.0)) / blk) * blk
    starts = []
    run = jnp.zeros((1, 1), F32)
    ends = []
    for e in range(N_EXPERTS):
        starts.append(run)
        run = run + padded[e:e + 1, :]
        ends.append(run)
    ids = ids_ref[...]
    e1, e2 = ids[0:1, :], ids[1:2, :]
    r1, r2 = ids[2:3, :], ids[3:4, :]
    t = ids.shape[1]
    s1 = jnp.zeros((1, t), F32)
    s2 = jnp.zeros((1, t), F32)
    for e in range(N_EXPERTS):
        s1 = jnp.where(e1 == e, starts[e], s1)
        s2 = jnp.where(e2 == e, starts[e], s2)
    zi = jnp.zeros((V7X_SUBLANES - 2, t), I32)
    dst_ref[...] = jnp.concatenate([s1.astype(I32) + r1, s2.astype(I32) + r2, zi], axis=0)

    w = blk_ref.shape[1]
    first_row = lax.broadcasted_iota(I32, (1, w), 1).astype(F32) * blk
    owner = jnp.zeros((1, w), F32)
    for e in range(N_EXPERTS):
        owner = owner + jnp.where(first_row >= ends[e], 1.0, 0.0)
    owner = jnp.minimum(owner, float(N_EXPERTS - 1))
    used = jnp.broadcast_to(ends[-1] / blk, (1, w))
    zb = jnp.zeros((V7X_SUBLANES - 2, w), I32)
    blk_ref[...] = jnp.concatenate([owner.astype(I32), used.astype(I32), zb], axis=0)


def _slots(ids, cnt, n_blocks):
    n_tok = ids.shape[1]
    tr = ROUTE_TILE
    wblk = -(-n_blocks // V7X_LANES) * V7X_LANES
    return pl.pallas_call(
        functools.partial(_slots_kernel, n_blocks=n_blocks),
        grid=(n_tok // tr,),
        in_specs=[
            pl.BlockSpec((V7X_SUBLANES, tr), lambda i: (0, i)),
            pl.BlockSpec((N_EXPERTS, V7X_LANES), lambda i: (0, 0)),
        ],
        out_specs=[
            pl.BlockSpec((V7X_SUBLANES, tr), lambda i: (0, i)),
            pl.BlockSpec((V7X_SUBLANES, wblk), lambda i: (0, 0)),
        ],
        out_shape=[
            jax.ShapeDtypeStruct((V7X_SUBLANES, n_tok), I32),
            jax.ShapeDtypeStruct((V7X_SUBLANES, wblk), I32),
        ],
        compiler_params=_cparams(("arbitrary",)),
        name="slots",
    )(ids, cnt)


def _row_copy(src_ref, src_row, dst_ref, dst_row, sem):
    return pltpu.make_async_copy(src_ref.at[pl.ds(src_row, 1)], dst_ref.at[pl.ds(dst_row, 1)], sem)


def _dispatch_kernel(dst_ref, h_ref, init_ref, xb_ref, sem):
    del init_ref
    t = h_ref.shape[0]

    def start(r, c):
        _row_copy(h_ref, r, xb_ref, dst_ref[0, r], sem).start()
        _row_copy(h_ref, r, xb_ref, dst_ref[1, r], sem).start()
        return c

    def wait(r, c):
        _row_copy(h_ref, r, xb_ref, dst_ref[0, r], sem).wait()
        _row_copy(h_ref, r, xb_ref, dst_ref[1, r], sem).wait()
        return c

    lax.fori_loop(0, t, start, 0)
    lax.fori_loop(0, t, wait, 0)


def _dispatch(dst_tiles, h2, n_slots):
    n_tok = h2.shape[0]
    tm = TOKEN_TILE
    init = jnp.zeros((n_slots, D_MODEL), F32)
    return pl.pallas_call(
        _dispatch_kernel,
        grid=(n_tok // tm,),
        in_specs=[
            pl.BlockSpec((None, 2, tm), lambda i: (i, 0, 0), memory_space=pltpu.SMEM),
            pl.BlockSpec((tm, D_MODEL), lambda i: (i, 0)),
            pl.BlockSpec(memory_space=pl.ANY),
        ],
        out_specs=pl.BlockSpec(memory_space=pl.ANY),
        out_shape=jax.ShapeDtypeStruct((n_slots, D_MODEL), F32),
        scratch_shapes=[pltpu.SemaphoreType.DMA],
        input_output_aliases={2: 0},
        compiler_params=_cparams(("arbitrary",)),
        name="dispatch",
    )(dst_tiles, h2, init)


def _experts_kernel(blk_ref, used_ref, x_ref, w1_ref, w3_ref, w2_ref, y_ref):
    i = pl.program_id(0)

    @pl.when(i < used_ref[0])
    def _():
        x = x_ref[...].astype(BF16)
        a = jnp.dot(x, w1_ref[...], preferred_element_type=F32)
        b = jnp.dot(x, w3_ref[...], preferred_element_type=F32)
        hid = (a * jax.nn.sigmoid(a) * b).astype(BF16)
        y_ref[...] = jnp.dot(hid, w2_ref[...], preferred_element_type=F32)

    @pl.when(i >= used_ref[0])
    def _():
        y_ref[...] = jnp.zeros_like(y_ref)


def _experts(blk_e, used, xb, w1, w3, w2):
    n_slots = xb.shape[0]
    bm = EXPERT_BLOCK
    grid_spec = pltpu.PrefetchScalarGridSpec(
        num_scalar_prefetch=2,
        grid=(n_slots // bm,),
        in_specs=[
            pl.BlockSpec((bm, D_MODEL), lambda i, be, nu: (i, 0)),
            pl.BlockSpec((None, D_MODEL, D_EXPERT), lambda i, be, nu: (be[i], 0, 0)),
            pl.BlockSpec((None, D_MODEL, D_EXPERT), lambda i, be, nu: (be[i], 0, 0)),
            pl.BlockSpec((None, D_EXPERT, D_MODEL), lambda i, be, nu: (be[i], 0, 0)),
        ],
        out_specs=pl.BlockSpec((bm, D_MODEL), lambda i, be, nu: (i, 0)),
    )
    return pl.pallas_call(
        _experts_kernel,
        grid_spec=grid_spec,
        out_shape=jax.ShapeDtypeStruct((n_slots, D_MODEL), F32),
        compiler_params=_cparams(("arbitrary",)),
        name="experts",
    )(blk_e, used, xb, w1, w3, w2)


def _combine_kernel(dst_ref, x_ref, mods_ref, w_ref, yb_ref, o_ref, ya, yc, sem,
                    *, tiles_per_seq, fixed_row):
    t = x_ref.shape[0]

    def start(r, c):
        _row_copy(yb_ref, dst_ref[0, r], ya, r, sem.at[0]).start()
        _row_copy(yb_ref, dst_ref[1, r], yc, r, sem.at[1]).start()
        return c

    def wait(r, c):
        _row_copy(yb_ref, dst_ref[0, r], ya, r, sem.at[0]).wait()
        _row_copy(yb_ref, dst_ref[1, r], yc, r, sem.at[1]).wait()
        return c

    lax.fori_loop(0, t, start, 0)
    lax.fori_loop(0, t, wait, 0)
    m = _mod_row(mods_ref, pl.program_id(0), tiles_per_seq, fixed_row)
    w = w_ref[...]
    y = w[:, 0:1] * ya[...] + w[:, 1:2] * yc[...]
    o_ref[...] = x_ref[...] + m[:, 5 * D_MODEL:6 * D_MODEL] * y


def _combine(dst_tiles, x_new, mods_l, wts_cols, yb, *, seq_len, fixed_row, tile_offset):
    n_tok = x_new.shape[0]
    tm = TOKEN_TILE
    tps = seq_len // tm
    return pl.pallas_call(
        functools.partial(_combine_kernel, tiles_per_seq=tps, fixed_row=fixed_row),
        grid=(n_tok // tm,),
        in_specs=[
            pl.BlockSpec((None, 2, tm), lambda i: (i + tile_offset, 0, 0), memory_space=pltpu.SMEM),
            pl.BlockSpec((tm, D_MODEL), lambda i: (i, 0)),
            pl.BlockSpec((MODS_ROWS, 6 * D_MODEL), lambda i: (0, 0)),
            pl.BlockSpec((tm, 2), lambda i: (i + tile_offset, 0)),
            pl.BlockSpec(memory_space=pl.ANY),
        ],
        out_specs=pl.BlockSpec((tm, D_MODEL), lambda i: (i, 0)),
        out_shape=jax.ShapeDtypeStruct((n_tok, D_MODEL), F32),
        scratch_shapes=[pltpu.VMEM((tm, D_MODEL), F32), pltpu.VMEM((tm, D_MODEL), F32),
                        pltpu.SemaphoreType.DMA((2,))],
        compiler_params=_cparams(("arbitrary",)),
        name="combine",
    )(dst_tiles, x_new, mods_l, wts_cols, yb)


def _channel_dft_tables():
    j = np.arange(F_GDIM)
    ang = 2.0 * np.pi * ((j[:, None] * j[None, :]) % F_GDIM) / F_GDIM
    eye = np.eye(F_GROUPS)
    return (jnp.asarray(np.kron(eye, np.cos(ang)), F32).astype(BF16),
            jnp.asarray(np.kron(eye, np.sin(ang)), F32).astype(BF16))


def _position_dft_tables(seq_len):
    scale = 1.0 / math.sqrt(seq_len * F_GDIM)
    k = np.arange(seq_len, dtype=np.int64)
    if seq_len <= 256:
        ang = 2.0 * np.pi * ((k[:, None] * k[None, :]) % seq_len) / seq_len
        return (jnp.asarray(np.cos(ang) * scale, F32).astype(BF16),
                jnp.asarray(np.sin(ang) * scale, F32).astype(BF16))
    hi = seq_len // 64
    t1 = np.arange(hi, dtype=np.int64)
    t0 = np.arange(64, dtype=np.int64)
    ang_a = 2.0 * np.pi * ((k[:, None] * t1[None, :] * 64) % seq_len) / seq_len
    ang_b = 2.0 * np.pi * ((k[:, None] * t0[None, :]) % seq_len) / seq_len
    ca = jnp.asarray(np.cos(ang_a) * scale, F32)[:, :, None]
    sa = jnp.asarray(np.sin(ang_a) * scale, F32)[:, :, None]
    cb = jnp.asarray(np.cos(ang_b), F32)[:, None, :]
    sb = jnp.asarray(np.sin(ang_b), F32)[:, None, :]
    c = (ca * cb - sa * sb).reshape(seq_len, seq_len).astype(BF16)
    s = (sa * cb + ca * sb).reshape(seq_len, seq_len).astype(BF16)
    return c, s


def _rope_tables(seq_len):
    t = np.arange(seq_len)
    row = (t // GRID_W).astype(np.float64)
    col = (t % GRID_W).astype(np.float64)
    inv = np.power(ROPE_BASE, -np.arange(ROPE_PER_AXIS, dtype=np.float64) / ROPE_PER_AXIS)
    ang = np.concatenate([row[:, None] * inv, col[:, None] * inv], axis=-1)
    cos = np.cos(ang)
    sin = np.sin(ang)
    cos_h = np.concatenate([cos, cos], axis=-1)
    sin_h = np.concatenate([-sin, sin], axis=-1)
    return (jnp.asarray(np.tile(cos_h, (1, NA_HEADS)), F32), jnp.asarray(np.tile(sin_h, (1, NA_HEADS)), F32))


def _bias_table(rpb_l):
    col = np.arange(GRID_W)
    col_start = np.clip(col - NA_WIN_C // 2, 0, GRID_W - NA_WIN_C)
    col_mask = (col[None, :] >= col_start[:, None]) & (col[None, :] < col_start[:, None] + NA_WIN_C)
    dc = np.clip(col[None, :] - col[:, None] + (NA_WIN_C - 1), 0, 2 * NA_WIN_C - 2)
    dr = np.arange(NA_WIN_R)[:, None] + np.arange(NA_WIN_R)[None, :]
    b = rpb_l[:, dr][:, :, :, dc]
    b = jnp.where(jnp.asarray(col_mask)[None, None, None], b, NEG_BIG)
    b = b.transpose(1, 0, 3, 2, 4)
    return b.reshape(NA_WIN_R, NA_HEADS, GRID_W, NA_WIN_R * GRID_W).astype(F32)


def _moe(h2, logits_t, w1, w3, w2, router_b):
    n_tok = h2.shape[0]
    n_assign = 2 * n_tok
    n_blocks = (n_assign + N_EXPERTS * (EXPERT_BLOCK - 1) + EXPERT_BLOCK - 1) // EXPERT_BLOCK
    n_slots = n_blocks * EXPERT_BLOCK
    ids, wts, cnt = _route(logits_t, router_b)
    dst, blk = _slots(ids, cnt, n_blocks)
    n_tiles = n_tok // TOKEN_TILE
    dst_tiles = dst[0:2].reshape(2, n_tiles, TOKEN_TILE).transpose(1, 0, 2)
    xb = _dispatch(dst_tiles, h2, n_slots)
    yb = _experts(blk[0, :n_blocks], blk[1, 0:1], xb, w1, w3, w2)
    return yb, dst_tiles, wts[0:2].T


def kernel(x, c, ctx, c_ctx, ada_w, ada_b, norm1_g, w_in, qn_g, kn_g, rpb, conv_w, w_f, w_na, w_cv, w_o,
           norm2_g, router_w, router_b, w1, w3, w2):
    bsz, seq_len, d = x.shape
    ctx_len = ctx.shape[1]
    n_lat = bsz * seq_len
    n_ctx = bsz * ctx_len
    ctx_row = bsz

    c8 = jnp.concatenate([c, c_ctx[None, :], jnp.zeros((MODS_ROWS - bsz - 1, d), F32)], axis=0)
    mods = _mods(c8, ada_w, ada_b)

    cbd, sbd = _channel_dft_tables()
    c_lat, s_lat = _position_dft_tables(seq_len)
    c_ctx_t, s_ctx_t = _position_dft_tables(ctx_len)
    cos_t, sin_t = _rope_tables(seq_len)
    mavg = jnp.asarray(np.kron(np.eye(NA_HEADS), np.full((HEAD_DIM, HEAD_DIM), 1.0 / HEAD_DIM)), F32).astype(BF16)
    rwt = router_w.T

    xl = x.reshape(n_lat, d)
    xc = ctx.reshape(n_ctx, d)
    for l in range(DEPTH):
        last = l == DEPTH - 1
        w_proj = w_in[l][:, :COL_G].astype(BF16)
        w_gate = w_in[l][:, COL_G:].astype(BF16)
        wf, wna, wcv, wo = (w_f[l].astype(BF16), w_na[l].astype(BF16), w_cv[l].astype(BF16), w_o[l].astype(BF16))
        e1, e3, e2 = w1[l].astype(BF16), w3[l].astype(BF16), w2[l].astype(BF16)
        n1 = norm1_g[l].reshape(1, d)
        n2 = norm2_g[l].reshape(1, d)
        qg = jnp.tile(qn_g[l], NA_HEADS).reshape(1, NA_WIDTH)
        kg = jnp.tile(kn_g[l], NA_HEADS).reshape(1, NA_WIDTH)
        bias_tab = _bias_table(rpb[l])
        mods_l = mods[l]

        a_c, b_c, q_c, k_c, v_c, u_c, bg_c = _proj(
            xc, mods_l, n1, w_proj, qg, kg, mavg, cbd, sbd, None, None,
            seq_len=ctx_len, n_seq=bsz, fixed_row=ctx_row)
        a_l, b_l, q_l, k_l, v_l, u_l, bg_l = _proj(
            xl, mods_l, n1, w_proj, qg, kg, mavg, cbd, sbd, cos_t, sin_t,
            seq_len=seq_len, n_seq=bsz, fixed_row=None)

        f_l = _fourier(c_lat, s_lat, a_l, b_l)
        attn_l = _attn(q_l, k_l, v_l, k_c, v_c, bias_tab, n_seq=bsz, seq_len=seq_len, ctx_len=ctx_len)
        xl_new, h2_l, lg_l = _merge(xl, mods_l, n1, n2, f_l, attn_l, u_l, bg_l, conv_w[l], w_gate,
                                    wf, wna, wcv, wo, rwt, seq_len=seq_len, fixed_row=None)
        if last:
            yb, dst_tiles, wcols = _moe(h2_l, lg_l, e1, e3, e2, router_b)
            xl = _combine(dst_tiles, xl_new, mods_l, wcols, yb, seq_len=seq_len, fixed_row=None, tile_offset=0)
        else:
            f_c = _fourier(c_ctx_t, s_ctx_t, a_c, b_c)
            attn_c = _ctx_attn(q_c, k_c, v_c, n_seq=bsz, ctx_len=ctx_len)
            xc_new, h2_c, lg_c = _merge(xc, mods_l, n1, n2, f_c, attn_c, u_c, bg_c, conv_w[l], w_gate,
                                        wf, wna, wcv, wo, rwt, seq_len=ctx_len, fixed_row=ctx_row)
            h2 = jnp.concatenate([h2_l, h2_c], axis=0)
            lg = jnp.concatenate([lg_l, lg_c], axis=1)
            yb, dst_tiles, wcols = _moe(h2, lg, e1, e3, e2, router_b)
            xl = _combine(dst_tiles, xl_new, mods_l, wcols, yb, seq_len=seq_len, fixed_row=None, tile_offset=0)
            xc = _combine(dst_tiles, xc_new, mods_l, wcols, yb, seq_len=ctx_len, fixed_row=ctx_row,
                          tile_offset=n_lat // TOKEN_TILE)
    return xl.reshape(bsz, seq_len, d)
```

```python
import functools
import math

import numpy as np
import jax
import jax.numpy as jnp
from jax import lax
from jax.experimental import pallas as pl
from jax.experimental.pallas import tpu as pltpu

F32 = jnp.float32
BF16 = jnp.bfloat16
I32 = jnp.int32
HIGHEST = lax.Precision.HIGHEST

D_MODEL = 1024
DEPTH = 2
GRID_W = 64
EPS = 1e-6
F_GROUPS = 4
F_GDIM = 64
F_WIDTH = 256
NA_HEADS = 8
HEAD_DIM = 64
NA_WIDTH = 512
NA_WIN_R = 8
NA_WIN_C = 16
ATTN_SCALE = HEAD_DIM ** -0.5
ROPE_BASE = 10000.0
ROPE_PER_AXIS = HEAD_DIM // 4
CONV_WIDTH = 256
COL_Q = 256
COL_K = 768
COL_V = 1280
COL_CX = 1792
COL_CB = 2048
COL_CC = 2304
COL_G = 2560
N_EXPERTS = 16
N_GROUPS = 4
EXPERTS_PER_GROUP = 4
D_EXPERT = 512

V7X_LANES = 128
V7X_SUBLANES = 8
V7X_MXU_DIM = 256

TOKEN_TILE = 256
ROUTE_TILE = 512
EXPERT_BLOCK = 256
SLOT_GROUP = V7X_SUBLANES
SORT_ROWS = -(-(2 * ROUTE_TILE + N_EXPERTS * (SLOT_GROUP - 1)) // V7X_LANES) * V7X_LANES
SORT_GROUPS_PAD = -(-(SORT_ROWS // SLOT_GROUP) // V7X_LANES) * V7X_LANES
HEADS_PER_GROUP = V7X_MXU_DIM // HEAD_DIM
NEG_BIG = -1e30
MODS_ROWS = 8
VMEM_LIMIT = 48 * 1024 * 1024


def _cparams(sem):
    return pltpu.CompilerParams(dimension_semantics=sem, vmem_limit_bytes=VMEM_LIMIT)


def _mods_kernel(c_ref, w_ref, b_ref, o_ref):
    c = c_ref[...]
    sc = c * jax.nn.sigmoid(c)
    o_ref[...] = jnp.dot(sc, w_ref[...], precision=HIGHEST, preferred_element_type=F32) + b_ref[...]


def _mods(c8, ada_w, ada_b):
    nb = 1536
    return pl.pallas_call(
        _mods_kernel,
        grid=(DEPTH, 6 * D_MODEL // nb),
        in_specs=[
            pl.BlockSpec((MODS_ROWS, D_MODEL), lambda l, j: (0, 0)),
            pl.BlockSpec((None, D_MODEL, nb), lambda l, j: (l, 0, j)),
            pl.BlockSpec((None, 1, nb), lambda l, j: (l, 0, j)),
        ],
        out_specs=pl.BlockSpec((None, MODS_ROWS, nb), lambda l, j: (l, 0, j)),
        out_shape=jax.ShapeDtypeStruct((DEPTH, MODS_ROWS, 6 * D_MODEL), F32),
        compiler_params=_cparams(("arbitrary", "arbitrary")),
        name="mods",
    )(c8, ada_w, ada_b.reshape(DEPTH, 1, 6 * D_MODEL))


def _norm_mod(x, g, shift, scale):
    ms = jnp.mean(x * x, axis=-1, keepdims=True)
    return (x * lax.rsqrt(ms + EPS) * g) * (1.0 + scale) + shift


def _mod_row(mods_ref, tile, tiles_per_seq, fixed_row):
    row = fixed_row if fixed_row is not None else tile // tiles_per_seq
    return mods_ref[pl.ds(row, 1), :]


def _proj_kernel(*refs, tiles_per_seq, fixed_row, rope):
    if rope:
        (x_ref, mods_ref, g_ref, w_ref, qg_ref, kg_ref, mavg_ref, cbd_ref, sbd_ref, cos_ref, sin_ref,
         a_ref, b_ref, q_ref, k_ref, v_ref, u_ref, bg_ref) = refs
    else:
        (x_ref, mods_ref, g_ref, w_ref, qg_ref, kg_ref, mavg_ref, cbd_ref, sbd_ref,
         a_ref, b_ref, q_ref, k_ref, v_ref, u_ref, bg_ref) = refs
    m = _mod_row(mods_ref, pl.program_id(0), tiles_per_seq, fixed_row)
    h = _norm_mod(x_ref[...], g_ref[...], m[:, 0:D_MODEL], m[:, D_MODEL:2 * D_MODEL])
    p = jnp.dot(h.astype(BF16), w_ref[...], preferred_element_type=F32)

    uf = p[:, 0:COL_Q].astype(BF16)
    a_ref[...] = jnp.dot(uf, cbd_ref[...], preferred_element_type=F32).astype(BF16)
    b_ref[...] = jnp.dot(uf, sbd_ref[...], preferred_element_type=F32).astype(BF16)

    def head_norm(t, g):
        ms = jnp.dot((t * t).astype(BF16), mavg_ref[...], preferred_element_type=F32)
        return t * lax.rsqrt(ms + EPS) * g

    def rotate(t):
        n = t.shape[-1]
        lane = lax.broadcasted_iota(I32, t.shape, 1)
        first_half = (lane % HEAD_DIM) < (HEAD_DIM // 2)
        swapped = jnp.where(first_half, pltpu.roll(t, n - HEAD_DIM // 2, 1), pltpu.roll(t, HEAD_DIM // 2, 1))
        return t * cos_ref[...] + swapped * sin_ref[...]

    q = head_norm(p[:, COL_Q:COL_K], qg_ref[...])
    k = head_norm(p[:, COL_K:COL_V], kg_ref[...])
    if rope:
        q = rotate(q)
        k = rotate(k)
    q_ref[...] = (q * ATTN_SCALE).astype(BF16)
    k_ref[...] = k.astype(BF16)
    v_ref[...] = p[:, COL_V:COL_CX].astype(BF16)
    u_ref[...] = p[:, COL_CC:COL_G] * p[:, COL_CX:COL_CB]
    bg_ref[...] = p[:, COL_CB:COL_CC]


def _proj(x2, mods_l, norm_g, w_proj, qg, kg, mavg, cbd, sbd, cos_t, sin_t, *, seq_len, n_seq, fixed_row):
    n_tok = x2.shape[0]
    tm = TOKEN_TILE
    tps = seq_len // tm
    rope = cos_t is not None
    const = lambda i: (0, 0)
    in_specs = [
        pl.BlockSpec((tm, D_MODEL), lambda i: (i, 0)),
        pl.BlockSpec((MODS_ROWS, 6 * D_MODEL), const),
        pl.BlockSpec((1, D_MODEL), const),
        pl.BlockSpec((D_MODEL, COL_G), const),
        pl.BlockSpec((1, NA_WIDTH), const),
        pl.BlockSpec((1, NA_WIDTH), const),
        pl.BlockSpec((NA_WIDTH, NA_WIDTH), const),
        pl.BlockSpec((F_WIDTH, F_WIDTH), const),
        pl.BlockSpec((F_WIDTH, F_WIDTH), const),
    ]
    args = [x2, mods_l, norm_g, w_proj, qg, kg, mavg, cbd, sbd]
    if rope:
        in_specs += [pl.BlockSpec((tm, NA_WIDTH), lambda i: (i % tps, 0))] * 2
        args += [cos_t, sin_t]
    tok = lambda w: pl.BlockSpec((tm, w), lambda i: (i, 0))
    fmap = pl.BlockSpec((tm, F_WIDTH), lambda i: (i % tps, i // tps))
    out_specs = [fmap, fmap, tok(NA_WIDTH), tok(NA_WIDTH), tok(NA_WIDTH), tok(CONV_WIDTH), tok(CONV_WIDTH)]
    out_shape = [
        jax.ShapeDtypeStruct((seq_len, n_seq * F_WIDTH), BF16),
        jax.ShapeDtypeStruct((seq_len, n_seq * F_WIDTH), BF16),
        jax.ShapeDtypeStruct((n_tok, NA_WIDTH), BF16),
        jax.ShapeDtypeStruct((n_tok, NA_WIDTH), BF16),
        jax.ShapeDtypeStruct((n_tok, NA_WIDTH), BF16),
        jax.ShapeDtypeStruct((n_tok, CONV_WIDTH), F32),
        jax.ShapeDtypeStruct((n_tok, CONV_WIDTH), F32),
    ]
    return pl.pallas_call(
        functools.partial(_proj_kernel, tiles_per_seq=tps, fixed_row=fixed_row, rope=rope),
        grid=(n_tok // tm,),
        in_specs=in_specs,
        out_specs=out_specs,
        out_shape=out_shape,
        compiler_params=_cparams(("arbitrary",)),
        name="proj",
    )(*args)


def _fourier_kernel(c_ref, s_ref, a_ref, b_ref, o_ref):
    o = (jnp.dot(c_ref[...], a_ref[...], preferred_element_type=F32)
         - jnp.dot(s_ref[...], b_ref[...], preferred_element_type=F32))
    o_ref[...] = o.astype(BF16)


def _fourier(c_tab, s_tab, a, b):
    seq_len, width = a.shape
    tk = min(seq_len, 256)
    full = lambda i: (0, 0)
    return pl.pallas_call(
        _fourier_kernel,
        grid=(seq_len // tk,),
        in_specs=[
            pl.BlockSpec((tk, seq_len), lambda i: (i, 0)),
            pl.BlockSpec((tk, seq_len), lambda i: (i, 0)),
            pl.BlockSpec((seq_len, width), full, pipeline_mode=pl.Buffered(1)),
            pl.BlockSpec((seq_len, width), full, pipeline_mode=pl.Buffered(1)),
        ],
        out_specs=pl.BlockSpec((tk, width), lambda i: (i, 0)),
        out_shape=jax.ShapeDtypeStruct((seq_len, width), BF16),
        compiler_params=_cparams(("arbitrary",)),
        name="fourier",
    )(c_tab, s_tab, a, b)


def _stack_heads(qg):
    lane_head = lax.broadcasted_iota(I32, qg.shape, 1) // HEAD_DIM
    zero = jnp.zeros_like(qg)
    return jnp.concatenate([jnp.where(lane_head == h, qg, zero) for h in range(HEADS_PER_GROUP)], axis=0)


def _unstack_heads(o, rows):
    lane_head = lax.broadcasted_iota(I32, (rows, o.shape[1]), 1) // HEAD_DIM
    acc = jnp.zeros((rows, o.shape[1]), F32)
    for h in range(HEADS_PER_GROUP):
        acc = acc + jnp.where(lane_head == h, o[h * rows:(h + 1) * rows, :], 0.0)
    return acc


_NT = (((1,), (1,)), ((), ()))


def _attn_kernel(q_ref, k_ref, v_ref, kc_ref, vc_ref, bias_ref, o_ref, *, rows):
    r = pl.program_id(1)
    rs = jnp.clip(r - NA_WIN_R // 2, 0, rows - NA_WIN_R)
    start = pl.multiple_of(rs * GRID_W, GRID_W)
    n_loc = NA_WIN_R * GRID_W
    kwin = k_ref[pl.ds(start, n_loc), :]
    vwin = v_ref[pl.ds(start, n_loc), :]
    q = q_ref[...]
    outs = []
    for g in range(NA_HEADS // HEADS_PER_GROUP):
        sl = slice(g * V7X_MXU_DIM, (g + 1) * V7X_MXU_DIM)
        qs = _stack_heads(q[:, sl])
        s_loc = lax.dot_general(qs, kwin[:, sl], _NT, preferred_element_type=F32)
        bias = bias_ref[g * HEADS_PER_GROUP:(g + 1) * HEADS_PER_GROUP].reshape(HEADS_PER_GROUP * GRID_W, n_loc)
        s_loc = s_loc + bias
        s_ctx = lax.dot_general(qs, kc_ref[:, sl], _NT, preferred_element_type=F32)
        m = jnp.maximum(jnp.max(s_loc, axis=-1, keepdims=True), jnp.max(s_ctx, axis=-1, keepdims=True))
        p_loc = jnp.exp(s_loc - m)
        p_ctx = jnp.exp(s_ctx - m)
        denom = jnp.sum(p_loc, axis=-1, keepdims=True) + jnp.sum(p_ctx, axis=-1, keepdims=True)
        o = (jnp.dot(p_loc.astype(BF16), vwin[:, sl], preferred_element_type=F32)
             + jnp.dot(p_ctx.astype(BF16), vc_ref[:, sl], preferred_element_type=F32))
        outs.append(_unstack_heads(o / denom, GRID_W))
    o_ref[...] = jnp.concatenate(outs, axis=1).astype(BF16)


def _attn(q, k, v, kc, vc, bias_tab, *, n_seq, seq_len, ctx_len):
    rows = seq_len // GRID_W

    def bias_map(b, r):
        rs = jnp.clip(r - NA_WIN_R // 2, 0, rows - NA_WIN_R)
        return (rs - r + NA_WIN_R - 1, 0, 0, 0)

    return pl.pallas_call(
        functools.partial(_attn_kernel, rows=rows),
        grid=(n_seq, rows),
        in_specs=[
            pl.BlockSpec((GRID_W, NA_WIDTH), lambda b, r: (b * rows + r, 0)),
            pl.BlockSpec((seq_len, NA_WIDTH), lambda b, r: (b, 0)),
            pl.BlockSpec((seq_len, NA_WIDTH), lambda b, r: (b, 0)),
            pl.BlockSpec((ctx_len, NA_WIDTH), lambda b, r: (b, 0)),
            pl.BlockSpec((ctx_len, NA_WIDTH), lambda b, r: (b, 0)),
            pl.BlockSpec((None, NA_HEADS, GRID_W, NA_WIN_R * GRID_W), bias_map),
        ],
        out_specs=pl.BlockSpec((GRID_W, NA_WIDTH), lambda b, r: (b * rows + r, 0)),
        out_shape=jax.ShapeDtypeStruct((n_seq * seq_len, NA_WIDTH), BF16),
        compiler_params=_cparams(("arbitrary", "arbitrary")),
        name="attn",
    )(q, k, v, kc, vc, bias_tab)


def _ctx_attn_kernel(q_ref, k_ref, v_ref, o_ref):
    q = q_ref[...]
    n = q.shape[0]
    outs = []
    for g in range(NA_HEADS // HEADS_PER_GROUP):
        sl = slice(g * V7X_MXU_DIM, (g + 1) * V7X_MXU_DIM)
        qs = _stack_heads(q[:, sl])
        s = lax.dot_general(qs, k_ref[:, sl], _NT, preferred_element_type=F32)
        m = jnp.max(s, axis=-1, keepdims=True)
        p = jnp.exp(s - m)
        denom = jnp.sum(p, axis=-1, keepdims=True)
        o = jnp.dot(p.astype(BF16), v_ref[:, sl], preferred_element_type=F32)
        outs.append(_unstack_heads(o / denom, n))
    o_ref[...] = jnp.concatenate(outs, axis=1).astype(BF16)


def _ctx_attn(q, k, v, *, n_seq, ctx_len):
    spec = pl.BlockSpec((ctx_len, NA_WIDTH), lambda b: (b, 0))
    return pl.pallas_call(
        _ctx_attn_kernel,
        grid=(n_seq,),
        in_specs=[spec, spec, spec],
        out_specs=spec,
        out_shape=jax.ShapeDtypeStruct((n_seq * ctx_len, NA_WIDTH), BF16),
        compiler_params=_cparams(("arbitrary",)),
        name="ctx_attn",
    )(q, k, v)


def _merge_kernel(x_ref, mods_ref, n1_ref, n2_ref, f_ref, at_ref, u_ref, up_ref, un_ref, bg_ref, cw_ref,
                  wg_ref, wf_ref, wna_ref, wcv_ref, wo_ref, rwt_ref,
                  xo_ref, h2_ref, lg_ref, *, tiles_per_seq, fixed_row):
    i = pl.program_id(0)
    m = _mod_row(mods_ref, i, tiles_per_seq, fixed_row)
    dm = D_MODEL
    x = x_ref[...]
    h = _norm_mod(x, n1_ref[...], m[:, 0:dm], m[:, dm:2 * dm]).astype(BF16)
    gates = jax.nn.sigmoid(jnp.dot(h, wg_ref[...], preferred_element_type=F32))

    y_f = jnp.dot(f_ref[...], wf_ref[...], preferred_element_type=F32)
    y_na = jnp.dot(at_ref[...], wna_ref[...], preferred_element_type=F32)

    u = u_ref[...]
    t = u.shape[0]
    ti = i % tiles_per_seq
    row = lax.broadcasted_iota(I32, u.shape, 0)
    prev_row = jnp.where(ti == 0, 0.0, up_ref[V7X_SUBLANES - 1:V7X_SUBLANES, :])
    next_row = jnp.where(ti == tiles_per_seq - 1, 0.0, un_ref[0:1, :])
    u_prev = jnp.where(row == 0, prev_row, pltpu.roll(u, 1, 0))
    u_next = jnp.where(row == t - 1, next_row, pltpu.roll(u, t - 1, 0))
    y_cv = bg_ref[...] * (cw_ref[0:1, :] * u_prev + cw_ref[1:2, :] * u + cw_ref[2:3, :] * u_next)
    y_cv = jnp.dot(y_cv.astype(BF16), wcv_ref[...], preferred_element_type=F32)

    merged = gates[:, 0:dm] * y_f + gates[:, dm:2 * dm] * y_na + gates[:, 2 * dm:3 * dm] * y_cv
    mixed = jnp.dot(merged.astype(BF16), wo_ref[...], preferred_element_type=F32)
    x_new = x + m[:, 2 * dm:3 * dm] * mixed
    xo_ref[...] = x_new

    h2 = _norm_mod(x_new, n2_ref[...], m[:, 3 * dm:4 * dm], m[:, 4 * dm:5 * dm])
    h2_ref[...] = h2
    lg_ref[...] = lax.dot_general(rwt_ref[...], h2, _NT, precision=HIGHEST, preferred_element_type=F32)


def _merge(x2, mods_l, n1, n2, f_all, attn, u, bg, conv_w, w_gate, w_f, w_na, w_cv, w_o, rwt,
           *, seq_len, fixed_row):
    n_tok = x2.shape[0]
    tm = TOKEN_TILE
    tps = seq_len // tm
    const = lambda i: (0, 0)
    halo = tm // V7X_SUBLANES
    n_halo = n_tok // V7X_SUBLANES
    in_specs = [
        pl.BlockSpec((tm, D_MODEL), lambda i: (i, 0)),
        pl.BlockSpec((MODS_ROWS, 6 * D_MODEL), const),
        pl.BlockSpec((1, D_MODEL), const),
        pl.BlockSpec((1, D_MODEL), const),
        pl.BlockSpec((tm, F_WIDTH), lambda i: (i % tps, i // tps)),
        pl.BlockSpec((tm, NA_WIDTH), lambda i: (i, 0)),
        pl.BlockSpec((tm, CONV_WIDTH), lambda i: (i, 0)),
        pl.BlockSpec((V7X_SUBLANES, CONV_WIDTH), lambda i: (jnp.maximum(i * halo - 1, 0), 0)),
        pl.BlockSpec((V7X_SUBLANES, CONV_WIDTH), lambda i: (jnp.minimum((i + 1) * halo, n_halo - 1), 0)),
        pl.BlockSpec((tm, CONV_WIDTH), lambda i: (i, 0)),
        pl.BlockSpec((3, CONV_WIDTH), const),
        pl.BlockSpec((D_MODEL, 3 * D_MODEL), const),
        pl.BlockSpec((F_WIDTH, D_MODEL), const),
        pl.BlockSpec((NA_WIDTH, D_MODEL), const),
        pl.BlockSpec((CONV_WIDTH, D_MODEL), const),
        pl.BlockSpec((D_MODEL, D_MODEL), const),
        pl.BlockSpec((N_EXPERTS, D_MODEL), const),
    ]
    out_specs = [
        pl.BlockSpec((tm, D_MODEL), lambda i: (i, 0)),
        pl.BlockSpec((tm, D_MODEL), lambda i: (i, 0)),
        pl.BlockSpec((N_EXPERTS, tm), lambda i: (0, i)),
    ]
    out_shape = [
        jax.ShapeDtypeStruct((n_tok, D_MODEL), F32),
        jax.ShapeDtypeStruct((n_tok, D_MODEL), F32),
        jax.ShapeDtypeStruct((N_EXPERTS, n_tok), F32),
    ]
    return pl.pallas_call(
        functools.partial(_merge_kernel, tiles_per_seq=tps, fixed_row=fixed_row),
        grid=(n_tok // tm,),
        in_specs=in_specs,
        out_specs=out_specs,
        out_shape=out_shape,
        compiler_params=_cparams(("arbitrary",)),
        name="merge",
    )(x2, mods_l, n1, n2, f_all, attn, u, u, u, bg, conv_w, w_gate, w_f, w_na, w_cv, w_o, rwt)


def _first_max(vals):
    best = vals[0]
    idx = jnp.zeros(best.shape, I32)
    for j in range(1, len(vals)):
        better = vals[j] > best
        idx = jnp.where(better, j, idx)
        best = jnp.where(better, vals[j], best)
    return best, idx


def _select(idx, vals):
    out = vals[-1]
    for j in range(len(vals) - 2, -1, -1):
        out = jnp.where(idx == j, vals[j], out)
    return out


def _route_kernel(lg_ref, rb_ref, ids_ref, wts_ref, cnt_ref, tot_ref, run_ref):
    step = pl.program_id(0)

    @pl.when(step == 0)
    def _():
        run_ref[...] = jnp.zeros_like(run_ref)

    s = jax.nn.sigmoid(lg_ref[...])
    sb = s + rb_ref[...]
    t = s.shape[1]
    s_rows = [s[e:e + 1, :] for e in range(N_EXPERTS)]
    b_rows = [sb[e:e + 1, :] for e in range(N_EXPERTS)]
    epg = EXPERTS_PER_GROUP
    gscore = []
    for g in range(N_GROUPS):
        v = b_rows[g * epg:(g + 1) * epg]
        pair = None
        for a in range(epg):
            for b in range(a + 1, epg):
                pair = v[a] + v[b] if pair is None else jnp.maximum(pair, v[a] + v[b])
        gscore.append(pair)
    _, gi = _first_max(gscore)
    bv = [_select(gi, [b_rows[g * epg + j] for g in range(N_GROUPS)]) for j in range(epg)]
    sv = [_select(gi, [s_rows[g * epg + j] for g in range(N_GROUPS)]) for j in range(epg)]
    _, i1 = _first_max(bv)
    _, i2 = _first_max([jnp.where(i1 == j, -jnp.inf, bv[j]) for j in range(epg)])
    s1 = _select(i1, sv)
    s2 = _select(i2, sv)
    tot = s1 + s2
    e1 = gi * epg + i1
    e2 = gi * epg + i2

    eid = lax.broadcasted_iota(I32, (N_EXPERTS, t), 0)
    hit1 = eid == e1
    hit2 = eid == e2
    onehot = jnp.where(hit1 | hit2, 1.0, 0.0)
    before = (lax.broadcasted_iota(I32, (t, t), 0) < lax.broadcasted_iota(I32, (t, t), 1))
    prefix = jnp.dot(onehot.astype(BF16), jnp.where(before, 1.0, 0.0).astype(BF16),
                     preferred_element_type=F32)
    r1 = jnp.sum(jnp.where(hit1, prefix, 0.0), axis=0, keepdims=True)
    r2 = jnp.sum(jnp.where(hit2, prefix, 0.0), axis=0, keepdims=True)
    grp = float(SLOT_GROUP)
    cnt = jnp.sum(onehot, axis=1, keepdims=True)
    cnt = jnp.floor((cnt + (grp - 1.0)) / grp) * grp
    run = run_ref[...] + cnt
    run_ref[...] = run
    cnt_ref[...] = jnp.broadcast_to(cnt, cnt_ref.shape)
    tot_ref[...] = jnp.broadcast_to(run, tot_ref.shape)

    zi = jnp.zeros((V7X_SUBLANES - 4, t), I32)
    ids_ref[...] = jnp.concatenate([e1, e2, r1.astype(I32), r2.astype(I32), zi], axis=0)
    zf = jnp.zeros((V7X_SUBLANES - 2, t), F32)
    wts_ref[...] = jnp.concatenate([s1 / tot, s2 / tot, zf], axis=0)


def _route(logits_t, router_b):
    n_tok = logits_t.shape[1]
    tr = ROUTE_TILE
    return pl.pallas_call(
        _route_kernel,
        grid=(n_tok // tr,),
        in_specs=[
            pl.BlockSpec((N_EXPERTS, tr), lambda i: (0, i)),
            pl.BlockSpec((N_EXPERTS, 1), lambda i: (0, 0)),
        ],
        out_specs=[
            pl.BlockSpec((V7X_SUBLANES, tr), lambda i: (0, i)),
            pl.BlockSpec((V7X_SUBLANES, tr), lambda i: (0, i)),
            pl.BlockSpec((None, N_EXPERTS, V7X_LANES), lambda i: (i, 0, 0)),
            pl.BlockSpec((N_EXPERTS, V7X_LANES), lambda i: (0, 0)),
        ],
        out_shape=[
            jax.ShapeDtypeStruct((V7X_SUBLANES, n_tok), I32),
            jax.ShapeDtypeStruct((V7X_SUBLANES, n_tok), F32),
            jax.ShapeDtypeStruct((n_tok // tr, N_EXPERTS, V7X_LANES), F32),
            jax.ShapeDtypeStruct((N_EXPERTS, V7X_LANES), F32),
        ],
        scratch_shapes=[pltpu.VMEM((N_EXPERTS, 1), F32)],
        compiler_params=_cparams(("arbitrary",)),
        name="route",
    )(logits_t, router_b.reshape(N_EXPERTS, 1))


def _lane_table(vals, width):
    lane = lax.broadcasted_iota(I32, (1, width), 1)
    out = jnp.zeros((1, width), F32)
    for e, v in enumerate(vals):
        out = jnp.where(lane == e, v, out)
    return out


def _slots_kernel(ids_ref, cnt_ref, tot_ref, loc_ref, gmap_ref, blk_ref, off_ref):
    step = pl.program_id(0)

    @pl.when(step == 0)
    def _():
        off_ref[...] = jnp.zeros_like(off_ref)

    blk = float(EXPERT_BLOCK)
    grp = float(SLOT_GROUP)
    cnt = cnt_ref[...][:, 0:1]
    tot = tot_ref[...][:, 0:1]
    off = off_ref[...]
    region = jnp.floor((tot + (blk - 1.0)) / blk) * blk
    starts, ends, local = [], [], []
    run = jnp.zeros((1, 1), F32)
    lrun = jnp.zeros((1, 1), F32)
    for e in range(N_EXPERTS):
        starts.append(run)
        run = run + region[e:e + 1, :]
        ends.append(run)
        local.append(lrun)
        lrun = lrun + cnt[e:e + 1, :]

    ids = ids_ref[...]
    e1, e2 = ids[0:1, :], ids[1:2, :]
    t = ids.shape[1]
    l1 = jnp.zeros((1, t), F32)
    l2 = jnp.zeros((1, t), F32)
    for e in range(N_EXPERTS):
        l1 = jnp.where(e1 == e, local[e], l1)
        l2 = jnp.where(e2 == e, local[e], l2)
    zi = jnp.zeros((V7X_SUBLANES - 2, t), I32)
    loc_ref[...] = jnp.concatenate([l1.astype(I32) + ids[2:3, :], l2.astype(I32) + ids[3:4, :], zi], axis=0)

    wg = gmap_ref.shape[1]
    first = lax.broadcasted_iota(I32, (1, wg), 1).astype(F32) * grp
    dest = jnp.zeros((1, wg), F32)
    for e in range(N_EXPERTS):
        inside = (first >= local[e]) & (first < local[e] + cnt[e:e + 1, :])
        dest = jnp.where(inside, starts[e] + off[e:e + 1, :] + (first - local[e]), dest)
    n_groups = jnp.broadcast_to(lrun / grp, (1, wg))
    zg = jnp.zeros((V7X_SUBLANES - 2, wg), I32)
    gmap_ref[...] = jnp.concatenate([(dest / grp).astype(I32), n_groups.astype(I32), zg], axis=0)
    off_ref[...] = off + cnt

    w = blk_ref.shape[1]
    first_row = lax.broadcasted_iota(I32, (1, w), 1).astype(F32) * blk
    owner = jnp.zeros((1, w), F32)
    for e in range(N_EXPERTS):
        owner = owner + jnp.where(first_row >= ends[e], 1.0, 0.0)
    owner = jnp.minimum(owner, float(N_EXPERTS - 1))
    used = jnp.broadcast_to(ends[-1] / blk, (1, w))
    pad_first = _lane_table([(starts[e] + tot[e:e + 1, :]) / grp for e in range(N_EXPERTS)], w)
    pad_count = _lane_table([(region[e:e + 1, :] - tot[e:e + 1, :]) / grp for e in range(N_EXPERTS)], w)
    zb = jnp.zeros((V7X_SUBLANES - 4, w), I32)
    blk_ref[...] = jnp.concatenate([owner.astype(I32), used.astype(I32), pad_first.astype(I32),
                                    pad_count.astype(I32), zb], axis=0)


def _slots(ids, cnt, tot, n_blocks):
    n_tok = ids.shape[1]
    tr = ROUTE_TILE
    wblk = -(-n_blocks // V7X_LANES) * V7X_LANES
    return pl.pallas_call(
        _slots_kernel,
        grid=(n_tok // tr,),
        in_specs=[
            pl.BlockSpec((V7X_SUBLANES, tr), lambda i: (0, i)),
            pl.BlockSpec((None, N_EXPERTS, V7X_LANES), lambda i: (i, 0, 0)),
            pl.BlockSpec((N_EXPERTS, V7X_LANES), lambda i: (0, 0)),
        ],
        out_specs=[
            pl.BlockSpec((V7X_SUBLANES, tr), lambda i: (0, i)),
            pl.BlockSpec((None, V7X_SUBLANES, SORT_GROUPS_PAD), lambda i: (i, 0, 0)),
            pl.BlockSpec((V7X_SUBLANES, wblk), lambda i: (0, 0)),
        ],
        out_shape=[
            jax.ShapeDtypeStruct((V7X_SUBLANES, n_tok), I32),
            jax.ShapeDtypeStruct((n_tok // tr, V7X_SUBLANES, SORT_GROUPS_PAD), I32),
            jax.ShapeDtypeStruct((V7X_SUBLANES, wblk), I32),
        ],
        scratch_shapes=[pltpu.VMEM((N_EXPERTS, 1), F32)],
        compiler_params=_cparams(("arbitrary",)),
        name="slots",
    )(ids, cnt, tot)


def _group_rows(group):
    if isinstance(group, int):
        return pl.ds(group * SLOT_GROUP, SLOT_GROUP)
    return pl.ds(pl.multiple_of(group * SLOT_GROUP, SLOT_GROUP), SLOT_GROUP)


def _group_copy(src_ref, src_group, dst_ref, dst_group, sem):
    return pltpu.make_async_copy(src_ref.at[_group_rows(src_group)], dst_ref.at[_group_rows(dst_group)], sem)


def _dispatch_kernel(gmap_ref, blk_ref, loc_ref, h_ref, xb_ref, sorted_ref, zero_ref, sem):
    step = pl.program_id(0)
    loc = loc_ref[...]
    slot = lax.broadcasted_iota(I32, (SORT_ROWS, loc.shape[1]), 0)
    perm = jnp.where(slot == loc[0:1, :], 1.0, jnp.where(slot == loc[1:2, :], 1.0, 0.0)).astype(BF16)
    sorted_ref[...] = jnp.dot(perm, h_ref[...].astype(BF16), preferred_element_type=F32)
    n_groups = gmap_ref[1, 0]

    def start(g, c):
        _group_copy(sorted_ref, g, xb_ref, gmap_ref[0, g], sem.at[0]).start()
        return c

    def wait(g, c):
        _group_copy(sorted_ref, g, xb_ref, gmap_ref[0, g], sem.at[0]).wait()
        return c

    lax.fori_loop(0, n_groups, start, 0)

    @pl.when(step == pl.num_programs(0) - 1)
    def _():
        zero_ref[...] = jnp.zeros_like(zero_ref)
        for e in range(N_EXPERTS):
            first = blk_ref[2, e]

            def zstart(g, c, first=first):
                _group_copy(zero_ref, 0, xb_ref, first + g, sem.at[1]).start()
                return c

            def zwait(g, c, first=first):
                _group_copy(zero_ref, 0, xb_ref, first + g, sem.at[1]).wait()
                return c

            lax.fori_loop(0, blk_ref[3, e], zstart, 0)
            lax.fori_loop(0, blk_ref[3, e], zwait, 0)

        def block_copy(b):
            rows = pl.ds(pl.multiple_of(b * EXPERT_BLOCK, EXPERT_BLOCK), EXPERT_BLOCK)
            return pltpu.make_async_copy(zero_ref, xb_ref.at[rows], sem.at[1])

        def bstart(b, c):
            block_copy(b).start()
            return c

        def bwait(b, c):
            block_copy(b).wait()
            return c

        n_blocks = xb_ref.shape[0] // EXPERT_BLOCK
        lax.fori_loop(blk_ref[1, 0], n_blocks, bstart, 0)
        lax.fori_loop(blk_ref[1, 0], n_blocks, bwait, 0)

    lax.fori_loop(0, n_groups, wait, 0)


def _dispatch(gmap, blk, loc, h2, n_slots):
    n_tok = h2.shape[0]
    tr = ROUTE_TILE
    return pl.pallas_call(
        _dispatch_kernel,
        grid=(n_tok // tr,),
        in_specs=[
            pl.BlockSpec((None, V7X_SUBLANES, SORT_GROUPS_PAD), lambda i: (i, 0, 0), memory_space=pltpu.SMEM),
            pl.BlockSpec(blk.shape, lambda i: (0, 0), memory_space=pltpu.SMEM),
            pl.BlockSpec((V7X_SUBLANES, tr), lambda i: (0, i)),
            pl.BlockSpec((tr, D_MODEL), lambda i: (i, 0)),
        ],
        out_specs=pl.BlockSpec(memory_space=pl.ANY),
        out_shape=jax.ShapeDtypeStruct((n_slots, D_MODEL), F32),
        scratch_shapes=[pltpu.VMEM((SORT_ROWS, D_MODEL), F32), pltpu.VMEM((EXPERT_BLOCK, D_MODEL), F32),
                        pltpu.SemaphoreType.DMA((2,))],
        compiler_params=_cparams(("arbitrary",)),
        name="dispatch",
    )(gmap, blk, loc, h2)


def _experts_kernel(blk_ref, used_ref, x_ref, w1_ref, w3_ref, w2_ref, y_ref, w1b, w3b, w2b):
    i = pl.program_id(0)
    prev = blk_ref[jnp.maximum(i - 1, 0)]

    @pl.when((i == 0) | (blk_ref[i] != prev))
    def _():
        w1b[...] = w1_ref[...].astype(BF16)
        w3b[...] = w3_ref[...].astype(BF16)
        w2b[...] = w2_ref[...].astype(BF16)

    @pl.when(i < used_ref[0])
    def _():
        x = x_ref[...].astype(BF16)
        a = jnp.dot(x, w1b[...], preferred_element_type=F32)
        b = jnp.dot(x, w3b[...], preferred_element_type=F32)
        hid = (a * jax.nn.sigmoid(a) * b).astype(BF16)
        y_ref[...] = jnp.dot(hid, w2b[...], preferred_element_type=F32)

    @pl.when(i >= used_ref[0])
    def _():
        y_ref[...] = jnp.zeros_like(y_ref)


def _experts(blk_e, used, xb, w1, w3, w2):
    n_slots = xb.shape[0]
    bm = EXPERT_BLOCK
    row_map = lambda i, be, nu: (jnp.minimum(i, nu[0] - 1), 0)
    grid_spec = pltpu.PrefetchScalarGridSpec(
        num_scalar_prefetch=2,
        grid=(n_slots // bm,),
        in_specs=[
            pl.BlockSpec((bm, D_MODEL), row_map),
            pl.BlockSpec((None, D_MODEL, D_EXPERT), lambda i, be, nu: (be[i], 0, 0)),
            pl.BlockSpec((None, D_MODEL, D_EXPERT), lambda i, be, nu: (be[i], 0, 0)),
            pl.BlockSpec((None, D_EXPERT, D_MODEL), lambda i, be, nu: (be[i], 0, 0)),
        ],
        out_specs=pl.BlockSpec((bm, D_MODEL), lambda i, be, nu: (i, 0)),
        scratch_shapes=[pltpu.VMEM((D_MODEL, D_EXPERT), BF16), pltpu.VMEM((D_MODEL, D_EXPERT), BF16),
                        pltpu.VMEM((D_EXPERT, D_MODEL), BF16)],
    )
    return pl.pallas_call(
        _experts_kernel,
        grid_spec=grid_spec,
        out_shape=jax.ShapeDtypeStruct((n_slots, D_MODEL), F32),
        compiler_params=_cparams(("arbitrary",)),
        name="experts",
    )(blk_e, used, xb, w1, w3, w2)


_TN = (((0,), (0,)), ((), ()))


def _combine_kernel(gmap_ref, loc_ref, wts_ref, x_ref, mods_ref, yb_ref, o_ref, ys_ref, sem,
                    *, tiles_per_seq, fixed_row):
    step = pl.program_id(0)

    @pl.when(step == 0)
    def _():
        ys_ref[...] = jnp.zeros_like(ys_ref)

    n_groups = gmap_ref[1, 0]

    def start(g, c):
        _group_copy(yb_ref, gmap_ref[0, g], ys_ref, g, sem).start()
        return c

    def wait(g, c):
        _group_copy(yb_ref, gmap_ref[0, g], ys_ref, g, sem).wait()
        return c

    lax.fori_loop(0, n_groups, start, 0)
    loc = loc_ref[...]
    wts = wts_ref[...]
    slot = lax.broadcasted_iota(I32, (SORT_ROWS, loc.shape[1]), 0)
    perm = jnp.where(slot == loc[0:1, :], wts[0:1, :], jnp.where(slot == loc[1:2, :], wts[1:2, :], 0.0))
    lax.fori_loop(0, n_groups, wait, 0)
    y = lax.dot_general(perm.astype(BF16), ys_ref[...].astype(BF16), _TN, preferred_element_type=F32)
    m = _mod_row(mods_ref, step, tiles_per_seq, fixed_row)
    o_ref[...] = x_ref[...] + m[:, 5 * D_MODEL:6 * D_MODEL] * y


def _combine(gmap, loc, wts, x_new, mods_l, yb, *, seq_len, fixed_row, tile_offset):
    n_tok = x_new.shape[0]
    tr = ROUTE_TILE
    tps = max(seq_len // tr, 1)
    return pl.pallas_call(
        functools.partial(_combine_kernel, tiles_per_seq=tps, fixed_row=fixed_row),
        grid=(n_tok // tr,),
        in_specs=[
            pl.BlockSpec((None, V7X_SUBLANES, SORT_GROUPS_PAD), lambda i: (i + tile_offset, 0, 0),
                         memory_space=pltpu.SMEM),
            pl.BlockSpec((V7X_SUBLANES, tr), lambda i: (0, i + tile_offset)),
            pl.BlockSpec((V7X_SUBLANES, tr), lambda i: (0, i + tile_offset)),
            pl.BlockSpec((tr, D_MODEL), lambda i: (i, 0)),
            pl.BlockSpec((MODS_ROWS, 6 * D_MODEL), lambda i: (0, 0)),
            pl.BlockSpec(memory_space=pl.ANY),
        ],
        out_specs=pl.BlockSpec((tr, D_MODEL), lambda i: (i, 0)),
        out_shape=jax.ShapeDtypeStruct((n_tok, D_MODEL), F32),
        scratch_shapes=[pltpu.VMEM((SORT_ROWS, D_MODEL), F32), pltpu.SemaphoreType.DMA],
        compiler_params=_cparams(("arbitrary",)),
        name="combine",
    )(gmap, loc, wts, x_new, mods_l, yb)


def _channel_dft_tables():
    j = np.arange(F_GDIM)
    ang = 2.0 * np.pi * ((j[:, None] * j[None, :]) % F_GDIM) / F_GDIM
    eye = np.eye(F_GROUPS)
    return (jnp.asarray(np.kron(eye, np.cos(ang)), F32).astype(BF16),
            jnp.asarray(np.kron(eye, np.sin(ang)), F32).astype(BF16))


def _position_dft_tables(seq_len):
    scale = 1.0 / math.sqrt(seq_len * F_GDIM)
    k = np.arange(seq_len, dtype=np.int64)
    if seq_len <= 256:
        ang = 2.0 * np.pi * ((k[:, None] * k[None, :]) % seq_len) / seq_len
        return (jnp.asarray(np.cos(ang) * scale, F32).astype(BF16),
                jnp.asarray(np.sin(ang) * scale, F32).astype(BF16))
    hi = seq_len // 64
    t1 = np.arange(hi, dtype=np.int64)
    t0 = np.arange(64, dtype=np.int64)
    ang_a = 2.0 * np.pi * ((k[:, None] * t1[None, :] * 64) % seq_len) / seq_len
    ang_b = 2.0 * np.pi * ((k[:, None] * t0[None, :]) % seq_len) / seq_len
    ca = jnp.asarray(np.cos(ang_a) * scale, F32)[:, :, None]
    sa = jnp.asarray(np.sin(ang_a) * scale, F32)[:, :, None]
    cb = jnp.asarray(np.cos(ang_b), F32)[:, None, :]
    sb = jnp.asarray(np.sin(ang_b), F32)[:, None, :]
    c = (ca * cb - sa * sb).reshape(seq_len, seq_len).astype(BF16)
    s = (sa * cb + ca * sb).reshape(seq_len, seq_len).astype(BF16)
    return c, s


def _rope_tables(seq_len):
    t = np.arange(seq_len)
    row = (t // GRID_W).astype(np.float64)
    col = (t % GRID_W).astype(np.float64)
    inv = np.power(ROPE_BASE, -np.arange(ROPE_PER_AXIS, dtype=np.float64) / ROPE_PER_AXIS)
    ang = np.concatenate([row[:, None] * inv, col[:, None] * inv], axis=-1)
    cos = np.cos(ang)
    sin = np.sin(ang)
    cos_h = np.concatenate([cos, cos], axis=-1)
    sin_h = np.concatenate([-sin, sin], axis=-1)
    return (jnp.asarray(np.tile(cos_h, (1, NA_HEADS)), F32), jnp.asarray(np.tile(sin_h, (1, NA_HEADS)), F32))


def _bias_table(rpb_l):
    col = np.arange(GRID_W)
    col_start = np.clip(col - NA_WIN_C // 2, 0, GRID_W - NA_WIN_C)
    col_mask = (col[None, :] >= col_start[:, None]) & (col[None, :] < col_start[:, None] + NA_WIN_C)
    dc = np.clip(col[None, :] - col[:, None] + (NA_WIN_C - 1), 0, 2 * NA_WIN_C - 2)
    n_dc = 2 * NA_WIN_C - 1
    pick = (dc.reshape(-1)[None, :] == np.arange(n_dc)[:, None]).astype(np.float32)
    e = jnp.dot(rpb_l.reshape(-1, n_dc), jnp.asarray(pick), precision=HIGHEST)
    e = e.reshape(NA_HEADS, 2 * NA_WIN_R - 1, GRID_W, GRID_W)
    e = jnp.where(jnp.asarray(col_mask)[None, None], e, NEG_BIG)
    b = jnp.stack([e[:, o:o + NA_WIN_R] for o in range(NA_WIN_R)], axis=0)
    b = b.transpose(0, 1, 3, 2, 4)
    return b.reshape(NA_WIN_R, NA_HEADS, GRID_W, NA_WIN_R * GRID_W)


def _moe(h2, logits_t, w1, w3, w2, router_b):
    n_tok = h2.shape[0]
    n_tiles = n_tok // ROUTE_TILE
    max_rows = 2 * n_tok + N_EXPERTS * n_tiles * (SLOT_GROUP - 1) + N_EXPERTS * (EXPERT_BLOCK - 1)
    n_blocks = -(-max_rows // EXPERT_BLOCK)
    n_slots = n_blocks * EXPERT_BLOCK
    ids, wts, cnt, tot = _route(logits_t, router_b)
    loc, gmap, blk = _slots(ids, cnt, tot, n_blocks)
    xb = _dispatch(gmap, blk, loc, h2, n_slots)
    yb = _experts(blk[0, :n_blocks], blk[1, 0:1], xb, w1, w3, w2)
    return yb, gmap, loc, wts


def kernel(x, c, ctx, c_ctx, ada_w, ada_b, norm1_g, w_in, qn_g, kn_g, rpb, conv_w, w_f, w_na, w_cv, w_o,
           norm2_g, router_w, router_b, w1, w3, w2):
    bsz, seq_len, d = x.shape
    ctx_len = ctx.shape[1]
    n_lat = bsz * seq_len
    n_ctx = bsz * ctx_len
    ctx_row = bsz

    c8 = jnp.concatenate([c, c_ctx[None, :], jnp.zeros((MODS_ROWS - bsz - 1, d), F32)], axis=0)
    mods = _mods(c8, ada_w, ada_b)

    cbd, sbd = _channel_dft_tables()
    c_lat, s_lat = _position_dft_tables(seq_len)
    c_ctx_t, s_ctx_t = _position_dft_tables(ctx_len)
    cos_t, sin_t = _rope_tables(seq_len)
    mavg = jnp.asarray(np.kron(np.eye(NA_HEADS), np.full((HEAD_DIM, HEAD_DIM), 1.0 / HEAD_DIM)), F32).astype(BF16)
    rwt = router_w.T

    xl = x.reshape(n_lat, d)
    xc = ctx.reshape(n_ctx, d)
    for l in range(DEPTH):
        last = l == DEPTH - 1
        w_proj = w_in[l][:, :COL_G].astype(BF16)
        w_gate = w_in[l][:, COL_G:].astype(BF16)
        wf, wna, wcv, wo = (w_f[l].astype(BF16), w_na[l].astype(BF16), w_cv[l].astype(BF16), w_o[l].astype(BF16))
        n1 = norm1_g[l].reshape(1, d)
        n2 = norm2_g[l].reshape(1, d)
        qg = jnp.tile(qn_g[l], NA_HEADS).reshape(1, NA_WIDTH)
        kg = jnp.tile(kn_g[l], NA_HEADS).reshape(1, NA_WIDTH)
        bias_tab = _bias_table(rpb[l])
        mods_l = mods[l]

        a_c, b_c, q_c, k_c, v_c, u_c, bg_c = _proj(
            xc, mods_l, n1, w_proj, qg, kg, mavg, cbd, sbd, None, None,
            seq_len=ctx_len, n_seq=bsz, fixed_row=ctx_row)
        a_l, b_l, q_l, k_l, v_l, u_l, bg_l = _proj(
            xl, mods_l, n1, w_proj, qg, kg, mavg, cbd, sbd, cos_t, sin_t,
            seq_len=seq_len, n_seq=bsz, fixed_row=None)

        f_l = _fourier(c_lat, s_lat, a_l, b_l)
        attn_l = _attn(q_l, k_l, v_l, k_c, v_c, bias_tab, n_seq=bsz, seq_len=seq_len, ctx_len=ctx_len)
        xl_new, h2_l, lg_l = _merge(xl, mods_l, n1, n2, f_l, attn_l, u_l, bg_l, conv_w[l], w_gate,
                                    wf, wna, wcv, wo, rwt, seq_len=seq_len, fixed_row=None)
        if last:
            yb, gmap, loc, wts = _moe(h2_l, lg_l, w1[l], w3[l], w2[l], router_b)
            xl = _combine(gmap, loc, wts, xl_new, mods_l, yb, seq_len=seq_len, fixed_row=None, tile_offset=0)
        else:
            f_c = _fourier(c_ctx_t, s_ctx_t, a_c, b_c)
            attn_c = _ctx_attn(q_c, k_c, v_c, n_seq=bsz, ctx_len=ctx_len)
            xc_new, h2_c, lg_c = _merge(xc, mods_l, n1, n2, f_c, attn_c, u_c, bg_c, conv_w[l], w_gate,
                                        wf, wna, wcv, wo, rwt, seq_len=ctx_len, fixed_row=ctx_row)
            h2 = jnp.concatenate([h2_l, h2_c], axis=0)
            lg = jnp.concatenate([lg_l, lg_c], axis=1)
            yb, gmap, loc, wts = _moe(h2, lg, w1[l], w3[l], w2[l], router_b)
            xl = _combine(gmap, loc, wts, xl_new, mods_l, yb, seq_len=seq_len, fixed_row=None, tile_offset=0)
            xc = _combine(gmap, loc, wts, xc_new, mods_l, yb, seq_len=ctx_len, fixed_row=ctx_row,
                          tile_offset=n_lat // ROUTE_TILE)
    return xl.reshape(bsz, seq_len, d)
```

```python
import functools
import math

import numpy as np
import jax
import jax.numpy as jnp
from jax import lax
from jax.experimental import pallas as pl
from jax.experimental.pallas import tpu as pltpu

F32 = jnp.float32
BF16 = jnp.bfloat16
I32 = jnp.int32
HIGHEST = lax.Precision.HIGHEST

D_MODEL = 1024
DEPTH = 2
GRID_W = 64
EPS = 1e-6
F_GROUPS = 4
F_GDIM = 64
F_WIDTH = 256
NA_HEADS = 8
HEAD_DIM = 64
NA_WIDTH = 512
NA_WIN_R = 8
NA_WIN_C = 16
ATTN_SCALE = HEAD_DIM ** -0.5
ROPE_BASE = 10000.0
ROPE_PER_AXIS = HEAD_DIM // 4
CONV_WIDTH = 256
COL_Q = 256
COL_K = 768
COL_V = 1280
COL_CX = 1792
COL_CB = 2048
COL_CC = 2304
COL_G = 2560
N_EXPERTS = 16
N_GROUPS = 4
EXPERTS_PER_GROUP = 4
D_EXPERT = 512

V7X_LANES = 128
V7X_SUBLANES = 8
V7X_MXU_DIM = 256

TOKEN_TILE = 256
ROUTE_TILE = 512
EXPERT_BLOCK = 256
SLOT_GROUP = 2 * V7X_SUBLANES
SORT_ROWS = -(-(2 * ROUTE_TILE + N_EXPERTS * (SLOT_GROUP - 1)) // V7X_LANES) * V7X_LANES
SORT_GROUPS_PAD = -(-(SORT_ROWS // SLOT_GROUP) // V7X_LANES) * V7X_LANES
HEADS_PER_GROUP = V7X_MXU_DIM // HEAD_DIM
ATTN_ROWS_PER_STEP = 4
NEG_BIG = -1e30
MODS_ROWS = 8
VMEM_LIMIT = 48 * 1024 * 1024


def _cparams(sem):
    return pltpu.CompilerParams(dimension_semantics=sem, vmem_limit_bytes=VMEM_LIMIT)


def _mods_kernel(c_ref, w_ref, b_ref, o_ref):
    c = c_ref[...]
    sc = c * jax.nn.sigmoid(c)
    o_ref[...] = jnp.dot(sc, w_ref[...], precision=HIGHEST, preferred_element_type=F32) + b_ref[...]


def _mods(c8, ada_w, ada_b):
    nb = 1536
    return pl.pallas_call(
        _mods_kernel,
        grid=(DEPTH, 6 * D_MODEL // nb),
        in_specs=[
            pl.BlockSpec((MODS_ROWS, D_MODEL), lambda l, j: (0, 0)),
            pl.BlockSpec((None, D_MODEL, nb), lambda l, j: (l, 0, j)),
            pl.BlockSpec((None, 1, nb), lambda l, j: (l, 0, j)),
        ],
        out_specs=pl.BlockSpec((None, MODS_ROWS, nb), lambda l, j: (l, 0, j)),
        out_shape=jax.ShapeDtypeStruct((DEPTH, MODS_ROWS, 6 * D_MODEL), F32),
        compiler_params=_cparams(("arbitrary", "arbitrary")),
        name="mods",
    )(c8, ada_w, ada_b.reshape(DEPTH, 1, 6 * D_MODEL))


def _norm_mod(x, g, shift, scale):
    ms = jnp.mean(x * x, axis=-1, keepdims=True)
    return (x * lax.rsqrt(ms + EPS) * g) * (1.0 + scale) + shift


def _mod_row(mods_ref, tile, tiles_per_seq, fixed_row):
    row = fixed_row if fixed_row is not None else tile // tiles_per_seq
    return mods_ref[pl.ds(row, 1), :]


def _proj_kernel(*refs, tiles_per_seq, fixed_row, rope):
    if rope:
        (x_ref, mods_ref, g_ref, w_ref, qg_ref, kg_ref, mavg_ref, cbd_ref, sbd_ref, cos_ref, sin_ref,
         a_ref, b_ref, q_ref, k_ref, v_ref, u_ref, bg_ref) = refs
    else:
        (x_ref, mods_ref, g_ref, w_ref, qg_ref, kg_ref, mavg_ref, cbd_ref, sbd_ref,
         a_ref, b_ref, q_ref, k_ref, v_ref, u_ref, bg_ref) = refs
    m = _mod_row(mods_ref, pl.program_id(0), tiles_per_seq, fixed_row)
    h = _norm_mod(x_ref[...], g_ref[...], m[:, 0:D_MODEL], m[:, D_MODEL:2 * D_MODEL])
    p = jnp.dot(h.astype(BF16), w_ref[...], preferred_element_type=F32)

    uf = p[:, 0:COL_Q].astype(BF16)
    a_ref[...] = jnp.dot(uf, cbd_ref[...], preferred_element_type=F32).astype(BF16)
    b_ref[...] = jnp.dot(uf, sbd_ref[...], preferred_element_type=F32).astype(BF16)

    def head_norm(t, g):
        ms = jnp.dot((t * t).astype(BF16), mavg_ref[...], preferred_element_type=F32)
        return t * lax.rsqrt(ms + EPS) * g

    def rotate(t):
        n = t.shape[-1]
        lane = lax.broadcasted_iota(I32, t.shape, 1)
        first_half = (lane % HEAD_DIM) < (HEAD_DIM // 2)
        swapped = jnp.where(first_half, pltpu.roll(t, n - HEAD_DIM // 2, 1), pltpu.roll(t, HEAD_DIM // 2, 1))
        return t * cos_ref[...] + swapped * sin_ref[...]

    q = head_norm(p[:, COL_Q:COL_K], qg_ref[...])
    k = head_norm(p[:, COL_K:COL_V], kg_ref[...])
    if rope:
        q = rotate(q)
        k = rotate(k)
    q_ref[...] = (q * ATTN_SCALE).astype(BF16)
    k_ref[...] = k.astype(BF16)
    v_ref[...] = p[:, COL_V:COL_CX].astype(BF16)
    u_ref[...] = p[:, COL_CC:COL_G] * p[:, COL_CX:COL_CB]
    bg_ref[...] = p[:, COL_CB:COL_CC]


def _proj(x2, mods_l, norm_g, w_proj, qg, kg, mavg, cbd, sbd, cos_t, sin_t, *, seq_len, n_seq, fixed_row):
    n_tok = x2.shape[0]
    tm = TOKEN_TILE
    tps = seq_len // tm
    rope = cos_t is not None
    const = lambda i: (0, 0)
    in_specs = [
        pl.BlockSpec((tm, D_MODEL), lambda i: (i, 0)),
        pl.BlockSpec((MODS_ROWS, 6 * D_MODEL), const),
        pl.BlockSpec((1, D_MODEL), const),
        pl.BlockSpec((D_MODEL, COL_G), const),
        pl.BlockSpec((1, NA_WIDTH), const),
        pl.BlockSpec((1, NA_WIDTH), const),
        pl.BlockSpec((NA_WIDTH, NA_WIDTH), const),
        pl.BlockSpec((F_WIDTH, F_WIDTH), const),
        pl.BlockSpec((F_WIDTH, F_WIDTH), const),
    ]
    args = [x2, mods_l, norm_g, w_proj, qg, kg, mavg, cbd, sbd]
    if rope:
        in_specs += [pl.BlockSpec((tm, NA_WIDTH), lambda i: (i % tps, 0))] * 2
        args += [cos_t, sin_t]
    tok = lambda w: pl.BlockSpec((tm, w), lambda i: (i, 0))
    fmap = pl.BlockSpec((tm, F_WIDTH), lambda i: (i % tps, i // tps))
    out_specs = [fmap, fmap, tok(NA_WIDTH), tok(NA_WIDTH), tok(NA_WIDTH), tok(CONV_WIDTH), tok(CONV_WIDTH)]
    out_shape = [
        jax.ShapeDtypeStruct((seq_len, n_seq * F_WIDTH), BF16),
        jax.ShapeDtypeStruct((seq_len, n_seq * F_WIDTH), BF16),
        jax.ShapeDtypeStruct((n_tok, NA_WIDTH), BF16),
        jax.ShapeDtypeStruct((n_tok, NA_WIDTH), BF16),
        jax.ShapeDtypeStruct((n_tok, NA_WIDTH), BF16),
        jax.ShapeDtypeStruct((n_tok, CONV_WIDTH), F32),
        jax.ShapeDtypeStruct((n_tok, CONV_WIDTH), F32),
    ]
    return pl.pallas_call(
        functools.partial(_proj_kernel, tiles_per_seq=tps, fixed_row=fixed_row, rope=rope),
        grid=(n_tok // tm,),
        in_specs=in_specs,
        out_specs=out_specs,
        out_shape=out_shape,
        compiler_params=_cparams(("arbitrary",)),
        name="proj",
    )(*args)


def _fourier_kernel(c_ref, s_ref, a_ref, b_ref, o_ref):
    o = (jnp.dot(c_ref[...], a_ref[...], preferred_element_type=F32)
         - jnp.dot(s_ref[...], b_ref[...], preferred_element_type=F32))
    o_ref[...] = o.astype(BF16)


def _fourier(c_tab, s_tab, a, b):
    seq_len, width = a.shape
    tk = min(seq_len, 256)
    full = lambda i: (0, 0)
    return pl.pallas_call(
        _fourier_kernel,
        grid=(seq_len // tk,),
        in_specs=[
            pl.BlockSpec((tk, seq_len), lambda i: (i, 0)),
            pl.BlockSpec((tk, seq_len), lambda i: (i, 0)),
            pl.BlockSpec((seq_len, width), full, pipeline_mode=pl.Buffered(1)),
            pl.BlockSpec((seq_len, width), full, pipeline_mode=pl.Buffered(1)),
        ],
        out_specs=pl.BlockSpec((tk, width), lambda i: (i, 0)),
        out_shape=jax.ShapeDtypeStruct((seq_len, width), BF16),
        compiler_params=_cparams(("arbitrary",)),
        name="fourier",
    )(c_tab, s_tab, a, b)


def _stack_heads(qg):
    lane_head = lax.broadcasted_iota(I32, qg.shape, 1) // HEAD_DIM
    zero = jnp.zeros_like(qg)
    return jnp.concatenate([jnp.where(lane_head == h, qg, zero) for h in range(HEADS_PER_GROUP)], axis=0)


def _unstack_heads(o, rows):
    lane_head = lax.broadcasted_iota(I32, (rows, o.shape[1]), 1) // HEAD_DIM
    acc = jnp.zeros((rows, o.shape[1]), F32)
    for h in range(HEADS_PER_GROUP):
        acc = acc + jnp.where(lane_head == h, o[h * rows:(h + 1) * rows, :], 0.0)
    return acc


_NT = (((1,), (1,)), ((), ()))


def _attn_kernel(q_ref, k_ref, v_ref, kc_ref, vc_ref, *rest, rows):
    bias_refs, o_ref = rest[:ATTN_ROWS_PER_STEP], rest[ATTN_ROWS_PER_STEP]
    n_loc = NA_WIN_R * GRID_W
    for j in range(ATTN_ROWS_PER_STEP):
        r = pl.program_id(1) * ATTN_ROWS_PER_STEP + j
        rs = jnp.clip(r - NA_WIN_R // 2, 0, rows - NA_WIN_R)
        start = pl.multiple_of(rs * GRID_W, GRID_W)
        kwin = k_ref[pl.ds(start, n_loc), :]
        vwin = v_ref[pl.ds(start, n_loc), :]
        q = q_ref[j * GRID_W:(j + 1) * GRID_W, :]
        bias_ref = bias_refs[j]
        outs = []
        for g in range(NA_HEADS // HEADS_PER_GROUP):
            sl = slice(g * V7X_MXU_DIM, (g + 1) * V7X_MXU_DIM)
            qs = _stack_heads(q[:, sl])
            s_loc = lax.dot_general(qs, kwin[:, sl], _NT, preferred_element_type=F32)
            bias = bias_ref[g * HEADS_PER_GROUP:(g + 1) * HEADS_PER_GROUP].reshape(HEADS_PER_GROUP * GRID_W, n_loc)
            s_loc = s_loc + bias
            s_ctx = lax.dot_general(qs, kc_ref[:, sl], _NT, preferred_element_type=F32)
            m = jnp.maximum(jnp.max(s_loc, axis=-1, keepdims=True), jnp.max(s_ctx, axis=-1, keepdims=True))
            p_loc = jnp.exp(s_loc - m)
            p_ctx = jnp.exp(s_ctx - m)
            denom = jnp.sum(p_loc, axis=-1, keepdims=True) + jnp.sum(p_ctx, axis=-1, keepdims=True)
            o = (jnp.dot(p_loc.astype(BF16), vwin[:, sl], preferred_element_type=F32)
                 + jnp.dot(p_ctx.astype(BF16), vc_ref[:, sl], preferred_element_type=F32))
            outs.append(_unstack_heads(o / denom, GRID_W))
        o_ref[j * GRID_W:(j + 1) * GRID_W, :] = jnp.concatenate(outs, axis=1).astype(BF16)


def _attn(q, k, v, kc, vc, bias_tab, *, n_seq, seq_len, ctx_len):
    rows = seq_len // GRID_W
    rps = ATTN_ROWS_PER_STEP
    steps = rows // rps

    def bias_map(j):
        def index(b, s):
            r = s * rps + j
            rs = jnp.clip(r - NA_WIN_R // 2, 0, rows - NA_WIN_R)
            return (rs - r + NA_WIN_R - 1, 0, 0, 0)
        return index

    bias_specs = [pl.BlockSpec((None, NA_HEADS, GRID_W, NA_WIN_R * GRID_W), bias_map(j)) for j in range(rps)]
    return pl.pallas_call(
        functools.partial(_attn_kernel, rows=rows),
        grid=(n_seq, steps),
        in_specs=[
            pl.BlockSpec((rps * GRID_W, NA_WIDTH), lambda b, s: (b * steps + s, 0)),
            pl.BlockSpec((seq_len, NA_WIDTH), lambda b, s: (b, 0)),
            pl.BlockSpec((seq_len, NA_WIDTH), lambda b, s: (b, 0)),
            pl.BlockSpec((ctx_len, NA_WIDTH), lambda b, s: (b, 0)),
            pl.BlockSpec((ctx_len, NA_WIDTH), lambda b, s: (b, 0)),
        ] + bias_specs,
        out_specs=pl.BlockSpec((rps * GRID_W, NA_WIDTH), lambda b, s: (b * steps + s, 0)),
        out_shape=jax.ShapeDtypeStruct((n_seq * seq_len, NA_WIDTH), BF16),
        compiler_params=_cparams(("arbitrary", "arbitrary")),
        name="attn",
    )(q, k, v, kc, vc, *([bias_tab] * rps))


def _ctx_attn_kernel(q_ref, k_ref, v_ref, o_ref):
    q = q_ref[...]
    n = q.shape[0]
    outs = []
    for g in range(NA_HEADS // HEADS_PER_GROUP):
        sl = slice(g * V7X_MXU_DIM, (g + 1) * V7X_MXU_DIM)
        qs = _stack_heads(q[:, sl])
        s = lax.dot_general(qs, k_ref[:, sl], _NT, preferred_element_type=F32)
        m = jnp.max(s, axis=-1, keepdims=True)
        p = jnp.exp(s - m)
        denom = jnp.sum(p, axis=-1, keepdims=True)
        o = jnp.dot(p.astype(BF16), v_ref[:, sl], preferred_element_type=F32)
        outs.append(_unstack_heads(o / denom, n))
    o_ref[...] = jnp.concatenate(outs, axis=1).astype(BF16)


def _ctx_attn(q, k, v, *, n_seq, ctx_len):
    spec = pl.BlockSpec((ctx_len, NA_WIDTH), lambda b: (b, 0))
    return pl.pallas_call(
        _ctx_attn_kernel,
        grid=(n_seq,),
        in_specs=[spec, spec, spec],
        out_specs=spec,
        out_shape=jax.ShapeDtypeStruct((n_seq * ctx_len, NA_WIDTH), BF16),
        compiler_params=_cparams(("arbitrary",)),
        name="ctx_attn",
    )(q, k, v)


def _merge_kernel(x_ref, mods_ref, n1_ref, n2_ref, f_ref, at_ref, u_ref, up_ref, un_ref, bg_ref, cw_ref,
                  wg_ref, wf_ref, wna_ref, wcv_ref, wo_ref, rw_ref,
                  xo_ref, h2_ref, lg_ref, *, tiles_per_seq, fixed_row):
    i = pl.program_id(0)
    m = _mod_row(mods_ref, i, tiles_per_seq, fixed_row)
    dm = D_MODEL
    x = x_ref[...]
    h = _norm_mod(x, n1_ref[...], m[:, 0:dm], m[:, dm:2 * dm]).astype(BF16)
    gates = jax.nn.sigmoid(jnp.dot(h, wg_ref[...], preferred_element_type=F32))

    y_f = jnp.dot(f_ref[...], wf_ref[...], preferred_element_type=F32)
    y_na = jnp.dot(at_ref[...], wna_ref[...], preferred_element_type=F32)

    u = u_ref[...]
    t = u.shape[0]
    ti = i % tiles_per_seq
    row = lax.broadcasted_iota(I32, u.shape, 0)
    prev_row = jnp.where(ti == 0, 0.0, up_ref[V7X_SUBLANES - 1:V7X_SUBLANES, :])
    next_row = jnp.where(ti == tiles_per_seq - 1, 0.0, un_ref[0:1, :])
    u_prev = jnp.where(row == 0, prev_row, pltpu.roll(u, 1, 0))
    u_next = jnp.where(row == t - 1, next_row, pltpu.roll(u, t - 1, 0))
    y_cv = bg_ref[...] * (cw_ref[0:1, :] * u_prev + cw_ref[1:2, :] * u + cw_ref[2:3, :] * u_next)
    y_cv = jnp.dot(y_cv.astype(BF16), wcv_ref[...], preferred_element_type=F32)

    merged = gates[:, 0:dm] * y_f + gates[:, dm:2 * dm] * y_na + gates[:, 2 * dm:3 * dm] * y_cv
    mixed = jnp.dot(merged.astype(BF16), wo_ref[...], preferred_element_type=F32)
    x_new = x + m[:, 2 * dm:3 * dm] * mixed
    xo_ref[...] = x_new

    h2 = _norm_mod(x_new, n2_ref[...], m[:, 3 * dm:4 * dm], m[:, 4 * dm:5 * dm])
    h2_ref[...] = h2
    hi = h2.astype(BF16)
    lo = (h2 - hi.astype(F32)).astype(BF16)
    p_hi = jnp.dot(hi, rw_ref[...], preferred_element_type=F32)
    p_lo = jnp.dot(lo, rw_ref[...], preferred_element_type=F32)
    lg_ref[...] = p_hi + pltpu.roll(p_hi, V7X_LANES - N_EXPERTS, 1) + p_lo


def _merge(x2, mods_l, n1, n2, f_all, attn, u, bg, conv_w, w_gate, w_f, w_na, w_cv, w_o, rwt,
           *, seq_len, fixed_row):
    n_tok = x2.shape[0]
    tm = TOKEN_TILE
    tps = seq_len // tm
    const = lambda i: (0, 0)
    halo = tm // V7X_SUBLANES
    n_halo = n_tok // V7X_SUBLANES
    in_specs = [
        pl.BlockSpec((tm, D_MODEL), lambda i: (i, 0)),
        pl.BlockSpec((MODS_ROWS, 6 * D_MODEL), const),
        pl.BlockSpec((1, D_MODEL), const),
        pl.BlockSpec((1, D_MODEL), const),
        pl.BlockSpec((tm, F_WIDTH), lambda i: (i % tps, i // tps)),
        pl.BlockSpec((tm, NA_WIDTH), lambda i: (i, 0)),
        pl.BlockSpec((tm, CONV_WIDTH), lambda i: (i, 0)),
        pl.BlockSpec((V7X_SUBLANES, CONV_WIDTH), lambda i: (jnp.maximum(i * halo - 1, 0), 0)),
        pl.BlockSpec((V7X_SUBLANES, CONV_WIDTH), lambda i: (jnp.minimum((i + 1) * halo, n_halo - 1), 0)),
        pl.BlockSpec((tm, CONV_WIDTH), lambda i: (i, 0)),
        pl.BlockSpec((3, CONV_WIDTH), const),
        pl.BlockSpec((D_MODEL, 3 * D_MODEL), const),
        pl.BlockSpec((F_WIDTH, D_MODEL), const),
        pl.BlockSpec((NA_WIDTH, D_MODEL), const),
        pl.BlockSpec((CONV_WIDTH, D_MODEL), const),
        pl.BlockSpec((D_MODEL, D_MODEL), const),
        pl.BlockSpec((D_MODEL, V7X_LANES), const),
    ]
    out_specs = [
        pl.BlockSpec((tm, D_MODEL), lambda i: (i, 0)),
        pl.BlockSpec((tm, D_MODEL), lambda i: (i, 0)),
        pl.BlockSpec((tm, V7X_LANES), lambda i: (i, 0)),
    ]
    out_shape = [
        jax.ShapeDtypeStruct((n_tok, D_MODEL), F32),
        jax.ShapeDtypeStruct((n_tok, D_MODEL), F32),
        jax.ShapeDtypeStruct((n_tok, V7X_LANES), F32),
    ]
    return pl.pallas_call(
        functools.partial(_merge_kernel, tiles_per_seq=tps, fixed_row=fixed_row),
        grid=(n_tok // tm,),
        in_specs=in_specs,
        out_specs=out_specs,
        out_shape=out_shape,
        compiler_params=_cparams(("arbitrary",)),
        name="merge",
    )(x2, mods_l, n1, n2, f_all, attn, u, u, u, bg, conv_w, w_gate, w_f, w_na, w_cv, w_o, rwt)


def _first_max(vals):
    best = vals[0]
    idx = jnp.zeros(best.shape, I32)
    for j in range(1, len(vals)):
        better = vals[j] > best
        idx = jnp.where(better, j, idx)
        best = jnp.where(better, vals[j], best)
    return best, idx


def _select(idx, vals):
    out = vals[-1]
    for j in range(len(vals) - 2, -1, -1):
        out = jnp.where(idx == j, vals[j], out)
    return out


def _route_kernel(lg_ref, rb_ref, ids_ref, wts_ref, cnt_ref, tot_ref, run_ref):
    step = pl.program_id(0)

    @pl.when(step == 0)
    def _():
        run_ref[...] = jnp.zeros_like(run_ref)

    s = jax.nn.sigmoid(lg_ref[...].T[0:N_EXPERTS, :])
    sb = s + rb_ref[...]
    t = s.shape[1]
    s_rows = [s[e:e + 1, :] for e in range(N_EXPERTS)]
    b_rows = [sb[e:e + 1, :] for e in range(N_EXPERTS)]
    epg = EXPERTS_PER_GROUP
    gscore = []
    for g in range(N_GROUPS):
        v = b_rows[g * epg:(g + 1) * epg]
        pair = None
        for a in range(epg):
            for b in range(a + 1, epg):
                pair = v[a] + v[b] if pair is None else jnp.maximum(pair, v[a] + v[b])
        gscore.append(pair)
    _, gi = _first_max(gscore)
    bv = [_select(gi, [b_rows[g * epg + j] for g in range(N_GROUPS)]) for j in range(epg)]
    sv = [_select(gi, [s_rows[g * epg + j] for g in range(N_GROUPS)]) for j in range(epg)]
    _, i1 = _first_max(bv)
    _, i2 = _first_max([jnp.where(i1 == j, -jnp.inf, bv[j]) for j in range(epg)])
    s1 = _select(i1, sv)
    s2 = _select(i2, sv)
    tot = s1 + s2
    e1 = gi * epg + i1
    e2 = gi * epg + i2

    eid = lax.broadcasted_iota(I32, (N_EXPERTS, t), 0)
    hit1 = eid == e1
    hit2 = eid == e2
    onehot = jnp.where(hit1 | hit2, 1.0, 0.0)
    before = (lax.broadcasted_iota(I32, (t, t), 0) < lax.broadcasted_iota(I32, (t, t), 1))
    prefix = jnp.dot(onehot.astype(BF16), jnp.where(before, 1.0, 0.0).astype(BF16),
                     preferred_element_type=F32)
    r1 = jnp.sum(jnp.where(hit1, prefix, 0.0), axis=0, keepdims=True)
    r2 = jnp.sum(jnp.where(hit2, prefix, 0.0), axis=0, keepdims=True)
    grp = float(SLOT_GROUP)
    cnt = jnp.sum(onehot, axis=1, keepdims=True)
    cnt = jnp.floor((cnt + (grp - 1.0)) / grp) * grp
    run = run_ref[...] + cnt
    run_ref[...] = run
    cnt_ref[...] = jnp.broadcast_to(cnt, cnt_ref.shape)
    tot_ref[...] = jnp.broadcast_to(run, tot_ref.shape)

    zi = jnp.zeros((V7X_SUBLANES - 4, t), I32)
    ids_ref[...] = jnp.concatenate([e1, e2, r1.astype(I32), r2.astype(I32), zi], axis=0)
    zf = jnp.zeros((V7X_SUBLANES - 2, t), F32)
    wts_ref[...] = jnp.concatenate([s1 / tot, s2 / tot, zf], axis=0)


def _route(logits, router_b):
    n_tok = logits.shape[0]
    tr = ROUTE_TILE
    return pl.pallas_call(
        _route_kernel,
        grid=(n_tok // tr,),
        in_specs=[
            pl.BlockSpec((tr, V7X_LANES), lambda i: (i, 0)),
            pl.BlockSpec((N_EXPERTS, 1), lambda i: (0, 0)),
        ],
        out_specs=[
            pl.BlockSpec((V7X_SUBLANES, tr), lambda i: (0, i)),
            pl.BlockSpec((V7X_SUBLANES, tr), lambda i: (0, i)),
            pl.BlockSpec((None, N_EXPERTS, V7X_LANES), lambda i: (i, 0, 0)),
            pl.BlockSpec((N_EXPERTS, V7X_LANES), lambda i: (0, 0)),
        ],
        out_shape=[
            jax.ShapeDtypeStruct((V7X_SUBLANES, n_tok), I32),
            jax.ShapeDtypeStruct((V7X_SUBLANES, n_tok), F32),
            jax.ShapeDtypeStruct((n_tok // tr, N_EXPERTS, V7X_LANES), F32),
            jax.ShapeDtypeStruct((N_EXPERTS, V7X_LANES), F32),
        ],
        scratch_shapes=[pltpu.VMEM((N_EXPERTS, 1), F32)],
        compiler_params=_cparams(("arbitrary",)),
        name="route",
    )(logits, router_b.reshape(N_EXPERTS, 1))


def _lane_table(vals, width):
    lane = lax.broadcasted_iota(I32, (1, width), 1)
    out = jnp.zeros((1, width), F32)
    for e, v in enumerate(vals):
        out = jnp.where(lane == e, v, out)
    return out


def _slots_kernel(ids_ref, cnt_ref, tot_ref, loc_ref, gmap_ref, blk_ref, off_ref):
    step = pl.program_id(0)

    @pl.when(step == 0)
    def _():
        off_ref[...] = jnp.zeros_like(off_ref)

    blk = float(EXPERT_BLOCK)
    grp = float(SLOT_GROUP)
    cnt = cnt_ref[...][:, 0:1]
    tot = tot_ref[...][:, 0:1]
    off = off_ref[...]
    region = jnp.floor((tot + (blk - 1.0)) / blk) * blk
    starts, ends, local = [], [], []
    run = jnp.zeros((1, 1), F32)
    lrun = jnp.zeros((1, 1), F32)
    for e in range(N_EXPERTS):
        starts.append(run)
        run = run + region[e:e + 1, :]
        ends.append(run)
        local.append(lrun)
        lrun = lrun + cnt[e:e + 1, :]

    ids = ids_ref[...]
    e1, e2 = ids[0:1, :], ids[1:2, :]
    t = ids.shape[1]
    l1 = jnp.zeros((1, t), F32)
    l2 = jnp.zeros((1, t), F32)
    for e in range(N_EXPERTS):
        l1 = jnp.where(e1 == e, local[e], l1)
        l2 = jnp.where(e2 == e, local[e], l2)
    zi = jnp.zeros((V7X_SUBLANES - 2, t), I32)
    loc_ref[...] = jnp.concatenate([l1.astype(I32) + ids[2:3, :], l2.astype(I32) + ids[3:4, :], zi], axis=0)

    wg = gmap_ref.shape[1]
    first = lax.broadcasted_iota(I32, (1, wg), 1).astype(F32) * grp
    dest = jnp.zeros((1, wg), F32)
    for e in range(N_EXPERTS):
        inside = (first >= local[e]) & (first < local[e] + cnt[e:e + 1, :])
        dest = jnp.where(inside, starts[e] + off[e:e + 1, :] + (first - local[e]), dest)
    n_groups = jnp.broadcast_to(lrun / grp, (1, wg))
    zg = jnp.zeros((V7X_SUBLANES - 2, wg), I32)
    gmap_ref[...] = jnp.concatenate([(dest / grp).astype(I32), n_groups.astype(I32), zg], axis=0)
    off_ref[...] = off + cnt

    w = blk_ref.shape[1]
    first_row = lax.broadcasted_iota(I32, (1, w), 1).astype(F32) * blk
    owner = jnp.zeros((1, w), F32)
    for e in range(N_EXPERTS):
        owner = owner + jnp.where(first_row >= ends[e], 1.0, 0.0)
    owner = jnp.minimum(owner, float(N_EXPERTS - 1))
    used = jnp.broadcast_to(ends[-1] / blk, (1, w))
    pad_first = _lane_table([(starts[e] + tot[e:e + 1, :]) / grp for e in range(N_EXPERTS)], w)
    pad_count = _lane_table([(region[e:e + 1, :] - tot[e:e + 1, :]) / grp for e in range(N_EXPERTS)], w)
    zb = jnp.zeros((V7X_SUBLANES - 4, w), I32)
    blk_ref[...] = jnp.concatenate([owner.astype(I32), used.astype(I32), pad_first.astype(I32),
                                    pad_count.astype(I32), zb], axis=0)


def _slots(ids, cnt, tot, n_blocks):
    n_tok = ids.shape[1]
    tr = ROUTE_TILE
    wblk = -(-n_blocks // V7X_LANES) * V7X_LANES
    return pl.pallas_call(
        _slots_kernel,
        grid=(n_tok // tr,),
        in_specs=[
            pl.BlockSpec((V7X_SUBLANES, tr), lambda i: (0, i)),
            pl.BlockSpec((None, N_EXPERTS, V7X_LANES), lambda i: (i, 0, 0)),
            pl.BlockSpec((N_EXPERTS, V7X_LANES), lambda i: (0, 0)),
        ],
        out_specs=[
            pl.BlockSpec((V7X_SUBLANES, tr), lambda i: (0, i)),
            pl.BlockSpec((None, V7X_SUBLANES, SORT_GROUPS_PAD), lambda i: (i, 0, 0)),
            pl.BlockSpec((V7X_SUBLANES, wblk), lambda i: (0, 0)),
        ],
        out_shape=[
            jax.ShapeDtypeStruct((V7X_SUBLANES, n_tok), I32),
            jax.ShapeDtypeStruct((n_tok // tr, V7X_SUBLANES, SORT_GROUPS_PAD), I32),
            jax.ShapeDtypeStruct((V7X_SUBLANES, wblk), I32),
        ],
        scratch_shapes=[pltpu.VMEM((N_EXPERTS, 1), F32)],
        compiler_params=_cparams(("arbitrary",)),
        name="slots",
    )(ids, cnt, tot)


def _group_rows(group):
    if isinstance(group, int):
        return pl.ds(group * SLOT_GROUP, SLOT_GROUP)
    return pl.ds(pl.multiple_of(group * SLOT_GROUP, SLOT_GROUP), SLOT_GROUP)


def _group_copy(src_ref, src_group, dst_ref, dst_group, sem):
    return pltpu.make_async_copy(src_ref.at[_group_rows(src_group)], dst_ref.at[_group_rows(dst_group)], sem)


def _dispatch_kernel(gmap_ref, blk_ref, loc_ref, *refs, n_first):
    if n_first is None:
        h_ref, xb_ref, sorted_ref, zero_ref, sem = refs
        second_ref = None
    else:
        h_ref, second_ref, xb_ref, sorted_ref, zero_ref, sem = refs
    step = pl.program_id(0)
    loc = loc_ref[...]
    slot = lax.broadcasted_iota(I32, (SORT_ROWS, loc.shape[1]), 0)
    perm = jnp.where(slot == loc[0:1, :], 1.0, jnp.where(slot == loc[1:2, :], 1.0, 0.0)).astype(BF16)

    def sort_rows(src_ref):
        sorted_ref[...] = jnp.dot(perm, src_ref[...].astype(BF16), preferred_element_type=F32).astype(BF16)

    if second_ref is None:
        sort_rows(h_ref)
    else:
        pl.when(step < n_first)(lambda: sort_rows(h_ref))
        pl.when(step >= n_first)(lambda: sort_rows(second_ref))
    n_groups = gmap_ref[1, 0]

    def start(g, c):
        _group_copy(sorted_ref, g, xb_ref, gmap_ref[0, g], sem.at[0]).start()
        return c

    def wait(g, c):
        _group_copy(sorted_ref, g, xb_ref, gmap_ref[0, g], sem.at[0]).wait()
        return c

    lax.fori_loop(0, n_groups, start, 0)

    @pl.when(step == pl.num_programs(0) - 1)
    def _():
        zero_ref[...] = jnp.zeros_like(zero_ref)
        for e in range(N_EXPERTS):
            first = blk_ref[2, e]

            def zstart(g, c, first=first):
                _group_copy(zero_ref, 0, xb_ref, first + g, sem.at[1]).start()
                return c

            def zwait(g, c, first=first):
                _group_copy(zero_ref, 0, xb_ref, first + g, sem.at[1]).wait()
                return c

            lax.fori_loop(0, blk_ref[3, e], zstart, 0)
            lax.fori_loop(0, blk_ref[3, e], zwait, 0)

        def block_copy(b):
            rows = pl.ds(pl.multiple_of(b * EXPERT_BLOCK, EXPERT_BLOCK), EXPERT_BLOCK)
            return pltpu.make_async_copy(zero_ref, xb_ref.at[rows], sem.at[1])

        def bstart(b, c):
            block_copy(b).start()
            return c

        def bwait(b, c):
            block_copy(b).wait()
            return c

        n_blocks = xb_ref.shape[0] // EXPERT_BLOCK
        lax.fori_loop(blk_ref[1, 0], n_blocks, bstart, 0)
        lax.fori_loop(blk_ref[1, 0], n_blocks, bwait, 0)

    lax.fori_loop(0, n_groups, wait, 0)


def _dispatch(gmap, blk, loc, h2, h2_second, n_slots):
    tr = ROUTE_TILE
    n_first = h2.shape[0] // tr
    n_tiles = n_first
    in_specs = [
        pl.BlockSpec((None, V7X_SUBLANES, SORT_GROUPS_PAD), lambda i: (i, 0, 0), memory_space=pltpu.SMEM),
        pl.BlockSpec(blk.shape, lambda i: (0, 0), memory_space=pltpu.SMEM),
        pl.BlockSpec((V7X_SUBLANES, tr), lambda i: (0, i)),
        pl.BlockSpec((tr, D_MODEL), lambda i: (jnp.minimum(i, n_first - 1), 0)),
    ]
    args = [gmap, blk, loc, h2]
    if h2_second is not None:
        n_tiles += h2_second.shape[0] // tr
        in_specs.append(pl.BlockSpec((tr, D_MODEL), lambda i: (jnp.maximum(i - n_first, 0), 0)))
        args.append(h2_second)
    return pl.pallas_call(
        functools.partial(_dispatch_kernel, n_first=None if h2_second is None else n_first),
        grid=(n_tiles,),
        in_specs=in_specs,
        out_specs=pl.BlockSpec(memory_space=pl.ANY),
        out_shape=jax.ShapeDtypeStruct((n_slots, D_MODEL), BF16),
        scratch_shapes=[pltpu.VMEM((SORT_ROWS, D_MODEL), BF16), pltpu.VMEM((EXPERT_BLOCK, D_MODEL), BF16),
                        pltpu.SemaphoreType.DMA((2,))],
        compiler_params=_cparams(("arbitrary",)),
        name="dispatch",
    )(*args)


def _experts_kernel(blk_ref, used_ref, x_ref, w1_ref, w3_ref, w2_ref, y_ref, w1b, w3b, w2b):
    i = pl.program_id(0)
    prev = blk_ref[jnp.maximum(i - 1, 0)]

    @pl.when((i == 0) | (blk_ref[i] != prev))
    def _():
        w1b[...] = w1_ref[...].astype(BF16)
        w3b[...] = w3_ref[...].astype(BF16)
        w2b[...] = w2_ref[...].astype(BF16)

    @pl.when(i < used_ref[0])
    def _():
        x = x_ref[...]
        a = jnp.dot(x, w1b[...], preferred_element_type=F32)
        b = jnp.dot(x, w3b[...], preferred_element_type=F32)
        hid = (a * jax.nn.sigmoid(a) * b).astype(BF16)
        y_ref[...] = jnp.dot(hid, w2b[...], preferred_element_type=F32).astype(BF16)

    @pl.when(i >= used_ref[0])
    def _():
        y_ref[...] = jnp.zeros_like(y_ref)


def _experts(blk_e, used, xb, w1, w3, w2, layer):
    n_slots = xb.shape[0]
    bm = EXPERT_BLOCK
    row_map = lambda i, be, nu: (jnp.minimum(i, nu[0] - 1), 0)
    w_map = lambda i, be, nu: (layer, be[i], 0, 0)
    grid_spec = pltpu.PrefetchScalarGridSpec(
        num_scalar_prefetch=2,
        grid=(n_slots // bm,),
        in_specs=[
            pl.BlockSpec((bm, D_MODEL), row_map),
            pl.BlockSpec((None, None, D_MODEL, D_EXPERT), w_map),
            pl.BlockSpec((None, None, D_MODEL, D_EXPERT), w_map),
            pl.BlockSpec((None, None, D_EXPERT, D_MODEL), w_map),
        ],
        out_specs=pl.BlockSpec((bm, D_MODEL), lambda i, be, nu: (i, 0)),
        scratch_shapes=[pltpu.VMEM((D_MODEL, D_EXPERT), BF16), pltpu.VMEM((D_MODEL, D_EXPERT), BF16),
                        pltpu.VMEM((D_EXPERT, D_MODEL), BF16)],
    )
    return pl.pallas_call(
        _experts_kernel,
        grid_spec=grid_spec,
        out_shape=jax.ShapeDtypeStruct((n_slots, D_MODEL), BF16),
        compiler_params=_cparams(("arbitrary",)),
        name="experts",
    )(blk_e, used, xb, w1, w3, w2)


_TN = (((0,), (0,)), ((), ()))


def _combine_kernel(gmap_ref, loc_ref, wts_ref, x_ref, mods_ref, yb_ref, o_ref, ys_ref, sem,
                    *, tiles_per_seq, fixed_row):
    step = pl.program_id(0)

    @pl.when(step == 0)
    def _():
        ys_ref[...] = jnp.zeros_like(ys_ref)

    n_groups = gmap_ref[1, 0]

    def start(g, c):
        _group_copy(yb_ref, gmap_ref[0, g], ys_ref, g, sem).start()
        return c

    def wait(g, c):
        _group_copy(yb_ref, gmap_ref[0, g], ys_ref, g, sem).wait()
        return c

    lax.fori_loop(0, n_groups, start, 0)
    loc = loc_ref[...]
    wts = wts_ref[...]
    slot = lax.broadcasted_iota(I32, (SORT_ROWS, loc.shape[1]), 0)
    perm = jnp.where(slot == loc[0:1, :], wts[0:1, :], jnp.where(slot == loc[1:2, :], wts[1:2, :], 0.0))
    lax.fori_loop(0, n_groups, wait, 0)
    y = lax.dot_general(perm.astype(BF16), ys_ref[...], _TN, preferred_element_type=F32)
    m = _mod_row(mods_ref, step, tiles_per_seq, fixed_row)
    o_ref[...] = x_ref[...] + m[:, 5 * D_MODEL:6 * D_MODEL] * y


def _combine(gmap, loc, wts, x_new, mods_l, yb, *, seq_len, fixed_row, tile_offset):
    n_tok = x_new.shape[0]
    tr = ROUTE_TILE
    tps = max(seq_len // tr, 1)
    return pl.pallas_call(
        functools.partial(_combine_kernel, tiles_per_seq=tps, fixed_row=fixed_row),
        grid=(n_tok // tr,),
        in_specs=[
            pl.BlockSpec((None, V7X_SUBLANES, SORT_GROUPS_PAD), lambda i: (i + tile_offset, 0, 0),
                         memory_space=pltpu.SMEM),
            pl.BlockSpec((V7X_SUBLANES, tr), lambda i: (0, i + tile_offset)),
            pl.BlockSpec((V7X_SUBLANES, tr), lambda i: (0, i + tile_offset)),
            pl.BlockSpec((tr, D_MODEL), lambda i: (i, 0)),
            pl.BlockSpec((MODS_ROWS, 6 * D_MODEL), lambda i: (0, 0)),
            pl.BlockSpec(memory_space=pl.ANY),
        ],
        out_specs=pl.BlockSpec((tr, D_MODEL), lambda i: (i, 0)),
        out_shape=jax.ShapeDtypeStruct((n_tok, D_MODEL), F32),
        scratch_shapes=[pltpu.VMEM((SORT_ROWS, D_MODEL), BF16), pltpu.SemaphoreType.DMA],
        compiler_params=_cparams(("arbitrary",)),
        name="combine",
    )(gmap, loc, wts, x_new, mods_l, yb)


def _channel_dft_tables():
    j = np.arange(F_GDIM)
    ang = 2.0 * np.pi * ((j[:, None] * j[None, :]) % F_GDIM) / F_GDIM
    eye = np.eye(F_GROUPS)
    return (jnp.asarray(np.kron(eye, np.cos(ang)), F32).astype(BF16),
            jnp.asarray(np.kron(eye, np.sin(ang)), F32).astype(BF16))


def _position_dft_tables(seq_len):
    scale = 1.0 / math.sqrt(seq_len * F_GDIM)
    k = np.arange(seq_len, dtype=np.int64)
    if seq_len <= 256:
        ang = 2.0 * np.pi * ((k[:, None] * k[None, :]) % seq_len) / seq_len
        return (jnp.asarray(np.cos(ang) * scale, F32).astype(BF16),
                jnp.asarray(np.sin(ang) * scale, F32).astype(BF16))
    hi = seq_len // 64
    t1 = np.arange(hi, dtype=np.int64)
    t0 = np.arange(64, dtype=np.int64)
    ang_a = 2.0 * np.pi * ((k[:, None] * t1[None, :] * 64) % seq_len) / seq_len
    ang_b = 2.0 * np.pi * ((k[:, None] * t0[None, :]) % seq_len) / seq_len
    ca = jnp.asarray(np.cos(ang_a) * scale, F32)[:, :, None]
    sa = jnp.asarray(np.sin(ang_a) * scale, F32)[:, :, None]
    cb = jnp.asarray(np.cos(ang_b), F32)[:, None, :]
    sb = jnp.asarray(np.sin(ang_b), F32)[:, None, :]
    c = (ca * cb - sa * sb).reshape(seq_len, seq_len).astype(BF16)
    s = (sa * cb + ca * sb).reshape(seq_len, seq_len).astype(BF16)
    return c, s


def _rope_tables(seq_len):
    t = np.arange(seq_len)
    row = (t // GRID_W).astype(np.float64)
    col = (t % GRID_W).astype(np.float64)
    inv = np.power(ROPE_BASE, -np.arange(ROPE_PER_AXIS, dtype=np.float64) / ROPE_PER_AXIS)
    ang = np.concatenate([row[:, None] * inv, col[:, None] * inv], axis=-1)
    cos = np.cos(ang)
    sin = np.sin(ang)
    cos_h = np.concatenate([cos, cos], axis=-1)
    sin_h = np.concatenate([-sin, sin], axis=-1)
    return (jnp.asarray(np.tile(cos_h, (1, NA_HEADS)), F32), jnp.asarray(np.tile(sin_h, (1, NA_HEADS)), F32))


def _bias_table(rpb_l):
    col = np.arange(GRID_W)
    col_start = np.clip(col - NA_WIN_C // 2, 0, GRID_W - NA_WIN_C)
    col_mask = (col[None, :] >= col_start[:, None]) & (col[None, :] < col_start[:, None] + NA_WIN_C)
    dc = np.clip(col[None, :] - col[:, None] + (NA_WIN_C - 1), 0, 2 * NA_WIN_C - 2)
    n_dc = 2 * NA_WIN_C - 1
    pick = (dc.reshape(-1)[None, :] == np.arange(n_dc)[:, None]).astype(np.float32)
    e = jnp.dot(rpb_l.reshape(-1, n_dc), jnp.asarray(pick), precision=HIGHEST)
    e = e.reshape(NA_HEADS, 2 * NA_WIN_R - 1, GRID_W, GRID_W)
    e = jnp.where(jnp.asarray(col_mask)[None, None], e, NEG_BIG)
    b = jnp.stack([e[:, o:o + NA_WIN_R] for o in range(NA_WIN_R)], axis=0)
    b = b.transpose(0, 1, 3, 2, 4)
    return b.reshape(NA_WIN_R, NA_HEADS, GRID_W, NA_WIN_R * GRID_W)


def _moe(h2, h2_second, logits, w1, w3, w2, layer, router_b):
    n_tok = logits.shape[0]
    n_tiles = n_tok // ROUTE_TILE
    max_rows = 2 * n_tok + N_EXPERTS * n_tiles * (SLOT_GROUP - 1) + N_EXPERTS * (EXPERT_BLOCK - 1)
    n_blocks = -(-max_rows // EXPERT_BLOCK)
    n_slots = n_blocks * EXPERT_BLOCK
    ids, wts, cnt, tot = _route(logits, router_b)
    loc, gmap, blk = _slots(ids, cnt, tot, n_blocks)
    xb = _dispatch(gmap, blk, loc, h2, h2_second, n_slots)
    yb = _experts(blk[0, :n_blocks], blk[1, 0:1], xb, w1, w3, w2, layer)
    return yb, gmap, loc, wts


def kernel(x, c, ctx, c_ctx, ada_w, ada_b, norm1_g, w_in, qn_g, kn_g, rpb, conv_w, w_f, w_na, w_cv, w_o,
           norm2_g, router_w, router_b, w1, w3, w2):
    bsz, seq_len, d = x.shape
    ctx_len = ctx.shape[1]
    n_lat = bsz * seq_len
    n_ctx = bsz * ctx_len
    ctx_row = bsz

    c8 = jnp.concatenate([c, c_ctx[None, :], jnp.zeros((MODS_ROWS - bsz - 1, d), F32)], axis=0)
    mods = _mods(c8, ada_w, ada_b)

    cbd, sbd = _channel_dft_tables()
    c_lat, s_lat = _position_dft_tables(seq_len)
    c_ctx_t, s_ctx_t = _position_dft_tables(ctx_len)
    cos_t, sin_t = _rope_tables(seq_len)
    mavg = jnp.asarray(np.kron(np.eye(NA_HEADS), np.full((HEAD_DIM, HEAD_DIM), 1.0 / HEAD_DIM)), F32).astype(BF16)
    rw_hi = router_w.astype(BF16)
    rw_lo = (router_w - rw_hi.astype(F32)).astype(BF16)
    rwt = jnp.concatenate([rw_hi, rw_lo, jnp.zeros((d, V7X_LANES - 2 * N_EXPERTS), BF16)], axis=1)

    xl = x.reshape(n_lat, d)
    xc = ctx.reshape(n_ctx, d)
    for l in range(DEPTH):
        last = l == DEPTH - 1
        w_proj = w_in[l][:, :COL_G].astype(BF16)
        w_gate = w_in[l][:, COL_G:].astype(BF16)
        wf, wna, wcv, wo = (w_f[l].astype(BF16), w_na[l].astype(BF16), w_cv[l].astype(BF16), w_o[l].astype(BF16))
        n1 = norm1_g[l].reshape(1, d)
        n2 = norm2_g[l].reshape(1, d)
        qg = jnp.tile(qn_g[l], NA_HEADS).reshape(1, NA_WIDTH)
        kg = jnp.tile(kn_g[l], NA_HEADS).reshape(1, NA_WIDTH)
        bias_tab = _bias_table(rpb[l])
        mods_l = mods[l]

        a_c, b_c, q_c, k_c, v_c, u_c, bg_c = _proj(
            xc, mods_l, n1, w_proj, qg, kg, mavg, cbd, sbd, None, None,
            seq_len=ctx_len, n_seq=bsz, fixed_row=ctx_row)
        a_l, b_l, q_l, k_l, v_l, u_l, bg_l = _proj(
            xl, mods_l, n1, w_proj, qg, kg, mavg, cbd, sbd, cos_t, sin_t,
            seq_len=seq_len, n_seq=bsz, fixed_row=None)

        f_l = _fourier(c_lat, s_lat, a_l, b_l)
        attn_l = _attn(q_l, k_l, v_l, k_c, v_c, bias_tab, n_seq=bsz, seq_len=seq_len, ctx_len=ctx_len)
        xl_new, h2_l, lg_l = _merge(xl, mods_l, n1, n2, f_l, attn_l, u_l, bg_l, conv_w[l], w_gate,
                                    wf, wna, wcv, wo, rwt, seq_len=seq_len, fixed_row=None)
        if last:
            yb, gmap, loc, wts = _moe(h2_l, None, lg_l, w1, w3, w2, l, router_b)
            xl = _combine(gmap, loc, wts, xl_new, mods_l, yb, seq_len=seq_len, fixed_row=None, tile_offset=0)
        else:
            f_c = _fourier(c_ctx_t, s_ctx_t, a_c, b_c)
            attn_c = _ctx_attn(q_c, k_c, v_c, n_seq=bsz, ctx_len=ctx_len)
            xc_new, h2_c, lg_c = _merge(xc, mods_l, n1, n2, f_c, attn_c, u_c, bg_c, conv_w[l], w_gate,
                                        wf, wna, wcv, wo, rwt, seq_len=ctx_len, fixed_row=ctx_row)
            lg = jnp.concatenate([lg_l, lg_c], axis=0)
            yb, gmap, loc, wts = _moe(h2_l, h2_c, lg, w1, w3, w2, l, router_b)
            xl = _combine(gmap, loc, wts, xl_new, mods_l, yb, seq_len=seq_len, fixed_row=None, tile_offset=0)
            xc = _combine(gmap, loc, wts, xc_new, mods_l, yb, seq_len=ctx_len, fixed_row=ctx_row,
                          tile_offset=n_lat // ROUTE_TILE)
    return xl.reshape(bsz, seq_len, d)
```

```python
import functools
import math

import numpy as np
import jax
import jax.numpy as jnp
from jax import lax
from jax.experimental import pallas as pl
from jax.experimental.pallas import tpu as pltpu

F32 = jnp.float32
BF16 = jnp.bfloat16
I32 = jnp.int32
HIGHEST = lax.Precision.HIGHEST

D_MODEL = 1024
DEPTH = 2
GRID_W = 64
EPS = 1e-6
F_GROUPS = 4
F_GDIM = 64
F_WIDTH = 256
NA_HEADS = 8
HEAD_DIM = 64
NA_WIDTH = 512
NA_WIN_R = 8
NA_WIN_C = 16
ATTN_SCALE = HEAD_DIM ** -0.5
ROPE_BASE = 10000.0
ROPE_PER_AXIS = HEAD_DIM // 4
CONV_WIDTH = 256
COL_Q = 256
COL_K = 768
COL_V = 1280
COL_CX = 1792
COL_CB = 2048
COL_CC = 2304
COL_G = 2560
N_EXPERTS = 16
N_GROUPS = 4
EXPERTS_PER_GROUP = 4
D_EXPERT = 512

V7X_LANES = 128
V7X_SUBLANES = 8
V7X_MXU_DIM = 256

TOKEN_TILE = 512
ROUTE_TILE = 512
EXPERT_BLOCK = 512
SLOT_GROUP = 2 * V7X_SUBLANES
SORT_ROWS = -(-(2 * ROUTE_TILE + N_EXPERTS * (SLOT_GROUP - 1)) // V7X_LANES) * V7X_LANES
SORT_GROUPS_PAD = -(-(SORT_ROWS // SLOT_GROUP) // V7X_LANES) * V7X_LANES
HEADS_PER_GROUP = V7X_MXU_DIM // HEAD_DIM
ATTN_ROWS_PER_STEP = 8
NEG_BIG = -1e30
MODS_ROWS = 8
VMEM_LIMIT = 48 * 1024 * 1024


def _cparams(sem):
    return pltpu.CompilerParams(dimension_semantics=sem, vmem_limit_bytes=VMEM_LIMIT)


def _mods_kernel(c_ref, w_ref, b_ref, o_ref):
    c = c_ref[...]
    sc = c * jax.nn.sigmoid(c)
    o_ref[...] = jnp.dot(sc, w_ref[...], precision=HIGHEST, preferred_element_type=F32) + b_ref[...]


def _mods(c8, ada_w, ada_b):
    nb = 1536
    return pl.pallas_call(
        _mods_kernel,
        grid=(DEPTH, 6 * D_MODEL // nb),
        in_specs=[
            pl.BlockSpec((MODS_ROWS, D_MODEL), lambda l, j: (0, 0)),
            pl.BlockSpec((None, D_MODEL, nb), lambda l, j: (l, 0, j)),
            pl.BlockSpec((None, 1, nb), lambda l, j: (l, 0, j)),
        ],
        out_specs=pl.BlockSpec((None, MODS_ROWS, nb), lambda l, j: (l, 0, j)),
        out_shape=jax.ShapeDtypeStruct((DEPTH, MODS_ROWS, 6 * D_MODEL), F32),
        compiler_params=_cparams(("arbitrary", "arbitrary")),
        name="mods",
    )(c8, ada_w, ada_b.reshape(DEPTH, 1, 6 * D_MODEL))


def _norm_mod(x, g, shift, scale):
    ms = jnp.mean(x * x, axis=-1, keepdims=True)
    return (x * lax.rsqrt(ms + EPS) * g) * (1.0 + scale) + shift


def _mod_row(mods_ref, tile, tiles_per_seq, fixed_row):
    row = fixed_row if fixed_row is not None else tile // tiles_per_seq
    return mods_ref[pl.ds(row, 1), :]


def _proj_kernel(*refs, tiles_per_seq, fixed_row, rope):
    if rope:
        (x_ref, mods_ref, g_ref, w_ref, qg_ref, kg_ref, mavg_ref, cbd_ref, sbd_ref, cos_ref, sin_ref,
         a_ref, b_ref, q_ref, k_ref, v_ref, u_ref, bg_ref) = refs
    else:
        (x_ref, mods_ref, g_ref, w_ref, qg_ref, kg_ref, mavg_ref, cbd_ref, sbd_ref,
         a_ref, b_ref, q_ref, k_ref, v_ref, u_ref, bg_ref) = refs
    m = _mod_row(mods_ref, pl.program_id(0), tiles_per_seq, fixed_row)
    h = _norm_mod(x_ref[...], g_ref[...], m[:, 0:D_MODEL], m[:, D_MODEL:2 * D_MODEL])
    p = jnp.dot(h.astype(BF16), w_ref[...], preferred_element_type=F32)

    uf = p[:, 0:COL_Q].astype(BF16)
    a_ref[...] = jnp.dot(uf, cbd_ref[...], preferred_element_type=F32).astype(BF16)
    b_ref[...] = jnp.dot(uf, sbd_ref[...], preferred_element_type=F32).astype(BF16)

    def head_norm(t, g):
        ms = jnp.dot((t * t).astype(BF16), mavg_ref[...], preferred_element_type=F32)
        return t * lax.rsqrt(ms + EPS) * g

    def rotate(t):
        n = t.shape[-1]
        lane = lax.broadcasted_iota(I32, t.shape, 1)
        first_half = (lane % HEAD_DIM) < (HEAD_DIM // 2)
        swapped = jnp.where(first_half, pltpu.roll(t, n - HEAD_DIM // 2, 1), pltpu.roll(t, HEAD_DIM // 2, 1))
        return t * cos_ref[...] + swapped * sin_ref[...]

    q = head_norm(p[:, COL_Q:COL_K], qg_ref[...])
    k = head_norm(p[:, COL_K:COL_V], kg_ref[...])
    if rope:
        q = rotate(q)
        k = rotate(k)
    q_ref[...] = (q * ATTN_SCALE).astype(BF16)
    k_ref[...] = k.astype(BF16)
    v_ref[...] = p[:, COL_V:COL_CX].astype(BF16)
    u_ref[...] = p[:, COL_CC:COL_G] * p[:, COL_CX:COL_CB]
    bg_ref[...] = p[:, COL_CB:COL_CC]


def _proj(x2, mods_l, norm_g, w_proj, qg, kg, mavg, cbd, sbd, cos_t, sin_t, *, seq_len, n_seq, fixed_row):
    n_tok = x2.shape[0]
    tm = min(TOKEN_TILE, seq_len)
    tps = seq_len // tm
    rope = cos_t is not None
    const = lambda i: (0, 0)
    in_specs = [
        pl.BlockSpec((tm, D_MODEL), lambda i: (i, 0)),
        pl.BlockSpec((MODS_ROWS, 6 * D_MODEL), const),
        pl.BlockSpec((1, D_MODEL), const),
        pl.BlockSpec((D_MODEL, COL_G), const),
        pl.BlockSpec((1, NA_WIDTH), const),
        pl.BlockSpec((1, NA_WIDTH), const),
        pl.BlockSpec((NA_WIDTH, NA_WIDTH), const),
        pl.BlockSpec((F_WIDTH, F_WIDTH), const),
        pl.BlockSpec((F_WIDTH, F_WIDTH), const),
    ]
    args = [x2, mods_l, norm_g, w_proj, qg, kg, mavg, cbd, sbd]
    if rope:
        in_specs += [pl.BlockSpec((tm, NA_WIDTH), lambda i: (i % tps, 0))] * 2
        args += [cos_t, sin_t]
    tok = lambda w: pl.BlockSpec((tm, w), lambda i: (i, 0))
    fmap = pl.BlockSpec((tm, F_WIDTH), lambda i: (i % tps, i // tps))
    out_specs = [fmap, fmap, tok(NA_WIDTH), tok(NA_WIDTH), tok(NA_WIDTH), tok(CONV_WIDTH), tok(CONV_WIDTH)]
    out_shape = [
        jax.ShapeDtypeStruct((seq_len, n_seq * F_WIDTH), BF16),
        jax.ShapeDtypeStruct((seq_len, n_seq * F_WIDTH), BF16),
        jax.ShapeDtypeStruct((n_tok, NA_WIDTH), BF16),
        jax.ShapeDtypeStruct((n_tok, NA_WIDTH), BF16),
        jax.ShapeDtypeStruct((n_tok, NA_WIDTH), BF16),
        jax.ShapeDtypeStruct((n_tok, CONV_WIDTH), F32),
        jax.ShapeDtypeStruct((n_tok, CONV_WIDTH), F32),
    ]
    return pl.pallas_call(
        functools.partial(_proj_kernel, tiles_per_seq=tps, fixed_row=fixed_row, rope=rope),
        grid=(n_tok // tm,),
        in_specs=in_specs,
        out_specs=out_specs,
        out_shape=out_shape,
        compiler_params=_cparams(("arbitrary",)),
        name="proj",
    )(*args)


def _fourier_kernel(c_ref, s_ref, a_ref, b_ref, o_ref):
    o = (jnp.dot(c_ref[...], a_ref[...], preferred_element_type=F32)
         - jnp.dot(s_ref[...], b_ref[...], preferred_element_type=F32))
    o_ref[...] = o.astype(BF16)


def _fourier(c_tab, s_tab, a, b):
    seq_len, width = a.shape
    tk = min(seq_len, 256)
    full = lambda i: (0, 0)
    return pl.pallas_call(
        _fourier_kernel,
        grid=(seq_len // tk,),
        in_specs=[
            pl.BlockSpec((tk, seq_len), lambda i: (i, 0)),
            pl.BlockSpec((tk, seq_len), lambda i: (i, 0)),
            pl.BlockSpec((seq_len, width), full, pipeline_mode=pl.Buffered(1)),
            pl.BlockSpec((seq_len, width), full, pipeline_mode=pl.Buffered(1)),
        ],
        out_specs=pl.BlockSpec((tk, width), lambda i: (i, 0)),
        out_shape=jax.ShapeDtypeStruct((seq_len, width), BF16),
        compiler_params=_cparams(("arbitrary",)),
        name="fourier",
    )(c_tab, s_tab, a, b)


def _stack_heads(qg):
    lane_head = lax.broadcasted_iota(I32, qg.shape, 1) // HEAD_DIM
    zero = jnp.zeros_like(qg)
    return jnp.concatenate([jnp.where(lane_head == h, qg, zero) for h in range(HEADS_PER_GROUP)], axis=0)


def _unstack_heads(o, rows):
    lane_head = lax.broadcasted_iota(I32, (rows, o.shape[1]), 1) // HEAD_DIM
    acc = jnp.zeros((rows, o.shape[1]), F32)
    for h in range(HEADS_PER_GROUP):
        acc = acc + jnp.where(lane_head == h, o[h * rows:(h + 1) * rows, :], 0.0)
    return acc


_NT = (((1,), (1,)), ((), ()))


def _attn_kernel(q_ref, k_ref, v_ref, kc_ref, vc_ref, *rest, rows):
    bias_refs, o_ref = rest[:ATTN_ROWS_PER_STEP], rest[ATTN_ROWS_PER_STEP]
    n_loc = NA_WIN_R * GRID_W
    for j in range(ATTN_ROWS_PER_STEP):
        r = pl.program_id(1) * ATTN_ROWS_PER_STEP + j
        rs = jnp.clip(r - NA_WIN_R // 2, 0, rows - NA_WIN_R)
        start = pl.multiple_of(rs * GRID_W, GRID_W)
        kwin = k_ref[pl.ds(start, n_loc), :]
        vwin = v_ref[pl.ds(start, n_loc), :]
        q = q_ref[j * GRID_W:(j + 1) * GRID_W, :]
        bias_ref = bias_refs[j]
        outs = []
        for g in range(NA_HEADS // HEADS_PER_GROUP):
            sl = slice(g * V7X_MXU_DIM, (g + 1) * V7X_MXU_DIM)
            qs = _stack_heads(q[:, sl])
            s_loc = lax.dot_general(qs, kwin[:, sl], _NT, preferred_element_type=F32)
            bias = bias_ref[g * HEADS_PER_GROUP:(g + 1) * HEADS_PER_GROUP].reshape(HEADS_PER_GROUP * GRID_W, n_loc)
            s_loc = s_loc + bias
            s_ctx = lax.dot_general(qs, kc_ref[:, sl], _NT, preferred_element_type=F32)
            m = jnp.maximum(jnp.max(s_loc, axis=-1, keepdims=True), jnp.max(s_ctx, axis=-1, keepdims=True))
            p_loc = jnp.exp(s_loc - m)
            p_ctx = jnp.exp(s_ctx - m)
            denom = jnp.sum(p_loc, axis=-1, keepdims=True) + jnp.sum(p_ctx, axis=-1, keepdims=True)
            o = (jnp.dot(p_loc.astype(BF16), vwin[:, sl], preferred_element_type=F32)
                 + jnp.dot(p_ctx.astype(BF16), vc_ref[:, sl], preferred_element_type=F32))
            outs.append(_unstack_heads(o / denom, GRID_W))
        o_ref[j * GRID_W:(j + 1) * GRID_W, :] = jnp.concatenate(outs, axis=1).astype(BF16)


def _attn(q, k, v, kc, vc, bias_tab, *, n_seq, seq_len, ctx_len):
    rows = seq_len // GRID_W
    rps = ATTN_ROWS_PER_STEP
    steps = rows // rps

    def bias_map(j):
        def index(b, s):
            r = s * rps + j
            rs = jnp.clip(r - NA_WIN_R // 2, 0, rows - NA_WIN_R)
            return (rs - r + NA_WIN_R - 1, 0, 0, 0)
        return index

    bias_specs = [pl.BlockSpec((None, NA_HEADS, GRID_W, NA_WIN_R * GRID_W), bias_map(j)) for j in range(rps)]
    return pl.pallas_call(
        functools.partial(_attn_kernel, rows=rows),
        grid=(n_seq, steps),
        in_specs=[
            pl.BlockSpec((rps * GRID_W, NA_WIDTH), lambda b, s: (b * steps + s, 0)),
            pl.BlockSpec((seq_len, NA_WIDTH), lambda b, s: (b, 0)),
            pl.BlockSpec((seq_len, NA_WIDTH), lambda b, s: (b, 0)),
            pl.BlockSpec((ctx_len, NA_WIDTH), lambda b, s: (b, 0)),
            pl.BlockSpec((ctx_len, NA_WIDTH), lambda b, s: (b, 0)),
        ] + bias_specs,
        out_specs=pl.BlockSpec((rps * GRID_W, NA_WIDTH), lambda b, s: (b * steps + s, 0)),
        out_shape=jax.ShapeDtypeStruct((n_seq * seq_len, NA_WIDTH), BF16),
        compiler_params=_cparams(("arbitrary", "arbitrary")),
        name="attn",
    )(q, k, v, kc, vc, *([bias_tab] * rps))


def _ctx_attn_kernel(q_ref, k_ref, v_ref, o_ref):
    q = q_ref[...]
    n = q.shape[0]
    outs = []
    for g in range(NA_HEADS // HEADS_PER_GROUP):
        sl = slice(g * V7X_MXU_DIM, (g + 1) * V7X_MXU_DIM)
        qs = _stack_heads(q[:, sl])
        s = lax.dot_general(qs, k_ref[:, sl], _NT, preferred_element_type=F32)
        m = jnp.max(s, axis=-1, keepdims=True)
        p = jnp.exp(s - m)
        denom = jnp.sum(p, axis=-1, keepdims=True)
        o = jnp.dot(p.astype(BF16), v_ref[:, sl], preferred_element_type=F32)
        outs.append(_unstack_heads(o / denom, n))
    o_ref[...] = jnp.concatenate(outs, axis=1).astype(BF16)


def _ctx_attn(q, k, v, *, n_seq, ctx_len):
    spec = pl.BlockSpec((ctx_len, NA_WIDTH), lambda b: (b, 0))
    return pl.pallas_call(
        _ctx_attn_kernel,
        grid=(n_seq,),
        in_specs=[spec, spec, spec],
        out_specs=spec,
        out_shape=jax.ShapeDtypeStruct((n_seq * ctx_len, NA_WIDTH), BF16),
        compiler_params=_cparams(("arbitrary",)),
        name="ctx_attn",
    )(q, k, v)


def _merge_kernel(x_ref, mods_ref, n1_ref, n2_ref, f_ref, at_ref, u_ref, up_ref, un_ref, bg_ref, cw_ref,
                  wg_ref, wf_ref, wna_ref, wcv_ref, wo_ref, rw_ref,
                  xo_ref, h2_ref, lg_ref, *, tiles_per_seq, fixed_row):
    i = pl.program_id(0)
    m = _mod_row(mods_ref, i, tiles_per_seq, fixed_row)
    dm = D_MODEL
    x = x_ref[...]
    h = _norm_mod(x, n1_ref[...], m[:, 0:dm], m[:, dm:2 * dm]).astype(BF16)
    gates = jax.nn.sigmoid(jnp.dot(h, wg_ref[...], preferred_element_type=F32))

    y_f = jnp.dot(f_ref[...], wf_ref[...], preferred_element_type=F32)
    y_na = jnp.dot(at_ref[...], wna_ref[...], preferred_element_type=F32)

    u = u_ref[...]
    t = u.shape[0]
    ti = i % tiles_per_seq
    row = lax.broadcasted_iota(I32, u.shape, 0)
    prev_row = jnp.where(ti == 0, 0.0, up_ref[V7X_SUBLANES - 1:V7X_SUBLANES, :])
    next_row = jnp.where(ti == tiles_per_seq - 1, 0.0, un_ref[0:1, :])
    u_prev = jnp.where(row == 0, prev_row, pltpu.roll(u, 1, 0))
    u_next = jnp.where(row == t - 1, next_row, pltpu.roll(u, t - 1, 0))
    y_cv = bg_ref[...] * (cw_ref[0:1, :] * u_prev + cw_ref[1:2, :] * u + cw_ref[2:3, :] * u_next)
    y_cv = jnp.dot(y_cv.astype(BF16), wcv_ref[...], preferred_element_type=F32)

    merged = gates[:, 0:dm] * y_f + gates[:, dm:2 * dm] * y_na + gates[:, 2 * dm:3 * dm] * y_cv
    mixed = jnp.dot(merged.astype(BF16), wo_ref[...], preferred_element_type=F32)
    x_new = x + m[:, 2 * dm:3 * dm] * mixed
    xo_ref[...] = x_new

    h2 = _norm_mod(x_new, n2_ref[...], m[:, 3 * dm:4 * dm], m[:, 4 * dm:5 * dm])
    h2_ref[...] = h2
    hi = h2.astype(BF16)
    lo = (h2 - hi.astype(F32)).astype(BF16)
    p_hi = jnp.dot(hi, rw_ref[...], preferred_element_type=F32)
    p_lo = jnp.dot(lo, rw_ref[...], preferred_element_type=F32)
    lg_ref[...] = p_hi + pltpu.roll(p_hi, V7X_LANES - N_EXPERTS, 1) + p_lo


def _merge(x2, mods_l, n1, n2, f_all, attn, u, bg, conv_w, w_gate, w_f, w_na, w_cv, w_o, rwt,
           *, seq_len, fixed_row):
    n_tok = x2.shape[0]
    tm = min(TOKEN_TILE, seq_len)
    tps = seq_len // tm
    const = lambda i: (0, 0)
    halo = tm // V7X_SUBLANES
    n_halo = n_tok // V7X_SUBLANES
    in_specs = [
        pl.BlockSpec((tm, D_MODEL), lambda i: (i, 0)),
        pl.BlockSpec((MODS_ROWS, 6 * D_MODEL), const),
        pl.BlockSpec((1, D_MODEL), const),
        pl.BlockSpec((1, D_MODEL), const),
        pl.BlockSpec((tm, F_WIDTH), lambda i: (i % tps, i // tps)),
        pl.BlockSpec((tm, NA_WIDTH), lambda i: (i, 0)),
        pl.BlockSpec((tm, CONV_WIDTH), lambda i: (i, 0)),
        pl.BlockSpec((V7X_SUBLANES, CONV_WIDTH), lambda i: (jnp.maximum(i * halo - 1, 0), 0)),
        pl.BlockSpec((V7X_SUBLANES, CONV_WIDTH), lambda i: (jnp.minimum((i + 1) * halo, n_halo - 1), 0)),
        pl.BlockSpec((tm, CONV_WIDTH), lambda i: (i, 0)),
        pl.BlockSpec((3, CONV_WIDTH), const),
        pl.BlockSpec((D_MODEL, 3 * D_MODEL), const),
        pl.BlockSpec((F_WIDTH, D_MODEL), const),
        pl.BlockSpec((NA_WIDTH, D_MODEL), const),
        pl.BlockSpec((CONV_WIDTH, D_MODEL), const),
        pl.BlockSpec((D_MODEL, D_MODEL), const),
        pl.BlockSpec((D_MODEL, V7X_LANES), const),
    ]
    out_specs = [
        pl.BlockSpec((tm, D_MODEL), lambda i: (i, 0)),
        pl.BlockSpec((tm, D_MODEL), lambda i: (i, 0)),
        pl.BlockSpec((tm, V7X_LANES), lambda i: (i, 0)),
    ]
    out_shape = [
        jax.ShapeDtypeStruct((n_tok, D_MODEL), F32),
        jax.ShapeDtypeStruct((n_tok, D_MODEL), F32),
        jax.ShapeDtypeStruct((n_tok, V7X_LANES), F32),
    ]
    return pl.pallas_call(
        functools.partial(_merge_kernel, tiles_per_seq=tps, fixed_row=fixed_row),
        grid=(n_tok // tm,),
        in_specs=in_specs,
        out_specs=out_specs,
        out_shape=out_shape,
        compiler_params=_cparams(("arbitrary",)),
        name="merge",
    )(x2, mods_l, n1, n2, f_all, attn, u, u, u, bg, conv_w, w_gate, w_f, w_na, w_cv, w_o, rwt)


def _first_max(vals):
    best = vals[0]
    idx = jnp.zeros(best.shape, I32)
    for j in range(1, len(vals)):
        better = vals[j] > best
        idx = jnp.where(better, j, idx)
        best = jnp.where(better, vals[j], best)
    return best, idx


def _select(idx, vals):
    out = vals[-1]
    for j in range(len(vals) - 2, -1, -1):
        out = jnp.where(idx == j, vals[j], out)
    return out


def _route_kernel(lg_ref, rb_ref, ids_ref, wts_ref, cnt_ref, tot_ref, run_ref):
    step = pl.program_id(0)

    @pl.when(step == 0)
    def _():
        run_ref[...] = jnp.zeros_like(run_ref)

    s = jax.nn.sigmoid(lg_ref[...].T[0:N_EXPERTS, :])
    sb = s + rb_ref[...]
    t = s.shape[1]
    s_rows = [s[e:e + 1, :] for e in range(N_EXPERTS)]
    b_rows = [sb[e:e + 1, :] for e in range(N_EXPERTS)]
    epg = EXPERTS_PER_GROUP
    gscore = []
    for g in range(N_GROUPS):
        v = b_rows[g * epg:(g + 1) * epg]
        pair = None
        for a in range(epg):
            for b in range(a + 1, epg):
                pair = v[a] + v[b] if pair is None else jnp.maximum(pair, v[a] + v[b])
        gscore.append(pair)
    _, gi = _first_max(gscore)
    bv = [_select(gi, [b_rows[g * epg + j] for g in range(N_GROUPS)]) for j in range(epg)]
    sv = [_select(gi, [s_rows[g * epg + j] for g in range(N_GROUPS)]) for j in range(epg)]
    _, i1 = _first_max(bv)
    _, i2 = _first_max([jnp.where(i1 == j, -jnp.inf, bv[j]) for j in range(epg)])
    s1 = _select(i1, sv)
    s2 = _select(i2, sv)
    tot = s1 + s2
    e1 = gi * epg + i1
    e2 = gi * epg + i2

    eid = lax.broadcasted_iota(I32, (N_EXPERTS, t), 0)
    hit1 = eid == e1
    hit2 = eid == e2
    onehot = jnp.where(hit1 | hit2, 1.0, 0.0)
    before = (lax.broadcasted_iota(I32, (t, t), 0) < lax.broadcasted_iota(I32, (t, t), 1))
    prefix = jnp.dot(onehot.astype(BF16), jnp.where(before, 1.0, 0.0).astype(BF16),
                     preferred_element_type=F32)
    r1 = jnp.sum(jnp.where(hit1, prefix, 0.0), axis=0, keepdims=True)
    r2 = jnp.sum(jnp.where(hit2, prefix, 0.0), axis=0, keepdims=True)
    grp = float(SLOT_GROUP)
    cnt = jnp.sum(onehot, axis=1, keepdims=True)
    cnt = jnp.floor((cnt + (grp - 1.0)) / grp) * grp
    run = run_ref[...] + cnt
    run_ref[...] = run
    cnt_ref[...] = jnp.broadcast_to(cnt, cnt_ref.shape)
    tot_ref[...] = jnp.broadcast_to(run, tot_ref.shape)

    zi = jnp.zeros((V7X_SUBLANES - 4, t), I32)
    ids_ref[...] = jnp.concatenate([e1, e2, r1.astype(I32), r2.astype(I32), zi], axis=0)
    zf = jnp.zeros((V7X_SUBLANES - 2, t), F32)
    wts_ref[...] = jnp.concatenate([s1 / tot, s2 / tot, zf], axis=0)


def _route(logits, router_b):
    n_tok = logits.shape[0]
    tr = ROUTE_TILE
    return pl.pallas_call(
        _route_kernel,
        grid=(n_tok // tr,),
        in_specs=[
            pl.BlockSpec((tr, V7X_LANES), lambda i: (i, 0)),
            pl.BlockSpec((N_EXPERTS, 1), lambda i: (0, 0)),
        ],
        out_specs=[
            pl.BlockSpec((V7X_SUBLANES, tr), lambda i: (0, i)),
            pl.BlockSpec((V7X_SUBLANES, tr), lambda i: (0, i)),
            pl.BlockSpec((None, N_EXPERTS, V7X_LANES), lambda i: (i, 0, 0)),
            pl.BlockSpec((N_EXPERTS, V7X_LANES), lambda i: (0, 0)),
        ],
        out_shape=[
            jax.ShapeDtypeStruct((V7X_SUBLANES, n_tok), I32),
            jax.ShapeDtypeStruct((V7X_SUBLANES, n_tok), F32),
            jax.ShapeDtypeStruct((n_tok // tr, N_EXPERTS, V7X_LANES), F32),
            jax.ShapeDtypeStruct((N_EXPERTS, V7X_LANES), F32),
        ],
        scratch_shapes=[pltpu.VMEM((N_EXPERTS, 1), F32)],
        compiler_params=_cparams(("arbitrary",)),
        name="route",
    )(logits, router_b.reshape(N_EXPERTS, 1))


def _lane_table(vals, width):
    lane = lax.broadcasted_iota(I32, (1, width), 1)
    out = jnp.zeros((1, width), F32)
    for e, v in enumerate(vals):
        out = jnp.where(lane == e, v, out)
    return out


def _slots_kernel(ids_ref, cnt_ref, tot_ref, loc_ref, gmap_ref, blk_ref, off_ref):
    step = pl.program_id(0)

    @pl.when(step == 0)
    def _():
        off_ref[...] = jnp.zeros_like(off_ref)

    blk = float(EXPERT_BLOCK)
    grp = float(SLOT_GROUP)
    cnt = cnt_ref[...][:, 0:1]
    tot = tot_ref[...][:, 0:1]
    off = off_ref[...]
    region = jnp.floor((tot + (blk - 1.0)) / blk) * blk
    starts, ends, local = [], [], []
    run = jnp.zeros((1, 1), F32)
    lrun = jnp.zeros((1, 1), F32)
    for e in range(N_EXPERTS):
        starts.append(run)
        run = run + region[e:e + 1, :]
        ends.append(run)
        local.append(lrun)
        lrun = lrun + cnt[e:e + 1, :]

    ids = ids_ref[...]
    e1, e2 = ids[0:1, :], ids[1:2, :]
    t = ids.shape[1]
    l1 = jnp.zeros((1, t), F32)
    l2 = jnp.zeros((1, t), F32)
    for e in range(N_EXPERTS):
        l1 = jnp.where(e1 == e, local[e], l1)
        l2 = jnp.where(e2 == e, local[e], l2)
    zi = jnp.zeros((V7X_SUBLANES - 2, t), I32)
    loc_ref[...] = jnp.concatenate([l1.astype(I32) + ids[2:3, :], l2.astype(I32) + ids[3:4, :], zi], axis=0)

    wg = gmap_ref.shape[1]
    first = lax.broadcasted_iota(I32, (1, wg), 1).astype(F32) * grp
    dest = jnp.zeros((1, wg), F32)
    for e in range(N_EXPERTS):
        inside = (first >= local[e]) & (first < local[e] + cnt[e:e + 1, :])
        dest = jnp.where(inside, starts[e] + off[e:e + 1, :] + (first - local[e]), dest)
    n_groups = jnp.broadcast_to(lrun / grp, (1, wg))
    zg = jnp.zeros((V7X_SUBLANES - 2, wg), I32)
    gmap_ref[...] = jnp.concatenate([(dest / grp).astype(I32), n_groups.astype(I32), zg], axis=0)
    off_ref[...] = off + cnt

    w = blk_ref.shape[1]
    first_row = lax.broadcasted_iota(I32, (1, w), 1).astype(F32) * blk
    owner = jnp.zeros((1, w), F32)
    for e in range(N_EXPERTS):
        owner = owner + jnp.where(first_row >= ends[e], 1.0, 0.0)
    owner = jnp.minimum(owner, float(N_EXPERTS - 1))
    used = jnp.broadcast_to(ends[-1] / blk, (1, w))
    pad_first = _lane_table([(starts[e] + tot[e:e + 1, :]) / grp for e in range(N_EXPERTS)], w)
    pad_count = _lane_table([(region[e:e + 1, :] - tot[e:e + 1, :]) / grp for e in range(N_EXPERTS)], w)
    zb = jnp.zeros((V7X_SUBLANES - 4, w), I32)
    blk_ref[...] = jnp.concatenate([owner.astype(I32), used.astype(I32), pad_first.astype(I32),
                                    pad_count.astype(I32), zb], axis=0)


def _slots(ids, cnt, tot, n_blocks):
    n_tok = ids.shape[1]
    tr = ROUTE_TILE
    wblk = -(-n_blocks // V7X_LANES) * V7X_LANES
    return pl.pallas_call(
        _slots_kernel,
        grid=(n_tok // tr,),
        in_specs=[
            pl.BlockSpec((V7X_SUBLANES, tr), lambda i: (0, i)),
            pl.BlockSpec((None, N_EXPERTS, V7X_LANES), lambda i: (i, 0, 0)),
            pl.BlockSpec((N_EXPERTS, V7X_LANES), lambda i: (0, 0)),
        ],
        out_specs=[
            pl.BlockSpec((V7X_SUBLANES, tr), lambda i: (0, i)),
            pl.BlockSpec((None, V7X_SUBLANES, SORT_GROUPS_PAD), lambda i: (i, 0, 0)),
            pl.BlockSpec((V7X_SUBLANES, wblk), lambda i: (0, 0)),
        ],
        out_shape=[
            jax.ShapeDtypeStruct((V7X_SUBLANES, n_tok), I32),
            jax.ShapeDtypeStruct((n_tok // tr, V7X_SUBLANES, SORT_GROUPS_PAD), I32),
            jax.ShapeDtypeStruct((V7X_SUBLANES, wblk), I32),
        ],
        scratch_shapes=[pltpu.VMEM((N_EXPERTS, 1), F32)],
        compiler_params=_cparams(("arbitrary",)),
        name="slots",
    )(ids, cnt, tot)


def _group_rows(group):
    if isinstance(group, int):
        return pl.ds(group * SLOT_GROUP, SLOT_GROUP)
    return pl.ds(pl.multiple_of(group * SLOT_GROUP, SLOT_GROUP), SLOT_GROUP)


def _group_copy(src_ref, src_group, dst_ref, dst_group, sem):
    return pltpu.make_async_copy(src_ref.at[_group_rows(src_group)], dst_ref.at[_group_rows(dst_group)], sem)


def _dispatch_kernel(gmap_ref, gprev_ref, blk_ref, loc_ref, *refs, n_first):
    if n_first is None:
        h_ref, xb_ref, sorted_ref, zero_ref, sem = refs
        second_ref = None
    else:
        h_ref, second_ref, xb_ref, sorted_ref, zero_ref, sem = refs
    step = pl.program_id(0)
    last = pl.num_programs(0) - 1
    buf = step % 2
    loc = loc_ref[...]
    slot = lax.broadcasted_iota(I32, (SORT_ROWS, loc.shape[1]), 0)
    perm = jnp.where(slot == loc[0:1, :], 1.0, jnp.where(slot == loc[1:2, :], 1.0, 0.0)).astype(BF16)

    def sort_rows(src_ref):
        sorted_ref[buf] = jnp.dot(perm, src_ref[...].astype(BF16), preferred_element_type=F32).astype(BF16)

    if second_ref is None:
        sort_rows(h_ref)
    else:
        pl.when(step < n_first)(lambda: sort_rows(h_ref))
        pl.when(step >= n_first)(lambda: sort_rows(second_ref))

    def tile_copy(map_ref, which, g):
        return _group_copy(sorted_ref.at[which], g, xb_ref, map_ref[0, g], sem.at[which])

    def start(g, c):
        tile_copy(gmap_ref, buf, g).start()
        return c

    def wait_prev(g, c):
        tile_copy(gprev_ref, 1 - buf, g).wait()
        return c

    def wait_own(g, c):
        tile_copy(gmap_ref, buf, g).wait()
        return c

    lax.fori_loop(0, gmap_ref[1, 0], start, 0)

    @pl.when(step == last)
    def _():
        zero_ref[...] = jnp.zeros_like(zero_ref)
        for e in range(N_EXPERTS):
            first = blk_ref[2, e]

            def zstart(g, c, first=first):
                _group_copy(zero_ref, 0, xb_ref, first + g, sem.at[2]).start()
                return c

            def zwait(g, c, first=first):
                _group_copy(zero_ref, 0, xb_ref, first + g, sem.at[2]).wait()
                return c

            lax.fori_loop(0, blk_ref[3, e], zstart, 0)
            lax.fori_loop(0, blk_ref[3, e], zwait, 0)

        def block_copy(b):
            rows = pl.ds(pl.multiple_of(b * EXPERT_BLOCK, EXPERT_BLOCK), EXPERT_BLOCK)
            return pltpu.make_async_copy(zero_ref, xb_ref.at[rows], sem.at[2])

        def bstart(b, c):
            block_copy(b).start()
            return c

        def bwait(b, c):
            block_copy(b).wait()
            return c

        n_blocks = xb_ref.shape[0] // EXPERT_BLOCK
        lax.fori_loop(blk_ref[1, 0], n_blocks, bstart, 0)
        lax.fori_loop(blk_ref[1, 0], n_blocks, bwait, 0)

    @pl.when(step > 0)
    def _():
        lax.fori_loop(0, gprev_ref[1, 0], wait_prev, 0)

    @pl.when(step == last)
    def _():
        lax.fori_loop(0, gmap_ref[1, 0], wait_own, 0)


def _dispatch(gmap, blk, loc, h2, h2_second, n_slots):
    tr = ROUTE_TILE
    n_first = h2.shape[0] // tr
    n_tiles = n_first
    in_specs = [
        pl.BlockSpec((None, V7X_SUBLANES, SORT_GROUPS_PAD), lambda i: (i, 0, 0), memory_space=pltpu.SMEM),
        pl.BlockSpec((None, V7X_SUBLANES, SORT_GROUPS_PAD), lambda i: (jnp.maximum(i - 1, 0), 0, 0),
                     memory_space=pltpu.SMEM),
        pl.BlockSpec(blk.shape, lambda i: (0, 0), memory_space=pltpu.SMEM),
        pl.BlockSpec((V7X_SUBLANES, tr), lambda i: (0, i)),
        pl.BlockSpec((tr, D_MODEL), lambda i: (jnp.minimum(i, n_first - 1), 0)),
    ]
    args = [gmap, gmap, blk, loc, h2]
    if h2_second is not None:
        n_tiles += h2_second.shape[0] // tr
        in_specs.append(pl.BlockSpec((tr, D_MODEL), lambda i: (jnp.maximum(i - n_first, 0), 0)))
        args.append(h2_second)
    return pl.pallas_call(
        functools.partial(_dispatch_kernel, n_first=None if h2_second is None else n_first),
        grid=(n_tiles,),
        in_specs=in_specs,
        out_specs=pl.BlockSpec(memory_space=pl.ANY),
        out_shape=jax.ShapeDtypeStruct((n_slots, D_MODEL), BF16),
        scratch_shapes=[pltpu.VMEM((2, SORT_ROWS, D_MODEL), BF16), pltpu.VMEM((EXPERT_BLOCK, D_MODEL), BF16),
                        pltpu.SemaphoreType.DMA((3,))],
        compiler_params=_cparams(("arbitrary",)),
        name="dispatch",
    )(*args)


def _experts_kernel(blk_ref, used_ref, x_ref, w1_ref, w3_ref, w2_ref, y_ref, w1b, w3b, w2b):
    i = pl.program_id(0)
    prev = blk_ref[jnp.maximum(i - 1, 0)]

    @pl.when((i == 0) | (blk_ref[i] != prev))
    def _():
        w1b[...] = w1_ref[...].astype(BF16)
        w3b[...] = w3_ref[...].astype(BF16)
        w2b[...] = w2_ref[...].astype(BF16)

    @pl.when(i < used_ref[0])
    def _():
        x = x_ref[...]
        a = jnp.dot(x, w1b[...], preferred_element_type=F32)
        b = jnp.dot(x, w3b[...], preferred_element_type=F32)
        hid = (a * jax.nn.sigmoid(a) * b).astype(BF16)
        y_ref[...] = jnp.dot(hid, w2b[...], preferred_element_type=F32).astype(BF16)

    @pl.when(i >= used_ref[0])
    def _():
        y_ref[...] = jnp.zeros_like(y_ref)


def _experts(blk_e, used, xb, w1, w3, w2, layer):
    n_slots = xb.shape[0]
    bm = EXPERT_BLOCK
    row_map = lambda i, be, nu: (jnp.minimum(i, nu[0] - 1), 0)
    w_map = lambda i, be, nu: (layer, be[i], 0, 0)
    grid_spec = pltpu.PrefetchScalarGridSpec(
        num_scalar_prefetch=2,
        grid=(n_slots // bm,),
        in_specs=[
            pl.BlockSpec((bm, D_MODEL), row_map),
            pl.BlockSpec((None, None, D_MODEL, D_EXPERT), w_map),
            pl.BlockSpec((None, None, D_MODEL, D_EXPERT), w_map),
            pl.BlockSpec((None, None, D_EXPERT, D_MODEL), w_map),
        ],
        out_specs=pl.BlockSpec((bm, D_MODEL), lambda i, be, nu: (i, 0)),
        scratch_shapes=[pltpu.VMEM((D_MODEL, D_EXPERT), BF16), pltpu.VMEM((D_MODEL, D_EXPERT), BF16),
                        pltpu.VMEM((D_EXPERT, D_MODEL), BF16)],
    )
    return pl.pallas_call(
        _experts_kernel,
        grid_spec=grid_spec,
        out_shape=jax.ShapeDtypeStruct((n_slots, D_MODEL), BF16),
        compiler_params=_cparams(("arbitrary",)),
        name="experts",
    )(blk_e, used, xb, w1, w3, w2)


_TN = (((0,), (0,)), ((), ()))


def _combine_kernel(gmap_ref, gnext_ref, loc_ref, wts_ref, x_ref, mods_ref, yb_ref, o_ref, ys_ref, sem,
                    *, tiles_per_seq, fixed_row):
    step = pl.program_id(0)
    buf = step % 2

    def fetch(map_ref, which, g):
        return _group_copy(yb_ref, map_ref[0, g], ys_ref.at[which], g, sem.at[which])

    def start_own(g, c):
        fetch(gmap_ref, buf, g).start()
        return c

    def start_next(g, c):
        fetch(gnext_ref, 1 - buf, g).start()
        return c

    def wait_own(g, c):
        fetch(gmap_ref, buf, g).wait()
        return c

    @pl.when(step == 0)
    def _():
        ys_ref[...] = jnp.zeros_like(ys_ref)
        lax.fori_loop(0, gmap_ref[1, 0], start_own, 0)

    @pl.when(step + 1 < pl.num_programs(0))
    def _():
        lax.fori_loop(0, gnext_ref[1, 0], start_next, 0)

    loc = loc_ref[...]
    wts = wts_ref[...]
    slot = lax.broadcasted_iota(I32, (SORT_ROWS, loc.shape[1]), 0)
    perm = jnp.where(slot == loc[0:1, :], wts[0:1, :], jnp.where(slot == loc[1:2, :], wts[1:2, :], 0.0))
    lax.fori_loop(0, gmap_ref[1, 0], wait_own, 0)
    y = lax.dot_general(perm.astype(BF16), ys_ref[buf], _TN, preferred_element_type=F32)
    m = _mod_row(mods_ref, step, tiles_per_seq, fixed_row)
    o_ref[...] = x_ref[...] + m[:, 5 * D_MODEL:6 * D_MODEL] * y


def _combine(gmap, loc, wts, x_new, mods_l, yb, *, seq_len, fixed_row, tile_offset):
    n_tok = x_new.shape[0]
    tr = ROUTE_TILE
    tps = max(seq_len // tr, 1)
    n_tiles = n_tok // tr
    return pl.pallas_call(
        functools.partial(_combine_kernel, tiles_per_seq=tps, fixed_row=fixed_row),
        grid=(n_tiles,),
        in_specs=[
            pl.BlockSpec((None, V7X_SUBLANES, SORT_GROUPS_PAD), lambda i: (i + tile_offset, 0, 0),
                         memory_space=pltpu.SMEM),
            pl.BlockSpec((None, V7X_SUBLANES, SORT_GROUPS_PAD),
                         lambda i: (jnp.minimum(i + 1, n_tiles - 1) + tile_offset, 0, 0), memory_space=pltpu.SMEM),
            pl.BlockSpec((V7X_SUBLANES, tr), lambda i: (0, i + tile_offset)),
            pl.BlockSpec((V7X_SUBLANES, tr), lambda i: (0, i + tile_offset)),
            pl.BlockSpec((tr, D_MODEL), lambda i: (i, 0)),
            pl.BlockSpec((MODS_ROWS, 6 * D_MODEL), lambda i: (0, 0)),
            pl.BlockSpec(memory_space=pl.ANY),
        ],
        out_specs=pl.BlockSpec((tr, D_MODEL), lambda i: (i, 0)),
        out_shape=jax.ShapeDtypeStruct((n_tok, D_MODEL), F32),
        scratch_shapes=[pltpu.VMEM((2, SORT_ROWS, D_MODEL), BF16), pltpu.SemaphoreType.DMA((2,))],
        compiler_params=_cparams(("arbitrary",)),
        name="combine",
    )(gmap, gmap, loc, wts, x_new, mods_l, yb)


def _channel_dft_tables():
    j = np.arange(F_GDIM)
    ang = 2.0 * np.pi * ((j[:, None] * j[None, :]) % F_GDIM) / F_GDIM
    eye = np.eye(F_GROUPS)
    return (jnp.asarray(np.kron(eye, np.cos(ang)), F32).astype(BF16),
            jnp.asarray(np.kron(eye, np.sin(ang)), F32).astype(BF16))


def _position_dft_tables(seq_len):
    scale = 1.0 / math.sqrt(seq_len * F_GDIM)
    k = np.arange(seq_len, dtype=np.int64)
    if seq_len <= 256:
        ang = 2.0 * np.pi * ((k[:, None] * k[None, :]) % seq_len) / seq_len
        return (jnp.asarray(np.cos(ang) * scale, F32).astype(BF16),
                jnp.asarray(np.sin(ang) * scale, F32).astype(BF16))
    hi = seq_len // 64
    t1 = np.arange(hi, dtype=np.int64)
    t0 = np.arange(64, dtype=np.int64)
    ang_a = 2.0 * np.pi * ((k[:, None] * t1[None, :] * 64) % seq_len) / seq_len
    ang_b = 2.0 * np.pi * ((k[:, None] * t0[None, :]) % seq_len) / seq_len
    ca = jnp.asarray(np.cos(ang_a) * scale, F32)[:, :, None]
    sa = jnp.asarray(np.sin(ang_a) * scale, F32)[:, :, None]
    cb = jnp.asarray(np.cos(ang_b), F32)[:, None, :]
    sb = jnp.asarray(np.sin(ang_b), F32)[:, None, :]
    c = (ca * cb - sa * sb).reshape(seq_len, seq_len).astype(BF16)
    s = (sa * cb + ca * sb).reshape(seq_len, seq_len).astype(BF16)
    return c, s


def _rope_tables(seq_len):
    t = np.arange(seq_len)
    row = (t // GRID_W).astype(np.float64)
    col = (t % GRID_W).astype(np.float64)
    inv = np.power(ROPE_BASE, -np.arange(ROPE_PER_AXIS, dtype=np.float64) / ROPE_PER_AXIS)
    ang = np.concatenate([row[:, None] * inv, col[:, None] * inv], axis=-1)
    cos = np.cos(ang)
    sin = np.sin(ang)
    cos_h = np.concatenate([cos, cos], axis=-1)
    sin_h = np.concatenate([-sin, sin], axis=-1)
    return (jnp.asarray(np.tile(cos_h, (1, NA_HEADS)), F32), jnp.asarray(np.tile(sin_h, (1, NA_HEADS)), F32))


def _bias_table(rpb_l):
    col = np.arange(GRID_W)
    col_start = np.clip(col - NA_WIN_C // 2, 0, GRID_W - NA_WIN_C)
    col_mask = (col[None, :] >= col_start[:, None]) & (col[None, :] < col_start[:, None] + NA_WIN_C)
    dc = np.clip(col[None, :] - col[:, None] + (NA_WIN_C - 1), 0, 2 * NA_WIN_C - 2)
    n_dc = 2 * NA_WIN_C - 1
    pick = (dc.reshape(-1)[None, :] == np.arange(n_dc)[:, None]).astype(np.float32)
    e = jnp.dot(rpb_l.reshape(-1, n_dc), jnp.asarray(pick), precision=HIGHEST)
    e = e.reshape(NA_HEADS, 2 * NA_WIN_R - 1, GRID_W, GRID_W)
    e = jnp.where(jnp.asarray(col_mask)[None, None], e, NEG_BIG)
    b = jnp.stack([e[:, o:o + NA_WIN_R] for o in range(NA_WIN_R)], axis=0)
    b = b.transpose(0, 1, 3, 2, 4)
    return b.reshape(NA_WIN_R, NA_HEADS, GRID_W, NA_WIN_R * GRID_W)


def _moe(h2, h2_second, logits, w1, w3, w2, layer, router_b):
    n_tok = logits.shape[0]
    n_tiles = n_tok // ROUTE_TILE
    max_rows = 2 * n_tok + N_EXPERTS * n_tiles * (SLOT_GROUP - 1) + N_EXPERTS * (EXPERT_BLOCK - 1)
    n_blocks = -(-max_rows // EXPERT_BLOCK)
    n_slots = n_blocks * EXPERT_BLOCK
    ids, wts, cnt, tot = _route(logits, router_b)
    loc, gmap, blk = _slots(ids, cnt, tot, n_blocks)
    xb = _dispatch(gmap, blk, loc, h2, h2_second, n_slots)
    yb = _experts(blk[0, :n_blocks], blk[1, 0:1], xb, w1, w3, w2, layer)
    return yb, gmap, loc, wts


def kernel(x, c, ctx, c_ctx, ada_w, ada_b, norm1_g, w_in, qn_g, kn_g, rpb, conv_w, w_f, w_na, w_cv, w_o,
           norm2_g, router_w, router_b, w1, w3, w2):
    bsz, seq_len, d = x.shape
    ctx_len = ctx.shape[1]
    n_lat = bsz * seq_len
    n_ctx = bsz * ctx_len
    ctx_row = bsz

    c8 = jnp.concatenate([c, c_ctx[None, :], jnp.zeros((MODS_ROWS - bsz - 1, d), F32)], axis=0)
    mods = _mods(c8, ada_w, ada_b)

    cbd, sbd = _channel_dft_tables()
    c_lat, s_lat = _position_dft_tables(seq_len)
    c_ctx_t, s_ctx_t = _position_dft_tables(ctx_len)
    cos_t, sin_t = _rope_tables(seq_len)
    mavg = jnp.asarray(np.kron(np.eye(NA_HEADS), np.full((HEAD_DIM, HEAD_DIM), 1.0 / HEAD_DIM)), F32).astype(BF16)
    rw_hi = router_w.astype(BF16)
    rw_lo = (router_w - rw_hi.astype(F32)).astype(BF16)
    rwt = jnp.concatenate([rw_hi, rw_lo, jnp.zeros((d, V7X_LANES - 2 * N_EXPERTS), BF16)], axis=1)

    xl = x.reshape(n_lat, d)
    xc = ctx.reshape(n_ctx, d)
    for l in range(DEPTH):
        last = l == DEPTH - 1
        w_proj = w_in[l][:, :COL_G].astype(BF16)
        w_gate = w_in[l][:, COL_G:].astype(BF16)
        wf, wna, wcv, wo = (w_f[l].astype(BF16), w_na[l].astype(BF16), w_cv[l].astype(BF16), w_o[l].astype(BF16))
        n1 = norm1_g[l].reshape(1, d)
        n2 = norm2_g[l].reshape(1, d)
        qg = jnp.tile(qn_g[l], NA_HEADS).reshape(1, NA_WIDTH)
        kg = jnp.tile(kn_g[l], NA_HEADS).reshape(1, NA_WIDTH)
        bias_tab = _bias_table(rpb[l])
        mods_l = mods[l]

        a_c, b_c, q_c, k_c, v_c, u_c, bg_c = _proj(
            xc, mods_l, n1, w_proj, qg, kg, mavg, cbd, sbd, None, None,
            seq_len=ctx_len, n_seq=bsz, fixed_row=ctx_row)
        a_l, b_l, q_l, k_l, v_l, u_l, bg_l = _proj(
            xl, mods_l, n1, w_proj, qg, kg, mavg, cbd, sbd, cos_t, sin_t,
            seq_len=seq_len, n_seq=bsz, fixed_row=None)

        f_l = _fourier(c_lat, s_lat, a_l, b_l)
        attn_l = _attn(q_l, k_l, v_l, k_c, v_c, bias_tab, n_seq=bsz, seq_len=seq_len, ctx_len=ctx_len)
        xl_new, h2_l, lg_l = _merge(xl, mods_l, n1, n2, f_l, attn_l, u_l, bg_l, conv_w[l], w_gate,
                                    wf, wna, wcv, wo, rwt, seq_len=seq_len, fixed_row=None)
        if last:
            yb, gmap, loc, wts = _moe(h2_l, None, lg_l, w1, w3, w2, l, router_b)
            xl = _combine(gmap, loc, wts, xl_new, mods_l, yb, seq_len=seq_len, fixed_row=None, tile_offset=0)
        else:
            f_c = _fourier(c_ctx_t, s_ctx_t, a_c, b_c)
            attn_c = _ctx_attn(q_c, k_c, v_c, n_seq=bsz, ctx_len=ctx_len)
            xc_new, h2_c, lg_c = _merge(xc, mods_l, n1, n2, f_c, attn_c, u_c, bg_c, conv_w[l], w_gate,
                                        wf, wna, wcv, wo, rwt, seq_len=ctx_len, fixed_row=ctx_row)
            lg = jnp.concatenate([lg_l, lg_c], axis=0)
            yb, gmap, loc, wts = _moe(h2_l, h2_c, lg, w1, w3, w2, l, router_b)
            xl = _combine(gmap, loc, wts, xl_new, mods_l, yb, seq_len=seq_len, fixed_row=None, tile_offset=0)
            xc = _combine(gmap, loc, wts, xc_new, mods_l, yb, seq_len=ctx_len, fixed_row=ctx_row,
                          tile_offset=n_lat // ROUTE_TILE)
    return xl.reshape(bsz, seq_len, d)
```

```python
import functools
import math

import numpy as np
import jax
import jax.numpy as jnp
from jax import lax
from jax.experimental import pallas as pl
from jax.experimental.pallas import tpu as pltpu

F32 = jnp.float32
BF16 = jnp.bfloat16
I32 = jnp.int32
HIGHEST = lax.Precision.HIGHEST

D_MODEL = 1024
DEPTH = 2
GRID_W = 64
EPS = 1e-6
F_GROUPS = 4
F_GDIM = 64
F_WIDTH = 256
NA_HEADS = 8
HEAD_DIM = 64
NA_WIDTH = 512
NA_WIN_R = 8
NA_WIN_C = 16
ATTN_SCALE = HEAD_DIM ** -0.5
ROPE_BASE = 10000.0
ROPE_PER_AXIS = HEAD_DIM // 4
CONV_WIDTH = 256
COL_Q = 256
COL_K = 768
COL_V = 1280
COL_CX = 1792
COL_CB = 2048
COL_CC = 2304
COL_G = 2560
N_EXPERTS = 16
N_GROUPS = 4
EXPERTS_PER_GROUP = 4
D_EXPERT = 512

V7X_LANES = 128
V7X_SUBLANES = 8
V7X_MXU_DIM = 256

TOKEN_TILE = 512
ROUTE_TILE = 512
EXPERT_BLOCK = 512
SLOT_GROUP = 2 * V7X_SUBLANES
SORT_ROWS = -(-(2 * ROUTE_TILE + N_EXPERTS * (SLOT_GROUP - 1)) // V7X_LANES) * V7X_LANES
SORT_GROUPS_PAD = -(-(SORT_ROWS // SLOT_GROUP) // V7X_LANES) * V7X_LANES
HEADS_PER_GROUP = V7X_MXU_DIM // HEAD_DIM
ATTN_ROWS_PER_STEP = 8
FFT_RADIX = 64
FFT_STEP = V7X_SUBLANES
NEG_BIG = -1e30
MODS_ROWS = 8
VMEM_LIMIT = 48 * 1024 * 1024


def _cparams(sem):
    return pltpu.CompilerParams(dimension_semantics=sem, vmem_limit_bytes=VMEM_LIMIT)


def _mods_kernel(c_ref, w_ref, b_ref, o_ref):
    c = c_ref[...]
    sc = c * jax.nn.sigmoid(c)
    o_ref[...] = jnp.dot(sc, w_ref[...], precision=HIGHEST, preferred_element_type=F32) + b_ref[...]


def _mods(c8, ada_w, ada_b):
    nb = 1536
    return pl.pallas_call(
        _mods_kernel,
        grid=(DEPTH, 6 * D_MODEL // nb),
        in_specs=[
            pl.BlockSpec((MODS_ROWS, D_MODEL), lambda l, j: (0, 0)),
            pl.BlockSpec((None, D_MODEL, nb), lambda l, j: (l, 0, j)),
            pl.BlockSpec((None, 1, nb), lambda l, j: (l, 0, j)),
        ],
        out_specs=pl.BlockSpec((None, MODS_ROWS, nb), lambda l, j: (l, 0, j)),
        out_shape=jax.ShapeDtypeStruct((DEPTH, MODS_ROWS, 6 * D_MODEL), F32),
        compiler_params=_cparams(("arbitrary", "arbitrary")),
        name="mods",
    )(c8, ada_w, ada_b.reshape(DEPTH, 1, 6 * D_MODEL))


def _norm_mod(x, g, shift, scale):
    ms = jnp.mean(x * x, axis=-1, keepdims=True)
    return (x * lax.rsqrt(ms + EPS) * g) * (1.0 + scale) + shift


def _mod_row(mods_ref, tile, tiles_per_seq, fixed_row):
    row = fixed_row if fixed_row is not None else tile // tiles_per_seq
    return mods_ref[pl.ds(row, 1), :]


def _proj_kernel(*refs, tiles_per_seq, fixed_row, rope):
    if rope:
        (x_ref, mods_ref, g_ref, w_ref, qg_ref, kg_ref, mavg_ref, cbd_ref, sbd_ref, cos_ref, sin_ref,
         a_ref, b_ref, q_ref, k_ref, v_ref, u_ref, bg_ref) = refs
    else:
        (x_ref, mods_ref, g_ref, w_ref, qg_ref, kg_ref, mavg_ref, cbd_ref, sbd_ref,
         a_ref, b_ref, q_ref, k_ref, v_ref, u_ref, bg_ref) = refs
    m = _mod_row(mods_ref, pl.program_id(0), tiles_per_seq, fixed_row)
    h = _norm_mod(x_ref[...], g_ref[...], m[:, 0:D_MODEL], m[:, D_MODEL:2 * D_MODEL])
    p = jnp.dot(h.astype(BF16), w_ref[...], preferred_element_type=F32)

    uf = p[:, 0:COL_Q].astype(BF16)
    a_ref[...] = jnp.dot(uf, cbd_ref[...], preferred_element_type=F32).astype(a_ref.dtype)
    b_ref[...] = jnp.dot(uf, sbd_ref[...], preferred_element_type=F32).astype(b_ref.dtype)

    def head_norm(t, g):
        ms = jnp.dot((t * t).astype(BF16), mavg_ref[...], preferred_element_type=F32)
        return t * lax.rsqrt(ms + EPS) * g

    def rotate(t):
        n = t.shape[-1]
        lane = lax.broadcasted_iota(I32, t.shape, 1)
        first_half = (lane % HEAD_DIM) < (HEAD_DIM // 2)
        swapped = jnp.where(first_half, pltpu.roll(t, n - HEAD_DIM // 2, 1), pltpu.roll(t, HEAD_DIM // 2, 1))
        return t * cos_ref[...] + swapped * sin_ref[...]

    q = head_norm(p[:, COL_Q:COL_K], qg_ref[...])
    k = head_norm(p[:, COL_K:COL_V], kg_ref[...])
    if rope:
        q = rotate(q)
        k = rotate(k)
    q_ref[...] = (q * ATTN_SCALE).astype(BF16)
    k_ref[...] = k.astype(BF16)
    v_ref[...] = p[:, COL_V:COL_CX].astype(BF16)
    u_ref[...] = p[:, COL_CC:COL_G] * p[:, COL_CX:COL_CB]
    bg_ref[...] = p[:, COL_CB:COL_CC]


def _proj(x2, mods_l, norm_g, w_proj, qg, kg, mavg, cbd, sbd, cos_t, sin_t, *, seq_len, n_seq, fixed_row,
          dft_dtype):
    n_tok = x2.shape[0]
    tm = min(TOKEN_TILE, seq_len)
    tps = seq_len // tm
    rope = cos_t is not None
    const = lambda i: (0, 0)
    in_specs = [
        pl.BlockSpec((tm, D_MODEL), lambda i: (i, 0)),
        pl.BlockSpec((MODS_ROWS, 6 * D_MODEL), const),
        pl.BlockSpec((1, D_MODEL), const),
        pl.BlockSpec((D_MODEL, COL_G), const),
        pl.BlockSpec((1, NA_WIDTH), const),
        pl.BlockSpec((1, NA_WIDTH), const),
        pl.BlockSpec((NA_WIDTH, NA_WIDTH), const),
        pl.BlockSpec((F_WIDTH, F_WIDTH), const),
        pl.BlockSpec((F_WIDTH, F_WIDTH), const),
    ]
    args = [x2, mods_l, norm_g, w_proj, qg, kg, mavg, cbd, sbd]
    if rope:
        in_specs += [pl.BlockSpec((tm, NA_WIDTH), lambda i: (i % tps, 0))] * 2
        args += [cos_t, sin_t]
    tok = lambda w: pl.BlockSpec((tm, w), lambda i: (i, 0))
    fmap = pl.BlockSpec((tm, F_WIDTH), lambda i: (i % tps, i // tps))
    out_specs = [fmap, fmap, tok(NA_WIDTH), tok(NA_WIDTH), tok(NA_WIDTH), tok(CONV_WIDTH), tok(CONV_WIDTH)]
    out_shape = [
        jax.ShapeDtypeStruct((seq_len, n_seq * F_WIDTH), dft_dtype),
        jax.ShapeDtypeStruct((seq_len, n_seq * F_WIDTH), dft_dtype),
        jax.ShapeDtypeStruct((n_tok, NA_WIDTH), BF16),
        jax.ShapeDtypeStruct((n_tok, NA_WIDTH), BF16),
        jax.ShapeDtypeStruct((n_tok, NA_WIDTH), BF16),
        jax.ShapeDtypeStruct((n_tok, CONV_WIDTH), F32),
        jax.ShapeDtypeStruct((n_tok, CONV_WIDTH), F32),
    ]
    return pl.pallas_call(
        functools.partial(_proj_kernel, tiles_per_seq=tps, fixed_row=fixed_row, rope=rope),
        grid=(n_tok // tm,),
        in_specs=in_specs,
        out_specs=out_specs,
        out_shape=out_shape,
        compiler_params=_cparams(("arbitrary",)),
        name="proj",
    )(*args)


def _fourier_kernel(c_ref, s_ref, a_ref, b_ref, o_ref):
    o = (jnp.dot(c_ref[...], a_ref[...], preferred_element_type=F32)
         - jnp.dot(s_ref[...], b_ref[...], preferred_element_type=F32))
    o_ref[...] = o.astype(BF16)


def _fourier(c_tab, s_tab, a, b):
    seq_len, width = a.shape
    tk = min(seq_len, 256)
    full = lambda i: (0, 0)
    return pl.pallas_call(
        _fourier_kernel,
        grid=(seq_len // tk,),
        in_specs=[
            pl.BlockSpec((tk, seq_len), lambda i: (i, 0)),
            pl.BlockSpec((tk, seq_len), lambda i: (i, 0)),
            pl.BlockSpec((seq_len, width), full, pipeline_mode=pl.Buffered(1)),
            pl.BlockSpec((seq_len, width), full, pipeline_mode=pl.Buffered(1)),
        ],
        out_specs=pl.BlockSpec((tk, width), lambda i: (i, 0)),
        out_shape=jax.ShapeDtypeStruct((seq_len, width), BF16),
        compiler_params=_cparams(("arbitrary",)),
        name="fourier",
    )(c_tab, s_tab, a, b)


def _fft1_kernel(a_ref, b_ref, fst_ref, tc_ref, ts_ref, zr_ref, zi_ref):
    fst = fst_ref[...]
    r = FFT_RADIX
    for i in range(FFT_STEP):
        r1 = jnp.dot(fst, a_ref[:, i, :].astype(BF16), preferred_element_type=F32)
        r2 = jnp.dot(fst, b_ref[:, i, :].astype(BF16), preferred_element_type=F32)
        yr = r1[0:r] - r2[r:2 * r]
        yi = -(r2[0:r] + r1[r:2 * r])
        tc = tc_ref[i][:, 0:1]
        ts = ts_ref[i][:, 0:1]
        zr_ref[:, i, :] = yr * tc + yi * ts
        zi_ref[:, i, :] = yi * tc - yr * ts


def _fft2_kernel(zr_ref, zi_ref, g_ref, f_ref):
    g = g_ref[...]
    for i in range(FFT_STEP):
        zz = jnp.concatenate([zr_ref[i], zi_ref[i]], axis=0).astype(BF16)
        f_ref[:, i, :] = jnp.dot(g, zz, preferred_element_type=F32)


def _fourier_two_stage(a, b, fst, g, tc3, ts3):
    seq_len, width = a.shape
    r = FFT_RADIX
    a3 = a.reshape(r, r, width)
    b3 = b.reshape(r, r, width)
    steps = r // FFT_STEP
    col_blk = pl.BlockSpec((r, FFT_STEP, width), lambda j: (0, j, 0))
    row_blk = pl.BlockSpec((FFT_STEP, r, width), lambda j: (j, 0, 0))
    tw_blk = pl.BlockSpec((FFT_STEP, r, V7X_LANES), lambda j: (j, 0, 0))
    z_shape = jax.ShapeDtypeStruct((r, r, width), F32)
    zr, zi = pl.pallas_call(
        _fft1_kernel,
        grid=(steps,),
        in_specs=[col_blk, col_blk, pl.BlockSpec((2 * r, r), lambda j: (0, 0)), tw_blk, tw_blk],
        out_specs=[col_blk, col_blk],
        out_shape=[z_shape, z_shape],
        compiler_params=_cparams(("arbitrary",)),
        name="fft1",
    )(a3, b3, fst, tc3, ts3)
    f3 = pl.pallas_call(
        _fft2_kernel,
        grid=(steps,),
        in_specs=[row_blk, row_blk, pl.BlockSpec((r, 2 * r), lambda j: (0, 0))],
        out_specs=col_blk,
        out_shape=z_shape,
        compiler_params=_cparams(("arbitrary",)),
        name="fft2",
    )(zr, zi, g)
    return f3.reshape(seq_len, width)


def _stack_heads(qg):
    lane_head = lax.broadcasted_iota(I32, qg.shape, 1) // HEAD_DIM
    zero = jnp.zeros_like(qg)
    return jnp.concatenate([jnp.where(lane_head == h, qg, zero) for h in range(HEADS_PER_GROUP)], axis=0)


def _unstack_heads(o, rows):
    lane_head = lax.broadcasted_iota(I32, (rows, o.shape[1]), 1) // HEAD_DIM
    acc = jnp.zeros((rows, o.shape[1]), F32)
    for h in range(HEADS_PER_GROUP):
        acc = acc + jnp.where(lane_head == h, o[h * rows:(h + 1) * rows, :], 0.0)
    return acc


_NT = (((1,), (1,)), ((), ()))


def _attn_kernel(q_ref, k_ref, v_ref, kc_ref, vc_ref, *rest, rows):
    bias_refs, o_ref = rest[:ATTN_ROWS_PER_STEP], rest[ATTN_ROWS_PER_STEP]
    n_loc = NA_WIN_R * GRID_W
    for j in range(ATTN_ROWS_PER_STEP):
        r = pl.program_id(1) * ATTN_ROWS_PER_STEP + j
        rs = jnp.clip(r - NA_WIN_R // 2, 0, rows - NA_WIN_R)
        start = pl.multiple_of(rs * GRID_W, GRID_W)
        kwin = k_ref[pl.ds(start, n_loc), :]
        vwin = v_ref[pl.ds(start, n_loc), :]
        q = q_ref[j * GRID_W:(j + 1) * GRID_W, :]
        bias_ref = bias_refs[j]
        outs = []
        for g in range(NA_HEADS // HEADS_PER_GROUP):
            sl = slice(g * V7X_MXU_DIM, (g + 1) * V7X_MXU_DIM)
            qs = _stack_heads(q[:, sl])
            s_loc = lax.dot_general(qs, kwin[:, sl], _NT, preferred_element_type=F32)
            bias = bias_ref[g * HEADS_PER_GROUP:(g + 1) * HEADS_PER_GROUP].reshape(HEADS_PER_GROUP * GRID_W, n_loc)
            s_loc = s_loc + bias
            s_ctx = lax.dot_general(qs, kc_ref[:, sl], _NT, preferred_element_type=F32)
            m = jnp.maximum(jnp.max(s_loc, axis=-1, keepdims=True), jnp.max(s_ctx, axis=-1, keepdims=True))
            p_loc = jnp.exp(s_loc - m)
            p_ctx = jnp.exp(s_ctx - m)
            denom = jnp.sum(p_loc, axis=-1, keepdims=True) + jnp.sum(p_ctx, axis=-1, keepdims=True)
            o = (jnp.dot(p_loc.astype(BF16), vwin[:, sl], preferred_element_type=F32)
                 + jnp.dot(p_ctx.astype(BF16), vc_ref[:, sl], preferred_element_type=F32))
            outs.append(_unstack_heads(o / denom, GRID_W))
        o_ref[j * GRID_W:(j + 1) * GRID_W, :] = jnp.concatenate(outs, axis=1).astype(BF16)


def _attn(q, k, v, kc, vc, bias_tab, *, n_seq, seq_len, ctx_len):
    rows = seq_len // GRID_W
    rps = ATTN_ROWS_PER_STEP
    steps = rows // rps

    def bias_map(j):
        def index(b, s):
            r = s * rps + j
            rs = jnp.clip(r - NA_WIN_R // 2, 0, rows - NA_WIN_R)
            return (rs - r + NA_WIN_R - 1, 0, 0, 0)
        return index

    bias_specs = [pl.BlockSpec((None, NA_HEADS, GRID_W, NA_WIN_R * GRID_W), bias_map(j)) for j in range(rps)]
    return pl.pallas_call(
        functools.partial(_attn_kernel, rows=rows),
        grid=(n_seq, steps),
        in_specs=[
            pl.BlockSpec((rps * GRID_W, NA_WIDTH), lambda b, s: (b * steps + s, 0)),
            pl.BlockSpec((seq_len, NA_WIDTH), lambda b, s: (b, 0)),
            pl.BlockSpec((seq_len, NA_WIDTH), lambda b, s: (b, 0)),
            pl.BlockSpec((ctx_len, NA_WIDTH), lambda b, s: (b, 0)),
            pl.BlockSpec((ctx_len, NA_WIDTH), lambda b, s: (b, 0)),
        ] + bias_specs,
        out_specs=pl.BlockSpec((rps * GRID_W, NA_WIDTH), lambda b, s: (b * steps + s, 0)),
        out_shape=jax.ShapeDtypeStruct((n_seq * seq_len, NA_WIDTH), BF16),
        compiler_params=_cparams(("arbitrary", "arbitrary")),
        name="attn",
    )(q, k, v, kc, vc, *([bias_tab] * rps))


def _ctx_attn_kernel(q_ref, k_ref, v_ref, o_ref):
    q = q_ref[...]
    n = q.shape[0]
    outs = []
    for g in range(NA_HEADS // HEADS_PER_GROUP):
        sl = slice(g * V7X_MXU_DIM, (g + 1) * V7X_MXU_DIM)
        qs = _stack_heads(q[:, sl])
        s = lax.dot_general(qs, k_ref[:, sl], _NT, preferred_element_type=F32)
        m = jnp.max(s, axis=-1, keepdims=True)
        p = jnp.exp(s - m)
        denom = jnp.sum(p, axis=-1, keepdims=True)
        o = jnp.dot(p.astype(BF16), v_ref[:, sl], preferred_element_type=F32)
        outs.append(_unstack_heads(o / denom, n))
    o_ref[...] = jnp.concatenate(outs, axis=1).astype(BF16)


def _ctx_attn(q, k, v, *, n_seq, ctx_len):
    spec = pl.BlockSpec((ctx_len, NA_WIDTH), lambda b: (b, 0))
    return pl.pallas_call(
        _ctx_attn_kernel,
        grid=(n_seq,),
        in_specs=[spec, spec, spec],
        out_specs=spec,
        out_shape=jax.ShapeDtypeStruct((n_seq * ctx_len, NA_WIDTH), BF16),
        compiler_params=_cparams(("arbitrary",)),
        name="ctx_attn",
    )(q, k, v)


def _merge_kernel(x_ref, mods_ref, n1_ref, n2_ref, f_ref, at_ref, u_ref, up_ref, un_ref, bg_ref, cw_ref,
                  wg_ref, wf_ref, wna_ref, wcv_ref, wo_ref, rw_ref,
                  xo_ref, h2_ref, lg_ref, *, tiles_per_seq, fixed_row):
    i = pl.program_id(0)
    m = _mod_row(mods_ref, i, tiles_per_seq, fixed_row)
    dm = D_MODEL
    x = x_ref[...]
    h = _norm_mod(x, n1_ref[...], m[:, 0:dm], m[:, dm:2 * dm]).astype(BF16)
    gates = jax.nn.sigmoid(jnp.dot(h, wg_ref[...], preferred_element_type=F32))

    y_f = jnp.dot(f_ref[...].astype(BF16), wf_ref[...], preferred_element_type=F32)
    y_na = jnp.dot(at_ref[...], wna_ref[...], preferred_element_type=F32)

    u = u_ref[...]
    t = u.shape[0]
    ti = i % tiles_per_seq
    row = lax.broadcasted_iota(I32, u.shape, 0)
    prev_row = jnp.where(ti == 0, 0.0, up_ref[V7X_SUBLANES - 1:V7X_SUBLANES, :])
    next_row = jnp.where(ti == tiles_per_seq - 1, 0.0, un_ref[0:1, :])
    u_prev = jnp.where(row == 0, prev_row, pltpu.roll(u, 1, 0))
    u_next = jnp.where(row == t - 1, next_row, pltpu.roll(u, t - 1, 0))
    y_cv = bg_ref[...] * (cw_ref[0:1, :] * u_prev + cw_ref[1:2, :] * u + cw_ref[2:3, :] * u_next)
    y_cv = jnp.dot(y_cv.astype(BF16), wcv_ref[...], preferred_element_type=F32)

    merged = gates[:, 0:dm] * y_f + gates[:, dm:2 * dm] * y_na + gates[:, 2 * dm:3 * dm] * y_cv
    mixed = jnp.dot(merged.astype(BF16), wo_ref[...], preferred_element_type=F32)
    x_new = x + m[:, 2 * dm:3 * dm] * mixed
    xo_ref[...] = x_new

    h2 = _norm_mod(x_new, n2_ref[...], m[:, 3 * dm:4 * dm], m[:, 4 * dm:5 * dm])
    h2_ref[...] = h2
    hi = h2.astype(BF16)
    lo = (h2 - hi.astype(F32)).astype(BF16)
    p_hi = jnp.dot(hi, rw_ref[...], preferred_element_type=F32)
    p_lo = jnp.dot(lo, rw_ref[...], preferred_element_type=F32)
    lg_ref[...] = p_hi + pltpu.roll(p_hi, V7X_LANES - N_EXPERTS, 1) + p_lo


def _merge(x2, mods_l, n1, n2, f_all, attn, u, bg, conv_w, w_gate, w_f, w_na, w_cv, w_o, rwt,
           *, seq_len, fixed_row):
    n_tok = x2.shape[0]
    tm = min(TOKEN_TILE, seq_len)
    tps = seq_len // tm
    const = lambda i: (0, 0)
    halo = tm // V7X_SUBLANES
    n_halo = n_tok // V7X_SUBLANES
    in_specs = [
        pl.BlockSpec((tm, D_MODEL), lambda i: (i, 0)),
        pl.BlockSpec((MODS_ROWS, 6 * D_MODEL), const),
        pl.BlockSpec((1, D_MODEL), const),
        pl.BlockSpec((1, D_MODEL), const),
        pl.BlockSpec((tm, F_WIDTH), lambda i: (i % tps, i // tps)),
        pl.BlockSpec((tm, NA_WIDTH), lambda i: (i, 0)),
        pl.BlockSpec((tm, CONV_WIDTH), lambda i: (i, 0)),
        pl.BlockSpec((V7X_SUBLANES, CONV_WIDTH), lambda i: (jnp.maximum(i * halo - 1, 0), 0)),
        pl.BlockSpec((V7X_SUBLANES, CONV_WIDTH), lambda i: (jnp.minimum((i + 1) * halo, n_halo - 1), 0)),
        pl.BlockSpec((tm, CONV_WIDTH), lambda i: (i, 0)),
        pl.BlockSpec((3, CONV_WIDTH), const),
        pl.BlockSpec((D_MODEL, 3 * D_MODEL), const),
        pl.BlockSpec((F_WIDTH, D_MODEL), const),
        pl.BlockSpec((NA_WIDTH, D_MODEL), const),
        pl.BlockSpec((CONV_WIDTH, D_MODEL), const),
        pl.BlockSpec((D_MODEL, D_MODEL), const),
        pl.BlockSpec((D_MODEL, V7X_LANES), const),
    ]
    out_specs = [
        pl.BlockSpec((tm, D_MODEL), lambda i: (i, 0)),
        pl.BlockSpec((tm, D_MODEL), lambda i: (i, 0)),
        pl.BlockSpec((tm, V7X_LANES), lambda i: (i, 0)),
    ]
    out_shape = [
        jax.ShapeDtypeStruct((n_tok, D_MODEL), F32),
        jax.ShapeDtypeStruct((n_tok, D_MODEL), F32),
        jax.ShapeDtypeStruct((n_tok, V7X_LANES), F32),
    ]
    return pl.pallas_call(
        functools.partial(_merge_kernel, tiles_per_seq=tps, fixed_row=fixed_row),
        grid=(n_tok // tm,),
        in_specs=in_specs,
        out_specs=out_specs,
        out_shape=out_shape,
        compiler_params=_cparams(("arbitrary",)),
        name="merge",
    )(x2, mods_l, n1, n2, f_all, attn, u, u, u, bg, conv_w, w_gate, w_f, w_na, w_cv, w_o, rwt)


def _first_max(vals):
    best = vals[0]
    idx = jnp.zeros(best.shape, I32)
    for j in range(1, len(vals)):
        better = vals[j] > best
        idx = jnp.where(better, j, idx)
        best = jnp.where(better, vals[j], best)
    return best, idx


def _select(idx, vals):
    out = vals[-1]
    for j in range(len(vals) - 2, -1, -1):
        out = jnp.where(idx == j, vals[j], out)
    return out


def _route_kernel(lg_ref, rb_ref, ids_ref, wts_ref, cnt_ref, tot_ref, run_ref):
    step = pl.program_id(0)

    @pl.when(step == 0)
    def _():
        run_ref[...] = jnp.zeros_like(run_ref)

    s = jax.nn.sigmoid(lg_ref[...].T[0:N_EXPERTS, :])
    sb = s + rb_ref[...]
    t = s.shape[1]
    s_rows = [s[e:e + 1, :] for e in range(N_EXPERTS)]
    b_rows = [sb[e:e + 1, :] for e in range(N_EXPERTS)]
    epg = EXPERTS_PER_GROUP
    gscore = []
    for g in range(N_GROUPS):
        v = b_rows[g * epg:(g + 1) * epg]
        pair = None
        for a in range(epg):
            for b in range(a + 1, epg):
                pair = v[a] + v[b] if pair is None else jnp.maximum(pair, v[a] + v[b])
        gscore.append(pair)
    _, gi = _first_max(gscore)
    bv = [_select(gi, [b_rows[g * epg + j] for g in range(N_GROUPS)]) for j in range(epg)]
    sv = [_select(gi, [s_rows[g * epg + j] for g in range(N_GROUPS)]) for j in range(epg)]
    _, i1 = _first_max(bv)
    _, i2 = _first_max([jnp.where(i1 == j, -jnp.inf, bv[j]) for j in range(epg)])
    s1 = _select(i1, sv)
    s2 = _select(i2, sv)
    tot = s1 + s2
    e1 = gi * epg + i1
    e2 = gi * epg + i2

    eid = lax.broadcasted_iota(I32, (N_EXPERTS, t), 0)
    hit1 = eid == e1
    hit2 = eid == e2
    onehot = jnp.where(hit1 | hit2, 1.0, 0.0)
    before = (lax.broadcasted_iota(I32, (t, t), 0) < lax.broadcasted_iota(I32, (t, t), 1))
    prefix = jnp.dot(onehot.astype(BF16), jnp.where(before, 1.0, 0.0).astype(BF16),
                     preferred_element_type=F32)
    r1 = jnp.sum(jnp.where(hit1, prefix, 0.0), axis=0, keepdims=True)
    r2 = jnp.sum(jnp.where(hit2, prefix, 0.0), axis=0, keepdims=True)
    grp = float(SLOT_GROUP)
    cnt = jnp.sum(onehot, axis=1, keepdims=True)
    cnt = jnp.floor((cnt + (grp - 1.0)) / grp) * grp
    run = run_ref[...] + cnt
    run_ref[...] = run
    cnt_ref[...] = jnp.broadcast_to(cnt, cnt_ref.shape)
    tot_ref[...] = jnp.broadcast_to(run, tot_ref.shape)

    zi = jnp.zeros((V7X_SUBLANES - 4, t), I32)
    ids_ref[...] = jnp.concatenate([e1, e2, r1.astype(I32), r2.astype(I32), zi], axis=0)
    zf = jnp.zeros((V7X_SUBLANES - 2, t), F32)
    wts_ref[...] = jnp.concatenate([s1 / tot, s2 / tot, zf], axis=0)


def _route(logits, router_b):
    n_tok = logits.shape[0]
    tr = ROUTE_TILE
    return pl.pallas_call(
        _route_kernel,
        grid=(n_tok // tr,),
        in_specs=[
            pl.BlockSpec((tr, V7X_LANES), lambda i: (i, 0)),
            pl.BlockSpec((N_EXPERTS, 1), lambda i: (0, 0)),
        ],
        out_specs=[
            pl.BlockSpec((V7X_SUBLANES, tr), lambda i: (0, i)),
            pl.BlockSpec((V7X_SUBLANES, tr), lambda i: (0, i)),
            pl.BlockSpec((None, N_EXPERTS, V7X_LANES), lambda i: (i, 0, 0)),
            pl.BlockSpec((N_EXPERTS, V7X_LANES), lambda i: (0, 0)),
        ],
        out_shape=[
            jax.ShapeDtypeStruct((V7X_SUBLANES, n_tok), I32),
            jax.ShapeDtypeStruct((V7X_SUBLANES, n_tok), F32),
            jax.ShapeDtypeStruct((n_tok // tr, N_EXPERTS, V7X_LANES), F32),
            jax.ShapeDtypeStruct((N_EXPERTS, V7X_LANES), F32),
        ],
        scratch_shapes=[pltpu.VMEM((N_EXPERTS, 1), F32)],
        compiler_params=_cparams(("arbitrary",)),
        name="route",
    )(logits, router_b.reshape(N_EXPERTS, 1))


def _lane_table(vals, width):
    lane = lax.broadcasted_iota(I32, (1, width), 1)
    out = jnp.zeros((1, width), F32)
    for e, v in enumerate(vals):
        out = jnp.where(lane == e, v, out)
    return out


def _slots_kernel(ids_ref, cnt_ref, tot_ref, loc_ref, gmap_ref, blk_ref, off_ref):
    step = pl.program_id(0)

    @pl.when(step == 0)
    def _():
        off_ref[...] = jnp.zeros_like(off_ref)

    blk = float(EXPERT_BLOCK)
    grp = float(SLOT_GROUP)
    cnt = cnt_ref[...][:, 0:1]
    tot = tot_ref[...][:, 0:1]
    off = off_ref[...]
    region = jnp.floor((tot + (blk - 1.0)) / blk) * blk
    starts, ends, local = [], [], []
    run = jnp.zeros((1, 1), F32)
    lrun = jnp.zeros((1, 1), F32)
    for e in range(N_EXPERTS):
        starts.append(run)
        run = run + region[e:e + 1, :]
        ends.append(run)
        local.append(lrun)
        lrun = lrun + cnt[e:e + 1, :]

    ids = ids_ref[...]
    e1, e2 = ids[0:1, :], ids[1:2, :]
    t = ids.shape[1]
    l1 = jnp.zeros((1, t), F32)
    l2 = jnp.zeros((1, t), F32)
    for e in range(N_EXPERTS):
        l1 = jnp.where(e1 == e, local[e], l1)
        l2 = jnp.where(e2 == e, local[e], l2)
    zi = jnp.zeros((V7X_SUBLANES - 2, t), I32)
    loc_ref[...] = jnp.concatenate([l1.astype(I32) + ids[2:3, :], l2.astype(I32) + ids[3:4, :], zi], axis=0)

    wg = gmap_ref.shape[1]
    first = lax.broadcasted_iota(I32, (1, wg), 1).astype(F32) * grp
    dest = jnp.zeros((1, wg), F32)
    for e in range(N_EXPERTS):
        inside = (first >= local[e]) & (first < local[e] + cnt[e:e + 1, :])
        dest = jnp.where(inside, starts[e] + off[e:e + 1, :] + (first - local[e]), dest)
    n_groups = jnp.broadcast_to(lrun / grp, (1, wg))
    zg = jnp.zeros((V7X_SUBLANES - 2, wg), I32)
    gmap_ref[...] = jnp.concatenate([(dest / grp).astype(I32), n_groups.astype(I32), zg], axis=0)
    off_ref[...] = off + cnt

    w = blk_ref.shape[1]
    first_row = lax.broadcasted_iota(I32, (1, w), 1).astype(F32) * blk
    owner = jnp.zeros((1, w), F32)
    for e in range(N_EXPERTS):
        owner = owner + jnp.where(first_row >= ends[e], 1.0, 0.0)
    owner = jnp.minimum(owner, float(N_EXPERTS - 1))
    used = jnp.broadcast_to(ends[-1] / blk, (1, w))
    pad_first = _lane_table([(starts[e] + tot[e:e + 1, :]) / grp for e in range(N_EXPERTS)], w)
    pad_count = _lane_table([(region[e:e + 1, :] - tot[e:e + 1, :]) / grp for e in range(N_EXPERTS)], w)
    zb = jnp.zeros((V7X_SUBLANES - 4, w), I32)
    blk_ref[...] = jnp.concatenate([owner.astype(I32), used.astype(I32), pad_first.astype(I32),
                                    pad_count.astype(I32), zb], axis=0)


def _slots(ids, cnt, tot, n_blocks):
    n_tok = ids.shape[1]
    tr = ROUTE_TILE
    wblk = -(-n_blocks // V7X_LANES) * V7X_LANES
    return pl.pallas_call(
        _slots_kernel,
        grid=(n_tok // tr,),
        in_specs=[
            pl.BlockSpec((V7X_SUBLANES, tr), lambda i: (0, i)),
            pl.BlockSpec((None, N_EXPERTS, V7X_LANES), lambda i: (i, 0, 0)),
            pl.BlockSpec((N_EXPERTS, V7X_LANES), lambda i: (0, 0)),
        ],
        out_specs=[
            pl.BlockSpec((V7X_SUBLANES, tr), lambda i: (0, i)),
            pl.BlockSpec((None, V7X_SUBLANES, SORT_GROUPS_PAD), lambda i: (i, 0, 0)),
            pl.BlockSpec((V7X_SUBLANES, wblk), lambda i: (0, 0)),
        ],
        out_shape=[
            jax.ShapeDtypeStruct((V7X_SUBLANES, n_tok), I32),
            jax.ShapeDtypeStruct((n_tok // tr, V7X_SUBLANES, SORT_GROUPS_PAD), I32),
            jax.ShapeDtypeStruct((V7X_SUBLANES, wblk), I32),
        ],
        scratch_shapes=[pltpu.VMEM((N_EXPERTS, 1), F32)],
        compiler_params=_cparams(("arbitrary",)),
        name="slots",
    )(ids, cnt, tot)


def _group_rows(group):
    if isinstance(group, int):
        return pl.ds(group * SLOT_GROUP, SLOT_GROUP)
    return pl.ds(pl.multiple_of(group * SLOT_GROUP, SLOT_GROUP), SLOT_GROUP)


def _group_copy(src_ref, src_group, dst_ref, dst_group, sem):
    return pltpu.make_async_copy(src_ref.at[_group_rows(src_group)], dst_ref.at[_group_rows(dst_group)], sem)


def _dispatch_kernel(gmap_ref, gprev_ref, blk_ref, loc_ref, *refs, n_first):
    if n_first is None:
        h_ref, xb_ref, sorted_ref, zero_ref, sem = refs
        second_ref = None
    else:
        h_ref, second_ref, xb_ref, sorted_ref, zero_ref, sem = refs
    step = pl.program_id(0)
    last = pl.num_programs(0) - 1
    buf = step % 2
    loc = loc_ref[...]
    slot = lax.broadcasted_iota(I32, (SORT_ROWS, loc.shape[1]), 0)
    perm = jnp.where(slot == loc[0:1, :], 1.0, jnp.where(slot == loc[1:2, :], 1.0, 0.0)).astype(BF16)

    def sort_rows(src_ref):
        sorted_ref[buf] = jnp.dot(perm, src_ref[...].astype(BF16), preferred_element_type=F32).astype(BF16)

    if second_ref is None:
        sort_rows(h_ref)
    else:
        pl.when(step < n_first)(lambda: sort_rows(h_ref))
        pl.when(step >= n_first)(lambda: sort_rows(second_ref))

    def tile_copy(map_ref, which, g):
        return _group_copy(sorted_ref.at[which], g, xb_ref, map_ref[0, g], sem.at[which])

    def start(g, c):
        tile_copy(gmap_ref, buf, g).start()
        return c

    def wait_prev(g, c):
        tile_copy(gprev_ref, 1 - buf, g).wait()
        return c

    def wait_own(g, c):
        tile_copy(gmap_ref, buf, g).wait()
        return c

    lax.fori_loop(0, gmap_ref[1, 0], start, 0)

    @pl.when(step == last)
    def _():
        zero_ref[...] = jnp.zeros_like(zero_ref)
        for e in range(N_EXPERTS):
            first = blk_ref[2, e]

            def zstart(g, c, first=first):
                _group_copy(zero_ref, 0, xb_ref, first + g, sem.at[2]).start()
                return c

            def zwait(g, c, first=first):
                _group_copy(zero_ref, 0, xb_ref, first + g, sem.at[2]).wait()
                return c

            lax.fori_loop(0, blk_ref[3, e], zstart, 0)
            lax.fori_loop(0, blk_ref[3, e], zwait, 0)

        def block_copy(b):
            rows = pl.ds(pl.multiple_of(b * EXPERT_BLOCK, EXPERT_BLOCK), EXPERT_BLOCK)
            return pltpu.make_async_copy(zero_ref, xb_ref.at[rows], sem.at[2])

        def bstart(b, c):
            block_copy(b).start()
            return c

        def bwait(b, c):
            block_copy(b).wait()
            return c

        n_blocks = xb_ref.shape[0] // EXPERT_BLOCK
        lax.fori_loop(blk_ref[1, 0], n_blocks, bstart, 0)
        lax.fori_loop(blk_ref[1, 0], n_blocks, bwait, 0)

    @pl.when(step > 0)
    def _():
        lax.fori_loop(0, gprev_ref[1, 0], wait_prev, 0)

    @pl.when(step == last)
    def _():
        lax.fori_loop(0, gmap_ref[1, 0], wait_own, 0)


def _dispatch(gmap, blk, loc, h2, h2_second, n_slots):
    tr = ROUTE_TILE
    n_first = h2.shape[0] // tr
    n_tiles = n_first
    in_specs = [
        pl.BlockSpec((None, V7X_SUBLANES, SORT_GROUPS_PAD), lambda i: (i, 0, 0), memory_space=pltpu.SMEM),
        pl.BlockSpec((None, V7X_SUBLANES, SORT_GROUPS_PAD), lambda i: (jnp.maximum(i - 1, 0), 0, 0),
                     memory_space=pltpu.SMEM),
        pl.BlockSpec(blk.shape, lambda i: (0, 0), memory_space=pltpu.SMEM),
        pl.BlockSpec((V7X_SUBLANES, tr), lambda i: (0, i)),
        pl.BlockSpec((tr, D_MODEL), lambda i: (jnp.minimum(i, n_first - 1), 0)),
    ]
    args = [gmap, gmap, blk, loc, h2]
    if h2_second is not None:
        n_tiles += h2_second.shape[0] // tr
        in_specs.append(pl.BlockSpec((tr, D_MODEL), lambda i: (jnp.maximum(i - n_first, 0), 0)))
        args.append(h2_second)
    return pl.pallas_call(
        functools.partial(_dispatch_kernel, n_first=None if h2_second is None else n_first),
        grid=(n_tiles,),
        in_specs=in_specs,
        out_specs=pl.BlockSpec(memory_space=pl.ANY),
        out_shape=jax.ShapeDtypeStruct((n_slots, D_MODEL), BF16),
        scratch_shapes=[pltpu.VMEM((2, SORT_ROWS, D_MODEL), BF16), pltpu.VMEM((EXPERT_BLOCK, D_MODEL), BF16),
                        pltpu.SemaphoreType.DMA((3,))],
        compiler_params=_cparams(("arbitrary",)),
        name="dispatch",
    )(*args)


def _experts_kernel(blk_ref, used_ref, x_ref, w1_ref, w3_ref, w2_ref, y_ref, w1b, w3b, w2b):
    i = pl.program_id(0)
    prev = blk_ref[jnp.maximum(i - 1, 0)]

    @pl.when((i == 0) | (blk_ref[i] != prev))
    def _():
        w1b[...] = w1_ref[...].astype(BF16)
        w3b[...] = w3_ref[...].astype(BF16)
        w2b[...] = w2_ref[...].astype(BF16)

    @pl.when(i < used_ref[0])
    def _():
        x = x_ref[...]
        a = jnp.dot(x, w1b[...], preferred_element_type=F32)
        b = jnp.dot(x, w3b[...], preferred_element_type=F32)
        hid = (a * jax.nn.sigmoid(a) * b).astype(BF16)
        y_ref[...] = jnp.dot(hid, w2b[...], preferred_element_type=F32).astype(BF16)

    @pl.when(i >= used_ref[0])
    def _():
        y_ref[...] = jnp.zeros_like(y_ref)


def _experts(blk_e, used, xb, w1, w3, w2, layer):
    n_slots = xb.shape[0]
    bm = EXPERT_BLOCK
    row_map = lambda i, be, nu: (jnp.minimum(i, nu[0] - 1), 0)
    w_map = lambda i, be, nu: (layer, be[i], 0, 0)
    grid_spec = pltpu.PrefetchScalarGridSpec(
        num_scalar_prefetch=2,
        grid=(n_slots // bm,),
        in_specs=[
            pl.BlockSpec((bm, D_MODEL), row_map),
            pl.BlockSpec((None, None, D_MODEL, D_EXPERT), w_map),
            pl.BlockSpec((None, None, D_MODEL, D_EXPERT), w_map),
            pl.BlockSpec((None, None, D_EXPERT, D_MODEL), w_map),
        ],
        out_specs=pl.BlockSpec((bm, D_MODEL), lambda i, be, nu: (i, 0)),
        scratch_shapes=[pltpu.VMEM((D_MODEL, D_EXPERT), BF16), pltpu.VMEM((D_MODEL, D_EXPERT), BF16),
                        pltpu.VMEM((D_EXPERT, D_MODEL), BF16)],
    )
    return pl.pallas_call(
        _experts_kernel,
        grid_spec=grid_spec,
        out_shape=jax.ShapeDtypeStruct((n_slots, D_MODEL), BF16),
        compiler_params=_cparams(("arbitrary",)),
        name="experts",
    )(blk_e, used, xb, w1, w3, w2)


_TN = (((0,), (0,)), ((), ()))


def _combine_kernel(gmap_ref, gnext_ref, loc_ref, wts_ref, x_ref, mods_ref, yb_ref, o_ref, ys_ref, sem,
                    *, tiles_per_seq, fixed_row):
    step = pl.program_id(0)
    buf = step % 2

    def fetch(map_ref, which, g):
        return _group_copy(yb_ref, map_ref[0, g], ys_ref.at[which], g, sem.at[which])

    def start_own(g, c):
        fetch(gmap_ref, buf, g).start()
        return c

    def start_next(g, c):
        fetch(gnext_ref, 1 - buf, g).start()
        return c

    def wait_own(g, c):
        fetch(gmap_ref, buf, g).wait()
        return c

    @pl.when(step == 0)
    def _():
        ys_ref[...] = jnp.zeros_like(ys_ref)
        lax.fori_loop(0, gmap_ref[1, 0], start_own, 0)

    @pl.when(step + 1 < pl.num_programs(0))
    def _():
        lax.fori_loop(0, gnext_ref[1, 0], start_next, 0)

    loc = loc_ref[...]
    wts = wts_ref[...]
    slot = lax.broadcasted_iota(I32, (SORT_ROWS, loc.shape[1]), 0)
    perm = jnp.where(slot == loc[0:1, :], wts[0:1, :], jnp.where(slot == loc[1:2, :], wts[1:2, :], 0.0))
    lax.fori_loop(0, gmap_ref[1, 0], wait_own, 0)
    y = lax.dot_general(perm.astype(BF16), ys_ref[buf], _TN, preferred_element_type=F32)
    m = _mod_row(mods_ref, step, tiles_per_seq, fixed_row)
    o_ref[...] = x_ref[...] + m[:, 5 * D_MODEL:6 * D_MODEL] * y


def _combine(gmap, loc, wts, x_new, mods_l, yb, *, seq_len, fixed_row, tile_offset):
    n_tok = x_new.shape[0]
    tr = ROUTE_TILE
    tps = max(seq_len // tr, 1)
    n_tiles = n_tok // tr
    return pl.pallas_call(
        functools.partial(_combine_kernel, tiles_per_seq=tps, fixed_row=fixed_row),
        grid=(n_tiles,),
        in_specs=[
            pl.BlockSpec((None, V7X_SUBLANES, SORT_GROUPS_PAD), lambda i: (i + tile_offset, 0, 0),
                         memory_space=pltpu.SMEM),
            pl.BlockSpec((None, V7X_SUBLANES, SORT_GROUPS_PAD),
                         lambda i: (jnp.minimum(i + 1, n_tiles - 1) + tile_offset, 0, 0), memory_space=pltpu.SMEM),
            pl.BlockSpec((V7X_SUBLANES, tr), lambda i: (0, i + tile_offset)),
            pl.BlockSpec((V7X_SUBLANES, tr), lambda i: (0, i + tile_offset)),
            pl.BlockSpec((tr, D_MODEL), lambda i: (i, 0)),
            pl.BlockSpec((MODS_ROWS, 6 * D_MODEL), lambda i: (0, 0)),
            pl.BlockSpec(memory_space=pl.ANY),
        ],
        out_specs=pl.BlockSpec((tr, D_MODEL), lambda i: (i, 0)),
        out_shape=jax.ShapeDtypeStruct((n_tok, D_MODEL), F32),
        scratch_shapes=[pltpu.VMEM((2, SORT_ROWS, D_MODEL), BF16), pltpu.SemaphoreType.DMA((2,))],
        compiler_params=_cparams(("arbitrary",)),
        name="combine",
    )(gmap, gmap, loc, wts, x_new, mods_l, yb)


def _channel_dft_tables():
    j = np.arange(F_GDIM)
    ang = 2.0 * np.pi * ((j[:, None] * j[None, :]) % F_GDIM) / F_GDIM
    eye = np.eye(F_GROUPS)
    return (jnp.asarray(np.kron(eye, np.cos(ang)), F32).astype(BF16),
            jnp.asarray(np.kron(eye, np.sin(ang)), F32).astype(BF16))


def _position_dft_tables(seq_len):
    scale = 1.0 / math.sqrt(seq_len * F_GDIM)
    k = np.arange(seq_len, dtype=np.int64)
    ang = 2.0 * np.pi * ((k[:, None] * k[None, :]) % seq_len) / seq_len
    return (jnp.asarray(np.cos(ang) * scale, F32).astype(BF16),
            jnp.asarray(np.sin(ang) * scale, F32).astype(BF16))


def _two_stage_dft_tables(seq_len):
    r = FFT_RADIX
    assert seq_len == r * r
    scale = 1.0 / math.sqrt(seq_len * F_GDIM)
    j = np.arange(r, dtype=np.int64)
    ang_r = 2.0 * np.pi * ((j[:, None] * j[None, :]) % r) / r
    cs, ss = np.cos(ang_r), np.sin(ang_r)
    fst = jnp.asarray(np.concatenate([cs, ss], axis=0) * scale, F32).astype(BF16)
    g = jnp.asarray(np.concatenate([cs, ss], axis=1), F32).astype(BF16)
    ang_t = 2.0 * np.pi * (j[:, None] * j[None, :]) / seq_len
    tc3 = jnp.asarray(np.repeat(np.cos(ang_t)[:, :, None], V7X_LANES, axis=2), F32)
    ts3 = jnp.asarray(np.repeat(np.sin(ang_t)[:, :, None], V7X_LANES, axis=2), F32)
    return fst, g, tc3, ts3


def _rope_tables(seq_len):
    t = np.arange(seq_len)
    row = (t // GRID_W).astype(np.float64)
    col = (t % GRID_W).astype(np.float64)
    inv = np.power(ROPE_BASE, -np.arange(ROPE_PER_AXIS, dtype=np.float64) / ROPE_PER_AXIS)
    ang = np.concatenate([row[:, None] * inv, col[:, None] * inv], axis=-1)
    cos = np.cos(ang)
    sin = np.sin(ang)
    cos_h = np.concatenate([cos, cos], axis=-1)
    sin_h = np.concatenate([-sin, sin], axis=-1)
    return (jnp.asarray(np.tile(cos_h, (1, NA_HEADS)), F32), jnp.asarray(np.tile(sin_h, (1, NA_HEADS)), F32))


def _bias_table(rpb_l):
    col = np.arange(GRID_W)
    col_start = np.clip(col - NA_WIN_C // 2, 0, GRID_W - NA_WIN_C)
    col_mask = (col[None, :] >= col_start[:, None]) & (col[None, :] < col_start[:, None] + NA_WIN_C)
    dc = np.clip(col[None, :] - col[:, None] + (NA_WIN_C - 1), 0, 2 * NA_WIN_C - 2)
    n_dc = 2 * NA_WIN_C - 1
    pick = (dc.reshape(-1)[None, :] == np.arange(n_dc)[:, None]).astype(np.float32)
    e = jnp.dot(rpb_l.reshape(-1, n_dc), jnp.asarray(pick), precision=HIGHEST)
    e = e.reshape(NA_HEADS, 2 * NA_WIN_R - 1, GRID_W, GRID_W)
    e = jnp.where(jnp.asarray(col_mask)[None, None], e, NEG_BIG)
    b = jnp.stack([e[:, o:o + NA_WIN_R] for o in range(NA_WIN_R)], axis=0)
    b = b.transpose(0, 1, 3, 2, 4)
    return b.reshape(NA_WIN_R, NA_HEADS, GRID_W, NA_WIN_R * GRID_W)


def _moe(h2, h2_second, logits, w1, w3, w2, layer, router_b):
    n_tok = logits.shape[0]
    n_tiles = n_tok // ROUTE_TILE
    max_rows = 2 * n_tok + N_EXPERTS * n_tiles * (SLOT_GROUP - 1) + N_EXPERTS * (EXPERT_BLOCK - 1)
    n_blocks = -(-max_rows // EXPERT_BLOCK)
    n_slots = n_blocks * EXPERT_BLOCK
    ids, wts, cnt, tot = _route(logits, router_b)
    loc, gmap, blk = _slots(ids, cnt, tot, n_blocks)
    xb = _dispatch(gmap, blk, loc, h2, h2_second, n_slots)
    yb = _experts(blk[0, :n_blocks], blk[1, 0:1], xb, w1, w3, w2, layer)
    return yb, gmap, loc, wts


def kernel(x, c, ctx, c_ctx, ada_w, ada_b, norm1_g, w_in, qn_g, kn_g, rpb, conv_w, w_f, w_na, w_cv, w_o,
           norm2_g, router_w, router_b, w1, w3, w2):
    bsz, seq_len, d = x.shape
    ctx_len = ctx.shape[1]
    n_lat = bsz * seq_len
    n_ctx = bsz * ctx_len
    ctx_row = bsz

    c8 = jnp.concatenate([c, c_ctx[None, :], jnp.zeros((MODS_ROWS - bsz - 1, d), F32)], axis=0)
    mods = _mods(c8, ada_w, ada_b)

    cbd, sbd = _channel_dft_tables()
    fst, g_dft, tc3, ts3 = _two_stage_dft_tables(seq_len)
    c_ctx_t, s_ctx_t = _position_dft_tables(ctx_len)
    cos_t, sin_t = _rope_tables(seq_len)
    mavg = jnp.asarray(np.kron(np.eye(NA_HEADS), np.full((HEAD_DIM, HEAD_DIM), 1.0 / HEAD_DIM)), F32).astype(BF16)
    rw_hi = router_w.astype(BF16)
    rw_lo = (router_w - rw_hi.astype(F32)).astype(BF16)
    rwt = jnp.concatenate([rw_hi, rw_lo, jnp.zeros((d, V7X_LANES - 2 * N_EXPERTS), BF16)], axis=1)

    xl = x.reshape(n_lat, d)
    xc = ctx.reshape(n_ctx, d)
    for l in range(DEPTH):
        last = l == DEPTH - 1
        w_proj = w_in[l][:, :COL_G].astype(BF16)
        w_gate = w_in[l][:, COL_G:].astype(BF16)
        wf, wna, wcv, wo = (w_f[l].astype(BF16), w_na[l].astype(BF16), w_cv[l].astype(BF16), w_o[l].astype(BF16))
        n1 = norm1_g[l].reshape(1, d)
        n2 = norm2_g[l].reshape(1, d)
        qg = jnp.tile(qn_g[l], NA_HEADS).reshape(1, NA_WIDTH)
        kg = jnp.tile(kn_g[l], NA_HEADS).reshape(1, NA_WIDTH)
        bias_tab = _bias_table(rpb[l])
        mods_l = mods[l]

        a_c, b_c, q_c, k_c, v_c, u_c, bg_c = _proj(
            xc, mods_l, n1, w_proj, qg, kg, mavg, cbd, sbd, None, None,
            seq_len=ctx_len, n_seq=bsz, fixed_row=ctx_row, dft_dtype=BF16)
        a_l, b_l, q_l, k_l, v_l, u_l, bg_l = _proj(
            xl, mods_l, n1, w_proj, qg, kg, mavg, cbd, sbd, cos_t, sin_t,
            seq_len=seq_len, n_seq=bsz, fixed_row=None, dft_dtype=F32)

        f_l = _fourier_two_stage(a_l, b_l, fst, g_dft, tc3, ts3)
        attn_l = _attn(q_l, k_l, v_l, k_c, v_c, bias_tab, n_seq=bsz, seq_len=seq_len, ctx_len=ctx_len)
        xl_new, h2_l, lg_l = _merge(xl, mods_l, n1, n2, f_l, attn_l, u_l, bg_l, conv_w[l], w_gate,
                                    wf, wna, wcv, wo, rwt, seq_len=seq_len, fixed_row=None)
        if last:
            yb, gmap, loc, wts = _moe(h2_l, None, lg_l, w1, w3, w2, l, router_b)
            xl = _combine(gmap, loc, wts, xl_new, mods_l, yb, seq_len=seq_len, fixed_row=None, tile_offset=0)
        else:
            f_c = _fourier(c_ctx_t, s_ctx_t, a_c, b_c)
            attn_c = _ctx_attn(q_c, k_c, v_c, n_seq=bsz, ctx_len=ctx_len)
            xc_new, h2_c, lg_c = _merge(xc, mods_l, n1, n2, f_c, attn_c, u_c, bg_c, conv_w[l], w_gate,
                                        wf, wna, wcv, wo, rwt, seq_len=ctx_len, fixed_row=ctx_row)
            lg = jnp.concatenate([lg_l, lg_c], axis=0)
            yb, gmap, loc, wts = _moe(h2_l, h2_c, lg, w1, w3, w2, l, router_b)
            xl = _combine(gmap, loc, wts, xl_new, mods_l, yb, seq_len=seq_len, fixed_row=None, tile_offset=0)
            xc = _combine(gmap, loc, wts, xc_new, mods_l, yb, seq_len=ctx_len, fixed_row=ctx_row,
                          tile_offset=n_lat // ROUTE_TILE)
    return xl.reshape(bsz, seq_len, d)
```

```python
import functools
import math

import numpy as np
import jax
import jax.numpy as jnp
from jax import lax
from jax.experimental import pallas as pl
from jax.experimental.pallas import tpu as pltpu

F32 = jnp.float32
BF16 = jnp.bfloat16
I32 = jnp.int32
HIGHEST = lax.Precision.HIGHEST

D_MODEL = 1024
DEPTH = 2
GRID_W = 64
EPS = 1e-6
F_GROUPS = 4
F_GDIM = 64
F_WIDTH = 256
NA_HEADS = 8
HEAD_DIM = 64
NA_WIDTH = 512
NA_WIN_R = 8
NA_WIN_C = 16
ATTN_SCALE = HEAD_DIM ** -0.5
LOG2E = math.log2(math.e)
ROPE_BASE = 10000.0
ROPE_PER_AXIS = HEAD_DIM // 4
CONV_WIDTH = 256
COL_Q = 256
COL_K = 768
COL_V = 1280
COL_CX = 1792
COL_CB = 2048
COL_CC = 2304
COL_G = 2560
N_EXPERTS = 16
N_GROUPS = 4
EXPERTS_PER_GROUP = 4
D_EXPERT = 512

V7X_LANES = 128
V7X_SUBLANES = 8
V7X_MXU_DIM = 256

TOKEN_TILE = 512
ROUTE_TILE = 512
EXPERT_BLOCK = 512
SLOT_GROUP = 2 * V7X_SUBLANES
SORT_ROWS = -(-(2 * ROUTE_TILE + N_EXPERTS * (SLOT_GROUP - 1)) // V7X_LANES) * V7X_LANES
SORT_GROUPS_PAD = -(-(SORT_ROWS // SLOT_GROUP) // V7X_LANES) * V7X_LANES
HEADS_PER_GROUP = V7X_MXU_DIM // HEAD_DIM
ATTN_ROWS_PER_STEP = 8
FFT_RADIX = 64
FFT_STEP = V7X_SUBLANES
NEG_BIG = -1e30
MODS_ROWS = 8
VMEM_LIMIT = 48 * 1024 * 1024


def _cparams(sem):
    return pltpu.CompilerParams(dimension_semantics=sem, vmem_limit_bytes=VMEM_LIMIT)


def _mods_kernel(c_ref, w_ref, b_ref, o_ref):
    c = c_ref[...]
    sc = c * jax.nn.sigmoid(c)
    o_ref[...] = jnp.dot(sc, w_ref[...], precision=HIGHEST, preferred_element_type=F32) + b_ref[...]


def _mods(c8, ada_w, ada_b):
    nb = 1536
    return pl.pallas_call(
        _mods_kernel,
        grid=(DEPTH, 6 * D_MODEL // nb),
        in_specs=[
            pl.BlockSpec((MODS_ROWS, D_MODEL), lambda l, j: (0, 0)),
            pl.BlockSpec((None, D_MODEL, nb), lambda l, j: (l, 0, j)),
            pl.BlockSpec((None, 1, nb), lambda l, j: (l, 0, j)),
        ],
        out_specs=pl.BlockSpec((None, MODS_ROWS, nb), lambda l, j: (l, 0, j)),
        out_shape=jax.ShapeDtypeStruct((DEPTH, MODS_ROWS, 6 * D_MODEL), F32),
        compiler_params=_cparams(("arbitrary", "arbitrary")),
        name="mods",
    )(c8, ada_w, ada_b.reshape(DEPTH, 1, 6 * D_MODEL))


def _norm_mod(x, g, shift, scale):
    ms = jnp.mean(x * x, axis=-1, keepdims=True)
    return (x * lax.rsqrt(ms + EPS) * g) * (1.0 + scale) + shift


def _mod_row(mods_ref, tile, tiles_per_seq, fixed_row):
    row = fixed_row if fixed_row is not None else tile // tiles_per_seq
    return mods_ref[pl.ds(row, 1), :]


def _proj_kernel(*refs, tiles_per_seq, fixed_row, rope):
    if rope:
        (x_ref, mods_ref, g_ref, w_ref, qg_ref, kg_ref, mavg_ref, cbd_ref, sbd_ref, cos_ref, sin_ref,
         a_ref, b_ref, q_ref, k_ref, v_ref, u_ref, bg_ref) = refs
    else:
        (x_ref, mods_ref, g_ref, w_ref, qg_ref, kg_ref, mavg_ref, cbd_ref, sbd_ref,
         a_ref, b_ref, q_ref, k_ref, v_ref, u_ref, bg_ref) = refs
    m = _mod_row(mods_ref, pl.program_id(0), tiles_per_seq, fixed_row)
    h = _norm_mod(x_ref[...], g_ref[...], m[:, 0:D_MODEL], m[:, D_MODEL:2 * D_MODEL])
    p = jnp.dot(h.astype(BF16), w_ref[...], preferred_element_type=F32)

    uf = p[:, 0:COL_Q].astype(BF16)
    a_ref[...] = jnp.dot(uf, cbd_ref[...], preferred_element_type=F32).astype(a_ref.dtype)
    b_ref[...] = jnp.dot(uf, sbd_ref[...], preferred_element_type=F32).astype(b_ref.dtype)

    def head_norm(t, g):
        ms = jnp.dot((t * t).astype(BF16), mavg_ref[...], preferred_element_type=F32)
        return t * lax.rsqrt(ms + EPS) * g

    def rotate(t):
        n = t.shape[-1]
        lane = lax.broadcasted_iota(I32, t.shape, 1)
        first_half = (lane % HEAD_DIM) < (HEAD_DIM // 2)
        swapped = jnp.where(first_half, pltpu.roll(t, n - HEAD_DIM // 2, 1), pltpu.roll(t, HEAD_DIM // 2, 1))
        return t * cos_ref[...] + swapped * sin_ref[...]

    q = head_norm(p[:, COL_Q:COL_K], qg_ref[...])
    k = head_norm(p[:, COL_K:COL_V], kg_ref[...])
    if rope:
        q = rotate(q)
        k = rotate(k)
    q_ref[...] = (q * (ATTN_SCALE * LOG2E)).astype(BF16)
    k_ref[...] = k.astype(BF16)
    v_ref[...] = p[:, COL_V:COL_CX].astype(BF16)
    u_ref[...] = p[:, COL_CC:COL_G] * p[:, COL_CX:COL_CB]
    bg_ref[...] = p[:, COL_CB:COL_CC]


def _proj(x2, mods_l, norm_g, w_proj, qg, kg, mavg, cbd, sbd, cos_t, sin_t, *, seq_len, n_seq, fixed_row,
          dft_dtype):
    n_tok = x2.shape[0]
    tm = min(TOKEN_TILE, seq_len)
    tps = seq_len // tm
    rope = cos_t is not None
    const = lambda i: (0, 0)
    in_specs = [
        pl.BlockSpec((tm, D_MODEL), lambda i: (i, 0)),
        pl.BlockSpec((MODS_ROWS, 6 * D_MODEL), const),
        pl.BlockSpec((1, D_MODEL), const),
        pl.BlockSpec((D_MODEL, COL_G), const),
        pl.BlockSpec((1, NA_WIDTH), const),
        pl.BlockSpec((1, NA_WIDTH), const),
        pl.BlockSpec((NA_WIDTH, NA_WIDTH), const),
        pl.BlockSpec((F_WIDTH, F_WIDTH), const),
        pl.BlockSpec((F_WIDTH, F_WIDTH), const),
    ]
    args = [x2, mods_l, norm_g, w_proj, qg, kg, mavg, cbd, sbd]
    if rope:
        in_specs += [pl.BlockSpec((tm, NA_WIDTH), lambda i: (i % tps, 0))] * 2
        args += [cos_t, sin_t]
    tok = lambda w: pl.BlockSpec((tm, w), lambda i: (i, 0))
    fmap = pl.BlockSpec((tm, F_WIDTH), lambda i: (i % tps, i // tps))
    out_specs = [fmap, fmap, tok(NA_WIDTH), tok(NA_WIDTH), tok(NA_WIDTH), tok(CONV_WIDTH), tok(CONV_WIDTH)]
    out_shape = [
        jax.ShapeDtypeStruct((seq_len, n_seq * F_WIDTH), dft_dtype),
        jax.ShapeDtypeStruct((seq_len, n_seq * F_WIDTH), dft_dtype),
        jax.ShapeDtypeStruct((n_tok, NA_WIDTH), BF16),
        jax.ShapeDtypeStruct((n_tok, NA_WIDTH), BF16),
        jax.ShapeDtypeStruct((n_tok, NA_WIDTH), BF16),
        jax.ShapeDtypeStruct((n_tok, CONV_WIDTH), F32),
        jax.ShapeDtypeStruct((n_tok, CONV_WIDTH), F32),
    ]
    return pl.pallas_call(
        functools.partial(_proj_kernel, tiles_per_seq=tps, fixed_row=fixed_row, rope=rope),
        grid=(n_tok // tm,),
        in_specs=in_specs,
        out_specs=out_specs,
        out_shape=out_shape,
        compiler_params=_cparams(("arbitrary",)),
        name="proj",
    )(*args)


def _fourier_kernel(c_ref, s_ref, a_ref, b_ref, o_ref):
    o = (jnp.dot(c_ref[...], a_ref[...], preferred_element_type=F32)
         - jnp.dot(s_ref[...], b_ref[...], preferred_element_type=F32))
    o_ref[...] = o.astype(BF16)


def _fourier(c_tab, s_tab, a, b):
    seq_len, width = a.shape
    tk = min(seq_len, 256)
    full = lambda i: (0, 0)
    return pl.pallas_call(
        _fourier_kernel,
        grid=(seq_len // tk,),
        in_specs=[
            pl.BlockSpec((tk, seq_len), lambda i: (i, 0)),
            pl.BlockSpec((tk, seq_len), lambda i: (i, 0)),
            pl.BlockSpec((seq_len, width), full, pipeline_mode=pl.Buffered(1)),
            pl.BlockSpec((seq_len, width), full, pipeline_mode=pl.Buffered(1)),
        ],
        out_specs=pl.BlockSpec((tk, width), lambda i: (i, 0)),
        out_shape=jax.ShapeDtypeStruct((seq_len, width), BF16),
        compiler_params=_cparams(("arbitrary",)),
        name="fourier",
    )(c_tab, s_tab, a, b)


def _fft1_kernel(a_ref, b_ref, fst_ref, tc_ref, ts_ref, zr_ref, zi_ref):
    fst = fst_ref[...]
    r = FFT_RADIX
    for i in range(FFT_STEP):
        r1 = jnp.dot(fst, a_ref[:, i, :].astype(BF16), preferred_element_type=F32)
        r2 = jnp.dot(fst, b_ref[:, i, :].astype(BF16), preferred_element_type=F32)
        yr = r1[0:r] - r2[r:2 * r]
        yi = -(r2[0:r] + r1[r:2 * r])
        tc = tc_ref[i][:, 0:1]
        ts = ts_ref[i][:, 0:1]
        zr_ref[:, i, :] = yr * tc + yi * ts
        zi_ref[:, i, :] = yi * tc - yr * ts


def _fft2_kernel(zr_ref, zi_ref, g_ref, f_ref):
    g = g_ref[...]
    for i in range(FFT_STEP):
        zz = jnp.concatenate([zr_ref[i], zi_ref[i]], axis=0).astype(BF16)
        f_ref[:, i, :] = jnp.dot(g, zz, preferred_element_type=F32)


def _fourier_two_stage(a, b, fst, g, tc3, ts3):
    seq_len, width = a.shape
    r = FFT_RADIX
    a3 = a.reshape(r, r, width)
    b3 = b.reshape(r, r, width)
    steps = r // FFT_STEP
    col_blk = pl.BlockSpec((r, FFT_STEP, width), lambda j: (0, j, 0))
    row_blk = pl.BlockSpec((FFT_STEP, r, width), lambda j: (j, 0, 0))
    tw_blk = pl.BlockSpec((FFT_STEP, r, V7X_LANES), lambda j: (j, 0, 0))
    z_shape = jax.ShapeDtypeStruct((r, r, width), F32)
    zr, zi = pl.pallas_call(
        _fft1_kernel,
        grid=(steps,),
        in_specs=[col_blk, col_blk, pl.BlockSpec((2 * r, r), lambda j: (0, 0)), tw_blk, tw_blk],
        out_specs=[col_blk, col_blk],
        out_shape=[z_shape, z_shape],
        compiler_params=_cparams(("arbitrary",)),
        name="fft1",
    )(a3, b3, fst, tc3, ts3)
    f3 = pl.pallas_call(
        _fft2_kernel,
        grid=(steps,),
        in_specs=[row_blk, row_blk, pl.BlockSpec((r, 2 * r), lambda j: (0, 0))],
        out_specs=col_blk,
        out_shape=z_shape,
        compiler_params=_cparams(("arbitrary",)),
        name="fft2",
    )(zr, zi, g)
    return f3.reshape(seq_len, width)


def _stack_heads(qg):
    lane_head = lax.broadcasted_iota(I32, qg.shape, 1) // HEAD_DIM
    zero = jnp.zeros_like(qg)
    return jnp.concatenate([jnp.where(lane_head == h, qg, zero) for h in range(HEADS_PER_GROUP)], axis=0)


def _unstack_heads(o, rows):
    lane_head = lax.broadcasted_iota(I32, (rows, o.shape[1]), 1) // HEAD_DIM
    acc = jnp.zeros((rows, o.shape[1]), F32)
    for h in range(HEADS_PER_GROUP):
        acc = acc + jnp.where(lane_head == h, o[h * rows:(h + 1) * rows, :], 0.0)
    return acc


_NT = (((1,), (1,)), ((), ()))


def _attn_kernel(q_ref, k_ref, v_ref, kc_ref, vc_ref, *rest, rows):
    bias_refs, o_ref = rest[:ATTN_ROWS_PER_STEP], rest[ATTN_ROWS_PER_STEP]
    n_loc = NA_WIN_R * GRID_W
    for j in range(ATTN_ROWS_PER_STEP):
        r = pl.program_id(1) * ATTN_ROWS_PER_STEP + j
        rs = jnp.clip(r - NA_WIN_R // 2, 0, rows - NA_WIN_R)
        start = pl.multiple_of(rs * GRID_W, GRID_W)
        kwin = k_ref[pl.ds(start, n_loc), :]
        vwin = v_ref[pl.ds(start, n_loc), :]
        q = q_ref[j * GRID_W:(j + 1) * GRID_W, :]
        bias_ref = bias_refs[j]
        outs = []
        for g in range(NA_HEADS // HEADS_PER_GROUP):
            sl = slice(g * V7X_MXU_DIM, (g + 1) * V7X_MXU_DIM)
            qs = _stack_heads(q[:, sl])
            s_loc = lax.dot_general(qs, kwin[:, sl], _NT, preferred_element_type=F32)
            bias = bias_ref[g * HEADS_PER_GROUP:(g + 1) * HEADS_PER_GROUP].reshape(HEADS_PER_GROUP * GRID_W, n_loc)
            s_loc = s_loc + bias
            s_ctx = lax.dot_general(qs, kc_ref[:, sl], _NT, preferred_element_type=F32)
            m = jnp.maximum(jnp.max(s_loc, axis=-1, keepdims=True), jnp.max(s_ctx, axis=-1, keepdims=True))
            p_loc = jnp.exp2(s_loc - m)
            p_ctx = jnp.exp2(s_ctx - m)
            denom = jnp.sum(p_loc, axis=-1, keepdims=True) + jnp.sum(p_ctx, axis=-1, keepdims=True)
            o = (jnp.dot(p_loc.astype(BF16), vwin[:, sl], preferred_element_type=F32)
                 + jnp.dot(p_ctx.astype(BF16), vc_ref[:, sl], preferred_element_type=F32))
            outs.append(_unstack_heads(o / denom, GRID_W))
        o_ref[j * GRID_W:(j + 1) * GRID_W, :] = jnp.concatenate(outs, axis=1).astype(BF16)


def _attn(q, k, v, kc, vc, bias_tab, *, n_seq, seq_len, ctx_len):
    rows = seq_len // GRID_W
    rps = ATTN_ROWS_PER_STEP
    steps = rows // rps

    def bias_map(j):
        def index(b, s):
            r = s * rps + j
            rs = jnp.clip(r - NA_WIN_R // 2, 0, rows - NA_WIN_R)
            return (rs - r + NA_WIN_R - 1, 0, 0, 0)
        return index

    bias_specs = [pl.BlockSpec((None, NA_HEADS, GRID_W, NA_WIN_R * GRID_W), bias_map(j)) for j in range(rps)]
    return pl.pallas_call(
        functools.partial(_attn_kernel, rows=rows),
        grid=(n_seq, steps),
        in_specs=[
            pl.BlockSpec((rps * GRID_W, NA_WIDTH), lambda b, s: (b * steps + s, 0)),
            pl.BlockSpec((seq_len, NA_WIDTH), lambda b, s: (b, 0)),
            pl.BlockSpec((seq_len, NA_WIDTH), lambda b, s: (b, 0)),
            pl.BlockSpec((ctx_len, NA_WIDTH), lambda b, s: (b, 0)),
            pl.BlockSpec((ctx_len, NA_WIDTH), lambda b, s: (b, 0)),
        ] + bias_specs,
        out_specs=pl.BlockSpec((rps * GRID_W, NA_WIDTH), lambda b, s: (b * steps + s, 0)),
        out_shape=jax.ShapeDtypeStruct((n_seq * seq_len, NA_WIDTH), BF16),
        compiler_params=_cparams(("arbitrary", "arbitrary")),
        name="attn",
    )(q, k, v, kc, vc, *([bias_tab] * rps))


def _ctx_attn_kernel(q_ref, k_ref, v_ref, o_ref):
    q = q_ref[...]
    n = q.shape[0]
    outs = []
    for g in range(NA_HEADS // HEADS_PER_GROUP):
        sl = slice(g * V7X_MXU_DIM, (g + 1) * V7X_MXU_DIM)
        qs = _stack_heads(q[:, sl])
        s = lax.dot_general(qs, k_ref[:, sl], _NT, preferred_element_type=F32)
        m = jnp.max(s, axis=-1, keepdims=True)
        p = jnp.exp2(s - m)
        denom = jnp.sum(p, axis=-1, keepdims=True)
        o = jnp.dot(p.astype(BF16), v_ref[:, sl], preferred_element_type=F32)
        outs.append(_unstack_heads(o / denom, n))
    o_ref[...] = jnp.concatenate(outs, axis=1).astype(BF16)


def _ctx_attn(q, k, v, *, n_seq, ctx_len):
    spec = pl.BlockSpec((ctx_len, NA_WIDTH), lambda b: (b, 0))
    return pl.pallas_call(
        _ctx_attn_kernel,
        grid=(n_seq,),
        in_specs=[spec, spec, spec],
        out_specs=spec,
        out_shape=jax.ShapeDtypeStruct((n_seq * ctx_len, NA_WIDTH), BF16),
        compiler_params=_cparams(("arbitrary",)),
        name="ctx_attn",
    )(q, k, v)


def _merge_kernel(x_ref, mods_ref, n1_ref, n2_ref, f_ref, at_ref, u_ref, up_ref, un_ref, bg_ref, cw_ref,
                  wg_ref, wf_ref, wna_ref, wcv_ref, wo_ref, rw_ref,
                  xo_ref, h2_ref, lg_ref, *, tiles_per_seq, fixed_row):
    i = pl.program_id(0)
    m = _mod_row(mods_ref, i, tiles_per_seq, fixed_row)
    dm = D_MODEL
    x = x_ref[...]
    h = _norm_mod(x, n1_ref[...], m[:, 0:dm], m[:, dm:2 * dm]).astype(BF16)
    gates = jax.nn.sigmoid(jnp.dot(h, wg_ref[...], preferred_element_type=F32))

    y_f = jnp.dot(f_ref[...].astype(BF16), wf_ref[...], preferred_element_type=F32)
    y_na = jnp.dot(at_ref[...], wna_ref[...], preferred_element_type=F32)

    u = u_ref[...]
    t = u.shape[0]
    ti = i % tiles_per_seq
    row = lax.broadcasted_iota(I32, u.shape, 0)
    prev_row = jnp.where(ti == 0, 0.0, up_ref[V7X_SUBLANES - 1:V7X_SUBLANES, :])
    next_row = jnp.where(ti == tiles_per_seq - 1, 0.0, un_ref[0:1, :])
    u_prev = jnp.where(row == 0, prev_row, pltpu.roll(u, 1, 0))
    u_next = jnp.where(row == t - 1, next_row, pltpu.roll(u, t - 1, 0))
    y_cv = bg_ref[...] * (cw_ref[0:1, :] * u_prev + cw_ref[1:2, :] * u + cw_ref[2:3, :] * u_next)
    y_cv = jnp.dot(y_cv.astype(BF16), wcv_ref[...], preferred_element_type=F32)

    merged = gates[:, 0:dm] * y_f + gates[:, dm:2 * dm] * y_na + gates[:, 2 * dm:3 * dm] * y_cv
    mixed = jnp.dot(merged.astype(BF16), wo_ref[...], preferred_element_type=F32)
    x_new = x + m[:, 2 * dm:3 * dm] * mixed
    xo_ref[...] = x_new

    h2 = _norm_mod(x_new, n2_ref[...], m[:, 3 * dm:4 * dm], m[:, 4 * dm:5 * dm])
    h2_ref[...] = h2.astype(BF16)
    hi = h2.astype(BF16)
    lo = (h2 - hi.astype(F32)).astype(BF16)
    p_hi = jnp.dot(hi, rw_ref[...], preferred_element_type=F32)
    p_lo = jnp.dot(lo, rw_ref[...], preferred_element_type=F32)
    lg_ref[...] = p_hi + pltpu.roll(p_hi, V7X_LANES - N_EXPERTS, 1) + p_lo


def _merge(x2, mods_l, n1, n2, f_all, attn, u, bg, conv_w, w_gate, w_f, w_na, w_cv, w_o, rwt,
           *, seq_len, fixed_row):
    n_tok = x2.shape[0]
    tm = min(TOKEN_TILE, seq_len)
    tps = seq_len // tm
    const = lambda i: (0, 0)
    halo = tm // V7X_SUBLANES
    n_halo = n_tok // V7X_SUBLANES
    in_specs = [
        pl.BlockSpec((tm, D_MODEL), lambda i: (i, 0)),
        pl.BlockSpec((MODS_ROWS, 6 * D_MODEL), const),
        pl.BlockSpec((1, D_MODEL), const),
        pl.BlockSpec((1, D_MODEL), const),
        pl.BlockSpec((tm, F_WIDTH), lambda i: (i % tps, i // tps)),
        pl.BlockSpec((tm, NA_WIDTH), lambda i: (i, 0)),
        pl.BlockSpec((tm, CONV_WIDTH), lambda i: (i, 0)),
        pl.BlockSpec((V7X_SUBLANES, CONV_WIDTH), lambda i: (jnp.maximum(i * halo - 1, 0), 0)),
        pl.BlockSpec((V7X_SUBLANES, CONV_WIDTH), lambda i: (jnp.minimum((i + 1) * halo, n_halo - 1), 0)),
        pl.BlockSpec((tm, CONV_WIDTH), lambda i: (i, 0)),
        pl.BlockSpec((3, CONV_WIDTH), const),
        pl.BlockSpec((D_MODEL, 3 * D_MODEL), const),
        pl.BlockSpec((F_WIDTH, D_MODEL), const),
        pl.BlockSpec((NA_WIDTH, D_MODEL), const),
        pl.BlockSpec((CONV_WIDTH, D_MODEL), const),
        pl.BlockSpec((D_MODEL, D_MODEL), const),
        pl.BlockSpec((D_MODEL, V7X_LANES), const),
    ]
    out_specs = [
        pl.BlockSpec((tm, D_MODEL), lambda i: (i, 0)),
        pl.BlockSpec((tm, D_MODEL), lambda i: (i, 0)),
        pl.BlockSpec((tm, V7X_LANES), lambda i: (i, 0)),
    ]
    out_shape = [
        jax.ShapeDtypeStruct((n_tok, D_MODEL), F32),
        jax.ShapeDtypeStruct((n_tok, D_MODEL), BF16),
        jax.ShapeDtypeStruct((n_tok, V7X_LANES), F32),
    ]
    return pl.pallas_call(
        functools.partial(_merge_kernel, tiles_per_seq=tps, fixed_row=fixed_row),
        grid=(n_tok // tm,),
        in_specs=in_specs,
        out_specs=out_specs,
        out_shape=out_shape,
        compiler_params=_cparams(("arbitrary",)),
        name="merge",
    )(x2, mods_l, n1, n2, f_all, attn, u, u, u, bg, conv_w, w_gate, w_f, w_na, w_cv, w_o, rwt)


def _first_max(vals):
    best = vals[0]
    idx = jnp.zeros(best.shape, I32)
    for j in range(1, len(vals)):
        better = vals[j] > best
        idx = jnp.where(better, j, idx)
        best = jnp.where(better, vals[j], best)
    return best, idx


def _select(idx, vals):
    out = vals[-1]
    for j in range(len(vals) - 2, -1, -1):
        out = jnp.where(idx == j, vals[j], out)
    return out


def _route_kernel(lg_ref, rb_ref, ids_ref, wts_ref, cnt_ref, tot_ref, run_ref):
    step = pl.program_id(0)

    @pl.when(step == 0)
    def _():
        run_ref[...] = jnp.zeros_like(run_ref)

    s = jax.nn.sigmoid(lg_ref[...].T[0:N_EXPERTS, :])
    sb = s + rb_ref[...]
    t = s.shape[1]
    s_rows = [s[e:e + 1, :] for e in range(N_EXPERTS)]
    b_rows = [sb[e:e + 1, :] for e in range(N_EXPERTS)]
    epg = EXPERTS_PER_GROUP
    gscore = []
    for g in range(N_GROUPS):
        v = b_rows[g * epg:(g + 1) * epg]
        pair = None
        for a in range(epg):
            for b in range(a + 1, epg):
                pair = v[a] + v[b] if pair is None else jnp.maximum(pair, v[a] + v[b])
        gscore.append(pair)
    _, gi = _first_max(gscore)
    bv = [_select(gi, [b_rows[g * epg + j] for g in range(N_GROUPS)]) for j in range(epg)]
    sv = [_select(gi, [s_rows[g * epg + j] for g in range(N_GROUPS)]) for j in range(epg)]
    _, i1 = _first_max(bv)
    _, i2 = _first_max([jnp.where(i1 == j, -jnp.inf, bv[j]) for j in range(epg)])
    s1 = _select(i1, sv)
    s2 = _select(i2, sv)
    tot = s1 + s2
    e1 = gi * epg + i1
    e2 = gi * epg + i2

    eid = lax.broadcasted_iota(I32, (N_EXPERTS, t), 0)
    hit1 = eid == e1
    hit2 = eid == e2
    onehot = jnp.where(hit1 | hit2, 1.0, 0.0)
    before = (lax.broadcasted_iota(I32, (t, t), 0) < lax.broadcasted_iota(I32, (t, t), 1))
    prefix = jnp.dot(onehot.astype(BF16), jnp.where(before, 1.0, 0.0).astype(BF16),
                     preferred_element_type=F32)
    r1 = jnp.sum(jnp.where(hit1, prefix, 0.0), axis=0, keepdims=True)
    r2 = jnp.sum(jnp.where(hit2, prefix, 0.0), axis=0, keepdims=True)
    grp = float(SLOT_GROUP)
    cnt = jnp.sum(onehot, axis=1, keepdims=True)
    cnt = jnp.floor((cnt + (grp - 1.0)) / grp) * grp
    run = run_ref[...] + cnt
    run_ref[...] = run
    cnt_ref[...] = jnp.broadcast_to(cnt, cnt_ref.shape)
    tot_ref[...] = jnp.broadcast_to(run, tot_ref.shape)

    zi = jnp.zeros((V7X_SUBLANES - 4, t), I32)
    ids_ref[...] = jnp.concatenate([e1, e2, r1.astype(I32), r2.astype(I32), zi], axis=0)
    zf = jnp.zeros((V7X_SUBLANES - 2, t), F32)
    wts_ref[...] = jnp.concatenate([s1 / tot, s2 / tot, zf], axis=0)


def _route(logits, router_b):
    n_tok = logits.shape[0]
    tr = ROUTE_TILE
    return pl.pallas_call(
        _route_kernel,
        grid=(n_tok // tr,),
        in_specs=[
            pl.BlockSpec((tr, V7X_LANES), lambda i: (i, 0)),
            pl.BlockSpec((N_EXPERTS, 1), lambda i: (0, 0)),
        ],
        out_specs=[
            pl.BlockSpec((V7X_SUBLANES, tr), lambda i: (0, i)),
            pl.BlockSpec((V7X_SUBLANES, tr), lambda i: (0, i)),
            pl.BlockSpec((None, N_EXPERTS, V7X_LANES), lambda i: (i, 0, 0)),
            pl.BlockSpec((N_EXPERTS, V7X_LANES), lambda i: (0, 0)),
        ],
        out_shape=[
            jax.ShapeDtypeStruct((V7X_SUBLANES, n_tok), I32),
            jax.ShapeDtypeStruct((V7X_SUBLANES, n_tok), F32),
            jax.ShapeDtypeStruct((n_tok // tr, N_EXPERTS, V7X_LANES), F32),
            jax.ShapeDtypeStruct((N_EXPERTS, V7X_LANES), F32),
        ],
        scratch_shapes=[pltpu.VMEM((N_EXPERTS, 1), F32)],
        compiler_params=_cparams(("arbitrary",)),
        name="route",
    )(logits, router_b.reshape(N_EXPERTS, 1))


def _lane_table(vals, width):
    lane = lax.broadcasted_iota(I32, (1, width), 1)
    out = jnp.zeros((1, width), F32)
    for e, v in enumerate(vals):
        out = jnp.where(lane == e, v, out)
    return out


def _slots_kernel(ids_ref, cnt_ref, tot_ref, loc_ref, gmap_ref, blk_ref, off_ref):
    step = pl.program_id(0)

    @pl.when(step == 0)
    def _():
        off_ref[...] = jnp.zeros_like(off_ref)

    blk = float(EXPERT_BLOCK)
    grp = float(SLOT_GROUP)
    cnt = cnt_ref[...][:, 0:1]
    tot = tot_ref[...][:, 0:1]
    off = off_ref[...]
    region = jnp.floor((tot + (blk - 1.0)) / blk) * blk
    starts, ends, local = [], [], []
    run = jnp.zeros((1, 1), F32)
    lrun = jnp.zeros((1, 1), F32)
    for e in range(N_EXPERTS):
        starts.append(run)
        run = run + region[e:e + 1, :]
        ends.append(run)
        local.append(lrun)
        lrun = lrun + cnt[e:e + 1, :]

    ids = ids_ref[...]
    e1, e2 = ids[0:1, :], ids[1:2, :]
    t = ids.shape[1]
    l1 = jnp.zeros((1, t), F32)
    l2 = jnp.zeros((1, t), F32)
    for e in range(N_EXPERTS):
        l1 = jnp.where(e1 == e, local[e], l1)
        l2 = jnp.where(e2 == e, local[e], l2)
    zi = jnp.zeros((V7X_SUBLANES - 2, t), I32)
    loc_ref[...] = jnp.concatenate([l1.astype(I32) + ids[2:3, :], l2.astype(I32) + ids[3:4, :], zi], axis=0)

    wg = gmap_ref.shape[1]
    first = lax.broadcasted_iota(I32, (1, wg), 1).astype(F32) * grp
    dest = jnp.zeros((1, wg), F32)
    for e in range(N_EXPERTS):
        inside = (first >= local[e]) & (first < local[e] + cnt[e:e + 1, :])
        dest = jnp.where(inside, starts[e] + off[e:e + 1, :] + (first - local[e]), dest)
    n_groups = jnp.broadcast_to(lrun / grp, (1, wg))
    zg = jnp.zeros((V7X_SUBLANES - 2, wg), I32)
    gmap_ref[...] = jnp.concatenate([(dest / grp).astype(I32), n_groups.astype(I32), zg], axis=0)
    off_ref[...] = off + cnt

    w = blk_ref.shape[1]
    first_row = lax.broadcasted_iota(I32, (1, w), 1).astype(F32) * blk
    owner = jnp.zeros((1, w), F32)
    for e in range(N_EXPERTS):
        owner = owner + jnp.where(first_row >= ends[e], 1.0, 0.0)
    owner = jnp.minimum(owner, float(N_EXPERTS - 1))
    used = jnp.broadcast_to(ends[-1] / blk, (1, w))
    pad_first = _lane_table([(starts[e] + tot[e:e + 1, :]) / grp for e in range(N_EXPERTS)], w)
    pad_count = _lane_table([(region[e:e + 1, :] - tot[e:e + 1, :]) / grp for e in range(N_EXPERTS)], w)
    zb = jnp.zeros((V7X_SUBLANES - 4, w), I32)
    blk_ref[...] = jnp.concatenate([owner.astype(I32), used.astype(I32), pad_first.astype(I32),
                                    pad_count.astype(I32), zb], axis=0)


def _slots(ids, cnt, tot, n_blocks):
    n_tok = ids.shape[1]
    tr = ROUTE_TILE
    wblk = -(-n_blocks // V7X_LANES) * V7X_LANES
    return pl.pallas_call(
        _slots_kernel,
        grid=(n_tok // tr,),
        in_specs=[
            pl.BlockSpec((V7X_SUBLANES, tr), lambda i: (0, i)),
            pl.BlockSpec((None, N_EXPERTS, V7X_LANES), lambda i: (i, 0, 0)),
            pl.BlockSpec((N_EXPERTS, V7X_LANES), lambda i: (0, 0)),
        ],
        out_specs=[
            pl.BlockSpec((V7X_SUBLANES, tr), lambda i: (0, i)),
            pl.BlockSpec((None, V7X_SUBLANES, SORT_GROUPS_PAD), lambda i: (i, 0, 0)),
            pl.BlockSpec((V7X_SUBLANES, wblk), lambda i: (0, 0)),
        ],
        out_shape=[
            jax.ShapeDtypeStruct((V7X_SUBLANES, n_tok), I32),
            jax.ShapeDtypeStruct((n_tok // tr, V7X_SUBLANES, SORT_GROUPS_PAD), I32),
            jax.ShapeDtypeStruct((V7X_SUBLANES, wblk), I32),
        ],
        scratch_shapes=[pltpu.VMEM((N_EXPERTS, 1), F32)],
        compiler_params=_cparams(("arbitrary",)),
        name="slots",
    )(ids, cnt, tot)


def _group_rows(group):
    if isinstance(group, int):
        return pl.ds(group * SLOT_GROUP, SLOT_GROUP)
    return pl.ds(pl.multiple_of(group * SLOT_GROUP, SLOT_GROUP), SLOT_GROUP)


def _group_copy(src_ref, src_group, dst_ref, dst_group, sem):
    return pltpu.make_async_copy(src_ref.at[_group_rows(src_group)], dst_ref.at[_group_rows(dst_group)], sem)


def _dispatch_kernel(gmap_ref, gprev_ref, blk_ref, loc_ref, *refs, n_first):
    if n_first is None:
        h_ref, xb_ref, sorted_ref, zero_ref, sem = refs
        second_ref = None
    else:
        h_ref, second_ref, xb_ref, sorted_ref, zero_ref, sem = refs
    step = pl.program_id(0)
    last = pl.num_programs(0) - 1
    buf = step % 2
    loc = loc_ref[...]
    slot = lax.broadcasted_iota(I32, (SORT_ROWS, loc.shape[1]), 0)
    perm = jnp.where(slot == loc[0:1, :], 1.0, jnp.where(slot == loc[1:2, :], 1.0, 0.0)).astype(BF16)

    def sort_rows(src_ref):
        sorted_ref[buf] = jnp.dot(perm, src_ref[...].astype(BF16), preferred_element_type=F32).astype(BF16)

    if second_ref is None:
        sort_rows(h_ref)
    else:
        pl.when(step < n_first)(lambda: sort_rows(h_ref))
        pl.when(step >= n_first)(lambda: sort_rows(second_ref))

    def tile_copy(map_ref, which, g):
        return _group_copy(sorted_ref.at[which], g, xb_ref, map_ref[0, g], sem.at[which])

    def start(g, c):
        tile_copy(gmap_ref, buf, g).start()
        return c

    def wait_tile(map_ref, which):
        rows = pl.ds(0, map_ref[1, 0] * SLOT_GROUP)
        pltpu.make_async_copy(sorted_ref.at[which, rows], xb_ref.at[rows], sem.at[which]).wait()

    lax.fori_loop(0, gmap_ref[1, 0], start, 0)

    @pl.when(step == last)
    def _():
        zero_ref[...] = jnp.zeros_like(zero_ref)
        for e in range(N_EXPERTS):
            first = blk_ref[2, e]

            def zstart(g, c, first=first):
                _group_copy(zero_ref, 0, xb_ref, first + g, sem.at[2]).start()
                return c

            def zwait(g, c, first=first):
                _group_copy(zero_ref, 0, xb_ref, first + g, sem.at[2]).wait()
                return c

            lax.fori_loop(0, blk_ref[3, e], zstart, 0)
            lax.fori_loop(0, blk_ref[3, e], zwait, 0)

        def block_copy(b):
            rows = pl.ds(pl.multiple_of(b * EXPERT_BLOCK, EXPERT_BLOCK), EXPERT_BLOCK)
            return pltpu.make_async_copy(zero_ref, xb_ref.at[rows], sem.at[2])

        def bstart(b, c):
            block_copy(b).start()
            return c

        def bwait(b, c):
            block_copy(b).wait()
            return c

        n_blocks = xb_ref.shape[0] // EXPERT_BLOCK
        lax.fori_loop(blk_ref[1, 0], n_blocks, bstart, 0)
        lax.fori_loop(blk_ref[1, 0], n_blocks, bwait, 0)

    @pl.when((step > 0) & (gprev_ref[1, 0] > 0))
    def _():
        wait_tile(gprev_ref, 1 - buf)

    @pl.when((step == last) & (gmap_ref[1, 0] > 0))
    def _():
        wait_tile(gmap_ref, buf)


def _dispatch(gmap, blk, loc, h2, h2_second, n_slots):
    tr = ROUTE_TILE
    n_first = h2.shape[0] // tr
    n_tiles = n_first
    in_specs = [
        pl.BlockSpec((None, V7X_SUBLANES, SORT_GROUPS_PAD), lambda i: (i, 0, 0), memory_space=pltpu.SMEM),
        pl.BlockSpec((None, V7X_SUBLANES, SORT_GROUPS_PAD), lambda i: (jnp.maximum(i - 1, 0), 0, 0),
                     memory_space=pltpu.SMEM),
        pl.BlockSpec(blk.shape, lambda i: (0, 0), memory_space=pltpu.SMEM),
        pl.BlockSpec((V7X_SUBLANES, tr), lambda i: (0, i)),
        pl.BlockSpec((tr, D_MODEL), lambda i: (jnp.minimum(i, n_first - 1), 0)),
    ]
    args = [gmap, gmap, blk, loc, h2]
    if h2_second is not None:
        n_tiles += h2_second.shape[0] // tr
        in_specs.append(pl.BlockSpec((tr, D_MODEL), lambda i: (jnp.maximum(i - n_first, 0), 0)))
        args.append(h2_second)
    return pl.pallas_call(
        functools.partial(_dispatch_kernel, n_first=None if h2_second is None else n_first),
        grid=(n_tiles,),
        in_specs=in_specs,
        out_specs=pl.BlockSpec(memory_space=pl.ANY),
        out_shape=jax.ShapeDtypeStruct((n_slots, D_MODEL), BF16),
        scratch_shapes=[pltpu.VMEM((2, SORT_ROWS, D_MODEL), BF16), pltpu.VMEM((EXPERT_BLOCK, D_MODEL), BF16),
                        pltpu.SemaphoreType.DMA((3,))],
        compiler_params=_cparams(("arbitrary",)),
        name="dispatch",
    )(*args)


def _experts_kernel(blk_ref, used_ref, x_ref, w1_ref, w3_ref, w2_ref, y_ref, w1b, w3b, w2b):
    i = pl.program_id(0)
    prev = blk_ref[jnp.maximum(i - 1, 0)]

    @pl.when((i == 0) | (blk_ref[i] != prev))
    def _():
        w1b[...] = w1_ref[...].astype(BF16)
        w3b[...] = w3_ref[...].astype(BF16)
        w2b[...] = w2_ref[...].astype(BF16)

    @pl.when(i < used_ref[0])
    def _():
        x = x_ref[...]
        a = jnp.dot(x, w1b[...], preferred_element_type=F32)
        b = jnp.dot(x, w3b[...], preferred_element_type=F32)
        hid = (a * jax.nn.sigmoid(a) * b).astype(BF16)
        y_ref[...] = jnp.dot(hid, w2b[...], preferred_element_type=F32).astype(BF16)

    @pl.when(i >= used_ref[0])
    def _():
        y_ref[...] = jnp.zeros_like(y_ref)


def _experts(blk_e, used, xb, w1, w3, w2, layer):
    n_slots = xb.shape[0]
    bm = EXPERT_BLOCK
    row_map = lambda i, be, nu: (jnp.minimum(i, nu[0] - 1), 0)
    w_map = lambda i, be, nu: (layer, be[i], 0, 0)
    grid_spec = pltpu.PrefetchScalarGridSpec(
        num_scalar_prefetch=2,
        grid=(n_slots // bm,),
        in_specs=[
            pl.BlockSpec((bm, D_MODEL), row_map),
            pl.BlockSpec((None, None, D_MODEL, D_EXPERT), w_map),
            pl.BlockSpec((None, None, D_MODEL, D_EXPERT), w_map),
            pl.BlockSpec((None, None, D_EXPERT, D_MODEL), w_map),
        ],
        out_specs=pl.BlockSpec((bm, D_MODEL), lambda i, be, nu: (i, 0)),
        scratch_shapes=[pltpu.VMEM((D_MODEL, D_EXPERT), BF16), pltpu.VMEM((D_MODEL, D_EXPERT), BF16),
                        pltpu.VMEM((D_EXPERT, D_MODEL), BF16)],
    )
    return pl.pallas_call(
        _experts_kernel,
        grid_spec=grid_spec,
        out_shape=jax.ShapeDtypeStruct((n_slots, D_MODEL), BF16),
        compiler_params=_cparams(("arbitrary",)),
        name="experts",
    )(blk_e, used, xb, w1, w3, w2)


_TN = (((0,), (0,)), ((), ()))


def _combine_kernel(gmap_ref, gnext_ref, loc_ref, wts_ref, x_ref, mods_ref, yb_ref, o_ref, ys_ref, sem,
                    *, tiles_per_seq, fixed_row):
    step = pl.program_id(0)
    buf = step % 2

    def fetch(map_ref, which, g):
        return _group_copy(yb_ref, map_ref[0, g], ys_ref.at[which], g, sem.at[which])

    def start_own(g, c):
        fetch(gmap_ref, buf, g).start()
        return c

    def start_next(g, c):
        fetch(gnext_ref, 1 - buf, g).start()
        return c

    def wait_own():
        rows = pl.ds(0, gmap_ref[1, 0] * SLOT_GROUP)
        pltpu.make_async_copy(yb_ref.at[rows], ys_ref.at[buf, rows], sem.at[buf]).wait()

    @pl.when(step == 0)
    def _():
        ys_ref[...] = jnp.zeros_like(ys_ref)
        lax.fori_loop(0, gmap_ref[1, 0], start_own, 0)

    @pl.when(step + 1 < pl.num_programs(0))
    def _():
        lax.fori_loop(0, gnext_ref[1, 0], start_next, 0)

    loc = loc_ref[...]
    wts = wts_ref[...]
    slot = lax.broadcasted_iota(I32, (SORT_ROWS, loc.shape[1]), 0)
    perm = jnp.where(slot == loc[0:1, :], wts[0:1, :], jnp.where(slot == loc[1:2, :], wts[1:2, :], 0.0))
    pl.when(gmap_ref[1, 0] > 0)(wait_own)
    y = lax.dot_general(perm.astype(BF16), ys_ref[buf], _TN, preferred_element_type=F32)
    m = _mod_row(mods_ref, step, tiles_per_seq, fixed_row)
    o_ref[...] = x_ref[...] + m[:, 5 * D_MODEL:6 * D_MODEL] * y


def _combine(gmap, loc, wts, x_new, mods_l, yb, *, seq_len, fixed_row, tile_offset):
    n_tok = x_new.shape[0]
    tr = ROUTE_TILE
    tps = max(seq_len // tr, 1)
    n_tiles = n_tok // tr
    return pl.pallas_call(
        functools.partial(_combine_kernel, tiles_per_seq=tps, fixed_row=fixed_row),
        grid=(n_tiles,),
        in_specs=[
            pl.BlockSpec((None, V7X_SUBLANES, SORT_GROUPS_PAD), lambda i: (i + tile_offset, 0, 0),
                         memory_space=pltpu.SMEM),
            pl.BlockSpec((None, V7X_SUBLANES, SORT_GROUPS_PAD),
                         lambda i: (jnp.minimum(i + 1, n_tiles - 1) + tile_offset, 0, 0), memory_space=pltpu.SMEM),
            pl.BlockSpec((V7X_SUBLANES, tr), lambda i: (0, i + tile_offset)),
            pl.BlockSpec((V7X_SUBLANES, tr), lambda i: (0, i + tile_offset)),
            pl.BlockSpec((tr, D_MODEL), lambda i: (i, 0)),
            pl.BlockSpec((MODS_ROWS, 6 * D_MODEL), lambda i: (0, 0)),
            pl.BlockSpec(memory_space=pl.ANY),
        ],
        out_specs=pl.BlockSpec((tr, D_MODEL), lambda i: (i, 0)),
        out_shape=jax.ShapeDtypeStruct((n_tok, D_MODEL), F32),
        scratch_shapes=[pltpu.VMEM((2, SORT_ROWS, D_MODEL), BF16), pltpu.SemaphoreType.DMA((2,))],
        compiler_params=_cparams(("arbitrary",)),
        name="combine",
    )(gmap, gmap, loc, wts, x_new, mods_l, yb)


def _channel_dft_tables():
    j = np.arange(F_GDIM)
    ang = 2.0 * np.pi * ((j[:, None] * j[None, :]) % F_GDIM) / F_GDIM
    eye = np.eye(F_GROUPS)
    return (jnp.asarray(np.kron(eye, np.cos(ang)), F32).astype(BF16),
            jnp.asarray(np.kron(eye, np.sin(ang)), F32).astype(BF16))


def _position_dft_tables(seq_len):
    scale = 1.0 / math.sqrt(seq_len * F_GDIM)
    k = np.arange(seq_len, dtype=np.int64)
    ang = 2.0 * np.pi * ((k[:, None] * k[None, :]) % seq_len) / seq_len
    return (jnp.asarray(np.cos(ang) * scale, F32).astype(BF16),
            jnp.asarray(np.sin(ang) * scale, F32).astype(BF16))


def _two_stage_dft_tables(seq_len):
    r = FFT_RADIX
    assert seq_len == r * r
    scale = 1.0 / math.sqrt(seq_len * F_GDIM)
    j = np.arange(r, dtype=np.int64)
    ang_r = 2.0 * np.pi * ((j[:, None] * j[None, :]) % r) / r
    cs, ss = np.cos(ang_r), np.sin(ang_r)
    fst = jnp.asarray(np.concatenate([cs, ss], axis=0) * scale, F32).astype(BF16)
    g = jnp.asarray(np.concatenate([cs, ss], axis=1), F32).astype(BF16)
    ang_t = 2.0 * np.pi * (j[:, None] * j[None, :]) / seq_len
    tc3 = jnp.asarray(np.repeat(np.cos(ang_t)[:, :, None], V7X_LANES, axis=2), F32)
    ts3 = jnp.asarray(np.repeat(np.sin(ang_t)[:, :, None], V7X_LANES, axis=2), F32)
    return fst, g, tc3, ts3


def _rope_tables(seq_len):
    t = np.arange(seq_len)
    row = (t // GRID_W).astype(np.float64)
    col = (t % GRID_W).astype(np.float64)
    inv = np.power(ROPE_BASE, -np.arange(ROPE_PER_AXIS, dtype=np.float64) / ROPE_PER_AXIS)
    ang = np.concatenate([row[:, None] * inv, col[:, None] * inv], axis=-1)
    cos = np.cos(ang)
    sin = np.sin(ang)
    cos_h = np.concatenate([cos, cos], axis=-1)
    sin_h = np.concatenate([-sin, sin], axis=-1)
    return (jnp.asarray(np.tile(cos_h, (1, NA_HEADS)), F32), jnp.asarray(np.tile(sin_h, (1, NA_HEADS)), F32))


def _bias_table(rpb_l):
    col = np.arange(GRID_W)
    col_start = np.clip(col - NA_WIN_C // 2, 0, GRID_W - NA_WIN_C)
    col_mask = (col[None, :] >= col_start[:, None]) & (col[None, :] < col_start[:, None] + NA_WIN_C)
    dc = np.clip(col[None, :] - col[:, None] + (NA_WIN_C - 1), 0, 2 * NA_WIN_C - 2)
    n_dc = 2 * NA_WIN_C - 1
    pick = (dc.reshape(-1)[None, :] == np.arange(n_dc)[:, None]).astype(np.float32)
    e = jnp.dot(rpb_l.reshape(-1, n_dc), jnp.asarray(pick), precision=HIGHEST)
    e = e.reshape(NA_HEADS, 2 * NA_WIN_R - 1, GRID_W, GRID_W)
    e = jnp.where(jnp.asarray(col_mask)[None, None], e * LOG2E, NEG_BIG)
    b = jnp.stack([e[:, o:o + NA_WIN_R] for o in range(NA_WIN_R)], axis=0)
    b = b.transpose(0, 1, 3, 2, 4)
    return b.reshape(NA_WIN_R, NA_HEADS, GRID_W, NA_WIN_R * GRID_W)


def _moe(h2, h2_second, logits, w1, w3, w2, layer, router_b):
    n_tok = logits.shape[0]
    n_tiles = n_tok // ROUTE_TILE
    max_rows = 2 * n_tok + N_EXPERTS * n_tiles * (SLOT_GROUP - 1) + N_EXPERTS * (EXPERT_BLOCK - 1)
    n_blocks = -(-max_rows // EXPERT_BLOCK)
    n_slots = n_blocks * EXPERT_BLOCK
    ids, wts, cnt, tot = _route(logits, router_b)
    loc, gmap, blk = _slots(ids, cnt, tot, n_blocks)
    xb = _dispatch(gmap, blk, loc, h2, h2_second, n_slots)
    yb = _experts(blk[0, :n_blocks], blk[1, 0:1], xb, w1, w3, w2, layer)
    return yb, gmap, loc, wts


def kernel(x, c, ctx, c_ctx, ada_w, ada_b, norm1_g, w_in, qn_g, kn_g, rpb, conv_w, w_f, w_na, w_cv, w_o,
           norm2_g, router_w, router_b, w1, w3, w2):
    bsz, seq_len, d = x.shape
    ctx_len = ctx.shape[1]
    n_lat = bsz * seq_len
    n_ctx = bsz * ctx_len
    ctx_row = bsz

    c8 = jnp.concatenate([c, c_ctx[None, :], jnp.zeros((MODS_ROWS - bsz - 1, d), F32)], axis=0)
    mods = _mods(c8, ada_w, ada_b)

    cbd, sbd = _channel_dft_tables()
    fst, g_dft, tc3, ts3 = _two_stage_dft_tables(seq_len)
    c_ctx_t, s_ctx_t = _position_dft_tables(ctx_len)
    cos_t, sin_t = _rope_tables(seq_len)
    mavg = jnp.asarray(np.kron(np.eye(NA_HEADS), np.full((HEAD_DIM, HEAD_DIM), 1.0 / HEAD_DIM)), F32).astype(BF16)
    rw_hi = router_w.astype(BF16)
    rw_lo = (router_w - rw_hi.astype(F32)).astype(BF16)
    rwt = jnp.concatenate([rw_hi, rw_lo, jnp.zeros((d, V7X_LANES - 2 * N_EXPERTS), BF16)], axis=1)

    xl = x.reshape(n_lat, d)
    xc = ctx.reshape(n_ctx, d)
    for l in range(DEPTH):
        last = l == DEPTH - 1
        w_proj = w_in[l][:, :COL_G].astype(BF16)
        w_gate = w_in[l][:, COL_G:].astype(BF16)
        wf, wna, wcv, wo = (w_f[l].astype(BF16), w_na[l].astype(BF16), w_cv[l].astype(BF16), w_o[l].astype(BF16))
        n1 = norm1_g[l].reshape(1, d)
        n2 = norm2_g[l].reshape(1, d)
        qg = jnp.tile(qn_g[l], NA_HEADS).reshape(1, NA_WIDTH)
        kg = jnp.tile(kn_g[l], NA_HEADS).reshape(1, NA_WIDTH)
        bias_tab = _bias_table(rpb[l])
        mods_l = mods[l]

        a_c, b_c, q_c, k_c, v_c, u_c, bg_c = _proj(
            xc, mods_l, n1, w_proj, qg, kg, mavg, cbd, sbd, None, None,
            seq_len=ctx_len, n_seq=bsz, fixed_row=ctx_row, dft_dtype=BF16)
        a_l, b_l, q_l, k_l, v_l, u_l, bg_l = _proj(
            xl, mods_l, n1, w_proj, qg, kg, mavg, cbd, sbd, cos_t, sin_t,
            seq_len=seq_len, n_seq=bsz, fixed_row=None, dft_dtype=F32)

        f_l = _fourier_two_stage(a_l, b_l, fst, g_dft, tc3, ts3)
        attn_l = _attn(q_l, k_l, v_l, k_c, v_c, bias_tab, n_seq=bsz, seq_len=seq_len, ctx_len=ctx_len)
        xl_new, h2_l, lg_l = _merge(xl, mods_l, n1, n2, f_l, attn_l, u_l, bg_l, conv_w[l], w_gate,
                                    wf, wna, wcv, wo, rwt, seq_len=seq_len, fixed_row=None)
        if last:
            yb, gmap, loc, wts = _moe(h2_l, None, lg_l, w1, w3, w2, l, router_b)
            xl = _combine(gmap, loc, wts, xl_new, mods_l, yb, seq_len=seq_len, fixed_row=None, tile_offset=0)
        else:
            f_c = _fourier(c_ctx_t, s_ctx_t, a_c, b_c)
            attn_c = _ctx_attn(q_c, k_c, v_c, n_seq=bsz, ctx_len=ctx_len)
            xc_new, h2_c, lg_c = _merge(xc, mods_l, n1, n2, f_c, attn_c, u_c, bg_c, conv_w[l], w_gate,
                                        wf, wna, wcv, wo, rwt, seq_len=ctx_len, fixed_row=ctx_row)
            lg = jnp.concatenate([lg_l, lg_c], axis=0)
            yb, gmap, loc, wts = _moe(h2_l, h2_c, lg, w1, w3, w2, l, router_b)
            xl = _combine(gmap, loc, wts, xl_new, mods_l, yb, seq_len=seq_len, fixed_row=None, tile_offset=0)
            xc = _combine(gmap, loc, wts, xc_new, mods_l, yb, seq_len=ctx_len, fixed_row=ctx_row,
                          tile_offset=n_lat // ROUTE_TILE)
    return xl.reshape(bsz, seq_len, d)
```

```python
import functools
import math

import numpy as np
import jax
import jax.numpy as jnp
from jax import lax
from jax.experimental import pallas as pl
from jax.experimental.pallas import tpu as pltpu

F32 = jnp.float32
BF16 = jnp.bfloat16
I32 = jnp.int32
HIGHEST = lax.Precision.HIGHEST

D_MODEL = 1024
DEPTH = 2
GRID_W = 64
EPS = 1e-6
F_GROUPS = 4
F_GDIM = 64
F_WIDTH = 256
NA_HEADS = 8
HEAD_DIM = 64
NA_WIDTH = 512
NA_WIN_R = 8
NA_WIN_C = 16
ATTN_SCALE = HEAD_DIM ** -0.5
LOG2E = math.log2(math.e)
ROPE_BASE = 10000.0
ROPE_PER_AXIS = HEAD_DIM // 4
CONV_WIDTH = 256
COL_Q = 256
COL_K = 768
COL_V = 1280
COL_CX = 1792
COL_CB = 2048
COL_CC = 2304
COL_G = 2560
N_EXPERTS = 16
N_GROUPS = 4
EXPERTS_PER_GROUP = 4
D_EXPERT = 512

V7X_LANES = 128
V7X_SUBLANES = 8
V7X_MXU_DIM = 256

TOKEN_TILE = 512
ROUTE_TILE = 512
EXPERT_BLOCK = 512
SLOT_GROUP = 2 * V7X_SUBLANES
SORT_ROWS = -(-(2 * ROUTE_TILE + N_EXPERTS * (SLOT_GROUP - 1)) // V7X_LANES) * V7X_LANES
SORT_GROUPS_PAD = -(-(SORT_ROWS // SLOT_GROUP) // V7X_LANES) * V7X_LANES
HEADS_PER_GROUP = V7X_MXU_DIM // HEAD_DIM
ATTN_ROWS_PER_STEP = 8
FFT_RADIX = 64
FFT_STEP = V7X_SUBLANES
NEG_BIG = -1e30
MODS_ROWS = 8
VMEM_LIMIT = 48 * 1024 * 1024


def _cparams(sem):
    return pltpu.CompilerParams(dimension_semantics=sem, vmem_limit_bytes=VMEM_LIMIT)


def _mods_kernel(c_ref, w_ref, b_ref, o_ref):
    c = c_ref[...]
    sc = c * jax.nn.sigmoid(c)
    o_ref[...] = jnp.dot(sc, w_ref[...], precision=HIGHEST, preferred_element_type=F32) + b_ref[...]


def _mods(c8, ada_w, ada_b):
    nb = 1536
    return pl.pallas_call(
        _mods_kernel,
        grid=(DEPTH, 6 * D_MODEL // nb),
        in_specs=[
            pl.BlockSpec((MODS_ROWS, D_MODEL), lambda l, j: (0, 0)),
            pl.BlockSpec((None, D_MODEL, nb), lambda l, j: (l, 0, j)),
            pl.BlockSpec((None, 1, nb), lambda l, j: (l, 0, j)),
        ],
        out_specs=pl.BlockSpec((None, MODS_ROWS, nb), lambda l, j: (l, 0, j)),
        out_shape=jax.ShapeDtypeStruct((DEPTH, MODS_ROWS, 6 * D_MODEL), F32),
        compiler_params=_cparams(("arbitrary", "arbitrary")),
        name="mods",
    )(c8, ada_w, ada_b.reshape(DEPTH, 1, 6 * D_MODEL))


def _norm_mod(x, g, shift, scale):
    ms = jnp.mean(x * x, axis=-1, keepdims=True)
    return (x * lax.rsqrt(ms + EPS) * g) * (1.0 + scale) + shift


def _mod_row(mods_ref, tile, tiles_per_seq, fixed_row):
    row = fixed_row if fixed_row is not None else tile // tiles_per_seq
    return mods_ref[pl.ds(row, 1), :]


def _proj_kernel(*refs, tiles_per_seq, fixed_row, rope):
    if rope:
        (x_ref, mods_ref, g_ref, w_ref, qg_ref, kg_ref, mavg_ref, cbd_ref, sbd_ref, cos_ref, sin_ref,
         a_ref, b_ref, q_ref, k_ref, v_ref, u_ref, bg_ref) = refs
    else:
        (x_ref, mods_ref, g_ref, w_ref, qg_ref, kg_ref, mavg_ref, cbd_ref, sbd_ref,
         a_ref, b_ref, q_ref, k_ref, v_ref, u_ref, bg_ref) = refs
    m = _mod_row(mods_ref, pl.program_id(0), tiles_per_seq, fixed_row)
    h = _norm_mod(x_ref[...], g_ref[...], m[:, 0:D_MODEL], m[:, D_MODEL:2 * D_MODEL])
    p = jnp.dot(h.astype(BF16), w_ref[...], preferred_element_type=F32)

    uf = p[:, 0:COL_Q].astype(BF16)
    a_ref[...] = jnp.dot(uf, cbd_ref[...], preferred_element_type=F32).astype(a_ref.dtype)
    b_ref[...] = jnp.dot(uf, sbd_ref[...], preferred_element_type=F32).astype(b_ref.dtype)

    def head_norm(t, g):
        ms = jnp.dot((t * t).astype(BF16), mavg_ref[...], preferred_element_type=F32)
        return t * lax.rsqrt(ms + EPS) * g

    def rotate(t):
        n = t.shape[-1]
        lane = lax.broadcasted_iota(I32, t.shape, 1)
        first_half = (lane % HEAD_DIM) < (HEAD_DIM // 2)
        swapped = jnp.where(first_half, pltpu.roll(t, n - HEAD_DIM // 2, 1), pltpu.roll(t, HEAD_DIM // 2, 1))
        return t * cos_ref[...] + swapped * sin_ref[...]

    q = head_norm(p[:, COL_Q:COL_K], qg_ref[...])
    k = head_norm(p[:, COL_K:COL_V], kg_ref[...])
    if rope:
        q = rotate(q)
        k = rotate(k)
    q_ref[...] = (q * (ATTN_SCALE * LOG2E)).astype(BF16)
    k_ref[...] = k.astype(BF16)
    v_ref[...] = p[:, COL_V:COL_CX].astype(BF16)
    u_ref[...] = p[:, COL_CC:COL_G] * p[:, COL_CX:COL_CB]
    bg_ref[...] = p[:, COL_CB:COL_CC]


def _proj(x2, mods_l, norm_g, w_proj, qg, kg, mavg, cbd, sbd, cos_t, sin_t, *, seq_len, n_seq, fixed_row,
          dft_dtype):
    n_tok = x2.shape[0]
    tm = min(TOKEN_TILE, seq_len)
    tps = seq_len // tm
    rope = cos_t is not None
    const = lambda i: (0, 0)
    in_specs = [
        pl.BlockSpec((tm, D_MODEL), lambda i: (i, 0)),
        pl.BlockSpec((MODS_ROWS, 6 * D_MODEL), const),
        pl.BlockSpec((1, D_MODEL), const),
        pl.BlockSpec((D_MODEL, COL_G), const),
        pl.BlockSpec((1, NA_WIDTH), const),
        pl.BlockSpec((1, NA_WIDTH), const),
        pl.BlockSpec((NA_WIDTH, NA_WIDTH), const),
        pl.BlockSpec((F_WIDTH, F_WIDTH), const),
        pl.BlockSpec((F_WIDTH, F_WIDTH), const),
    ]
    args = [x2, mods_l, norm_g, w_proj, qg, kg, mavg, cbd, sbd]
    if rope:
        in_specs += [pl.BlockSpec((tm, NA_WIDTH), lambda i: (i % tps, 0))] * 2
        args += [cos_t, sin_t]
    tok = lambda w: pl.BlockSpec((tm, w), lambda i: (i, 0))
    fmap = pl.BlockSpec((tm, F_WIDTH), lambda i: (i % tps, i // tps))
    out_specs = [fmap, fmap, tok(NA_WIDTH), tok(NA_WIDTH), tok(NA_WIDTH), tok(CONV_WIDTH), tok(CONV_WIDTH)]
    out_shape = [
        jax.ShapeDtypeStruct((seq_len, n_seq * F_WIDTH), dft_dtype),
        jax.ShapeDtypeStruct((seq_len, n_seq * F_WIDTH), dft_dtype),
        jax.ShapeDtypeStruct((n_tok, NA_WIDTH), BF16),
        jax.ShapeDtypeStruct((n_tok, NA_WIDTH), BF16),
        jax.ShapeDtypeStruct((n_tok, NA_WIDTH), BF16),
        jax.ShapeDtypeStruct((n_tok, CONV_WIDTH), F32),
        jax.ShapeDtypeStruct((n_tok, CONV_WIDTH), F32),
    ]
    return pl.pallas_call(
        functools.partial(_proj_kernel, tiles_per_seq=tps, fixed_row=fixed_row, rope=rope),
        grid=(n_tok // tm,),
        in_specs=in_specs,
        out_specs=out_specs,
        out_shape=out_shape,
        compiler_params=_cparams(("arbitrary",)),
        name="proj",
    )(*args)


def _fourier_kernel(c_ref, s_ref, a_ref, b_ref, o_ref):
    o = (jnp.dot(c_ref[...], a_ref[...], preferred_element_type=F32)
         - jnp.dot(s_ref[...], b_ref[...], preferred_element_type=F32))
    o_ref[...] = o.astype(BF16)


def _fourier(c_tab, s_tab, a, b):
    seq_len, width = a.shape
    tk = min(seq_len, 256)
    full = lambda i: (0, 0)
    return pl.pallas_call(
        _fourier_kernel,
        grid=(seq_len // tk,),
        in_specs=[
            pl.BlockSpec((tk, seq_len), lambda i: (i, 0)),
            pl.BlockSpec((tk, seq_len), lambda i: (i, 0)),
            pl.BlockSpec((seq_len, width), full, pipeline_mode=pl.Buffered(1)),
            pl.BlockSpec((seq_len, width), full, pipeline_mode=pl.Buffered(1)),
        ],
        out_specs=pl.BlockSpec((tk, width), lambda i: (i, 0)),
        out_shape=jax.ShapeDtypeStruct((seq_len, width), BF16),
        compiler_params=_cparams(("arbitrary",)),
        name="fourier",
    )(c_tab, s_tab, a, b)


def _fft1_kernel(a_ref, b_ref, k1_ref, tc_ref, ts_ref, zr_ref, zi_ref):
    n = FFT_RADIX * FFT_STEP
    width = a_ref.shape[2]
    k1 = k1_ref[...]
    r1 = jnp.dot(k1, a_ref[...].reshape(n, width).astype(BF16), preferred_element_type=F32)
    r2 = jnp.dot(k1, b_ref[...].reshape(n, width).astype(BF16), preferred_element_type=F32)
    yr = r1[0:n] - r2[n:2 * n]
    yi = -(r2[0:n] + r1[n:2 * n])
    tc = tc_ref[...][:, 0:1]
    ts = ts_ref[...][:, 0:1]
    zr_ref[...] = (yr * tc + yi * ts).reshape(zr_ref.shape)
    zi_ref[...] = (yi * tc - yr * ts).reshape(zi_ref.shape)


def _fft2_kernel(zr_ref, zi_ref, k2_ref, f_ref):
    n = FFT_RADIX * FFT_STEP
    width = zr_ref.shape[2]
    zz = jnp.concatenate([zr_ref[...].reshape(n, width), zi_ref[...].reshape(n, width)], axis=0).astype(BF16)
    f_ref[...] = jnp.dot(k2_ref[...], zz, preferred_element_type=F32).reshape(f_ref.shape)


def _fourier_two_stage(a, b, k1, k2, tc, ts):
    seq_len, width = a.shape
    r = FFT_RADIX
    n = r * FFT_STEP
    a3 = a.reshape(r, r, width)
    b3 = b.reshape(r, r, width)
    steps = r // FFT_STEP
    col_blk = pl.BlockSpec((r, FFT_STEP, width), lambda j: (0, j, 0))
    row_blk = pl.BlockSpec((FFT_STEP, r, width), lambda j: (j, 0, 0))
    tw_blk = pl.BlockSpec((n, V7X_LANES), lambda j: (j, 0))
    z_shape = jax.ShapeDtypeStruct((r, r, width), F32)
    zr, zi = pl.pallas_call(
        _fft1_kernel,
        grid=(steps,),
        in_specs=[col_blk, col_blk, pl.BlockSpec((2 * n, n), lambda j: (0, 0)), tw_blk, tw_blk],
        out_specs=[col_blk, col_blk],
        out_shape=[z_shape, z_shape],
        compiler_params=_cparams(("arbitrary",)),
        name="fft1",
    )(a3, b3, k1, tc, ts)
    f3 = pl.pallas_call(
        _fft2_kernel,
        grid=(steps,),
        in_specs=[row_blk, row_blk, pl.BlockSpec((n, 2 * n), lambda j: (0, 0))],
        out_specs=col_blk,
        out_shape=z_shape,
        compiler_params=_cparams(("arbitrary",)),
        name="fft2",
    )(zr, zi, k2)
    return f3.reshape(seq_len, width)


def _stack_heads(qg):
    lane_head = lax.broadcasted_iota(I32, qg.shape, 1) // HEAD_DIM
    zero = jnp.zeros_like(qg)
    return jnp.concatenate([jnp.where(lane_head == h, qg, zero) for h in range(HEADS_PER_GROUP)], axis=0)


def _unstack_heads(o, rows):
    lane_head = lax.broadcasted_iota(I32, (rows, o.shape[1]), 1) // HEAD_DIM
    acc = jnp.zeros((rows, o.shape[1]), F32)
    for h in range(HEADS_PER_GROUP):
        acc = acc + jnp.where(lane_head == h, o[h * rows:(h + 1) * rows, :], 0.0)
    return acc


_NT = (((1,), (1,)), ((), ()))


def _attn_kernel(q_ref, k_ref, v_ref, kc_ref, vc_ref, *rest, rows):
    bias_refs, o_ref = rest[:ATTN_ROWS_PER_STEP], rest[ATTN_ROWS_PER_STEP]
    n_loc = NA_WIN_R * GRID_W
    for j in range(ATTN_ROWS_PER_STEP):
        r = pl.program_id(1) * ATTN_ROWS_PER_STEP + j
        rs = jnp.clip(r - NA_WIN_R // 2, 0, rows - NA_WIN_R)
        start = pl.multiple_of(rs * GRID_W, GRID_W)
        kwin = k_ref[pl.ds(start, n_loc), :]
        vwin = v_ref[pl.ds(start, n_loc), :]
        q = q_ref[j * GRID_W:(j + 1) * GRID_W, :]
        bias_ref = bias_refs[j]
        outs = []
        for g in range(NA_HEADS // HEADS_PER_GROUP):
            sl = slice(g * V7X_MXU_DIM, (g + 1) * V7X_MXU_DIM)
            qs = _stack_heads(q[:, sl])
            s_loc = lax.dot_general(qs, kwin[:, sl], _NT, preferred_element_type=F32)
            bias = bias_ref[g * HEADS_PER_GROUP:(g + 1) * HEADS_PER_GROUP].reshape(HEADS_PER_GROUP * GRID_W, n_loc)
            s_ctx = lax.dot_general(qs, kc_ref[:, sl], _NT, preferred_element_type=F32)
            s = jnp.concatenate([s_loc + bias, s_ctx], axis=1)
            m = jnp.max(s, axis=-1, keepdims=True)
            p = jnp.exp2(s - m)
            denom = jnp.sum(p, axis=-1, keepdims=True)
            pb = p.astype(BF16)
            o = (jnp.dot(pb[:, :n_loc], vwin[:, sl], preferred_element_type=F32)
                 + jnp.dot(pb[:, n_loc:], vc_ref[:, sl], preferred_element_type=F32))
            outs.append(_unstack_heads(o / denom, GRID_W))
        o_ref[j * GRID_W:(j + 1) * GRID_W, :] = jnp.concatenate(outs, axis=1).astype(BF16)


def _attn(q, k, v, kc, vc, bias_tab, *, n_seq, seq_len, ctx_len):
    rows = seq_len // GRID_W
    rps = ATTN_ROWS_PER_STEP
    steps = rows // rps

    def bias_map(j):
        def index(b, s):
            r = s * rps + j
            rs = jnp.clip(r - NA_WIN_R // 2, 0, rows - NA_WIN_R)
            return (rs - r + NA_WIN_R - 1, 0, 0, 0)
        return index

    bias_specs = [pl.BlockSpec((None, NA_HEADS, GRID_W, NA_WIN_R * GRID_W), bias_map(j)) for j in range(rps)]
    return pl.pallas_call(
        functools.partial(_attn_kernel, rows=rows),
        grid=(n_seq, steps),
        in_specs=[
            pl.BlockSpec((rps * GRID_W, NA_WIDTH), lambda b, s: (b * steps + s, 0)),
            pl.BlockSpec((seq_len, NA_WIDTH), lambda b, s: (b, 0)),
            pl.BlockSpec((seq_len, NA_WIDTH), lambda b, s: (b, 0)),
            pl.BlockSpec((ctx_len, NA_WIDTH), lambda b, s: (b, 0)),
            pl.BlockSpec((ctx_len, NA_WIDTH), lambda b, s: (b, 0)),
        ] + bias_specs,
        out_specs=pl.BlockSpec((rps * GRID_W, NA_WIDTH), lambda b, s: (b * steps + s, 0)),
        out_shape=jax.ShapeDtypeStruct((n_seq * seq_len, NA_WIDTH), BF16),
        compiler_params=_cparams(("arbitrary", "arbitrary")),
        name="attn",
    )(q, k, v, kc, vc, *([bias_tab] * rps))


def _ctx_attn_kernel(q_ref, k_ref, v_ref, o_ref):
    q = q_ref[...]
    n = q.shape[0]
    outs = []
    for g in range(NA_HEADS // HEADS_PER_GROUP):
        sl = slice(g * V7X_MXU_DIM, (g + 1) * V7X_MXU_DIM)
        qs = _stack_heads(q[:, sl])
        s = lax.dot_general(qs, k_ref[:, sl], _NT, preferred_element_type=F32)
        m = jnp.max(s, axis=-1, keepdims=True)
        p = jnp.exp2(s - m)
        denom = jnp.sum(p, axis=-1, keepdims=True)
        o = jnp.dot(p.astype(BF16), v_ref[:, sl], preferred_element_type=F32)
        outs.append(_unstack_heads(o / denom, n))
    o_ref[...] = jnp.concatenate(outs, axis=1).astype(BF16)


def _ctx_attn(q, k, v, *, n_seq, ctx_len):
    spec = pl.BlockSpec((ctx_len, NA_WIDTH), lambda b: (b, 0))
    return pl.pallas_call(
        _ctx_attn_kernel,
        grid=(n_seq,),
        in_specs=[spec, spec, spec],
        out_specs=spec,
        out_shape=jax.ShapeDtypeStruct((n_seq * ctx_len, NA_WIDTH), BF16),
        compiler_params=_cparams(("arbitrary",)),
        name="ctx_attn",
    )(q, k, v)


def _merge_kernel(x_ref, mods_ref, n1_ref, n2_ref, f_ref, at_ref, u_ref, up_ref, un_ref, bg_ref, cw_ref,
                  wg_ref, wf_ref, wna_ref, wcv_ref, wo_ref, rw_ref,
                  xo_ref, h2_ref, lg_ref, *, tiles_per_seq, fixed_row):
    i = pl.program_id(0)
    m = _mod_row(mods_ref, i, tiles_per_seq, fixed_row)
    dm = D_MODEL
    x = x_ref[...]
    h = _norm_mod(x, n1_ref[...], m[:, 0:dm], m[:, dm:2 * dm]).astype(BF16)
    gates = jax.nn.sigmoid(jnp.dot(h, wg_ref[...], preferred_element_type=F32))

    y_f = jnp.dot(f_ref[...].astype(BF16), wf_ref[...], preferred_element_type=F32)
    y_na = jnp.dot(at_ref[...], wna_ref[...], preferred_element_type=F32)

    u = u_ref[...]
    t = u.shape[0]
    ti = i % tiles_per_seq
    row = lax.broadcasted_iota(I32, u.shape, 0)
    prev_row = jnp.where(ti == 0, 0.0, up_ref[V7X_SUBLANES - 1:V7X_SUBLANES, :])
    next_row = jnp.where(ti == tiles_per_seq - 1, 0.0, un_ref[0:1, :])
    u_prev = jnp.where(row == 0, prev_row, pltpu.roll(u, 1, 0))
    u_next = jnp.where(row == t - 1, next_row, pltpu.roll(u, t - 1, 0))
    y_cv = bg_ref[...] * (cw_ref[0:1, :] * u_prev + cw_ref[1:2, :] * u + cw_ref[2:3, :] * u_next)
    y_cv = jnp.dot(y_cv.astype(BF16), wcv_ref[...], preferred_element_type=F32)

    merged = gates[:, 0:dm] * y_f + gates[:, dm:2 * dm] * y_na + gates[:, 2 * dm:3 * dm] * y_cv
    mixed = jnp.dot(merged.astype(BF16), wo_ref[...], preferred_element_type=F32)
    x_new = x + m[:, 2 * dm:3 * dm] * mixed
    xo_ref[...] = x_new

    h2 = _norm_mod(x_new, n2_ref[...], m[:, 3 * dm:4 * dm], m[:, 4 * dm:5 * dm])
    h2_ref[...] = h2.astype(BF16)
    hi = h2.astype(BF16)
    lo = (h2 - hi.astype(F32)).astype(BF16)
    p_hi = jnp.dot(hi, rw_ref[...], preferred_element_type=F32)
    p_lo = jnp.dot(lo, rw_ref[...], preferred_element_type=F32)
    lg_ref[...] = p_hi + pltpu.roll(p_hi, V7X_LANES - N_EXPERTS, 1) + p_lo


def _merge(x2, mods_l, n1, n2, f_all, attn, u, bg, conv_w, w_gate, w_f, w_na, w_cv, w_o, rwt,
           *, seq_len, fixed_row):
    n_tok = x2.shape[0]
    tm = min(TOKEN_TILE, seq_len)
    tps = seq_len // tm
    const = lambda i: (0, 0)
    halo = tm // V7X_SUBLANES
    n_halo = n_tok // V7X_SUBLANES
    in_specs = [
        pl.BlockSpec((tm, D_MODEL), lambda i: (i, 0)),
        pl.BlockSpec((MODS_ROWS, 6 * D_MODEL), const),
        pl.BlockSpec((1, D_MODEL), const),
        pl.BlockSpec((1, D_MODEL), const),
        pl.BlockSpec((tm, F_WIDTH), lambda i: (i % tps, i // tps)),
        pl.BlockSpec((tm, NA_WIDTH), lambda i: (i, 0)),
        pl.BlockSpec((tm, CONV_WIDTH), lambda i: (i, 0)),
        pl.BlockSpec((V7X_SUBLANES, CONV_WIDTH), lambda i: (jnp.maximum(i * halo - 1, 0), 0)),
        pl.BlockSpec((V7X_SUBLANES, CONV_WIDTH), lambda i: (jnp.minimum((i + 1) * halo, n_halo - 1), 0)),
        pl.BlockSpec((tm, CONV_WIDTH), lambda i: (i, 0)),
        pl.BlockSpec((3, CONV_WIDTH), const),
        pl.BlockSpec((D_MODEL, 3 * D_MODEL), const),
        pl.BlockSpec((F_WIDTH, D_MODEL), const),
        pl.BlockSpec((NA_WIDTH, D_MODEL), const),
        pl.BlockSpec((CONV_WIDTH, D_MODEL), const),
        pl.BlockSpec((D_MODEL, D_MODEL), const),
        pl.BlockSpec((D_MODEL, V7X_LANES), const),
    ]
    out_specs = [
        pl.BlockSpec((tm, D_MODEL), lambda i: (i, 0)),
        pl.BlockSpec((tm, D_MODEL), lambda i: (i, 0)),
        pl.BlockSpec((tm, V7X_LANES), lambda i: (i, 0)),
    ]
    out_shape = [
        jax.ShapeDtypeStruct((n_tok, D_MODEL), F32),
        jax.ShapeDtypeStruct((n_tok, D_MODEL), BF16),
        jax.ShapeDtypeStruct((n_tok, V7X_LANES), F32),
    ]
    return pl.pallas_call(
        functools.partial(_merge_kernel, tiles_per_seq=tps, fixed_row=fixed_row),
        grid=(n_tok // tm,),
        in_specs=in_specs,
        out_specs=out_specs,
        out_shape=out_shape,
        compiler_params=_cparams(("arbitrary",)),
        name="merge",
    )(x2, mods_l, n1, n2, f_all, attn, u, u, u, bg, conv_w, w_gate, w_f, w_na, w_cv, w_o, rwt)


def _first_max(vals):
    best = vals[0]
    idx = jnp.zeros(best.shape, I32)
    for j in range(1, len(vals)):
        better = vals[j] > best
        idx = jnp.where(better, j, idx)
        best = jnp.where(better, vals[j], best)
    return best, idx


def _select(idx, vals):
    out = vals[-1]
    for j in range(len(vals) - 2, -1, -1):
        out = jnp.where(idx == j, vals[j], out)
    return out


def _route_kernel(lg_ref, rb_ref, ids_ref, wts_ref, cnt_ref, tot_ref, run_ref):
    step = pl.program_id(0)

    @pl.when(step == 0)
    def _():
        run_ref[...] = jnp.zeros_like(run_ref)

    s = jax.nn.sigmoid(lg_ref[...].T[0:N_EXPERTS, :])
    sb = s + rb_ref[...]
    t = s.shape[1]
    s_rows = [s[e:e + 1, :] for e in range(N_EXPERTS)]
    b_rows = [sb[e:e + 1, :] for e in range(N_EXPERTS)]
    epg = EXPERTS_PER_GROUP
    gscore = []
    for g in range(N_GROUPS):
        v = b_rows[g * epg:(g + 1) * epg]
        pair = None
        for a in range(epg):
            for b in range(a + 1, epg):
                pair = v[a] + v[b] if pair is None else jnp.maximum(pair, v[a] + v[b])
        gscore.append(pair)
    _, gi = _first_max(gscore)
    bv = [_select(gi, [b_rows[g * epg + j] for g in range(N_GROUPS)]) for j in range(epg)]
    sv = [_select(gi, [s_rows[g * epg + j] for g in range(N_GROUPS)]) for j in range(epg)]
    _, i1 = _first_max(bv)
    _, i2 = _first_max([jnp.where(i1 == j, -jnp.inf, bv[j]) for j in range(epg)])
    s1 = _select(i1, sv)
    s2 = _select(i2, sv)
    tot = s1 + s2
    e1 = gi * epg + i1
    e2 = gi * epg + i2

    eid = lax.broadcasted_iota(I32, (N_EXPERTS, t), 0)
    hit1 = eid == e1
    hit2 = eid == e2
    onehot = jnp.where(hit1 | hit2, 1.0, 0.0)
    before = (lax.broadcasted_iota(I32, (t, t), 0) < lax.broadcasted_iota(I32, (t, t), 1))
    prefix = jnp.dot(onehot.astype(BF16), jnp.where(before, 1.0, 0.0).astype(BF16),
                     preferred_element_type=F32)
    r1 = jnp.sum(jnp.where(hit1, prefix, 0.0), axis=0, keepdims=True)
    r2 = jnp.sum(jnp.where(hit2, prefix, 0.0), axis=0, keepdims=True)
    grp = float(SLOT_GROUP)
    cnt = jnp.sum(onehot, axis=1, keepdims=True)
    cnt = jnp.floor((cnt + (grp - 1.0)) / grp) * grp
    run = run_ref[...] + cnt
    run_ref[...] = run
    cnt_ref[...] = jnp.broadcast_to(cnt, cnt_ref.shape)
    tot_ref[...] = jnp.broadcast_to(run, tot_ref.shape)

    zi = jnp.zeros((V7X_SUBLANES - 4, t), I32)
    ids_ref[...] = jnp.concatenate([e1, e2, r1.astype(I32), r2.astype(I32), zi], axis=0)
    zf = jnp.zeros((V7X_SUBLANES - 2, t), F32)
    wts_ref[...] = jnp.concatenate([s1 / tot, s2 / tot, zf], axis=0)


def _route(logits, router_b):
    n_tok = logits.shape[0]
    tr = ROUTE_TILE
    return pl.pallas_call(
        _route_kernel,
        grid=(n_tok // tr,),
        in_specs=[
            pl.BlockSpec((tr, V7X_LANES), lambda i: (i, 0)),
            pl.BlockSpec((N_EXPERTS, 1), lambda i: (0, 0)),
        ],
        out_specs=[
            pl.BlockSpec((V7X_SUBLANES, tr), lambda i: (0, i)),
            pl.BlockSpec((V7X_SUBLANES, tr), lambda i: (0, i)),
            pl.BlockSpec((None, N_EXPERTS, V7X_LANES), lambda i: (i, 0, 0)),
            pl.BlockSpec((N_EXPERTS, V7X_LANES), lambda i: (0, 0)),
        ],
        out_shape=[
            jax.ShapeDtypeStruct((V7X_SUBLANES, n_tok), I32),
            jax.ShapeDtypeStruct((V7X_SUBLANES, n_tok), F32),
            jax.ShapeDtypeStruct((n_tok // tr, N_EXPERTS, V7X_LANES), F32),
            jax.ShapeDtypeStruct((N_EXPERTS, V7X_LANES), F32),
        ],
        scratch_shapes=[pltpu.VMEM((N_EXPERTS, 1), F32)],
        compiler_params=_cparams(("arbitrary",)),
        name="route",
    )(logits, router_b.reshape(N_EXPERTS, 1))


def _lane_table(vals, width):
    lane = lax.broadcasted_iota(I32, (1, width), 1)
    out = jnp.zeros((1, width), F32)
    for e, v in enumerate(vals):
        out = jnp.where(lane == e, v, out)
    return out


def _slots_kernel(ids_ref, cnt_ref, tot_ref, loc_ref, gmap_ref, blk_ref, off_ref):
    step = pl.program_id(0)

    @pl.when(step == 0)
    def _():
        off_ref[...] = jnp.zeros_like(off_ref)

    blk = float(EXPERT_BLOCK)
    grp = float(SLOT_GROUP)
    cnt = cnt_ref[...][:, 0:1]
    tot = tot_ref[...][:, 0:1]
    off = off_ref[...]
    region = jnp.floor((tot + (blk - 1.0)) / blk) * blk
    starts, ends, local = [], [], []
    run = jnp.zeros((1, 1), F32)
    lrun = jnp.zeros((1, 1), F32)
    for e in range(N_EXPERTS):
        starts.append(run)
        run = run + region[e:e + 1, :]
        ends.append(run)
        local.append(lrun)
        lrun = lrun + cnt[e:e + 1, :]

    ids = ids_ref[...]
    e1, e2 = ids[0:1, :], ids[1:2, :]
    t = ids.shape[1]
    l1 = jnp.zeros((1, t), F32)
    l2 = jnp.zeros((1, t), F32)
    for e in range(N_EXPERTS):
        l1 = jnp.where(e1 == e, local[e], l1)
        l2 = jnp.where(e2 == e, local[e], l2)
    zi = jnp.zeros((V7X_SUBLANES - 2, t), I32)
    loc_ref[...] = jnp.concatenate([l1.astype(I32) + ids[2:3, :], l2.astype(I32) + ids[3:4, :], zi], axis=0)

    wg = gmap_ref.shape[1]
    first = lax.broadcasted_iota(I32, (1, wg), 1).astype(F32) * grp
    dest = jnp.zeros((1, wg), F32)
    for e in range(N_EXPERTS):
        inside = (first >= local[e]) & (first < local[e] + cnt[e:e + 1, :])
        dest = jnp.where(inside, starts[e] + off[e:e + 1, :] + (first - local[e]), dest)
    n_groups = jnp.broadcast_to(lrun / grp, (1, wg))
    zg = jnp.zeros((V7X_SUBLANES - 2, wg), I32)
    gmap_ref[...] = jnp.concatenate([(dest / grp).astype(I32), n_groups.astype(I32), zg], axis=0)
    off_ref[...] = off + cnt

    w = blk_ref.shape[1]
    first_row = lax.broadcasted_iota(I32, (1, w), 1).astype(F32) * blk
    owner = jnp.zeros((1, w), F32)
    for e in range(N_EXPERTS):
        owner = owner + jnp.where(first_row >= ends[e], 1.0, 0.0)
    owner = jnp.minimum(owner, float(N_EXPERTS - 1))
    used = jnp.broadcast_to(ends[-1] / blk, (1, w))
    pad_first = _lane_table([(starts[e] + tot[e:e + 1, :]) / grp for e in range(N_EXPERTS)], w)
    pad_count = _lane_table([(region[e:e + 1, :] - tot[e:e + 1, :]) / grp for e in range(N_EXPERTS)], w)
    zb = jnp.zeros((V7X_SUBLANES - 4, w), I32)
    blk_ref[...] = jnp.concatenate([owner.astype(I32), used.astype(I32), pad_first.astype(I32),
                                    pad_count.astype(I32), zb], axis=0)


def _slots(ids, cnt, tot, n_blocks):
    n_tok = ids.shape[1]
    tr = ROUTE_TILE
    wblk = -(-n_blocks // V7X_LANES) * V7X_LANES
    return pl.pallas_call(
        _slots_kernel,
        grid=(n_tok // tr,),
        in_specs=[
            pl.BlockSpec((V7X_SUBLANES, tr), lambda i: (0, i)),
            pl.BlockSpec((None, N_EXPERTS, V7X_LANES), lambda i: (i, 0, 0)),
            pl.BlockSpec((N_EXPERTS, V7X_LANES), lambda i: (0, 0)),
        ],
        out_specs=[
            pl.BlockSpec((V7X_SUBLANES, tr), lambda i: (0, i)),
            pl.BlockSpec((None, V7X_SUBLANES, SORT_GROUPS_PAD), lambda i: (i, 0, 0)),
            pl.BlockSpec((V7X_SUBLANES, wblk), lambda i: (0, 0)),
        ],
        out_shape=[
            jax.ShapeDtypeStruct((V7X_SUBLANES, n_tok), I32),
            jax.ShapeDtypeStruct((n_tok // tr, V7X_SUBLANES, SORT_GROUPS_PAD), I32),
            jax.ShapeDtypeStruct((V7X_SUBLANES, wblk), I32),
        ],
        scratch_shapes=[pltpu.VMEM((N_EXPERTS, 1), F32)],
        compiler_params=_cparams(("arbitrary",)),
        name="slots",
    )(ids, cnt, tot)


def _group_rows(group):
    if isinstance(group, int):
        return pl.ds(group * SLOT_GROUP, SLOT_GROUP)
    return pl.ds(pl.multiple_of(group * SLOT_GROUP, SLOT_GROUP), SLOT_GROUP)


def _for_each_group(n, body):
    unroll = 4

    def chunk(q, c):
        for u in range(unroll):
            body(q * unroll + u)
        return c

    def single(g, c):
        body(g)
        return c

    whole = n // unroll
    lax.fori_loop(0, whole, chunk, 0)
    lax.fori_loop(whole * unroll, n, single, 0)


def _group_copy(src_ref, src_group, dst_ref, dst_group, sem):
    return pltpu.make_async_copy(src_ref.at[_group_rows(src_group)], dst_ref.at[_group_rows(dst_group)], sem)


def _dispatch_kernel(gmap_ref, gprev_ref, blk_ref, loc_ref, *refs, n_first):
    if n_first is None:
        h_ref, xb_ref, sorted_ref, zero_ref, sem = refs
        second_ref = None
    else:
        h_ref, second_ref, xb_ref, sorted_ref, zero_ref, sem = refs
    step = pl.program_id(0)
    last = pl.num_programs(0) - 1
    buf = step % 2
    loc = loc_ref[...]
    slot = lax.broadcasted_iota(I32, (SORT_ROWS, loc.shape[1]), 0)
    perm = jnp.where(slot == loc[0:1, :], 1.0, jnp.where(slot == loc[1:2, :], 1.0, 0.0)).astype(BF16)

    def sort_rows(src_ref):
        sorted_ref[buf] = jnp.dot(perm, src_ref[...].astype(BF16), preferred_element_type=F32).astype(BF16)

    if second_ref is None:
        sort_rows(h_ref)
    else:
        pl.when(step < n_first)(lambda: sort_rows(h_ref))
        pl.when(step >= n_first)(lambda: sort_rows(second_ref))

    def tile_copy(map_ref, which, g):
        return _group_copy(sorted_ref.at[which], g, xb_ref, map_ref[0, g], sem.at[which])

    def start(g):
        tile_copy(gmap_ref, buf, g).start()

    def wait_tile(map_ref, which):
        rows = pl.ds(0, map_ref[1, 0] * SLOT_GROUP)
        pltpu.make_async_copy(sorted_ref.at[which, rows], xb_ref.at[rows], sem.at[which]).wait()

    _for_each_group(gmap_ref[1, 0], start)

    @pl.when(step == last)
    def _():
        zero_ref[...] = jnp.zeros_like(zero_ref)
        for e in range(N_EXPERTS):
            first = blk_ref[2, e]

            def zstart(g, c, first=first):
                _group_copy(zero_ref, 0, xb_ref, first + g, sem.at[2]).start()
                return c

            def zwait(g, c, first=first):
                _group_copy(zero_ref, 0, xb_ref, first + g, sem.at[2]).wait()
                return c

            lax.fori_loop(0, blk_ref[3, e], zstart, 0)
            lax.fori_loop(0, blk_ref[3, e], zwait, 0)

        def block_copy(b):
            rows = pl.ds(pl.multiple_of(b * EXPERT_BLOCK, EXPERT_BLOCK), EXPERT_BLOCK)
            return pltpu.make_async_copy(zero_ref, xb_ref.at[rows], sem.at[2])

        def bstart(b, c):
            block_copy(b).start()
            return c

        def bwait(b, c):
            block_copy(b).wait()
            return c

        n_blocks = xb_ref.shape[0] // EXPERT_BLOCK
        lax.fori_loop(blk_ref[1, 0], n_blocks, bstart, 0)
        lax.fori_loop(blk_ref[1, 0], n_blocks, bwait, 0)

    @pl.when((step > 0) & (gprev_ref[1, 0] > 0))
    def _():
        wait_tile(gprev_ref, 1 - buf)

    @pl.when((step == last) & (gmap_ref[1, 0] > 0))
    def _():
        wait_tile(gmap_ref, buf)


def _dispatch(gmap, blk, loc, h2, h2_second, n_slots):
    tr = ROUTE_TILE
    n_first = h2.shape[0] // tr
    n_tiles = n_first
    in_specs = [
        pl.BlockSpec((None, V7X_SUBLANES, SORT_GROUPS_PAD), lambda i: (i, 0, 0), memory_space=pltpu.SMEM),
        pl.BlockSpec((None, V7X_SUBLANES, SORT_GROUPS_PAD), lambda i: (jnp.maximum(i - 1, 0), 0, 0),
                     memory_space=pltpu.SMEM),
        pl.BlockSpec(blk.shape, lambda i: (0, 0), memory_space=pltpu.SMEM),
        pl.BlockSpec((V7X_SUBLANES, tr), lambda i: (0, i)),
        pl.BlockSpec((tr, D_MODEL), lambda i: (jnp.minimum(i, n_first - 1), 0)),
    ]
    args = [gmap, gmap, blk, loc, h2]
    if h2_second is not None:
        n_tiles += h2_second.shape[0] // tr
        in_specs.append(pl.BlockSpec((tr, D_MODEL), lambda i: (jnp.maximum(i - n_first, 0), 0)))
        args.append(h2_second)
    return pl.pallas_call(
        functools.partial(_dispatch_kernel, n_first=None if h2_second is None else n_first),
        grid=(n_tiles,),
        in_specs=in_specs,
        out_specs=pl.BlockSpec(memory_space=pl.ANY),
        out_shape=jax.ShapeDtypeStruct((n_slots, D_MODEL), BF16),
        scratch_shapes=[pltpu.VMEM((2, SORT_ROWS, D_MODEL), BF16), pltpu.VMEM((EXPERT_BLOCK, D_MODEL), BF16),
                        pltpu.SemaphoreType.DMA((3,))],
        compiler_params=_cparams(("arbitrary",)),
        name="dispatch",
    )(*args)


def _experts_kernel(blk_ref, used_ref, x_ref, w1_ref, w3_ref, w2_ref, y_ref, w1b, w3b, w2b):
    i = pl.program_id(0)
    prev = blk_ref[jnp.maximum(i - 1, 0)]

    @pl.when((i == 0) | (blk_ref[i] != prev))
    def _():
        w1b[...] = w1_ref[...].astype(BF16)
        w3b[...] = w3_ref[...].astype(BF16)
        w2b[...] = w2_ref[...].astype(BF16)

    @pl.when(i < used_ref[0])
    def _():
        x = x_ref[...]
        a = jnp.dot(x, w1b[...], preferred_element_type=F32)
        b = jnp.dot(x, w3b[...], preferred_element_type=F32)
        hid = (a * jax.nn.sigmoid(a) * b).astype(BF16)
        y_ref[...] = jnp.dot(hid, w2b[...], preferred_element_type=F32).astype(BF16)

    @pl.when(i >= used_ref[0])
    def _():
        y_ref[...] = jnp.zeros_like(y_ref)


def _experts(blk_e, used, xb, w1, w3, w2, layer):
    n_slots = xb.shape[0]
    bm = EXPERT_BLOCK
    row_map = lambda i, be, nu: (jnp.minimum(i, nu[0] - 1), 0)
    w_map = lambda i, be, nu: (layer, be[i], 0, 0)
    grid_spec = pltpu.PrefetchScalarGridSpec(
        num_scalar_prefetch=2,
        grid=(n_slots // bm,),
        in_specs=[
            pl.BlockSpec((bm, D_MODEL), row_map),
            pl.BlockSpec((None, None, D_MODEL, D_EXPERT), w_map),
            pl.BlockSpec((None, None, D_MODEL, D_EXPERT), w_map),
            pl.BlockSpec((None, None, D_EXPERT, D_MODEL), w_map),
        ],
        out_specs=pl.BlockSpec((bm, D_MODEL), lambda i, be, nu: (i, 0)),
        scratch_shapes=[pltpu.VMEM((D_MODEL, D_EXPERT), BF16), pltpu.VMEM((D_MODEL, D_EXPERT), BF16),
                        pltpu.VMEM((D_EXPERT, D_MODEL), BF16)],
    )
    return pl.pallas_call(
        _experts_kernel,
        grid_spec=grid_spec,
        out_shape=jax.ShapeDtypeStruct((n_slots, D_MODEL), BF16),
        compiler_params=_cparams(("arbitrary",)),
        name="experts",
    )(blk_e, used, xb, w1, w3, w2)


_TN = (((0,), (0,)), ((), ()))


def _combine_kernel(gmap_ref, gnext_ref, loc_ref, wts_ref, x_ref, mods_ref, yb_ref, o_ref, ys_ref, sem,
                    *, tiles_per_seq, fixed_row):
    step = pl.program_id(0)
    buf = step % 2

    def fetch(map_ref, which, g):
        return _group_copy(yb_ref, map_ref[0, g], ys_ref.at[which], g, sem.at[which])

    def start_own(g):
        fetch(gmap_ref, buf, g).start()

    def start_next(g):
        fetch(gnext_ref, 1 - buf, g).start()

    def wait_own():
        rows = pl.ds(0, gmap_ref[1, 0] * SLOT_GROUP)
        pltpu.make_async_copy(yb_ref.at[rows], ys_ref.at[buf, rows], sem.at[buf]).wait()

    @pl.when(step == 0)
    def _():
        ys_ref[...] = jnp.zeros_like(ys_ref)
        _for_each_group(gmap_ref[1, 0], start_own)

    @pl.when(step + 1 < pl.num_programs(0))
    def _():
        _for_each_group(gnext_ref[1, 0], start_next)

    loc = loc_ref[...]
    wts = wts_ref[...]
    slot = lax.broadcasted_iota(I32, (SORT_ROWS, loc.shape[1]), 0)
    perm = jnp.where(slot == loc[0:1, :], wts[0:1, :], jnp.where(slot == loc[1:2, :], wts[1:2, :], 0.0))
    pl.when(gmap_ref[1, 0] > 0)(wait_own)
    y = lax.dot_general(perm.astype(BF16), ys_ref[buf], _TN, preferred_element_type=F32)
    m = _mod_row(mods_ref, step, tiles_per_seq, fixed_row)
    o_ref[...] = x_ref[...] + m[:, 5 * D_MODEL:6 * D_MODEL] * y


def _combine(gmap, loc, wts, x_new, mods_l, yb, *, seq_len, fixed_row, tile_offset):
    n_tok = x_new.shape[0]
    tr = ROUTE_TILE
    tps = max(seq_len // tr, 1)
    n_tiles = n_tok // tr
    return pl.pallas_call(
        functools.partial(_combine_kernel, tiles_per_seq=tps, fixed_row=fixed_row),
        grid=(n_tiles,),
        in_specs=[
            pl.BlockSpec((None, V7X_SUBLANES, SORT_GROUPS_PAD), lambda i: (i + tile_offset, 0, 0),
                         memory_space=pltpu.SMEM),
            pl.BlockSpec((None, V7X_SUBLANES, SORT_GROUPS_PAD),
                         lambda i: (jnp.minimum(i + 1, n_tiles - 1) + tile_offset, 0, 0), memory_space=pltpu.SMEM),
            pl.BlockSpec((V7X_SUBLANES, tr), lambda i: (0, i + tile_offset)),
            pl.BlockSpec((V7X_SUBLANES, tr), lambda i: (0, i + tile_offset)),
            pl.BlockSpec((tr, D_MODEL), lambda i: (i, 0)),
            pl.BlockSpec((MODS_ROWS, 6 * D_MODEL), lambda i: (0, 0)),
            pl.BlockSpec(memory_space=pl.ANY),
        ],
        out_specs=pl.BlockSpec((tr, D_MODEL), lambda i: (i, 0)),
        out_shape=jax.ShapeDtypeStruct((n_tok, D_MODEL), F32),
        scratch_shapes=[pltpu.VMEM((2, SORT_ROWS, D_MODEL), BF16), pltpu.SemaphoreType.DMA((2,))],
        compiler_params=_cparams(("arbitrary",)),
        name="combine",
    )(gmap, gmap, loc, wts, x_new, mods_l, yb)


def _channel_dft_tables():
    j = np.arange(F_GDIM)
    ang = 2.0 * np.pi * ((j[:, None] * j[None, :]) % F_GDIM) / F_GDIM
    eye = np.eye(F_GROUPS)
    return (jnp.asarray(np.kron(eye, np.cos(ang)), F32).astype(BF16),
            jnp.asarray(np.kron(eye, np.sin(ang)), F32).astype(BF16))


def _position_dft_tables(seq_len):
    scale = 1.0 / math.sqrt(seq_len * F_GDIM)
    k = np.arange(seq_len, dtype=np.int64)
    ang = 2.0 * np.pi * ((k[:, None] * k[None, :]) % seq_len) / seq_len
    return (jnp.asarray(np.cos(ang) * scale, F32).astype(BF16),
            jnp.asarray(np.sin(ang) * scale, F32).astype(BF16))


def _two_stage_dft_tables(seq_len):
    r = FFT_RADIX
    assert seq_len == r * r
    scale = 1.0 / math.sqrt(seq_len * F_GDIM)
    j = np.arange(r, dtype=np.int64)
    ang_r = 2.0 * np.pi * ((j[:, None] * j[None, :]) % r) / r
    cs, ss = np.cos(ang_r), np.sin(ang_r)
    s = FFT_STEP
    eye = np.eye(s)
    k1 = np.concatenate([np.kron(cs, eye), np.kron(ss, eye)], axis=0) * scale

    def spread(m):
        out = np.zeros((r, s, s, r))
        for i in range(s):
            out[:, i, i, :] = m
        return out.reshape(r * s, s * r)

    k2 = np.concatenate([spread(cs), spread(ss)], axis=1)
    ka = j[None, :, None]
    t0 = (np.arange(r // s)[:, None, None] * s + np.arange(s)[None, None, :])
    ang_t = (2.0 * np.pi * ka * t0 / seq_len).reshape(-1, 1)
    tc = jnp.asarray(np.repeat(np.cos(ang_t), V7X_LANES, axis=1), F32)
    ts = jnp.asarray(np.repeat(np.sin(ang_t), V7X_LANES, axis=1), F32)
    return jnp.asarray(k1, F32).astype(BF16), jnp.asarray(k2, F32).astype(BF16), tc, ts


def _rope_tables(seq_len):
    t = np.arange(seq_len)
    row = (t // GRID_W).astype(np.float64)
    col = (t % GRID_W).astype(np.float64)
    inv = np.power(ROPE_BASE, -np.arange(ROPE_PER_AXIS, dtype=np.float64) / ROPE_PER_AXIS)
    ang = np.concatenate([row[:, None] * inv, col[:, None] * inv], axis=-1)
    cos = np.cos(ang)
    sin = np.sin(ang)
    cos_h = np.concatenate([cos, cos], axis=-1)
    sin_h = np.concatenate([-sin, sin], axis=-1)
    return (jnp.asarray(np.tile(cos_h, (1, NA_HEADS)), F32), jnp.asarray(np.tile(sin_h, (1, NA_HEADS)), F32))


def _bias_table(rpb_l):
    col = np.arange(GRID_W)
    col_start = np.clip(col - NA_WIN_C // 2, 0, GRID_W - NA_WIN_C)
    col_mask = (col[None, :] >= col_start[:, None]) & (col[None, :] < col_start[:, None] + NA_WIN_C)
    dc = np.clip(col[None, :] - col[:, None] + (NA_WIN_C - 1), 0, 2 * NA_WIN_C - 2)
    n_dc = 2 * NA_WIN_C - 1
    pick = (dc.reshape(-1)[None, :] == np.arange(n_dc)[:, None]).astype(np.float32)
    e = jnp.dot(rpb_l.reshape(-1, n_dc), jnp.asarray(pick), precision=HIGHEST)
    e = e.reshape(NA_HEADS, 2 * NA_WIN_R - 1, GRID_W, GRID_W)
    e = jnp.where(jnp.asarray(col_mask)[None, None], e * LOG2E, NEG_BIG)
    b = jnp.stack([e[:, o:o + NA_WIN_R] for o in range(NA_WIN_R)], axis=0)
    b = b.transpose(0, 1, 3, 2, 4)
    return b.reshape(NA_WIN_R, NA_HEADS, GRID_W, NA_WIN_R * GRID_W)


def _moe(h2, h2_second, logits, w1, w3, w2, layer, router_b):
    n_tok = logits.shape[0]
    n_tiles = n_tok // ROUTE_TILE
    max_rows = 2 * n_tok + N_EXPERTS * n_tiles * (SLOT_GROUP - 1) + N_EXPERTS * (EXPERT_BLOCK - 1)
    n_blocks = -(-max_rows // EXPERT_BLOCK)
    n_slots = n_blocks * EXPERT_BLOCK
    ids, wts, cnt, tot = _route(logits, router_b)
    loc, gmap, blk = _slots(ids, cnt, tot, n_blocks)
    xb = _dispatch(gmap, blk, loc, h2, h2_second, n_slots)
    yb = _experts(blk[0, :n_blocks], blk[1, 0:1], xb, w1, w3, w2, layer)
    return yb, gmap, loc, wts


def kernel(x, c, ctx, c_ctx, ada_w, ada_b, norm1_g, w_in, qn_g, kn_g, rpb, conv_w, w_f, w_na, w_cv, w_o,
           norm2_g, router_w, router_b, w1, w3, w2):
    bsz, seq_len, d = x.shape
    ctx_len = ctx.shape[1]
    n_lat = bsz * seq_len
    n_ctx = bsz * ctx_len
    ctx_row = bsz

    c8 = jnp.concatenate([c, c_ctx[None, :], jnp.zeros((MODS_ROWS - bsz - 1, d), F32)], axis=0)
    mods = _mods(c8, ada_w, ada_b)

    cbd, sbd = _channel_dft_tables()
    fst, g_dft, tc3, ts3 = _two_stage_dft_tables(seq_len)
    c_ctx_t, s_ctx_t = _position_dft_tables(ctx_len)
    cos_t, sin_t = _rope_tables(seq_len)
    mavg = jnp.asarray(np.kron(np.eye(NA_HEADS), np.full((HEAD_DIM, HEAD_DIM), 1.0 / HEAD_DIM)), F32).astype(BF16)
    rw_hi = router_w.astype(BF16)
    rw_lo = (router_w - rw_hi.astype(F32)).astype(BF16)
    rwt = jnp.concatenate([rw_hi, rw_lo, jnp.zeros((d, V7X_LANES - 2 * N_EXPERTS), BF16)], axis=1)

    xl = x.reshape(n_lat, d)
    xc = ctx.reshape(n_ctx, d)
    for l in range(DEPTH):
        last = l == DEPTH - 1
        w_proj = w_in[l][:, :COL_G].astype(BF16)
        w_gate = w_in[l][:, COL_G:].astype(BF16)
        wf, wna, wcv, wo = (w_f[l].astype(BF16), w_na[l].astype(BF16), w_cv[l].astype(BF16), w_o[l].astype(BF16))
        n1 = norm1_g[l].reshape(1, d)
        n2 = norm2_g[l].reshape(1, d)
        qg = jnp.tile(qn_g[l], NA_HEADS).reshape(1, NA_WIDTH)
        kg = jnp.tile(kn_g[l], NA_HEADS).reshape(1, NA_WIDTH)
        bias_tab = _bias_table(rpb[l])
        mods_l = mods[l]

        a_c, b_c, q_c, k_c, v_c, u_c, bg_c = _proj(
            xc, mods_l, n1, w_proj, qg, kg, mavg, cbd, sbd, None, None,
            seq_len=ctx_len, n_seq=bsz, fixed_row=ctx_row, dft_dtype=BF16)
        a_l, b_l, q_l, k_l, v_l, u_l, bg_l = _proj(
            xl, mods_l, n1, w_proj, qg, kg, mavg, cbd, sbd, cos_t, sin_t,
            seq_len=seq_len, n_seq=bsz, fixed_row=None, dft_dtype=F32)

        f_l = _fourier_two_stage(a_l, b_l, fst, g_dft, tc3, ts3)
        attn_l = _attn(q_l, k_l, v_l, k_c, v_c, bias_tab, n_seq=bsz, seq_len=seq_len, ctx_len=ctx_len)
        xl_new, h2_l, lg_l = _merge(xl, mods_l, n1, n2, f_l, attn_l, u_l, bg_l, conv_w[l], w_gate,
                                    wf, wna, wcv, wo, rwt, seq_len=seq_len, fixed_row=None)
        if last:
            yb, gmap, loc, wts = _moe(h2_l, None, lg_l, w1, w3, w2, l, router_b)
            xl = _combine(gmap, loc, wts, xl_new, mods_l, yb, seq_len=seq_len, fixed_row=None, tile_offset=0)
        else:
            f_c = _fourier(c_ctx_t, s_ctx_t, a_c, b_c)
            attn_c = _ctx_attn(q_c, k_c, v_c, n_seq=bsz, ctx_len=ctx_len)
            xc_new, h2_c, lg_c = _merge(xc, mods_l, n1, n2, f_c, attn_c, u_c, bg_c, conv_w[l], w_gate,
                                        wf, wna, wcv, wo, rwt, seq_len=ctx_len, fixed_row=ctx_row)
            lg = jnp.concatenate([lg_l, lg_c], axis=0)
            yb, gmap, loc, wts = _moe(h2_l, h2_c, lg, w1, w3, w2, l, router_b)
            xl = _combine(gmap, loc, wts, xl_new, mods_l, yb, seq_len=seq_len, fixed_row=None, tile_offset=0)
            xc = _combine(gmap, loc, wts, xc_new, mods_l, yb, seq_len=ctx_len, fixed_row=ctx_row,
                          tile_offset=n_lat // ROUTE_TILE)
    return xl.reshape(bsz, seq_len, d)
```

```python
import functools
import math

import numpy as np
import jax
import jax.numpy as jnp
from jax import lax
from jax.experimental import pallas as pl
from jax.experimental.pallas import tpu as pltpu

F32 = jnp.float32
BF16 = jnp.bfloat16
I32 = jnp.int32
HIGHEST = lax.Precision.HIGHEST

D_MODEL = 1024
DEPTH = 2
GRID_W = 64
EPS = 1e-6
F_GROUPS = 4
F_GDIM = 64
F_WIDTH = 256
NA_HEADS = 8
HEAD_DIM = 64
NA_WIDTH = 512
NA_WIN_R = 8
NA_WIN_C = 16
ATTN_SCALE = HEAD_DIM ** -0.5
LOG2E = math.log2(math.e)
ROPE_BASE = 10000.0
ROPE_PER_AXIS = HEAD_DIM // 4
CONV_WIDTH = 256
COL_Q = 256
COL_K = 768
COL_V = 1280
COL_CX = 1792
COL_CB = 2048
COL_CC = 2304
COL_G = 2560
N_EXPERTS = 16
N_GROUPS = 4
EXPERTS_PER_GROUP = 4
D_EXPERT = 512

V7X_LANES = 128
V7X_SUBLANES = 8
V7X_MXU_DIM = 256

TOKEN_TILE = 512
ROUTE_TILE = 512
EXPERT_BLOCK = 512
SLOT_GROUP = 2 * V7X_SUBLANES
SORT_ROWS = -(-(2 * ROUTE_TILE + N_EXPERTS * (SLOT_GROUP - 1)) // V7X_LANES) * V7X_LANES
SORT_GROUPS_PAD = -(-(SORT_ROWS // SLOT_GROUP) // V7X_LANES) * V7X_LANES
HEADS_PER_GROUP = V7X_MXU_DIM // HEAD_DIM
ATTN_ROWS_PER_STEP = 8
FFT_RADIX = 64
FFT_STEP = V7X_SUBLANES
NEG_BIG = -1e30
MODS_ROWS = 8
VMEM_LIMIT = 48 * 1024 * 1024


def _cparams(sem):
    return pltpu.CompilerParams(dimension_semantics=sem, vmem_limit_bytes=VMEM_LIMIT)


def _mods_kernel(c_ref, w_ref, b_ref, o_ref):
    c = c_ref[...]
    sc = c * jax.nn.sigmoid(c)
    o_ref[...] = jnp.dot(sc, w_ref[...], precision=HIGHEST, preferred_element_type=F32) + b_ref[...]


def _mods(c8, ada_w, ada_b):
    nb = 1536
    return pl.pallas_call(
        _mods_kernel,
        grid=(DEPTH, 6 * D_MODEL // nb),
        in_specs=[
            pl.BlockSpec((MODS_ROWS, D_MODEL), lambda l, j: (0, 0)),
            pl.BlockSpec((None, D_MODEL, nb), lambda l, j: (l, 0, j)),
            pl.BlockSpec((None, 1, nb), lambda l, j: (l, 0, j)),
        ],
        out_specs=pl.BlockSpec((None, MODS_ROWS, nb), lambda l, j: (l, 0, j)),
        out_shape=jax.ShapeDtypeStruct((DEPTH, MODS_ROWS, 6 * D_MODEL), F32),
        compiler_params=_cparams(("arbitrary", "arbitrary")),
        name="mods",
    )(c8, ada_w, ada_b.reshape(DEPTH, 1, 6 * D_MODEL))


def _norm_mod(x, g, shift, scale):
    ms = jnp.mean(x * x, axis=-1, keepdims=True)
    return (x * lax.rsqrt(ms + EPS) * g) * (1.0 + scale) + shift


def _mod_row(mods_ref, tile, tiles_per_seq, fixed_row):
    row = fixed_row if fixed_row is not None else tile // tiles_per_seq
    return mods_ref[pl.ds(row, 1), :]


def _proj_kernel(*refs, tiles_per_seq, fixed_row, rope):
    if rope:
        (x_ref, mods_ref, g_ref, w_ref, qg_ref, kg_ref, mavg_ref, cbd_ref, sbd_ref, cos_ref, sin_ref,
         a_ref, b_ref, q_ref, k_ref, v_ref, u_ref, bg_ref) = refs
    else:
        (x_ref, mods_ref, g_ref, w_ref, qg_ref, kg_ref, mavg_ref, cbd_ref, sbd_ref,
         a_ref, b_ref, q_ref, k_ref, v_ref, u_ref, bg_ref) = refs
    m = _mod_row(mods_ref, pl.program_id(0), tiles_per_seq, fixed_row)
    h = _norm_mod(x_ref[...], g_ref[...], m[:, 0:D_MODEL], m[:, D_MODEL:2 * D_MODEL])
    p = jnp.dot(h.astype(BF16), w_ref[...], preferred_element_type=F32)

    uf = p[:, 0:COL_Q].astype(BF16)
    a_ref[...] = jnp.dot(uf, cbd_ref[...], preferred_element_type=F32).astype(a_ref.dtype)
    b_ref[...] = jnp.dot(uf, sbd_ref[...], preferred_element_type=F32).astype(b_ref.dtype)

    def head_norm(t, g):
        sq = (t * t).astype(BF16)
        half = V7X_MXU_DIM
        ms = jnp.concatenate(
            [jnp.dot(sq[:, i:i + half], mavg_ref[...], preferred_element_type=F32)
             for i in range(0, t.shape[1], half)], axis=1)
        return t * lax.rsqrt(ms + EPS) * g

    def rotate(t):
        n = t.shape[-1]
        lane = lax.broadcasted_iota(I32, t.shape, 1)
        first_half = (lane % HEAD_DIM) < (HEAD_DIM // 2)
        swapped = jnp.where(first_half, pltpu.roll(t, n - HEAD_DIM // 2, 1), pltpu.roll(t, HEAD_DIM // 2, 1))
        return t * cos_ref[...] + swapped * sin_ref[...]

    q = head_norm(p[:, COL_Q:COL_K], qg_ref[...])
    k = head_norm(p[:, COL_K:COL_V], kg_ref[...])
    if rope:
        q = rotate(q)
        k = rotate(k)
    q_ref[...] = (q * (ATTN_SCALE * LOG2E)).astype(BF16)
    k_ref[...] = k.astype(BF16)
    v_ref[...] = p[:, COL_V:COL_CX].astype(BF16)
    u_ref[...] = p[:, COL_CC:COL_G] * p[:, COL_CX:COL_CB]
    bg_ref[...] = p[:, COL_CB:COL_CC]


def _proj(x2, mods_l, norm_g, w_proj, qg, kg, mavg, cbd, sbd, cos_t, sin_t, *, seq_len, n_seq, fixed_row,
          dft_dtype):
    n_tok = x2.shape[0]
    tm = min(TOKEN_TILE, seq_len)
    tps = seq_len // tm
    rope = cos_t is not None
    const = lambda i: (0, 0)
    in_specs = [
        pl.BlockSpec((tm, D_MODEL), lambda i: (i, 0)),
        pl.BlockSpec((MODS_ROWS, 6 * D_MODEL), const),
        pl.BlockSpec((1, D_MODEL), const),
        pl.BlockSpec((D_MODEL, COL_G), const),
        pl.BlockSpec((1, NA_WIDTH), const),
        pl.BlockSpec((1, NA_WIDTH), const),
        pl.BlockSpec((V7X_MXU_DIM, V7X_MXU_DIM), const),
        pl.BlockSpec((F_WIDTH, F_WIDTH), const),
        pl.BlockSpec((F_WIDTH, F_WIDTH), const),
    ]
    args = [x2, mods_l, norm_g, w_proj, qg, kg, mavg, cbd, sbd]
    if rope:
        in_specs += [pl.BlockSpec((tm, NA_WIDTH), lambda i: (i % tps, 0))] * 2
        args += [cos_t, sin_t]
    tok = lambda w: pl.BlockSpec((tm, w), lambda i: (i, 0))
    fmap = pl.BlockSpec((tm, F_WIDTH), lambda i: (i % tps, i // tps))
    out_specs = [fmap, fmap, tok(NA_WIDTH), tok(NA_WIDTH), tok(NA_WIDTH), tok(CONV_WIDTH), tok(CONV_WIDTH)]
    out_shape = [
        jax.ShapeDtypeStruct((seq_len, n_seq * F_WIDTH), dft_dtype),
        jax.ShapeDtypeStruct((seq_len, n_seq * F_WIDTH), dft_dtype),
        jax.ShapeDtypeStruct((n_tok, NA_WIDTH), BF16),
        jax.ShapeDtypeStruct((n_tok, NA_WIDTH), BF16),
        jax.ShapeDtypeStruct((n_tok, NA_WIDTH), BF16),
        jax.ShapeDtypeStruct((n_tok, CONV_WIDTH), F32),
        jax.ShapeDtypeStruct((n_tok, CONV_WIDTH), F32),
    ]
    return pl.pallas_call(
        functools.partial(_proj_kernel, tiles_per_seq=tps, fixed_row=fixed_row, rope=rope),
        grid=(n_tok // tm,),
        in_specs=in_specs,
        out_specs=out_specs,
        out_shape=out_shape,
        compiler_params=_cparams(("arbitrary",)),
        name="proj",
    )(*args)


def _fourier_kernel(c_ref, s_ref, a_ref, b_ref, o_ref):
    o = (jnp.dot(c_ref[...], a_ref[...], preferred_element_type=F32)
         - jnp.dot(s_ref[...], b_ref[...], preferred_element_type=F32))
    o_ref[...] = o.astype(BF16)


def _fourier(c_tab, s_tab, a, b):
    seq_len, width = a.shape
    tk = min(seq_len, 256)
    full = lambda i: (0, 0)
    return pl.pallas_call(
        _fourier_kernel,
        grid=(seq_len // tk,),
        in_specs=[
            pl.BlockSpec((tk, seq_len), lambda i: (i, 0)),
            pl.BlockSpec((tk, seq_len), lambda i: (i, 0)),
            pl.BlockSpec((seq_len, width), full, pipeline_mode=pl.Buffered(1)),
            pl.BlockSpec((seq_len, width), full, pipeline_mode=pl.Buffered(1)),
        ],
        out_specs=pl.BlockSpec((tk, width), lambda i: (i, 0)),
        out_shape=jax.ShapeDtypeStruct((seq_len, width), BF16),
        compiler_params=_cparams(("arbitrary",)),
        name="fourier",
    )(c_tab, s_tab, a, b)


def _fft1_kernel(a_ref, b_ref, k1_ref, tc_ref, ts_ref, zr_ref, zi_ref):
    n = FFT_RADIX * FFT_STEP
    width = a_ref.shape[2]
    k1 = k1_ref[...]
    r1 = jnp.dot(k1, a_ref[...].reshape(n, width).astype(BF16), preferred_element_type=F32)
    r2 = jnp.dot(k1, b_ref[...].reshape(n, width).astype(BF16), preferred_element_type=F32)
    yr = r1[0:n] - r2[n:2 * n]
    yi = -(r2[0:n] + r1[n:2 * n])
    tc = tc_ref[...][:, 0:1]
    ts = ts_ref[...][:, 0:1]
    zr_ref[...] = (yr * tc + yi * ts).reshape(zr_ref.shape)
    zi_ref[...] = (yi * tc - yr * ts).reshape(zi_ref.shape)


def _fft2_kernel(zr_ref, zi_ref, k2_ref, f_ref):
    n = FFT_RADIX * FFT_STEP
    width = zr_ref.shape[2]
    zz = jnp.concatenate([zr_ref[...].reshape(n, width), zi_ref[...].reshape(n, width)], axis=0).astype(BF16)
    f_ref[...] = jnp.dot(k2_ref[...], zz, preferred_element_type=F32).reshape(f_ref.shape)


def _fourier_two_stage(a, b, k1, k2, tc, ts):
    seq_len, width = a.shape
    r = FFT_RADIX
    n = r * FFT_STEP
    a3 = a.reshape(r, r, width)
    b3 = b.reshape(r, r, width)
    steps = r // FFT_STEP
    col_blk = pl.BlockSpec((r, FFT_STEP, width), lambda j: (0, j, 0))
    row_blk = pl.BlockSpec((FFT_STEP, r, width), lambda j: (j, 0, 0))
    tw_blk = pl.BlockSpec((n, V7X_LANES), lambda j: (j, 0))
    z_shape = jax.ShapeDtypeStruct((r, r, width), F32)
    zr, zi = pl.pallas_call(
        _fft1_kernel,
        grid=(steps,),
        in_specs=[col_blk, col_blk, pl.BlockSpec((2 * n, n), lambda j: (0, 0)), tw_blk, tw_blk],
        out_specs=[col_blk, col_blk],
        out_shape=[z_shape, z_shape],
        compiler_params=_cparams(("arbitrary",)),
        name="fft1",
    )(a3, b3, k1, tc, ts)
    f3 = pl.pallas_call(
        _fft2_kernel,
        grid=(steps,),
        in_specs=[row_blk, row_blk, pl.BlockSpec((n, 2 * n), lambda j: (0, 0))],
        out_specs=col_blk,
        out_shape=z_shape,
        compiler_params=_cparams(("arbitrary",)),
        name="fft2",
    )(zr, zi, k2)
    return f3.reshape(seq_len, width)


def _stack_heads(qg):
    lane_head = lax.broadcasted_iota(I32, qg.shape, 1) // HEAD_DIM
    zero = jnp.zeros_like(qg)
    return jnp.concatenate([jnp.where(lane_head == h, qg, zero) for h in range(HEADS_PER_GROUP)], axis=0)


def _unstack_heads(o, rows):
    lane_head = lax.broadcasted_iota(I32, (rows, o.shape[1]), 1) // HEAD_DIM
    acc = jnp.zeros((rows, o.shape[1]), F32)
    for h in range(HEADS_PER_GROUP):
        acc = acc + jnp.where(lane_head == h, o[h * rows:(h + 1) * rows, :], 0.0)
    return acc


_NT = (((1,), (1,)), ((), ()))


def _attn_kernel(q_ref, k_ref, v_ref, kc_ref, vc_ref, *rest, rows):
    bias_refs, o_ref = rest[:ATTN_ROWS_PER_STEP], rest[ATTN_ROWS_PER_STEP]
    n_loc = NA_WIN_R * GRID_W
    for j in range(ATTN_ROWS_PER_STEP):
        r = pl.program_id(1) * ATTN_ROWS_PER_STEP + j
        rs = jnp.clip(r - NA_WIN_R // 2, 0, rows - NA_WIN_R)
        start = pl.multiple_of(rs * GRID_W, GRID_W)
        kwin = k_ref[pl.ds(start, n_loc), :]
        vwin = v_ref[pl.ds(start, n_loc), :]
        q = q_ref[j * GRID_W:(j + 1) * GRID_W, :]
        bias_ref = bias_refs[j]
        outs = []
        for g in range(NA_HEADS // HEADS_PER_GROUP):
            sl = slice(g * V7X_MXU_DIM, (g + 1) * V7X_MXU_DIM)
            qs = _stack_heads(q[:, sl])
            s_loc = lax.dot_general(qs, kwin[:, sl], _NT, preferred_element_type=F32)
            bias = bias_ref[g * HEADS_PER_GROUP:(g + 1) * HEADS_PER_GROUP].reshape(HEADS_PER_GROUP * GRID_W, n_loc)
            s_ctx = lax.dot_general(qs, kc_ref[:, sl], _NT, preferred_element_type=F32)
            s = jnp.concatenate([s_loc + bias, s_ctx], axis=1)
            m = jnp.max(s, axis=-1, keepdims=True)
            p = jnp.exp2(s - m)
            denom = jnp.sum(p, axis=-1, keepdims=True)
            pb = p.astype(BF16)
            o = (jnp.dot(pb[:, :n_loc], vwin[:, sl], preferred_element_type=F32)
                 + jnp.dot(pb[:, n_loc:], vc_ref[:, sl], preferred_element_type=F32))
            outs.append(_unstack_heads(o / denom, GRID_W))
        o_ref[j * GRID_W:(j + 1) * GRID_W, :] = jnp.concatenate(outs, axis=1).astype(BF16)


def _attn(q, k, v, kc, vc, bias_tab, *, n_seq, seq_len, ctx_len):
    rows = seq_len // GRID_W
    rps = ATTN_ROWS_PER_STEP
    steps = rows // rps

    def bias_map(j):
        def index(b, s):
            r = s * rps + j
            rs = jnp.clip(r - NA_WIN_R // 2, 0, rows - NA_WIN_R)
            return (rs - r + NA_WIN_R - 1, 0, 0, 0)
        return index

    bias_specs = [pl.BlockSpec((None, NA_HEADS, GRID_W, NA_WIN_R * GRID_W), bias_map(j)) for j in range(rps)]
    return pl.pallas_call(
        functools.partial(_attn_kernel, rows=rows),
        grid=(n_seq, steps),
        in_specs=[
            pl.BlockSpec((rps * GRID_W, NA_WIDTH), lambda b, s: (b * steps + s, 0)),
            pl.BlockSpec((seq_len, NA_WIDTH), lambda b, s: (b, 0)),
            pl.BlockSpec((seq_len, NA_WIDTH), lambda b, s: (b, 0)),
            pl.BlockSpec((ctx_len, NA_WIDTH), lambda b, s: (b, 0)),
            pl.BlockSpec((ctx_len, NA_WIDTH), lambda b, s: (b, 0)),
        ] + bias_specs,
        out_specs=pl.BlockSpec((rps * GRID_W, NA_WIDTH), lambda b, s: (b * steps + s, 0)),
        out_shape=jax.ShapeDtypeStruct((n_seq * seq_len, NA_WIDTH), BF16),
        compiler_params=_cparams(("arbitrary", "arbitrary")),
        name="attn",
    )(q, k, v, kc, vc, *([bias_tab] * rps))


def _ctx_attn_kernel(q_ref, k_ref, v_ref, o_ref):
    q = q_ref[...]
    n = q.shape[0]
    outs = []
    for g in range(NA_HEADS // HEADS_PER_GROUP):
        sl = slice(g * V7X_MXU_DIM, (g + 1) * V7X_MXU_DIM)
        qs = _stack_heads(q[:, sl])
        s = lax.dot_general(qs, k_ref[:, sl], _NT, preferred_element_type=F32)
        m = jnp.max(s, axis=-1, keepdims=True)
        p = jnp.exp2(s - m)
        denom = jnp.sum(p, axis=-1, keepdims=True)
        o = jnp.dot(p.astype(BF16), v_ref[:, sl], preferred_element_type=F32)
        outs.append(_unstack_heads(o / denom, n))
    o_ref[...] = jnp.concatenate(outs, axis=1).astype(BF16)


def _ctx_attn(q, k, v, *, n_seq, ctx_len):
    spec = pl.BlockSpec((ctx_len, NA_WIDTH), lambda b: (b, 0))
    return pl.pallas_call(
        _ctx_attn_kernel,
        grid=(n_seq,),
        in_specs=[spec, spec, spec],
        out_specs=spec,
        out_shape=jax.ShapeDtypeStruct((n_seq * ctx_len, NA_WIDTH), BF16),
        compiler_params=_cparams(("arbitrary",)),
        name="ctx_attn",
    )(q, k, v)


def _merge_kernel(x_ref, mods_ref, n1_ref, n2_ref, f_ref, at_ref, u_ref, up_ref, un_ref, bg_ref, cw_ref,
                  wg_ref, wf_ref, wna_ref, wcv_ref, wo_ref, rw_ref,
                  xo_ref, h2_ref, lg_ref, *, tiles_per_seq, fixed_row):
    i = pl.program_id(0)
    m = _mod_row(mods_ref, i, tiles_per_seq, fixed_row)
    dm = D_MODEL
    x = x_ref[...]
    h = _norm_mod(x, n1_ref[...], m[:, 0:dm], m[:, dm:2 * dm]).astype(BF16)
    gates = jax.nn.sigmoid(jnp.dot(h, wg_ref[...], preferred_element_type=F32))

    y_f = jnp.dot(f_ref[...].astype(BF16), wf_ref[...], preferred_element_type=F32)
    y_na = jnp.dot(at_ref[...], wna_ref[...], preferred_element_type=F32)

    u = u_ref[...]
    t = u.shape[0]
    ti = i % tiles_per_seq
    row = lax.broadcasted_iota(I32, u.shape, 0)
    prev_row = jnp.where(ti == 0, 0.0, up_ref[V7X_SUBLANES - 1:V7X_SUBLANES, :])
    next_row = jnp.where(ti == tiles_per_seq - 1, 0.0, un_ref[0:1, :])
    u_prev = jnp.where(row == 0, prev_row, pltpu.roll(u, 1, 0))
    u_next = jnp.where(row == t - 1, next_row, pltpu.roll(u, t - 1, 0))
    y_cv = bg_ref[...] * (cw_ref[0:1, :] * u_prev + cw_ref[1:2, :] * u + cw_ref[2:3, :] * u_next)
    y_cv = jnp.dot(y_cv.astype(BF16), wcv_ref[...], preferred_element_type=F32)

    merged = gates[:, 0:dm] * y_f + gates[:, dm:2 * dm] * y_na + gates[:, 2 * dm:3 * dm] * y_cv
    mixed = jnp.dot(merged.astype(BF16), wo_ref[...], preferred_element_type=F32)
    x_new = x + m[:, 2 * dm:3 * dm] * mixed
    xo_ref[...] = x_new

    h2 = _norm_mod(x_new, n2_ref[...], m[:, 3 * dm:4 * dm], m[:, 4 * dm:5 * dm])
    h2_ref[...] = h2.astype(BF16)
    hi = h2.astype(BF16)
    lo = (h2 - hi.astype(F32)).astype(BF16)
    p_hi = jnp.dot(hi, rw_ref[...], preferred_element_type=F32)
    p_lo = jnp.dot(lo, rw_ref[...], preferred_element_type=F32)
    lg_ref[...] = p_hi + pltpu.roll(p_hi, V7X_LANES - N_EXPERTS, 1) + p_lo


def _merge(x2, mods_l, n1, n2, f_all, attn, u, bg, conv_w, w_gate, w_f, w_na, w_cv, w_o, rwt,
           *, seq_len, fixed_row):
    n_tok = x2.shape[0]
    tm = min(TOKEN_TILE, seq_len)
    tps = seq_len // tm
    const = lambda i: (0, 0)
    halo = tm // V7X_SUBLANES
    n_halo = n_tok // V7X_SUBLANES
    in_specs = [
        pl.BlockSpec((tm, D_MODEL), lambda i: (i, 0)),
        pl.BlockSpec((MODS_ROWS, 6 * D_MODEL), const),
        pl.BlockSpec((1, D_MODEL), const),
        pl.BlockSpec((1, D_MODEL), const),
        pl.BlockSpec((tm, F_WIDTH), lambda i: (i % tps, i // tps)),
        pl.BlockSpec((tm, NA_WIDTH), lambda i: (i, 0)),
        pl.BlockSpec((tm, CONV_WIDTH), lambda i: (i, 0)),
        pl.BlockSpec((V7X_SUBLANES, CONV_WIDTH), lambda i: (jnp.maximum(i * halo - 1, 0), 0)),
        pl.BlockSpec((V7X_SUBLANES, CONV_WIDTH), lambda i: (jnp.minimum((i + 1) * halo, n_halo - 1), 0)),
        pl.BlockSpec((tm, CONV_WIDTH), lambda i: (i, 0)),
        pl.BlockSpec((3, CONV_WIDTH), const),
        pl.BlockSpec((D_MODEL, 3 * D_MODEL), const),
        pl.BlockSpec((F_WIDTH, D_MODEL), const),
        pl.BlockSpec((NA_WIDTH, D_MODEL), const),
        pl.BlockSpec((CONV_WIDTH, D_MODEL), const),
        pl.BlockSpec((D_MODEL, D_MODEL), const),
        pl.BlockSpec((D_MODEL, V7X_LANES), const),
    ]
    out_specs = [
        pl.BlockSpec((tm, D_MODEL), lambda i: (i, 0)),
        pl.BlockSpec((tm, D_MODEL), lambda i: (i, 0)),
        pl.BlockSpec((tm, V7X_LANES), lambda i: (i, 0)),
    ]
    out_shape = [
        jax.ShapeDtypeStruct((n_tok, D_MODEL), F32),
        jax.ShapeDtypeStruct((n_tok, D_MODEL), BF16),
        jax.ShapeDtypeStruct((n_tok, V7X_LANES), F32),
    ]
    return pl.pallas_call(
        functools.partial(_merge_kernel, tiles_per_seq=tps, fixed_row=fixed_row),
        grid=(n_tok // tm,),
        in_specs=in_specs,
        out_specs=out_specs,
        out_shape=out_shape,
        compiler_params=_cparams(("arbitrary",)),
        name="merge",
    )(x2, mods_l, n1, n2, f_all, attn, u, u, u, bg, conv_w, w_gate, w_f, w_na, w_cv, w_o, rwt)


def _first_max(vals):
    best = vals[0]
    idx = jnp.zeros(best.shape, I32)
    for j in range(1, len(vals)):
        better = vals[j] > best
        idx = jnp.where(better, j, idx)
        best = jnp.where(better, vals[j], best)
    return best, idx


def _select(idx, vals):
    out = vals[-1]
    for j in range(len(vals) - 2, -1, -1):
        out = jnp.where(idx == j, vals[j], out)
    return out


def _route_kernel(lg_ref, rb_ref, ids_ref, wts_ref, cnt_ref, tot_ref, run_ref):
    step = pl.program_id(0)

    @pl.when(step == 0)
    def _():
        run_ref[...] = jnp.zeros_like(run_ref)

    s = jax.nn.sigmoid(lg_ref[...].T[0:N_EXPERTS, :])
    sb = s + rb_ref[...]
    t = s.shape[1]
    s_rows = [s[e:e + 1, :] for e in range(N_EXPERTS)]
    b_rows = [sb[e:e + 1, :] for e in range(N_EXPERTS)]
    epg = EXPERTS_PER_GROUP
    gscore = []
    for g in range(N_GROUPS):
        v = b_rows[g * epg:(g + 1) * epg]
        pair = None
        for a in range(epg):
            for b in range(a + 1, epg):
                pair = v[a] + v[b] if pair is None else jnp.maximum(pair, v[a] + v[b])
        gscore.append(pair)
    _, gi = _first_max(gscore)
    bv = [_select(gi, [b_rows[g * epg + j] for g in range(N_GROUPS)]) for j in range(epg)]
    sv = [_select(gi, [s_rows[g * epg + j] for g in range(N_GROUPS)]) for j in range(epg)]
    _, i1 = _first_max(bv)
    _, i2 = _first_max([jnp.where(i1 == j, -jnp.inf, bv[j]) for j in range(epg)])
    s1 = _select(i1, sv)
    s2 = _select(i2, sv)
    tot = s1 + s2
    e1 = gi * epg + i1
    e2 = gi * epg + i2

    eid = lax.broadcasted_iota(I32, (N_EXPERTS, t), 0)
    hit1 = eid == e1
    hit2 = eid == e2
    onehot = jnp.where(hit1 | hit2, 1.0, 0.0)
    before = (lax.broadcasted_iota(I32, (t, t), 0) < lax.broadcasted_iota(I32, (t, t), 1))
    prefix = jnp.dot(onehot.astype(BF16), jnp.where(before, 1.0, 0.0).astype(BF16),
                     preferred_element_type=F32)
    r1 = jnp.sum(jnp.where(hit1, prefix, 0.0), axis=0, keepdims=True)
    r2 = jnp.sum(jnp.where(hit2, prefix, 0.0), axis=0, keepdims=True)
    grp = float(SLOT_GROUP)
    cnt = jnp.sum(onehot, axis=1, keepdims=True)
    cnt = jnp.floor((cnt + (grp - 1.0)) / grp) * grp
    run = run_ref[...] + cnt
    run_ref[...] = run
    cnt_ref[...] = jnp.broadcast_to(cnt, cnt_ref.shape)
    tot_ref[...] = jnp.broadcast_to(run, tot_ref.shape)

    zi = jnp.zeros((V7X_SUBLANES - 4, t), I32)
    ids_ref[...] = jnp.concatenate([e1, e2, r1.astype(I32), r2.astype(I32), zi], axis=0)
    zf = jnp.zeros((V7X_SUBLANES - 2, t), F32)
    wts_ref[...] = jnp.concatenate([s1 / tot, s2 / tot, zf], axis=0)


def _route(logits, router_b):
    n_tok = logits.shape[0]
    tr = ROUTE_TILE
    return pl.pallas_call(
        _route_kernel,
        grid=(n_tok // tr,),
        in_specs=[
            pl.BlockSpec((tr, V7X_LANES), lambda i: (i, 0)),
            pl.BlockSpec((N_EXPERTS, 1), lambda i: (0, 0)),
        ],
        out_specs=[
            pl.BlockSpec((V7X_SUBLANES, tr), lambda i: (0, i)),
            pl.BlockSpec((V7X_SUBLANES, tr), lambda i: (0, i)),
            pl.BlockSpec((None, N_EXPERTS, V7X_LANES), lambda i: (i, 0, 0)),
            pl.BlockSpec((N_EXPERTS, V7X_LANES), lambda i: (0, 0)),
        ],
        out_shape=[
            jax.ShapeDtypeStruct((V7X_SUBLANES, n_tok), I32),
            jax.ShapeDtypeStruct((V7X_SUBLANES, n_tok), F32),
            jax.ShapeDtypeStruct((n_tok // tr, N_EXPERTS, V7X_LANES), F32),
            jax.ShapeDtypeStruct((N_EXPERTS, V7X_LANES), F32),
        ],
        scratch_shapes=[pltpu.VMEM((N_EXPERTS, 1), F32)],
        compiler_params=_cparams(("arbitrary",)),
        name="route",
    )(logits, router_b.reshape(N_EXPERTS, 1))


def _lane_table(vals, width):
    lane = lax.broadcasted_iota(I32, (1, width), 1)
    out = jnp.zeros((1, width), F32)
    for e, v in enumerate(vals):
        out = jnp.where(lane == e, v, out)
    return out


def _slots_kernel(ids_ref, cnt_ref, tot_ref, loc_ref, gmap_ref, blk_ref, off_ref):
    step = pl.program_id(0)

    @pl.when(step == 0)
    def _():
        off_ref[...] = jnp.zeros_like(off_ref)

    blk = float(EXPERT_BLOCK)
    grp = float(SLOT_GROUP)
    cnt = cnt_ref[...][:, 0:1]
    tot = tot_ref[...][:, 0:1]
    off = off_ref[...]
    region = jnp.floor((tot + (blk - 1.0)) / blk) * blk
    starts, ends, local = [], [], []
    run = jnp.zeros((1, 1), F32)
    lrun = jnp.zeros((1, 1), F32)
    for e in range(N_EXPERTS):
        starts.append(run)
        run = run + region[e:e + 1, :]
        ends.append(run)
        local.append(lrun)
        lrun = lrun + cnt[e:e + 1, :]

    ids = ids_ref[...]
    e1, e2 = ids[0:1, :], ids[1:2, :]
    t = ids.shape[1]
    l1 = jnp.zeros((1, t), F32)
    l2 = jnp.zeros((1, t), F32)
    for e in range(N_EXPERTS):
        l1 = jnp.where(e1 == e, local[e], l1)
        l2 = jnp.where(e2 == e, local[e], l2)
    zi = jnp.zeros((V7X_SUBLANES - 2, t), I32)
    loc_ref[...] = jnp.concatenate([l1.astype(I32) + ids[2:3, :], l2.astype(I32) + ids[3:4, :], zi], axis=0)

    wg = gmap_ref.shape[1]
    first = lax.broadcasted_iota(I32, (1, wg), 1).astype(F32) * grp
    dest = jnp.zeros((1, wg), F32)
    for e in range(N_EXPERTS):
        inside = (first >= local[e]) & (first < local[e] + cnt[e:e + 1, :])
        dest = jnp.where(inside, starts[e] + off[e:e + 1, :] + (first - local[e]), dest)
    n_groups = jnp.broadcast_to(lrun / grp, (1, wg))
    zg = jnp.zeros((V7X_SUBLANES - 2, wg), I32)
    gmap_ref[...] = jnp.concatenate([(dest / grp).astype(I32), n_groups.astype(I32), zg], axis=0)
    off_ref[...] = off + cnt

    w = blk_ref.shape[1]
    first_row = lax.broadcasted_iota(I32, (1, w), 1).astype(F32) * blk
    owner = jnp.zeros((1, w), F32)
    for e in range(N_EXPERTS):
        owner = owner + jnp.where(first_row >= ends[e], 1.0, 0.0)
    owner = jnp.minimum(owner, float(N_EXPERTS - 1))
    used = jnp.broadcast_to(ends[-1] / blk, (1, w))
    pad_first = _lane_table([(starts[e] + tot[e:e + 1, :]) / grp for e in range(N_EXPERTS)], w)
    pad_count = _lane_table([(region[e:e + 1, :] - tot[e:e + 1, :]) / grp for e in range(N_EXPERTS)], w)
    region_first = _lane_table([starts[e] / blk for e in range(N_EXPERTS)], w)
    region_blocks = _lane_table([region[e:e + 1, :] / blk for e in range(N_EXPERTS)], w)
    zb = jnp.zeros((V7X_SUBLANES - 6, w), I32)
    blk_ref[...] = jnp.concatenate([owner.astype(I32), used.astype(I32), pad_first.astype(I32),
                                    pad_count.astype(I32), region_first.astype(I32),
                                    region_blocks.astype(I32), zb], axis=0)


def _slots(ids, cnt, tot, n_blocks):
    n_tok = ids.shape[1]
    tr = ROUTE_TILE
    wblk = -(-n_blocks // V7X_LANES) * V7X_LANES
    return pl.pallas_call(
        _slots_kernel,
        grid=(n_tok // tr,),
        in_specs=[
            pl.BlockSpec((V7X_SUBLANES, tr), lambda i: (0, i)),
            pl.BlockSpec((None, N_EXPERTS, V7X_LANES), lambda i: (i, 0, 0)),
            pl.BlockSpec((N_EXPERTS, V7X_LANES), lambda i: (0, 0)),
        ],
        out_specs=[
            pl.BlockSpec((V7X_SUBLANES, tr), lambda i: (0, i)),
            pl.BlockSpec((None, V7X_SUBLANES, SORT_GROUPS_PAD), lambda i: (i, 0, 0)),
            pl.BlockSpec((V7X_SUBLANES, wblk), lambda i: (0, 0)),
        ],
        out_shape=[
            jax.ShapeDtypeStruct((V7X_SUBLANES, n_tok), I32),
            jax.ShapeDtypeStruct((n_tok // tr, V7X_SUBLANES, SORT_GROUPS_PAD), I32),
            jax.ShapeDtypeStruct((V7X_SUBLANES, wblk), I32),
        ],
        scratch_shapes=[pltpu.VMEM((N_EXPERTS, 1), F32)],
        compiler_params=_cparams(("arbitrary",)),
        name="slots",
    )(ids, cnt, tot)


def _group_rows(group):
    if isinstance(group, int):
        return pl.ds(group * SLOT_GROUP, SLOT_GROUP)
    return pl.ds(pl.multiple_of(group * SLOT_GROUP, SLOT_GROUP), SLOT_GROUP)


def _for_each_group(n, body):
    unroll = 4

    def chunk(q, c):
        for u in range(unroll):
            body(q * unroll + u)
        return c

    def single(g, c):
        body(g)
        return c

    whole = n // unroll
    lax.fori_loop(0, whole, chunk, 0)
    lax.fori_loop(whole * unroll, n, single, 0)


def _group_copy(src_ref, src_group, dst_ref, dst_group, sem):
    return pltpu.make_async_copy(src_ref.at[_group_rows(src_group)], dst_ref.at[_group_rows(dst_group)], sem)


def _dispatch_kernel(gmap_ref, gprev_ref, blk_ref, loc_ref, *refs, n_first):
    if n_first is None:
        h_ref, xb_ref, sorted_ref, zero_ref, sem = refs
        second_ref = None
    else:
        h_ref, second_ref, xb_ref, sorted_ref, zero_ref, sem = refs
    step = pl.program_id(0)
    last = pl.num_programs(0) - 1
    buf = step % 2
    loc = loc_ref[...]
    slot = lax.broadcasted_iota(I32, (SORT_ROWS, loc.shape[1]), 0)
    perm = jnp.where(slot == loc[0:1, :], 1.0, jnp.where(slot == loc[1:2, :], 1.0, 0.0)).astype(BF16)

    def sort_rows(src_ref):
        sorted_ref[buf] = jnp.dot(perm, src_ref[...].astype(BF16), preferred_element_type=F32).astype(BF16)

    if second_ref is None:
        sort_rows(h_ref)
    else:
        pl.when(step < n_first)(lambda: sort_rows(h_ref))
        pl.when(step >= n_first)(lambda: sort_rows(second_ref))

    def tile_copy(map_ref, which, g):
        return _group_copy(sorted_ref.at[which], g, xb_ref, map_ref[0, g], sem.at[which])

    def start(g):
        tile_copy(gmap_ref, buf, g).start()

    def wait_tile(map_ref, which):
        rows = pl.ds(0, map_ref[1, 0] * SLOT_GROUP)
        pltpu.make_async_copy(sorted_ref.at[which, rows], xb_ref.at[rows], sem.at[which]).wait()

    _for_each_group(gmap_ref[1, 0], start)

    @pl.when(step == last)
    def _():
        zero_ref[...] = jnp.zeros_like(zero_ref)
        for e in range(N_EXPERTS):
            first = blk_ref[2, e]

            def zstart(g, c, first=first):
                _group_copy(zero_ref, 0, xb_ref, first + g, sem.at[2]).start()
                return c

            def zwait(g, c, first=first):
                _group_copy(zero_ref, 0, xb_ref, first + g, sem.at[2]).wait()
                return c

            lax.fori_loop(0, blk_ref[3, e], zstart, 0)
            lax.fori_loop(0, blk_ref[3, e], zwait, 0)

        def block_copy(b):
            rows = pl.ds(pl.multiple_of(b * EXPERT_BLOCK, EXPERT_BLOCK), EXPERT_BLOCK)
            return pltpu.make_async_copy(zero_ref, xb_ref.at[rows], sem.at[2])

        def bstart(b, c):
            block_copy(b).start()
            return c

        def bwait(b, c):
            block_copy(b).wait()
            return c

        n_blocks = xb_ref.shape[0] // EXPERT_BLOCK
        lax.fori_loop(blk_ref[1, 0], n_blocks, bstart, 0)
        lax.fori_loop(blk_ref[1, 0], n_blocks, bwait, 0)

    @pl.when((step > 0) & (gprev_ref[1, 0] > 0))
    def _():
        wait_tile(gprev_ref, 1 - buf)

    @pl.when((step == last) & (gmap_ref[1, 0] > 0))
    def _():
        wait_tile(gmap_ref, buf)


def _dispatch(gmap, blk, loc, h2, h2_second, n_slots):
    tr = ROUTE_TILE
    n_first = h2.shape[0] // tr
    n_tiles = n_first
    in_specs = [
        pl.BlockSpec((None, V7X_SUBLANES, SORT_GROUPS_PAD), lambda i: (i, 0, 0), memory_space=pltpu.SMEM),
        pl.BlockSpec((None, V7X_SUBLANES, SORT_GROUPS_PAD), lambda i: (jnp.maximum(i - 1, 0), 0, 0),
                     memory_space=pltpu.SMEM),
        pl.BlockSpec(blk.shape, lambda i: (0, 0), memory_space=pltpu.SMEM),
        pl.BlockSpec((V7X_SUBLANES, tr), lambda i: (0, i)),
        pl.BlockSpec((tr, D_MODEL), lambda i: (jnp.minimum(i, n_first - 1), 0)),
    ]
    args = [gmap, gmap, blk, loc, h2]
    if h2_second is not None:
        n_tiles += h2_second.shape[0] // tr
        in_specs.append(pl.BlockSpec((tr, D_MODEL), lambda i: (jnp.maximum(i - n_first, 0), 0)))
        args.append(h2_second)
    return pl.pallas_call(
        functools.partial(_dispatch_kernel, n_first=None if h2_second is None else n_first),
        grid=(n_tiles,),
        in_specs=in_specs,
        out_specs=pl.BlockSpec(memory_space=pl.ANY),
        out_shape=jax.ShapeDtypeStruct((n_slots, D_MODEL), BF16),
        scratch_shapes=[pltpu.VMEM((2, SORT_ROWS, D_MODEL), BF16), pltpu.VMEM((EXPERT_BLOCK, D_MODEL), BF16),
                        pltpu.SemaphoreType.DMA((3,))],
        compiler_params=_cparams(("arbitrary",)),
        name="dispatch",
    )(*args)


def _experts_kernel(first_ref, count_ref, used_ref, w1_ref, w3_ref, w2_ref, xb_ref, yb_ref,
                    xbuf, ybuf, w1b, w3b, w2b, sem_in, sem_out):
    e = pl.program_id(0)
    w1b[...] = w1_ref[...].astype(BF16)
    w3b[...] = w3_ref[...].astype(BF16)
    w2b[...] = w2_ref[...].astype(BF16)
    first = first_ref[e]
    n_blk = count_ref[e]

    def rows(b):
        return pl.ds(pl.multiple_of((first + b) * EXPERT_BLOCK, EXPERT_BLOCK), EXPERT_BLOCK)

    def fetch(b, slot):
        return pltpu.make_async_copy(xb_ref.at[rows(b)], xbuf.at[slot], sem_in.at[slot])

    def put(b, slot):
        return pltpu.make_async_copy(ybuf.at[slot], yb_ref.at[rows(b)], sem_out.at[slot])

    @pl.when(n_blk > 0)
    def _():
        fetch(0, 0).start()

    def block(b, c):
        slot = b % 2
        fetch(b, slot).wait()

        @pl.when(b + 1 < n_blk)
        def _():
            fetch(b + 1, 1 - slot).start()

        @pl.when(b >= 2)
        def _():
            put(b - 2, slot).wait()

        x = xbuf[slot]
        a = jnp.dot(x, w1b[...], preferred_element_type=F32)
        g = jnp.dot(x, w3b[...], preferred_element_type=F32)
        hid = (a * jax.nn.sigmoid(a) * g).astype(BF16)
        ybuf[slot] = jnp.dot(hid, w2b[...], preferred_element_type=F32).astype(BF16)
        put(b, slot).start()
        return c

    lax.fori_loop(0, n_blk, block, 0)

    @pl.when(n_blk >= 2)
    def _():
        put(n_blk - 2, n_blk % 2).wait()

    @pl.when(n_blk >= 1)
    def _():
        put(n_blk - 1, (n_blk - 1) % 2).wait()

    @pl.when(e == pl.num_programs(0) - 1)
    def _():
        ybuf[0] = jnp.zeros(ybuf.shape[1:], ybuf.dtype)

        def tail(b):
            r = pl.ds(pl.multiple_of(b * EXPERT_BLOCK, EXPERT_BLOCK), EXPERT_BLOCK)
            return pltpu.make_async_copy(ybuf.at[0], yb_ref.at[r], sem_out.at[0])

        def tstart(b, c):
            tail(b).start()
            return c

        def twait(b, c):
            tail(b).wait()
            return c

        total = yb_ref.shape[0] // EXPERT_BLOCK
        lax.fori_loop(used_ref[0], total, tstart, 0)
        lax.fori_loop(used_ref[0], total, twait, 0)


def _experts(first_blk, n_blk, used, xb, w1, w3, w2, layer):
    n_slots = xb.shape[0]
    bm = EXPERT_BLOCK
    w_map = lambda e, fb, nb, nu: (layer, e, 0, 0)
    grid_spec = pltpu.PrefetchScalarGridSpec(
        num_scalar_prefetch=3,
        grid=(N_EXPERTS,),
        in_specs=[
            pl.BlockSpec((None, None, D_MODEL, D_EXPERT), w_map),
            pl.BlockSpec((None, None, D_MODEL, D_EXPERT), w_map),
            pl.BlockSpec((None, None, D_EXPERT, D_MODEL), w_map),
            pl.BlockSpec(memory_space=pl.ANY),
        ],
        out_specs=pl.BlockSpec(memory_space=pl.ANY),
        scratch_shapes=[pltpu.VMEM((2, bm, D_MODEL), BF16), pltpu.VMEM((2, bm, D_MODEL), BF16),
                        pltpu.VMEM((D_MODEL, D_EXPERT), BF16), pltpu.VMEM((D_MODEL, D_EXPERT), BF16),
                        pltpu.VMEM((D_EXPERT, D_MODEL), BF16),
                        pltpu.SemaphoreType.DMA((2,)), pltpu.SemaphoreType.DMA((2,))],
    )
    return pl.pallas_call(
        _experts_kernel,
        grid_spec=grid_spec,
        out_shape=jax.ShapeDtypeStruct((n_slots, D_MODEL), BF16),
        compiler_params=_cparams(("arbitrary",)),
        name="experts",
    )(first_blk, n_blk, used, w1, w3, w2, xb)


_TN = (((0,), (0,)), ((), ()))


def _combine_kernel(gmap_ref, gnext_ref, loc_ref, wts_ref, x_ref, mods_ref, yb_ref, o_ref, ys_ref, sem,
                    *, tiles_per_seq, fixed_row):
    step = pl.program_id(0)
    buf = step % 2

    def fetch(map_ref, which, g):
        return _group_copy(yb_ref, map_ref[0, g], ys_ref.at[which], g, sem.at[which])

    def start_own(g):
        fetch(gmap_ref, buf, g).start()

    def start_next(g):
        fetch(gnext_ref, 1 - buf, g).start()

    def wait_own():
        rows = pl.ds(0, gmap_ref[1, 0] * SLOT_GROUP)
        pltpu.make_async_copy(yb_ref.at[rows], ys_ref.at[buf, rows], sem.at[buf]).wait()

    @pl.when(step == 0)
    def _():
        ys_ref[...] = jnp.zeros_like(ys_ref)
        _for_each_group(gmap_ref[1, 0], start_own)

    @pl.when(step + 1 < pl.num_programs(0))
    def _():
        _for_each_group(gnext_ref[1, 0], start_next)

    loc = loc_ref[...]
    wts = wts_ref[...]
    slot = lax.broadcasted_iota(I32, (SORT_ROWS, loc.shape[1]), 0)
    perm = jnp.where(slot == loc[0:1, :], wts[0:1, :], jnp.where(slot == loc[1:2, :], wts[1:2, :], 0.0))
    pl.when(gmap_ref[1, 0] > 0)(wait_own)
    y = lax.dot_general(perm.astype(BF16), ys_ref[buf], _TN, preferred_element_type=F32)
    m = _mod_row(mods_ref, step, tiles_per_seq, fixed_row)
    o_ref[...] = x_ref[...] + m[:, 5 * D_MODEL:6 * D_MODEL] * y


def _combine(gmap, loc, wts, x_new, mods_l, yb, *, seq_len, fixed_row, tile_offset):
    n_tok = x_new.shape[0]
    tr = ROUTE_TILE
    tps = max(seq_len // tr, 1)
    n_tiles = n_tok // tr
    return pl.pallas_call(
        functools.partial(_combine_kernel, tiles_per_seq=tps, fixed_row=fixed_row),
        grid=(n_tiles,),
        in_specs=[
            pl.BlockSpec((None, V7X_SUBLANES, SORT_GROUPS_PAD), lambda i: (i + tile_offset, 0, 0),
                         memory_space=pltpu.SMEM),
            pl.BlockSpec((None, V7X_SUBLANES, SORT_GROUPS_PAD),
                         lambda i: (jnp.minimum(i + 1, n_tiles - 1) + tile_offset, 0, 0), memory_space=pltpu.SMEM),
            pl.BlockSpec((V7X_SUBLANES, tr), lambda i: (0, i + tile_offset)),
            pl.BlockSpec((V7X_SUBLANES, tr), lambda i: (0, i + tile_offset)),
            pl.BlockSpec((tr, D_MODEL), lambda i: (i, 0)),
            pl.BlockSpec((MODS_ROWS, 6 * D_MODEL), lambda i: (0, 0)),
            pl.BlockSpec(memory_space=pl.ANY),
        ],
        out_specs=pl.BlockSpec((tr, D_MODEL), lambda i: (i, 0)),
        out_shape=jax.ShapeDtypeStruct((n_tok, D_MODEL), F32),
        scratch_shapes=[pltpu.VMEM((2, SORT_ROWS, D_MODEL), BF16), pltpu.SemaphoreType.DMA((2,))],
        compiler_params=_cparams(("arbitrary",)),
        name="combine",
    )(gmap, gmap, loc, wts, x_new, mods_l, yb)


def _channel_dft_tables():
    j = np.arange(F_GDIM)
    ang = 2.0 * np.pi * ((j[:, None] * j[None, :]) % F_GDIM) / F_GDIM
    eye = np.eye(F_GROUPS)
    return (jnp.asarray(np.kron(eye, np.cos(ang)), F32).astype(BF16),
            jnp.asarray(np.kron(eye, np.sin(ang)), F32).astype(BF16))


def _position_dft_tables(seq_len):
    scale = 1.0 / math.sqrt(seq_len * F_GDIM)
    k = np.arange(seq_len, dtype=np.int64)
    ang = 2.0 * np.pi * ((k[:, None] * k[None, :]) % seq_len) / seq_len
    return (jnp.asarray(np.cos(ang) * scale, F32).astype(BF16),
            jnp.asarray(np.sin(ang) * scale, F32).astype(BF16))


def _two_stage_dft_tables(seq_len):
    r = FFT_RADIX
    assert seq_len == r * r
    scale = 1.0 / math.sqrt(seq_len * F_GDIM)
    j = np.arange(r, dtype=np.int64)
    ang_r = 2.0 * np.pi * ((j[:, None] * j[None, :]) % r) / r
    cs, ss = np.cos(ang_r), np.sin(ang_r)
    s = FFT_STEP
    eye = np.eye(s)
    k1 = np.concatenate([np.kron(cs, eye), np.kron(ss, eye)], axis=0) * scale

    def spread(m):
        out = np.zeros((r, s, s, r))
        for i in range(s):
            out[:, i, i, :] = m
        return out.reshape(r * s, s * r)

    k2 = np.concatenate([spread(cs), spread(ss)], axis=1)
    ka = j[None, :, None]
    t0 = (np.arange(r // s)[:, None, None] * s + np.arange(s)[None, None, :])
    ang_t = (2.0 * np.pi * ka * t0 / seq_len).reshape(-1, 1)
    tc = jnp.asarray(np.repeat(np.cos(ang_t), V7X_LANES, axis=1), F32)
    ts = jnp.asarray(np.repeat(np.sin(ang_t), V7X_LANES, axis=1), F32)
    return jnp.asarray(k1, F32).astype(BF16), jnp.asarray(k2, F32).astype(BF16), tc, ts


def _rope_tables(seq_len):
    t = np.arange(seq_len)
    row = (t // GRID_W).astype(np.float64)
    col = (t % GRID_W).astype(np.float64)
    inv = np.power(ROPE_BASE, -np.arange(ROPE_PER_AXIS, dtype=np.float64) / ROPE_PER_AXIS)
    ang = np.concatenate([row[:, None] * inv, col[:, None] * inv], axis=-1)
    cos = np.cos(ang)
    sin = np.sin(ang)
    cos_h = np.concatenate([cos, cos], axis=-1)
    sin_h = np.concatenate([-sin, sin], axis=-1)
    return (jnp.asarray(np.tile(cos_h, (1, NA_HEADS)), F32), jnp.asarray(np.tile(sin_h, (1, NA_HEADS)), F32))


def _bias_table(rpb_l):
    col = np.arange(GRID_W)
    col_start = np.clip(col - NA_WIN_C // 2, 0, GRID_W - NA_WIN_C)
    col_mask = (col[None, :] >= col_start[:, None]) & (col[None, :] < col_start[:, None] + NA_WIN_C)
    dc = np.clip(col[None, :] - col[:, None] + (NA_WIN_C - 1), 0, 2 * NA_WIN_C - 2)
    n_dc = 2 * NA_WIN_C - 1
    pick = (dc.reshape(-1)[None, :] == np.arange(n_dc)[:, None]).astype(np.float32)
    e = jnp.dot(rpb_l.reshape(-1, n_dc), jnp.asarray(pick), precision=HIGHEST)
    e = e.reshape(NA_HEADS, 2 * NA_WIN_R - 1, GRID_W, GRID_W)
    e = jnp.where(jnp.asarray(col_mask)[None, None], e * LOG2E, NEG_BIG)
    b = jnp.stack([e[:, o:o + NA_WIN_R] for o in range(NA_WIN_R)], axis=0)
    b = b.transpose(0, 1, 3, 2, 4)
    return b.reshape(NA_WIN_R, NA_HEADS, GRID_W, NA_WIN_R * GRID_W)


def _moe(h2, h2_second, logits, w1, w3, w2, layer, router_b):
    n_tok = logits.shape[0]
    n_tiles = n_tok // ROUTE_TILE
    max_rows = 2 * n_tok + N_EXPERTS * n_tiles * (SLOT_GROUP - 1) + N_EXPERTS * (EXPERT_BLOCK - 1)
    n_blocks = -(-max_rows // EXPERT_BLOCK)
    n_slots = n_blocks * EXPERT_BLOCK
    ids, wts, cnt, tot = _route(logits, router_b)
    loc, gmap, blk = _slots(ids, cnt, tot, n_blocks)
    xb = _dispatch(gmap, blk, loc, h2, h2_second, n_slots)
    yb = _experts(blk[4, :N_EXPERTS], blk[5, :N_EXPERTS], blk[1, 0:1], xb, w1, w3, w2, layer)
    return yb, gmap, loc, wts


def kernel(x, c, ctx, c_ctx, ada_w, ada_b, norm1_g, w_in, qn_g, kn_g, rpb, conv_w, w_f, w_na, w_cv, w_o,
           norm2_g, router_w, router_b, w1, w3, w2):
    bsz, seq_len, d = x.shape
    ctx_len = ctx.shape[1]
    n_lat = bsz * seq_len
    n_ctx = bsz * ctx_len
    ctx_row = bsz

    c8 = jnp.concatenate([c, c_ctx[None, :], jnp.zeros((MODS_ROWS - bsz - 1, d), F32)], axis=0)
    mods = _mods(c8, ada_w, ada_b)

    cbd, sbd = _channel_dft_tables()
    fst, g_dft, tc3, ts3 = _two_stage_dft_tables(seq_len)
    c_ctx_t, s_ctx_t = _position_dft_tables(ctx_len)
    cos_t, sin_t = _rope_tables(seq_len)
    mavg = jnp.asarray(np.kron(np.eye(HEADS_PER_GROUP), np.full((HEAD_DIM, HEAD_DIM), 1.0 / HEAD_DIM)),
                       F32).astype(BF16)
    rw_hi = router_w.astype(BF16)
    rw_lo = (router_w - rw_hi.astype(F32)).astype(BF16)
    rwt = jnp.concatenate([rw_hi, rw_lo, jnp.zeros((d, V7X_LANES - 2 * N_EXPERTS), BF16)], axis=1)

    xl = x.reshape(n_lat, d)
    xc = ctx.reshape(n_ctx, d)
    for l in range(DEPTH):
        last = l == DEPTH - 1
        w_proj = w_in[l][:, :COL_G].astype(BF16)
        w_gate = w_in[l][:, COL_G:].astype(BF16)
        wf, wna, wcv, wo = (w_f[l].astype(BF16), w_na[l].astype(BF16), w_cv[l].astype(BF16), w_o[l].astype(BF16))
        n1 = norm1_g[l].reshape(1, d)
        n2 = norm2_g[l].reshape(1, d)
        qg = jnp.tile(qn_g[l], NA_HEADS).reshape(1, NA_WIDTH)
        kg = jnp.tile(kn_g[l], NA_HEADS).reshape(1, NA_WIDTH)
        bias_tab = _bias_table(rpb[l])
        mods_l = mods[l]

        a_c, b_c, q_c, k_c, v_c, u_c, bg_c = _proj(
            xc, mods_l, n1, w_proj, qg, kg, mavg, cbd, sbd, None, None,
            seq_len=ctx_len, n_seq=bsz, fixed_row=ctx_row, dft_dtype=BF16)
        a_l, b_l, q_l, k_l, v_l, u_l, bg_l = _proj(
            xl, mods_l, n1, w_proj, qg, kg, mavg, cbd, sbd, cos_t, sin_t,
            seq_len=seq_len, n_seq=bsz, fixed_row=None, dft_dtype=F32)

        f_l = _fourier_two_stage(a_l, b_l, fst, g_dft, tc3, ts3)
        attn_l = _attn(q_l, k_l, v_l, k_c, v_c, bias_tab, n_seq=bsz, seq_len=seq_len, ctx_len=ctx_len)
        xl_new, h2_l, lg_l = _merge(xl, mods_l, n1, n2, f_l, attn_l, u_l, bg_l, conv_w[l], w_gate,
                                    wf, wna, wcv, wo, rwt, seq_len=seq_len, fixed_row=None)
        if last:
            yb, gmap, loc, wts = _moe(h2_l, None, lg_l, w1, w3, w2, l, router_b)
            xl = _combine(gmap, loc, wts, xl_new, mods_l, yb, seq_len=seq_len, fixed_row=None, tile_offset=0)
        else:
            f_c = _fourier(c_ctx_t, s_ctx_t, a_c, b_c)
            attn_c = _ctx_attn(q_c, k_c, v_c, n_seq=bsz, ctx_len=ctx_len)
            xc_new, h2_c, lg_c = _merge(xc, mods_l, n1, n2, f_c, attn_c, u_c, bg_c, conv_w[l], w_gate,
                                        wf, wna, wcv, wo, rwt, seq_len=ctx_len, fixed_row=ctx_row)
            lg = jnp.concatenate([lg_l, lg_c], axis=0)
            yb, gmap, loc, wts = _moe(h2_l, h2_c, lg, w1, w3, w2, l, router_b)
            xl = _combine(gmap, loc, wts, xl_new, mods_l, yb, seq_len=seq_len, fixed_row=None, tile_offset=0)
            xc = _combine(gmap, loc, wts, xc_new, mods_l, yb, seq_len=ctx_len, fixed_row=ctx_row,
                          tile_offset=n_lat // ROUTE_TILE)
    return xl.reshape(bsz, seq_len, d)
```

```python
import functools
import math

import numpy as np
import jax
import jax.numpy as jnp
from jax import lax
from jax.experimental import pallas as pl
from jax.experimental.pallas import tpu as pltpu

F32 = jnp.float32
BF16 = jnp.bfloat16
I32 = jnp.int32
HIGHEST = lax.Precision.HIGHEST

D_MODEL = 1024
DEPTH = 2
GRID_W = 64
EPS = 1e-6
F_GROUPS = 4
F_GDIM = 64
F_WIDTH = 256
NA_HEADS = 8
HEAD_DIM = 64
NA_WIDTH = 512
NA_WIN_R = 8
NA_WIN_C = 16
ATTN_SCALE = HEAD_DIM ** -0.5
LOG2E = math.log2(math.e)
ROPE_BASE = 10000.0
ROPE_PER_AXIS = HEAD_DIM // 4
CONV_WIDTH = 256
COL_Q = 256
COL_K = 768
COL_V = 1280
COL_CX = 1792
COL_CB = 2048
COL_CC = 2304
COL_G = 2560
N_EXPERTS = 16
N_GROUPS = 4
EXPERTS_PER_GROUP = 4
D_EXPERT = 512

V7X_LANES = 128
V7X_SUBLANES = 8
V7X_MXU_DIM = 256

TOKEN_TILE = 512
PROJ_CHUNK_ROWS = 128
MERGE_CHUNK_ROWS = 512
ROUTE_TILE = 512
EXPERT_BLOCK = 512
SLOT_GROUP = 2 * V7X_SUBLANES
SORT_ROWS = -(-(2 * ROUTE_TILE + N_EXPERTS * (SLOT_GROUP - 1)) // V7X_LANES) * V7X_LANES
SORT_GROUPS_PAD = -(-(SORT_ROWS // SLOT_GROUP) // V7X_LANES) * V7X_LANES
HEADS_PER_GROUP = V7X_MXU_DIM // HEAD_DIM
ATTN_ROWS_PER_STEP = 8
FFT_RADIX = 64
FFT_STEP = V7X_SUBLANES
NEG_BIG = -1e30
MODS_ROWS = 8
VMEM_LIMIT = 48 * 1024 * 1024


def _cparams(sem):
    return pltpu.CompilerParams(dimension_semantics=sem, vmem_limit_bytes=VMEM_LIMIT)


def _mods_kernel(c_ref, w_ref, b_ref, o_ref):
    c = c_ref[...]
    sc = c * jax.nn.sigmoid(c)
    o_ref[...] = jnp.dot(sc, w_ref[...], precision=HIGHEST, preferred_element_type=F32) + b_ref[...]


def _mods(c8, ada_w, ada_b):
    nb = 1536
    return pl.pallas_call(
        _mods_kernel,
        grid=(DEPTH, 6 * D_MODEL // nb),
        in_specs=[
            pl.BlockSpec((MODS_ROWS, D_MODEL), lambda l, j: (0, 0)),
            pl.BlockSpec((None, D_MODEL, nb), lambda l, j: (l, 0, j)),
            pl.BlockSpec((None, 1, nb), lambda l, j: (l, 0, j)),
        ],
        out_specs=pl.BlockSpec((None, MODS_ROWS, nb), lambda l, j: (l, 0, j)),
        out_shape=jax.ShapeDtypeStruct((DEPTH, MODS_ROWS, 6 * D_MODEL), F32),
        compiler_params=_cparams(("arbitrary", "arbitrary")),
        name="mods",
    )(c8, ada_w, ada_b.reshape(DEPTH, 1, 6 * D_MODEL))


def _norm_mod(x, g, shift, scale):
    ms = jnp.mean(x * x, axis=-1, keepdims=True)
    return (x * lax.rsqrt(ms + EPS) * g) * (1.0 + scale) + shift


def _mod_row(mods_ref, tile, tiles_per_seq, fixed_row):
    row = fixed_row if fixed_row is not None else tile // tiles_per_seq
    return mods_ref[pl.ds(row, 1), :]


def _proj_kernel(*refs, tiles_per_seq, fixed_row, rope):
    if rope:
        (x_ref, mods_ref, g_ref, w_ref, qg_ref, kg_ref, mavg_ref, cbd_ref, sbd_ref, cos_ref, sin_ref,
         a_ref, b_ref, q_ref, k_ref, v_ref, u_ref, bg_ref) = refs
    else:
        (x_ref, mods_ref, g_ref, w_ref, qg_ref, kg_ref, mavg_ref, cbd_ref, sbd_ref,
         a_ref, b_ref, q_ref, k_ref, v_ref, u_ref, bg_ref) = refs
    m = _mod_row(mods_ref, pl.program_id(0), tiles_per_seq, fixed_row)
    chunk = min(PROJ_CHUNK_ROWS, x_ref.shape[0])
    for c in range(x_ref.shape[0] // chunk):
        rows = slice(c * chunk, (c + 1) * chunk)
        _proj_rows(rows, m, rope, x_ref, g_ref, w_ref, qg_ref, kg_ref, mavg_ref, cbd_ref, sbd_ref,
                   cos_ref if rope else None, sin_ref if rope else None,
                   a_ref, b_ref, q_ref, k_ref, v_ref, u_ref, bg_ref)


def _proj_rows(rows, m, rope, x_ref, g_ref, w_ref, qg_ref, kg_ref, mavg_ref, cbd_ref, sbd_ref, cos_ref, sin_ref,
               a_ref, b_ref, q_ref, k_ref, v_ref, u_ref, bg_ref):
    h = _norm_mod(x_ref[rows, :], g_ref[...], m[:, 0:D_MODEL], m[:, D_MODEL:2 * D_MODEL])
    p = jnp.dot(h.astype(BF16), w_ref[...], preferred_element_type=F32)

    uf = p[:, 0:COL_Q].astype(BF16)
    a_ref[rows, :] = jnp.dot(uf, cbd_ref[...], preferred_element_type=F32).astype(a_ref.dtype)
    b_ref[rows, :] = jnp.dot(uf, sbd_ref[...], preferred_element_type=F32).astype(b_ref.dtype)

    def head_norm(t, g):
        sq = (t * t).astype(BF16)
        half = V7X_MXU_DIM
        ms = jnp.concatenate(
            [jnp.dot(sq[:, i:i + half], mavg_ref[...], preferred_element_type=F32)
             for i in range(0, t.shape[1], half)], axis=1)
        return t * lax.rsqrt(ms + EPS) * g

    def rotate(t):
        n = t.shape[-1]
        lane = lax.broadcasted_iota(I32, t.shape, 1)
        first_half = (lane % HEAD_DIM) < (HEAD_DIM // 2)
        swapped = jnp.where(first_half, pltpu.roll(t, n - HEAD_DIM // 2, 1), pltpu.roll(t, HEAD_DIM // 2, 1))
        return t * cos_ref[rows, :] + swapped * sin_ref[rows, :]

    q = head_norm(p[:, COL_Q:COL_K], qg_ref[...])
    k = head_norm(p[:, COL_K:COL_V], kg_ref[...])
    if rope:
        q = rotate(q)
        k = rotate(k)
    q_ref[rows, :] = (q * (ATTN_SCALE * LOG2E)).astype(BF16)
    k_ref[rows, :] = k.astype(BF16)
    v_ref[rows, :] = p[:, COL_V:COL_CX].astype(BF16)
    u_ref[rows, :] = p[:, COL_CC:COL_G] * p[:, COL_CX:COL_CB]
    bg_ref[rows, :] = p[:, COL_CB:COL_CC]


def _proj(x2, mods_l, norm_g, w_proj, qg, kg, mavg, cbd, sbd, cos_t, sin_t, *, seq_len, n_seq, fixed_row,
          dft_dtype):
    n_tok = x2.shape[0]
    tm = min(TOKEN_TILE, seq_len)
    tps = seq_len // tm
    rope = cos_t is not None
    const = lambda i: (0, 0)
    in_specs = [
        pl.BlockSpec((tm, D_MODEL), lambda i: (i, 0)),
        pl.BlockSpec((MODS_ROWS, 6 * D_MODEL), const),
        pl.BlockSpec((1, D_MODEL), const),
        pl.BlockSpec((D_MODEL, COL_G), const),
        pl.BlockSpec((1, NA_WIDTH), const),
        pl.BlockSpec((1, NA_WIDTH), const),
        pl.BlockSpec((V7X_MXU_DIM, V7X_MXU_DIM), const),
        pl.BlockSpec((F_WIDTH, F_WIDTH), const),
        pl.BlockSpec((F_WIDTH, F_WIDTH), const),
    ]
    args = [x2, mods_l, norm_g, w_proj, qg, kg, mavg, cbd, sbd]
    if rope:
        in_specs += [pl.BlockSpec((tm, NA_WIDTH), lambda i: (i % tps, 0))] * 2
        args += [cos_t, sin_t]
    tok = lambda w: pl.BlockSpec((tm, w), lambda i: (i, 0))
    fmap = pl.BlockSpec((tm, F_WIDTH), lambda i: (i % tps, i // tps))
    out_specs = [fmap, fmap, tok(NA_WIDTH), tok(NA_WIDTH), tok(NA_WIDTH), tok(CONV_WIDTH), tok(CONV_WIDTH)]
    out_shape = [
        jax.ShapeDtypeStruct((seq_len, n_seq * F_WIDTH), dft_dtype),
        jax.ShapeDtypeStruct((seq_len, n_seq * F_WIDTH), dft_dtype),
        jax.ShapeDtypeStruct((n_tok, NA_WIDTH), BF16),
        jax.ShapeDtypeStruct((n_tok, NA_WIDTH), BF16),
        jax.ShapeDtypeStruct((n_tok, NA_WIDTH), BF16),
        jax.ShapeDtypeStruct((n_tok, CONV_WIDTH), F32),
        jax.ShapeDtypeStruct((n_tok, CONV_WIDTH), F32),
    ]
    return pl.pallas_call(
        functools.partial(_proj_kernel, tiles_per_seq=tps, fixed_row=fixed_row, rope=rope),
        grid=(n_tok // tm,),
        in_specs=in_specs,
        out_specs=out_specs,
        out_shape=out_shape,
        compiler_params=_cparams(("arbitrary",)),
        name="proj",
    )(*args)


def _fourier_kernel(c_ref, s_ref, a_ref, b_ref, o_ref):
    o = (jnp.dot(c_ref[...], a_ref[...], preferred_element_type=F32)
         - jnp.dot(s_ref[...], b_ref[...], preferred_element_type=F32))
    o_ref[...] = o.astype(BF16)


def _fourier(c_tab, s_tab, a, b):
    seq_len, width = a.shape
    tk = min(seq_len, 256)
    full = lambda i: (0, 0)
    return pl.pallas_call(
        _fourier_kernel,
        grid=(seq_len // tk,),
        in_specs=[
            pl.BlockSpec((tk, seq_len), lambda i: (i, 0)),
            pl.BlockSpec((tk, seq_len), lambda i: (i, 0)),
            pl.BlockSpec((seq_len, width), full, pipeline_mode=pl.Buffered(1)),
            pl.BlockSpec((seq_len, width), full, pipeline_mode=pl.Buffered(1)),
        ],
        out_specs=pl.BlockSpec((tk, width), lambda i: (i, 0)),
        out_shape=jax.ShapeDtypeStruct((seq_len, width), BF16),
        compiler_params=_cparams(("arbitrary",)),
        name="fourier",
    )(c_tab, s_tab, a, b)


def _fft1_kernel(a_ref, b_ref, k1_ref, tc_ref, ts_ref, zr_ref, zi_ref):
    n = FFT_RADIX * FFT_STEP
    width = a_ref.shape[2]
    k1 = k1_ref[...]
    r1 = jnp.dot(k1, a_ref[...].reshape(n, width).astype(BF16), preferred_element_type=F32)
    r2 = jnp.dot(k1, b_ref[...].reshape(n, width).astype(BF16), preferred_element_type=F32)
    yr = r1[0:n] - r2[n:2 * n]
    yi = -(r2[0:n] + r1[n:2 * n])
    tc = tc_ref[...][:, 0:1]
    ts = ts_ref[...][:, 0:1]
    zr_ref[...] = (yr * tc + yi * ts).reshape(zr_ref.shape)
    zi_ref[...] = (yi * tc - yr * ts).reshape(zi_ref.shape)


def _fft2_kernel(zr_ref, zi_ref, k2_ref, f_ref):
    n = FFT_RADIX * FFT_STEP
    width = zr_ref.shape[2]
    zz = jnp.concatenate([zr_ref[...].reshape(n, width), zi_ref[...].reshape(n, width)], axis=0).astype(BF16)
    f_ref[...] = jnp.dot(k2_ref[...], zz, preferred_element_type=F32).reshape(f_ref.shape)


def _fourier_two_stage(a, b, k1, k2, tc, ts):
    seq_len, width = a.shape
    r = FFT_RADIX
    n = r * FFT_STEP
    a3 = a.reshape(r, r, width)
    b3 = b.reshape(r, r, width)
    steps = r // FFT_STEP
    col_blk = pl.BlockSpec((r, FFT_STEP, width), lambda j: (0, j, 0))
    row_blk = pl.BlockSpec((FFT_STEP, r, width), lambda j: (j, 0, 0))
    tw_blk = pl.BlockSpec((n, V7X_LANES), lambda j: (j, 0))
    z_shape = jax.ShapeDtypeStruct((r, r, width), F32)
    zr, zi = pl.pallas_call(
        _fft1_kernel,
        grid=(steps,),
        in_specs=[col_blk, col_blk, pl.BlockSpec((2 * n, n), lambda j: (0, 0)), tw_blk, tw_blk],
        out_specs=[col_blk, col_blk],
        out_shape=[z_shape, z_shape],
        compiler_params=_cparams(("arbitrary",)),
        name="fft1",
    )(a3, b3, k1, tc, ts)
    f3 = pl.pallas_call(
        _fft2_kernel,
        grid=(steps,),
        in_specs=[row_blk, row_blk, pl.BlockSpec((n, 2 * n), lambda j: (0, 0))],
        out_specs=col_blk,
        out_shape=z_shape,
        compiler_params=_cparams(("arbitrary",)),
        name="fft2",
    )(zr, zi, k2)
    return f3.reshape(seq_len, width)


def _stack_heads(qg):
    lane_head = lax.broadcasted_iota(I32, qg.shape, 1) // HEAD_DIM
    zero = jnp.zeros_like(qg)
    return jnp.concatenate([jnp.where(lane_head == h, qg, zero) for h in range(HEADS_PER_GROUP)], axis=0)


def _unstack_heads(o, rows):
    lane_head = lax.broadcasted_iota(I32, (rows, o.shape[1]), 1) // HEAD_DIM
    acc = jnp.zeros((rows, o.shape[1]), F32)
    for h in range(HEADS_PER_GROUP):
        acc = acc + jnp.where(lane_head == h, o[h * rows:(h + 1) * rows, :], 0.0)
    return acc


_NT = (((1,), (1,)), ((), ()))


def _attn_kernel(q_ref, k_ref, v_ref, kc_ref, vc_ref, *rest, rows):
    bias_refs, o_ref = rest[:ATTN_ROWS_PER_STEP], rest[ATTN_ROWS_PER_STEP]
    n_loc = NA_WIN_R * GRID_W
    for j in range(ATTN_ROWS_PER_STEP):
        r = pl.program_id(1) * ATTN_ROWS_PER_STEP + j
        rs = jnp.clip(r - NA_WIN_R // 2, 0, rows - NA_WIN_R)
        start = pl.multiple_of(rs * GRID_W, GRID_W)
        kwin = k_ref[pl.ds(start, n_loc), :]
        vwin = v_ref[pl.ds(start, n_loc), :]
        q = q_ref[j * GRID_W:(j + 1) * GRID_W, :]
        bias_ref = bias_refs[j]
        outs = []
        for g in range(NA_HEADS // HEADS_PER_GROUP):
            sl = slice(g * V7X_MXU_DIM, (g + 1) * V7X_MXU_DIM)
            qs = _stack_heads(q[:, sl])
            s_loc = lax.dot_general(qs, kwin[:, sl], _NT, preferred_element_type=F32)
            bias = bias_ref[g * HEADS_PER_GROUP:(g + 1) * HEADS_PER_GROUP].reshape(HEADS_PER_GROUP * GRID_W, n_loc)
            s_ctx = lax.dot_general(qs, kc_ref[:, sl], _NT, preferred_element_type=F32)
            s = jnp.concatenate([s_loc + bias, s_ctx], axis=1)
            m = jnp.max(s, axis=-1, keepdims=True)
            p = jnp.exp2(s - m)
            denom = jnp.sum(p, axis=-1, keepdims=True)
            pb = p.astype(BF16)
            o = (jnp.dot(pb[:, :n_loc], vwin[:, sl], preferred_element_type=F32)
                 + jnp.dot(pb[:, n_loc:], vc_ref[:, sl], preferred_element_type=F32))
            outs.append(_unstack_heads(o / denom, GRID_W))
        o_ref[j * GRID_W:(j + 1) * GRID_W, :] = jnp.concatenate(outs, axis=1).astype(BF16)


def _attn(q, k, v, kc, vc, bias_tab, *, n_seq, seq_len, ctx_len):
    rows = seq_len // GRID_W
    rps = ATTN_ROWS_PER_STEP
    steps = rows // rps

    def bias_map(j):
        def index(b, s):
            r = s * rps + j
            rs = jnp.clip(r - NA_WIN_R // 2, 0, rows - NA_WIN_R)
            return (rs - r + NA_WIN_R - 1, 0, 0, 0)
        return index

    bias_specs = [pl.BlockSpec((None, NA_HEADS, GRID_W, NA_WIN_R * GRID_W), bias_map(j)) for j in range(rps)]
    return pl.pallas_call(
        functools.partial(_attn_kernel, rows=rows),
        grid=(n_seq, steps),
        in_specs=[
            pl.BlockSpec((rps * GRID_W, NA_WIDTH), lambda b, s: (b * steps + s, 0)),
            pl.BlockSpec((seq_len, NA_WIDTH), lambda b, s: (b, 0)),
            pl.BlockSpec((seq_len, NA_WIDTH), lambda b, s: (b, 0)),
            pl.BlockSpec((ctx_len, NA_WIDTH), lambda b, s: (b, 0)),
            pl.BlockSpec((ctx_len, NA_WIDTH), lambda b, s: (b, 0)),
        ] + bias_specs,
        out_specs=pl.BlockSpec((rps * GRID_W, NA_WIDTH), lambda b, s: (b * steps + s, 0)),
        out_shape=jax.ShapeDtypeStruct((n_seq * seq_len, NA_WIDTH), BF16),
        compiler_params=_cparams(("arbitrary", "arbitrary")),
        name="attn",
    )(q, k, v, kc, vc, *([bias_tab] * rps))


def _ctx_attn_kernel(q_ref, k_ref, v_ref, o_ref):
    q = q_ref[...]
    n = q.shape[0]
    outs = []
    for g in range(NA_HEADS // HEADS_PER_GROUP):
        sl = slice(g * V7X_MXU_DIM, (g + 1) * V7X_MXU_DIM)
        qs = _stack_heads(q[:, sl])
        s = lax.dot_general(qs, k_ref[:, sl], _NT, preferred_element_type=F32)
        m = jnp.max(s, axis=-1, keepdims=True)
        p = jnp.exp2(s - m)
        denom = jnp.sum(p, axis=-1, keepdims=True)
        o = jnp.dot(p.astype(BF16), v_ref[:, sl], preferred_element_type=F32)
        outs.append(_unstack_heads(o / denom, n))
    o_ref[...] = jnp.concatenate(outs, axis=1).astype(BF16)


def _ctx_attn(q, k, v, *, n_seq, ctx_len):
    spec = pl.BlockSpec((ctx_len, NA_WIDTH), lambda b: (b, 0))
    return pl.pallas_call(
        _ctx_attn_kernel,
        grid=(n_seq,),
        in_specs=[spec, spec, spec],
        out_specs=spec,
        out_shape=jax.ShapeDtypeStruct((n_seq * ctx_len, NA_WIDTH), BF16),
        compiler_params=_cparams(("arbitrary",)),
        name="ctx_attn",
    )(q, k, v)


def _merge_kernel(x_ref, mods_ref, n1_ref, n2_ref, f_ref, at_ref, u_ref, up_ref, un_ref, bg_ref, cw_ref,
                  wg_ref, wf_ref, wna_ref, wcv_ref, wo_ref, rw_ref,
                  xo_ref, h2_ref, lg_ref, *, tiles_per_seq, fixed_row):
    i = pl.program_id(0)
    m = _mod_row(mods_ref, i, tiles_per_seq, fixed_row)
    dm = D_MODEL

    u = u_ref[...]
    t = u.shape[0]
    ti = i % tiles_per_seq
    row = lax.broadcasted_iota(I32, u.shape, 0)
    prev_row = jnp.where(ti == 0, 0.0, up_ref[V7X_SUBLANES - 1:V7X_SUBLANES, :])
    next_row = jnp.where(ti == tiles_per_seq - 1, 0.0, un_ref[0:1, :])
    u_prev = jnp.where(row == 0, prev_row, pltpu.roll(u, 1, 0))
    u_next = jnp.where(row == t - 1, next_row, pltpu.roll(u, t - 1, 0))
    conv = (bg_ref[...] * (cw_ref[0:1, :] * u_prev + cw_ref[1:2, :] * u + cw_ref[2:3, :] * u_next)).astype(BF16)

    chunk = min(MERGE_CHUNK_ROWS, t)
    for c in range(t // chunk):
        rows = slice(c * chunk, (c + 1) * chunk)
        x = x_ref[rows, :]
        h = _norm_mod(x, n1_ref[...], m[:, 0:dm], m[:, dm:2 * dm]).astype(BF16)
        gates = jax.nn.sigmoid(jnp.dot(h, wg_ref[...], preferred_element_type=F32))
        y_f = jnp.dot(f_ref[rows, :].astype(BF16), wf_ref[...], preferred_element_type=F32)
        y_na = jnp.dot(at_ref[rows, :], wna_ref[...], preferred_element_type=F32)
        y_cv = jnp.dot(conv[rows, :], wcv_ref[...], preferred_element_type=F32)
        merged = gates[:, 0:dm] * y_f + gates[:, dm:2 * dm] * y_na + gates[:, 2 * dm:3 * dm] * y_cv
        mixed = jnp.dot(merged.astype(BF16), wo_ref[...], preferred_element_type=F32)
        x_new = x + m[:, 2 * dm:3 * dm] * mixed
        xo_ref[rows, :] = x_new

        h2 = _norm_mod(x_new, n2_ref[...], m[:, 3 * dm:4 * dm], m[:, 4 * dm:5 * dm])
        h2_ref[rows, :] = h2.astype(BF16)
        hi = h2.astype(BF16)
        lo = (h2 - hi.astype(F32)).astype(BF16)
        p_hi = jnp.dot(hi, rw_ref[...], preferred_element_type=F32)
        p_lo = jnp.dot(lo, rw_ref[...], preferred_element_type=F32)
        lg_ref[rows, :] = p_hi + pltpu.roll(p_hi, V7X_LANES - N_EXPERTS, 1) + p_lo


def _merge(x2, mods_l, n1, n2, f_all, attn, u, bg, conv_w, w_gate, w_f, w_na, w_cv, w_o, rwt,
           *, seq_len, fixed_row):
    n_tok = x2.shape[0]
    tm = min(TOKEN_TILE, seq_len)
    tps = seq_len // tm
    const = lambda i: (0, 0)
    halo = tm // V7X_SUBLANES
    n_halo = n_tok // V7X_SUBLANES
    in_specs = [
        pl.BlockSpec((tm, D_MODEL), lambda i: (i, 0)),
        pl.BlockSpec((MODS_ROWS, 6 * D_MODEL), const),
        pl.BlockSpec((1, D_MODEL), const),
        pl.BlockSpec((1, D_MODEL), const),
        pl.BlockSpec((tm, F_WIDTH), lambda i: (i % tps, i // tps)),
        pl.BlockSpec((tm, NA_WIDTH), lambda i: (i, 0)),
        pl.BlockSpec((tm, CONV_WIDTH), lambda i: (i, 0)),
        pl.BlockSpec((V7X_SUBLANES, CONV_WIDTH), lambda i: (jnp.maximum(i * halo - 1, 0), 0)),
        pl.BlockSpec((V7X_SUBLANES, CONV_WIDTH), lambda i: (jnp.minimum((i + 1) * halo, n_halo - 1), 0)),
        pl.BlockSpec((tm, CONV_WIDTH), lambda i: (i, 0)),
        pl.BlockSpec((3, CONV_WIDTH), const),
        pl.BlockSpec((D_MODEL, 3 * D_MODEL), const),
        pl.BlockSpec((F_WIDTH, D_MODEL), const),
        pl.BlockSpec((NA_WIDTH, D_MODEL), const),
        pl.BlockSpec((CONV_WIDTH, D_MODEL), const),
        pl.BlockSpec((D_MODEL, D_MODEL), const),
        pl.BlockSpec((D_MODEL, V7X_LANES), const),
    ]
    out_specs = [
        pl.BlockSpec((tm, D_MODEL), lambda i: (i, 0)),
        pl.BlockSpec((tm, D_MODEL), lambda i: (i, 0)),
        pl.BlockSpec((tm, V7X_LANES), lambda i: (i, 0)),
    ]
    out_shape = [
        jax.ShapeDtypeStruct((n_tok, D_MODEL), F32),
        jax.ShapeDtypeStruct((n_tok, D_MODEL), BF16),
        jax.ShapeDtypeStruct((n_tok, V7X_LANES), F32),
    ]
    return pl.pallas_call(
        functools.partial(_merge_kernel, tiles_per_seq=tps, fixed_row=fixed_row),
        grid=(n_tok // tm,),
        in_specs=in_specs,
        out_specs=out_specs,
        out_shape=out_shape,
        compiler_params=_cparams(("arbitrary",)),
        name="merge",
    )(x2, mods_l, n1, n2, f_all, attn, u, u, u, bg, conv_w, w_gate, w_f, w_na, w_cv, w_o, rwt)


def _first_max(vals):
    best = vals[0]
    idx = jnp.zeros(best.shape, I32)
    for j in range(1, len(vals)):
        better = vals[j] > best
        idx = jnp.where(better, j, idx)
        best = jnp.where(better, vals[j], best)
    return best, idx


def _select(idx, vals):
    out = vals[-1]
    for j in range(len(vals) - 2, -1, -1):
        out = jnp.where(idx == j, vals[j], out)
    return out


def _route_kernel(lg_ref, rb_ref, ids_ref, wts_ref, cnt_ref, tot_ref, run_ref):
    step = pl.program_id(0)

    @pl.when(step == 0)
    def _():
        run_ref[...] = jnp.zeros_like(run_ref)

    s = jax.nn.sigmoid(lg_ref[...].T[0:N_EXPERTS, :])
    sb = s + rb_ref[...]
    t = s.shape[1]
    s_rows = [s[e:e + 1, :] for e in range(N_EXPERTS)]
    b_rows = [sb[e:e + 1, :] for e in range(N_EXPERTS)]
    epg = EXPERTS_PER_GROUP
    gscore = []
    for g in range(N_GROUPS):
        v = b_rows[g * epg:(g + 1) * epg]
        pair = None
        for a in range(epg):
            for b in range(a + 1, epg):
                pair = v[a] + v[b] if pair is None else jnp.maximum(pair, v[a] + v[b])
        gscore.append(pair)
    _, gi = _first_max(gscore)
    bv = [_select(gi, [b_rows[g * epg + j] for g in range(N_GROUPS)]) for j in range(epg)]
    sv = [_select(gi, [s_rows[g * epg + j] for g in range(N_GROUPS)]) for j in range(epg)]
    _, i1 = _first_max(bv)
    _, i2 = _first_max([jnp.where(i1 == j, -jnp.inf, bv[j]) for j in range(epg)])
    s1 = _select(i1, sv)
    s2 = _select(i2, sv)
    tot = s1 + s2
    e1 = gi * epg + i1
    e2 = gi * epg + i2

    eid = lax.broadcasted_iota(I32, (N_EXPERTS, t), 0)
    hit1 = eid == e1
    hit2 = eid == e2
    onehot = jnp.where(hit1 | hit2, 1.0, 0.0)
    before = (lax.broadcasted_iota(I32, (t, t), 0) < lax.broadcasted_iota(I32, (t, t), 1))
    prefix = jnp.dot(onehot.astype(BF16), jnp.where(before, 1.0, 0.0).astype(BF16),
                     preferred_element_type=F32)
    r1 = jnp.sum(jnp.where(hit1, prefix, 0.0), axis=0, keepdims=True)
    r2 = jnp.sum(jnp.where(hit2, prefix, 0.0), axis=0, keepdims=True)
    grp = float(SLOT_GROUP)
    cnt = jnp.sum(onehot, axis=1, keepdims=True)
    cnt = jnp.floor((cnt + (grp - 1.0)) / grp) * grp
    run = run_ref[...] + cnt
    run_ref[...] = run
    cnt_ref[...] = jnp.broadcast_to(cnt, cnt_ref.shape)
    tot_ref[...] = jnp.broadcast_to(run, tot_ref.shape)

    zi = jnp.zeros((V7X_SUBLANES - 4, t), I32)
    ids_ref[...] = jnp.concatenate([e1, e2, r1.astype(I32), r2.astype(I32), zi], axis=0)
    zf = jnp.zeros((V7X_SUBLANES - 2, t), F32)
    wts_ref[...] = jnp.concatenate([s1 / tot, s2 / tot, zf], axis=0)


def _route(logits, router_b):
    n_tok = logits.shape[0]
    tr = ROUTE_TILE
    return pl.pallas_call(
        _route_kernel,
        grid=(n_tok // tr,),
        in_specs=[
            pl.BlockSpec((tr, V7X_LANES), lambda i: (i, 0)),
            pl.BlockSpec((N_EXPERTS, 1), lambda i: (0, 0)),
        ],
        out_specs=[
            pl.BlockSpec((V7X_SUBLANES, tr), lambda i: (0, i)),
            pl.BlockSpec((V7X_SUBLANES, tr), lambda i: (0, i)),
            pl.BlockSpec((None, N_EXPERTS, V7X_LANES), lambda i: (i, 0, 0)),
            pl.BlockSpec((N_EXPERTS, V7X_LANES), lambda i: (0, 0)),
        ],
        out_shape=[
            jax.ShapeDtypeStruct((V7X_SUBLANES, n_tok), I32),
            jax.ShapeDtypeStruct((V7X_SUBLANES, n_tok), F32),
            jax.ShapeDtypeStruct((n_tok // tr, N_EXPERTS, V7X_LANES), F32),
            jax.ShapeDtypeStruct((N_EXPERTS, V7X_LANES), F32),
        ],
        scratch_shapes=[pltpu.VMEM((N_EXPERTS, 1), F32)],
        compiler_params=_cparams(("arbitrary",)),
        name="route",
    )(logits, router_b.reshape(N_EXPERTS, 1))


def _lane_table(vals, width):
    lane = lax.broadcasted_iota(I32, (1, width), 1)
    out = jnp.zeros((1, width), F32)
    for e, v in enumerate(vals):
        out = jnp.where(lane == e, v, out)
    return out


def _slots_kernel(ids_ref, cnt_ref, tot_ref, loc_ref, gmap_ref, blk_ref, off_ref):
    step = pl.program_id(0)

    @pl.when(step == 0)
    def _():
        off_ref[...] = jnp.zeros_like(off_ref)

    blk = float(EXPERT_BLOCK)
    grp = float(SLOT_GROUP)
    cnt = cnt_ref[...][:, 0:1]
    tot = tot_ref[...][:, 0:1]
    off = off_ref[...]
    region = jnp.floor((tot + (blk - 1.0)) / blk) * blk
    starts, ends, local = [], [], []
    run = jnp.zeros((1, 1), F32)
    lrun = jnp.zeros((1, 1), F32)
    for e in range(N_EXPERTS):
        starts.append(run)
        run = run + region[e:e + 1, :]
        ends.append(run)
        local.append(lrun)
        lrun = lrun + cnt[e:e + 1, :]

    ids = ids_ref[...]
    e1, e2 = ids[0:1, :], ids[1:2, :]
    t = ids.shape[1]
    l1 = jnp.zeros((1, t), F32)
    l2 = jnp.zeros((1, t), F32)
    for e in range(N_EXPERTS):
        l1 = jnp.where(e1 == e, local[e], l1)
        l2 = jnp.where(e2 == e, local[e], l2)
    zi = jnp.zeros((V7X_SUBLANES - 2, t), I32)
    loc_ref[...] = jnp.concatenate([l1.astype(I32) + ids[2:3, :], l2.astype(I32) + ids[3:4, :], zi], axis=0)

    wg = gmap_ref.shape[1]
    first = lax.broadcasted_iota(I32, (1, wg), 1).astype(F32) * grp
    dest = jnp.zeros((1, wg), F32)
    for e in range(N_EXPERTS):
        inside = (first >= local[e]) & (first < local[e] + cnt[e:e + 1, :])
        dest = jnp.where(inside, starts[e] + off[e:e + 1, :] + (first - local[e]), dest)
    n_groups = jnp.broadcast_to(lrun / grp, (1, wg))
    zg = jnp.zeros((V7X_SUBLANES - 2, wg), I32)
    gmap_ref[...] = jnp.concatenate([(dest / grp).astype(I32), n_groups.astype(I32), zg], axis=0)
    off_ref[...] = off + cnt

    w = blk_ref.shape[1]
    first_row = lax.broadcasted_iota(I32, (1, w), 1).astype(F32) * blk
    owner = jnp.zeros((1, w), F32)
    for e in range(N_EXPERTS):
        owner = owner + jnp.where(first_row >= ends[e], 1.0, 0.0)
    owner = jnp.minimum(owner, float(N_EXPERTS - 1))
    used = jnp.broadcast_to(ends[-1] / blk, (1, w))
    pad_first = _lane_table([(starts[e] + tot[e:e + 1, :]) / grp for e in range(N_EXPERTS)], w)
    pad_count = _lane_table([(region[e:e + 1, :] - tot[e:e + 1, :]) / grp for e in range(N_EXPERTS)], w)
    zb = jnp.zeros((V7X_SUBLANES - 4, w), I32)
    blk_ref[...] = jnp.concatenate([owner.astype(I32), used.astype(I32), pad_first.astype(I32),
                                    pad_count.astype(I32), zb], axis=0)


def _slots(ids, cnt, tot, n_blocks):
    n_tok = ids.shape[1]
    tr = ROUTE_TILE
    wblk = -(-n_blocks // V7X_LANES) * V7X_LANES
    return pl.pallas_call(
        _slots_kernel,
        grid=(n_tok // tr,),
        in_specs=[
            pl.BlockSpec((V7X_SUBLANES, tr), lambda i: (0, i)),
            pl.BlockSpec((None, N_EXPERTS, V7X_LANES), lambda i: (i, 0, 0)),
            pl.BlockSpec((N_EXPERTS, V7X_LANES), lambda i: (0, 0)),
        ],
        out_specs=[
            pl.BlockSpec((V7X_SUBLANES, tr), lambda i: (0, i)),
            pl.BlockSpec((None, V7X_SUBLANES, SORT_GROUPS_PAD), lambda i: (i, 0, 0)),
            pl.BlockSpec((V7X_SUBLANES, wblk), lambda i: (0, 0)),
        ],
        out_shape=[
            jax.ShapeDtypeStruct((V7X_SUBLANES, n_tok), I32),
            jax.ShapeDtypeStruct((n_tok // tr, V7X_SUBLANES, SORT_GROUPS_PAD), I32),
            jax.ShapeDtypeStruct((V7X_SUBLANES, wblk), I32),
        ],
        scratch_shapes=[pltpu.VMEM((N_EXPERTS, 1), F32)],
        compiler_params=_cparams(("arbitrary",)),
        name="slots",
    )(ids, cnt, tot)


def _group_rows(group):
    if isinstance(group, int):
        return pl.ds(group * SLOT_GROUP, SLOT_GROUP)
    return pl.ds(pl.multiple_of(group * SLOT_GROUP, SLOT_GROUP), SLOT_GROUP)


def _for_each_group(n, body):
    unroll = 4

    def chunk(q, c):
        for u in range(unroll):
            body(q * unroll + u)
        return c

    def single(g, c):
        body(g)
        return c

    whole = n // unroll
    lax.fori_loop(0, whole, chunk, 0)
    lax.fori_loop(whole * unroll, n, single, 0)


def _group_copy(src_ref, src_group, dst_ref, dst_group, sem):
    return pltpu.make_async_copy(src_ref.at[_group_rows(src_group)], dst_ref.at[_group_rows(dst_group)], sem)


def _dispatch_kernel(gmap_ref, gprev_ref, blk_ref, loc_ref, *refs, n_first):
    if n_first is None:
        h_ref, xb_ref, sorted_ref, zero_ref, sem = refs
        second_ref = None
    else:
        h_ref, second_ref, xb_ref, sorted_ref, zero_ref, sem = refs
    step = pl.program_id(0)
    last = pl.num_programs(0) - 1
    buf = step % 2
    loc = loc_ref[...]
    slot = lax.broadcasted_iota(I32, (SORT_ROWS, loc.shape[1]), 0)
    perm = jnp.where(slot == loc[0:1, :], 1.0, jnp.where(slot == loc[1:2, :], 1.0, 0.0)).astype(BF16)

    def sort_rows(src_ref):
        sorted_ref[buf] = jnp.dot(perm, src_ref[...].astype(BF16), preferred_element_type=F32).astype(BF16)

    if second_ref is None:
        sort_rows(h_ref)
    else:
        pl.when(step < n_first)(lambda: sort_rows(h_ref))
        pl.when(step >= n_first)(lambda: sort_rows(second_ref))

    def tile_copy(map_ref, which, g):
        return _group_copy(sorted_ref.at[which], g, xb_ref, map_ref[0, g], sem.at[which])

    def start(g):
        tile_copy(gmap_ref, buf, g).start()

    def wait_tile(map_ref, which):
        rows = pl.ds(0, map_ref[1, 0] * SLOT_GROUP)
        pltpu.make_async_copy(sorted_ref.at[which, rows], xb_ref.at[rows], sem.at[which]).wait()

    _for_each_group(gmap_ref[1, 0], start)

    @pl.when(step == last)
    def _():
        zero_ref[...] = jnp.zeros_like(zero_ref)
        for e in range(N_EXPERTS):
            first = blk_ref[2, e]

            def zstart(g, c, first=first):
                _group_copy(zero_ref, 0, xb_ref, first + g, sem.at[2]).start()
                return c

            def zwait(g, c, first=first):
                _group_copy(zero_ref, 0, xb_ref, first + g, sem.at[2]).wait()
                return c

            lax.fori_loop(0, blk_ref[3, e], zstart, 0)
            lax.fori_loop(0, blk_ref[3, e], zwait, 0)

        def block_copy(b):
            rows = pl.ds(pl.multiple_of(b * EXPERT_BLOCK, EXPERT_BLOCK), EXPERT_BLOCK)
            return pltpu.make_async_copy(zero_ref, xb_ref.at[rows], sem.at[2])

        def bstart(b, c):
            block_copy(b).start()
            return c

        def bwait(b, c):
            block_copy(b).wait()
            return c

        n_blocks = xb_ref.shape[0] // EXPERT_BLOCK
        lax.fori_loop(blk_ref[1, 0], n_blocks, bstart, 0)
        lax.fori_loop(blk_ref[1, 0], n_blocks, bwait, 0)

    @pl.when((step > 0) & (gprev_ref[1, 0] > 0))
    def _():
        wait_tile(gprev_ref, 1 - buf)

    @pl.when((step == last) & (gmap_ref[1, 0] > 0))
    def _():
        wait_tile(gmap_ref, buf)


def _dispatch(gmap, blk, loc, h2, h2_second, n_slots):
    tr = ROUTE_TILE
    n_first = h2.shape[0] // tr
    n_tiles = n_first
    in_specs = [
        pl.BlockSpec((None, V7X_SUBLANES, SORT_GROUPS_PAD), lambda i: (i, 0, 0), memory_space=pltpu.SMEM),
        pl.BlockSpec((None, V7X_SUBLANES, SORT_GROUPS_PAD), lambda i: (jnp.maximum(i - 1, 0), 0, 0),
                     memory_space=pltpu.SMEM),
        pl.BlockSpec(blk.shape, lambda i: (0, 0), memory_space=pltpu.SMEM),
        pl.BlockSpec((V7X_SUBLANES, tr), lambda i: (0, i)),
        pl.BlockSpec((tr, D_MODEL), lambda i: (jnp.minimum(i, n_first - 1), 0)),
    ]
    args = [gmap, gmap, blk, loc, h2]
    if h2_second is not None:
        n_tiles += h2_second.shape[0] // tr
        in_specs.append(pl.BlockSpec((tr, D_MODEL), lambda i: (jnp.maximum(i - n_first, 0), 0)))
        args.append(h2_second)
    return pl.pallas_call(
        functools.partial(_dispatch_kernel, n_first=None if h2_second is None else n_first),
        grid=(n_tiles,),
        in_specs=in_specs,
        out_specs=pl.BlockSpec(memory_space=pl.ANY),
        out_shape=jax.ShapeDtypeStruct((n_slots, D_MODEL), BF16),
        scratch_shapes=[pltpu.VMEM((2, SORT_ROWS, D_MODEL), BF16), pltpu.VMEM((EXPERT_BLOCK, D_MODEL), BF16),
                        pltpu.SemaphoreType.DMA((3,))],
        compiler_params=_cparams(("arbitrary",)),
        name="dispatch",
    )(*args)


def _experts_kernel(blk_ref, used_ref, x_ref, w1_ref, w3_ref, w2_ref, y_ref, w1b, w3b, w2b):
    i = pl.program_id(0)
    prev = blk_ref[jnp.maximum(i - 1, 0)]

    @pl.when((i == 0) | (blk_ref[i] != prev))
    def _():
        w1b[...] = w1_ref[...].astype(BF16)
        w3b[...] = w3_ref[...].astype(BF16)
        w2b[...] = w2_ref[...].astype(BF16)

    @pl.when(i < used_ref[0])
    def _():
        x = x_ref[...]
        a = jnp.dot(x, w1b[...], preferred_element_type=F32)
        b = jnp.dot(x, w3b[...], preferred_element_type=F32)
        hid = (a * jax.nn.sigmoid(a) * b).astype(BF16)
        y_ref[...] = jnp.dot(hid, w2b[...], preferred_element_type=F32).astype(BF16)

    @pl.when(i >= used_ref[0])
    def _():
        y_ref[...] = jnp.zeros_like(y_ref)


def _experts(blk_e, used, xb, w1, w3, w2, layer):
    n_slots = xb.shape[0]
    bm = EXPERT_BLOCK
    row_map = lambda i, be, nu: (jnp.minimum(i, nu[0] - 1), 0)
    w_map = lambda i, be, nu: (layer, be[i], 0, 0)
    grid_spec = pltpu.PrefetchScalarGridSpec(
        num_scalar_prefetch=2,
        grid=(n_slots // bm,),
        in_specs=[
            pl.BlockSpec((bm, D_MODEL), row_map),
            pl.BlockSpec((None, None, D_MODEL, D_EXPERT), w_map),
            pl.BlockSpec((None, None, D_MODEL, D_EXPERT), w_map),
            pl.BlockSpec((None, None, D_EXPERT, D_MODEL), w_map),
        ],
        out_specs=pl.BlockSpec((bm, D_MODEL), lambda i, be, nu: (i, 0)),
        scratch_shapes=[pltpu.VMEM((D_MODEL, D_EXPERT), BF16), pltpu.VMEM((D_MODEL, D_EXPERT), BF16),
                        pltpu.VMEM((D_EXPERT, D_MODEL), BF16)],
    )
    return pl.pallas_call(
        _experts_kernel,
        grid_spec=grid_spec,
        out_shape=jax.ShapeDtypeStruct((n_slots, D_MODEL), BF16),
        compiler_params=_cparams(("arbitrary",)),
        name="experts",
    )(blk_e, used, xb, w1, w3, w2)


_TN = (((0,), (0,)), ((), ()))


def _combine_kernel(gmap_ref, gnext_ref, loc_ref, wts_ref, x_ref, mods_ref, yb_ref, o_ref, ys_ref, sem,
                    *, tiles_per_seq, fixed_row):
    step = pl.program_id(0)
    buf = step % 2

    def fetch(map_ref, which, g):
        return _group_copy(yb_ref, map_ref[0, g], ys_ref.at[which], g, sem.at[which])

    def start_own(g):
        fetch(gmap_ref, buf, g).start()

    def start_next(g):
        fetch(gnext_ref, 1 - buf, g).start()

    def wait_own():
        rows = pl.ds(0, gmap_ref[1, 0] * SLOT_GROUP)
        pltpu.make_async_copy(yb_ref.at[rows], ys_ref.at[buf, rows], sem.at[buf]).wait()

    @pl.when(step == 0)
    def _():
        ys_ref[...] = jnp.zeros_like(ys_ref)
        _for_each_group(gmap_ref[1, 0], start_own)

    @pl.when(step + 1 < pl.num_programs(0))
    def _():
        _for_each_group(gnext_ref[1, 0], start_next)

    loc = loc_ref[...]
    wts = wts_ref[...]
    slot = lax.broadcasted_iota(I32, (SORT_ROWS, loc.shape[1]), 0)
    perm = jnp.where(slot == loc[0:1, :], wts[0:1, :], jnp.where(slot == loc[1:2, :], wts[1:2, :], 0.0))
    pl.when(gmap_ref[1, 0] > 0)(wait_own)
    y = lax.dot_general(perm.astype(BF16), ys_ref[buf], _TN, preferred_element_type=F32)
    m = _mod_row(mods_ref, step, tiles_per_seq, fixed_row)
    o_ref[...] = x_ref[...] + m[:, 5 * D_MODEL:6 * D_MODEL] * y


def _combine(gmap, loc, wts, x_new, mods_l, yb, *, seq_len, fixed_row, tile_offset):
    n_tok = x_new.shape[0]
    tr = ROUTE_TILE
    tps = max(seq_len // tr, 1)
    n_tiles = n_tok // tr
    return pl.pallas_call(
        functools.partial(_combine_kernel, tiles_per_seq=tps, fixed_row=fixed_row),
        grid=(n_tiles,),
        in_specs=[
            pl.BlockSpec((None, V7X_SUBLANES, SORT_GROUPS_PAD), lambda i: (i + tile_offset, 0, 0),
                         memory_space=pltpu.SMEM),
            pl.BlockSpec((None, V7X_SUBLANES, SORT_GROUPS_PAD),
                         lambda i: (jnp.minimum(i + 1, n_tiles - 1) + tile_offset, 0, 0), memory_space=pltpu.SMEM),
            pl.BlockSpec((V7X_SUBLANES, tr), lambda i: (0, i + tile_offset)),
            pl.BlockSpec((V7X_SUBLANES, tr), lambda i: (0, i + tile_offset)),
            pl.BlockSpec((tr, D_MODEL), lambda i: (i, 0)),
            pl.BlockSpec((MODS_ROWS, 6 * D_MODEL), lambda i: (0, 0)),
            pl.BlockSpec(memory_space=pl.ANY),
        ],
        out_specs=pl.BlockSpec((tr, D_MODEL), lambda i: (i, 0)),
        out_shape=jax.ShapeDtypeStruct((n_tok, D_MODEL), F32),
        scratch_shapes=[pltpu.VMEM((2, SORT_ROWS, D_MODEL), BF16), pltpu.SemaphoreType.DMA((2,))],
        compiler_params=_cparams(("arbitrary",)),
        name="combine",
    )(gmap, gmap, loc, wts, x_new, mods_l, yb)


def _channel_dft_tables():
    j = np.arange(F_GDIM)
    ang = 2.0 * np.pi * ((j[:, None] * j[None, :]) % F_GDIM) / F_GDIM
    eye = np.eye(F_GROUPS)
    return (jnp.asarray(np.kron(eye, np.cos(ang)), F32).astype(BF16),
            jnp.asarray(np.kron(eye, np.sin(ang)), F32).astype(BF16))


def _position_dft_tables(seq_len):
    scale = 1.0 / math.sqrt(seq_len * F_GDIM)
    k = np.arange(seq_len, dtype=np.int64)
    ang = 2.0 * np.pi * ((k[:, None] * k[None, :]) % seq_len) / seq_len
    return (jnp.asarray(np.cos(ang) * scale, F32).astype(BF16),
            jnp.asarray(np.sin(ang) * scale, F32).astype(BF16))


def _two_stage_dft_tables(seq_len):
    r = FFT_RADIX
    assert seq_len == r * r
    scale = 1.0 / math.sqrt(seq_len * F_GDIM)
    j = np.arange(r, dtype=np.int64)
    ang_r = 2.0 * np.pi * ((j[:, None] * j[None, :]) % r) / r
    cs, ss = np.cos(ang_r), np.sin(ang_r)
    s = FFT_STEP
    eye = np.eye(s)
    k1 = np.concatenate([np.kron(cs, eye), np.kron(ss, eye)], axis=0) * scale

    def spread(m):
        out = np.zeros((r, s, s, r))
        for i in range(s):
            out[:, i, i, :] = m
        return out.reshape(r * s, s * r)

    k2 = np.concatenate([spread(cs), spread(ss)], axis=1)
    ka = j[None, :, None]
    t0 = (np.arange(r // s)[:, None, None] * s + np.arange(s)[None, None, :])
    ang_t = (2.0 * np.pi * ka * t0 / seq_len).reshape(-1, 1)
    tc = jnp.asarray(np.repeat(np.cos(ang_t), V7X_LANES, axis=1), F32)
    ts = jnp.asarray(np.repeat(np.sin(ang_t), V7X_LANES, axis=1), F32)
    return jnp.asarray(k1, F32).astype(BF16), jnp.asarray(k2, F32).astype(BF16), tc, ts


def _rope_tables(seq_len):
    t = np.arange(seq_len)
    row = (t // GRID_W).astype(np.float64)
    col = (t % GRID_W).astype(np.float64)
    inv = np.power(ROPE_BASE, -np.arange(ROPE_PER_AXIS, dtype=np.float64) / ROPE_PER_AXIS)
    ang = np.concatenate([row[:, None] * inv, col[:, None] * inv], axis=-1)
    cos = np.cos(ang)
    sin = np.sin(ang)
    cos_h = np.concatenate([cos, cos], axis=-1)
    sin_h = np.concatenate([-sin, sin], axis=-1)
    return (jnp.asarray(np.tile(cos_h, (1, NA_HEADS)), F32), jnp.asarray(np.tile(sin_h, (1, NA_HEADS)), F32))


def _bias_table(rpb_l):
    col = np.arange(GRID_W)
    col_start = np.clip(col - NA_WIN_C // 2, 0, GRID_W - NA_WIN_C)
    col_mask = (col[None, :] >= col_start[:, None]) & (col[None, :] < col_start[:, None] + NA_WIN_C)
    dc = np.clip(col[None, :] - col[:, None] + (NA_WIN_C - 1), 0, 2 * NA_WIN_C - 2)
    n_dc = 2 * NA_WIN_C - 1
    pick = (dc.reshape(-1)[None, :] == np.arange(n_dc)[:, None]).astype(np.float32)
    e = jnp.dot(rpb_l.reshape(-1, n_dc), jnp.asarray(pick), precision=HIGHEST)
    e = e.reshape(NA_HEADS, 2 * NA_WIN_R - 1, GRID_W, GRID_W)
    e = jnp.where(jnp.asarray(col_mask)[None, None], e * LOG2E, NEG_BIG)
    b = jnp.stack([e[:, o:o + NA_WIN_R] for o in range(NA_WIN_R)], axis=0)
    b = b.transpose(0, 1, 3, 2, 4)
    return b.reshape(NA_WIN_R, NA_HEADS, GRID_W, NA_WIN_R * GRID_W)


def _moe(h2, h2_second, logits, w1, w3, w2, layer, router_b):
    n_tok = logits.shape[0]
    n_tiles = n_tok // ROUTE_TILE
    max_rows = 2 * n_tok + N_EXPERTS * n_tiles * (SLOT_GROUP - 1) + N_EXPERTS * (EXPERT_BLOCK - 1)
    n_blocks = -(-max_rows // EXPERT_BLOCK)
    n_slots = n_blocks * EXPERT_BLOCK
    ids, wts, cnt, tot = _route(logits, router_b)
    loc, gmap, blk = _slots(ids, cnt, tot, n_blocks)
    xb = _dispatch(gmap, blk, loc, h2, h2_second, n_slots)
    yb = _experts(blk[0, :n_blocks], blk[1, 0:1], xb, w1, w3, w2, layer)
    return yb, gmap, loc, wts


def kernel(x, c, ctx, c_ctx, ada_w, ada_b, norm1_g, w_in, qn_g, kn_g, rpb, conv_w, w_f, w_na, w_cv, w_o,
           norm2_g, router_w, router_b, w1, w3, w2):
    bsz, seq_len, d = x.shape
    ctx_len = ctx.shape[1]
    n_lat = bsz * seq_len
    n_ctx = bsz * ctx_len
    ctx_row = bsz

    c8 = jnp.concatenate([c, c_ctx[None, :], jnp.zeros((MODS_ROWS - bsz - 1, d), F32)], axis=0)
    mods = _mods(c8, ada_w, ada_b)

    cbd, sbd = _channel_dft_tables()
    fst, g_dft, tc3, ts3 = _two_stage_dft_tables(seq_len)
    c_ctx_t, s_ctx_t = _position_dft_tables(ctx_len)
    cos_t, sin_t = _rope_tables(seq_len)
    mavg = jnp.asarray(np.kron(np.eye(HEADS_PER_GROUP), np.full((HEAD_DIM, HEAD_DIM), 1.0 / HEAD_DIM)),
                       F32).astype(BF16)
    rw_hi = router_w.astype(BF16)
    rw_lo = (router_w - rw_hi.astype(F32)).astype(BF16)
    rwt = jnp.concatenate([rw_hi, rw_lo, jnp.zeros((d, V7X_LANES - 2 * N_EXPERTS), BF16)], axis=1)

    xl = x.reshape(n_lat, d)
    xc = ctx.reshape(n_ctx, d)
    for l in range(DEPTH):
        last = l == DEPTH - 1
        w_proj = w_in[l][:, :COL_G].astype(BF16)
        w_gate = w_in[l][:, COL_G:].astype(BF16)
        wf, wna, wcv, wo = (w_f[l].astype(BF16), w_na[l].astype(BF16), w_cv[l].astype(BF16), w_o[l].astype(BF16))
        n1 = norm1_g[l].reshape(1, d)
        n2 = norm2_g[l].reshape(1, d)
        qg = jnp.tile(qn_g[l], NA_HEADS).reshape(1, NA_WIDTH)
        kg = jnp.tile(kn_g[l], NA_HEADS).reshape(1, NA_WIDTH)
        bias_tab = _bias_table(rpb[l])
        mods_l = mods[l]

        a_c, b_c, q_c, k_c, v_c, u_c, bg_c = _proj(
            xc, mods_l, n1, w_proj, qg, kg, mavg, cbd, sbd, None, None,
            seq_len=ctx_len, n_seq=bsz, fixed_row=ctx_row, dft_dtype=BF16)
        a_l, b_l, q_l, k_l, v_l, u_l, bg_l = _proj(
            xl, mods_l, n1, w_proj, qg, kg, mavg, cbd, sbd, cos_t, sin_t,
            seq_len=seq_len, n_seq=bsz, fixed_row=None, dft_dtype=F32)

        f_l = _fourier_two_stage(a_l, b_l, fst, g_dft, tc3, ts3)
        attn_l = _attn(q_l, k_l, v_l, k_c, v_c, bias_tab, n_seq=bsz, seq_len=seq_len, ctx_len=ctx_len)
        xl_new, h2_l, lg_l = _merge(xl, mods_l, n1, n2, f_l, attn_l, u_l, bg_l, conv_w[l], w_gate,
                                    wf, wna, wcv, wo, rwt, seq_len=seq_len, fixed_row=None)
        if last:
            yb, gmap, loc, wts = _moe(h2_l, None, lg_l, w1, w3, w2, l, router_b)
            xl = _combine(gmap, loc, wts, xl_new, mods_l, yb, seq_len=seq_len, fixed_row=None, tile_offset=0)
        else:
            f_c = _fourier(c_ctx_t, s_ctx_t, a_c, b_c)
            attn_c = _ctx_attn(q_c, k_c, v_c, n_seq=bsz, ctx_len=ctx_len)
            xc_new, h2_c, lg_c = _merge(xc, mods_l, n1, n2, f_c, attn_c, u_c, bg_c, conv_w[l], w_gate,
                                        wf, wna, wcv, wo, rwt, seq_len=ctx_len, fixed_row=ctx_row)
            lg = jnp.concatenate([lg_l, lg_c], axis=0)
            yb, gmap, loc, wts = _moe(h2_l, h2_c, lg, w1, w3, w2, l, router_b)
            xl = _combine(gmap, loc, wts, xl_new, mods_l, yb, seq_len=seq_len, fixed_row=None, tile_offset=0)
            xc = _combine(gmap, loc, wts, xc_new, mods_l, yb, seq_len=ctx_len, fixed_row=ctx_row,
                          tile_offset=n_lat // ROUTE_TILE)
    return xl.reshape(bsz, seq_len, d)
```

```python
import functools
import math

import numpy as np
import jax
import jax.numpy as jnp
from jax import lax
from jax.experimental import pallas as pl
from jax.experimental.pallas import tpu as pltpu

F32 = jnp.float32
BF16 = jnp.bfloat16
I32 = jnp.int32
HIGHEST = lax.Precision.HIGHEST

D_MODEL = 1024
DEPTH = 2
GRID_W = 64
EPS = 1e-6
F_GROUPS = 4
F_GDIM = 64
F_WIDTH = 256
NA_HEADS = 8
HEAD_DIM = 64
NA_WIDTH = 512
NA_WIN_R = 8
NA_WIN_C = 16
ATTN_SCALE = HEAD_DIM ** -0.5
LOG2E = math.log2(math.e)
ROPE_BASE = 10000.0
ROPE_PER_AXIS = HEAD_DIM // 4
CONV_WIDTH = 256
COL_Q = 256
COL_K = 768
COL_V = 1280
COL_CX = 1792
COL_CB = 2048
COL_CC = 2304
COL_G = 2560
N_EXPERTS = 16
N_GROUPS = 4
EXPERTS_PER_GROUP = 4
D_EXPERT = 512

V7X_LANES = 128
V7X_SUBLANES = 8
V7X_MXU_DIM = 256

TOKEN_TILE = 512
PROJ_CHUNK_ROWS = 128
MERGE_CHUNK_ROWS = 512
ROUTE_TILE = 512
EXPERT_BLOCK = 1024
SLOT_GROUP = 2 * V7X_SUBLANES
SORT_ROWS = -(-(2 * ROUTE_TILE + N_EXPERTS * (SLOT_GROUP - 1)) // V7X_LANES) * V7X_LANES
SORT_GROUPS_PAD = -(-(SORT_ROWS // SLOT_GROUP) // V7X_LANES) * V7X_LANES
HEADS_PER_GROUP = V7X_MXU_DIM // HEAD_DIM
ATTN_ROWS_PER_STEP = 16
FFT_RADIX = 64
FFT_STEP = V7X_SUBLANES
NEG_BIG = -1e30
MODS_ROWS = 8
VMEM_LIMIT = 48 * 1024 * 1024


def _cparams(sem):
    return pltpu.CompilerParams(dimension_semantics=sem, vmem_limit_bytes=VMEM_LIMIT)


def _mods_kernel(c_ref, w_ref, b_ref, o_ref):
    c = c_ref[...]
    sc = c * jax.nn.sigmoid(c)
    w = w_ref[...]
    w_hi = w.astype(BF16)
    w_lo = (w - w_hi.astype(F32)).astype(BF16)
    s_hi = sc.astype(BF16)
    s_lo = (sc - s_hi.astype(F32)).astype(BF16)
    acc = jnp.dot(s_hi, w_hi, preferred_element_type=F32)
    acc = acc + jnp.dot(s_hi, w_lo, preferred_element_type=F32)
    acc = acc + jnp.dot(s_lo, w_hi, preferred_element_type=F32)
    o_ref[...] = acc + b_ref[...]


def _mods(c8, ada_w, ada_b):
    nb = 1536
    return pl.pallas_call(
        _mods_kernel,
        grid=(DEPTH, 6 * D_MODEL // nb),
        in_specs=[
            pl.BlockSpec((MODS_ROWS, D_MODEL), lambda l, j: (0, 0)),
            pl.BlockSpec((None, D_MODEL, nb), lambda l, j: (l, 0, j)),
            pl.BlockSpec((None, 1, nb), lambda l, j: (l, 0, j)),
        ],
        out_specs=pl.BlockSpec((None, MODS_ROWS, nb), lambda l, j: (l, 0, j)),
        out_shape=jax.ShapeDtypeStruct((DEPTH, MODS_ROWS, 6 * D_MODEL), F32),
        compiler_params=_cparams(("arbitrary", "arbitrary")),
        name="mods",
    )(c8, ada_w, ada_b.reshape(DEPTH, 1, 6 * D_MODEL))


def _norm_mod(x, g, shift, scale):
    ms = jnp.mean(x * x, axis=-1, keepdims=True)
    return (x * lax.rsqrt(ms + EPS) * g) * (1.0 + scale) + shift


def _mod_row(mods_ref, tile, tiles_per_seq, fixed_row):
    row = fixed_row if fixed_row is not None else tile // tiles_per_seq
    return mods_ref[pl.ds(row, 1), :]


def _proj_kernel(*refs, tiles_per_seq, fixed_row, rope):
    if rope:
        (x_ref, mods_ref, g_ref, w_ref, qg_ref, kg_ref, mavg_ref, cbd_ref, sbd_ref, cos_ref, sin_ref,
         a_ref, b_ref, q_ref, k_ref, v_ref, u_ref, bg_ref) = refs
    else:
        (x_ref, mods_ref, g_ref, w_ref, qg_ref, kg_ref, mavg_ref, cbd_ref, sbd_ref,
         a_ref, b_ref, q_ref, k_ref, v_ref, u_ref, bg_ref) = refs
    m = _mod_row(mods_ref, pl.program_id(0), tiles_per_seq, fixed_row)
    chunk = min(PROJ_CHUNK_ROWS, x_ref.shape[0])
    for c in range(x_ref.shape[0] // chunk):
        rows = slice(c * chunk, (c + 1) * chunk)
        _proj_rows(rows, m, rope, x_ref, g_ref, w_ref, qg_ref, kg_ref, mavg_ref, cbd_ref, sbd_ref,
                   cos_ref if rope else None, sin_ref if rope else None,
                   a_ref, b_ref, q_ref, k_ref, v_ref, u_ref, bg_ref)


def _proj_rows(rows, m, rope, x_ref, g_ref, w_ref, qg_ref, kg_ref, mavg_ref, cbd_ref, sbd_ref, cos_ref, sin_ref,
               a_ref, b_ref, q_ref, k_ref, v_ref, u_ref, bg_ref):
    h = _norm_mod(x_ref[rows, :], g_ref[...], m[:, 0:D_MODEL], m[:, D_MODEL:2 * D_MODEL])
    p = jnp.dot(h.astype(BF16), w_ref[...], preferred_element_type=F32)

    uf = p[:, 0:COL_Q].astype(BF16)
    a_ref[rows, :] = jnp.dot(uf, cbd_ref[...], preferred_element_type=F32).astype(a_ref.dtype)
    b_ref[rows, :] = jnp.dot(uf, sbd_ref[...], preferred_element_type=F32).astype(b_ref.dtype)

    def head_norm(t, g):
        sq = (t * t).astype(BF16)
        half = V7X_MXU_DIM
        ms = jnp.concatenate(
            [jnp.dot(sq[:, i:i + half], mavg_ref[...], preferred_element_type=F32)
             for i in range(0, t.shape[1], half)], axis=1)
        return t * lax.rsqrt(ms + EPS) * g

    def rotate(t):
        n = t.shape[-1]
        lane = lax.broadcasted_iota(I32, t.shape, 1)
        first_half = (lane % HEAD_DIM) < (HEAD_DIM // 2)
        swapped = jnp.where(first_half, pltpu.roll(t, n - HEAD_DIM // 2, 1), pltpu.roll(t, HEAD_DIM // 2, 1))
        return t * cos_ref[rows, :] + swapped * sin_ref[rows, :]

    q = head_norm(p[:, COL_Q:COL_K], qg_ref[...])
    k = head_norm(p[:, COL_K:COL_V], kg_ref[...])
    if rope:
        q = rotate(q)
        k = rotate(k)
    q_ref[rows, :] = (q * (ATTN_SCALE * LOG2E)).astype(BF16)
    k_ref[rows, :] = k.astype(BF16)
    v_ref[rows, :] = p[:, COL_V:COL_CX].astype(BF16)
    u_ref[rows, :] = p[:, COL_CC:COL_G] * p[:, COL_CX:COL_CB]
    bg_ref[rows, :] = p[:, COL_CB:COL_CC]


def _proj(x2, mods_l, norm_g, w_proj, qg, kg, mavg, cbd, sbd, cos_t, sin_t, *, seq_len, n_seq, fixed_row,
          dft_dtype):
    n_tok = x2.shape[0]
    tm = min(TOKEN_TILE, seq_len)
    tps = seq_len // tm
    rope = cos_t is not None
    const = lambda i: (0, 0)
    in_specs = [
        pl.BlockSpec((tm, D_MODEL), lambda i: (i, 0)),
        pl.BlockSpec((MODS_ROWS, 6 * D_MODEL), const),
        pl.BlockSpec((1, D_MODEL), const),
        pl.BlockSpec((D_MODEL, COL_G), const),
        pl.BlockSpec((1, NA_WIDTH), const),
        pl.BlockSpec((1, NA_WIDTH), const),
        pl.BlockSpec((V7X_MXU_DIM, V7X_MXU_DIM), const),
        pl.BlockSpec((F_WIDTH, F_WIDTH), const),
        pl.BlockSpec((F_WIDTH, F_WIDTH), const),
    ]
    args = [x2, mods_l, norm_g, w_proj, qg, kg, mavg, cbd, sbd]
    if rope:
        in_specs += [pl.BlockSpec((tm, NA_WIDTH), lambda i: (i % tps, 0))] * 2
        args += [cos_t, sin_t]
    tok = lambda w: pl.BlockSpec((tm, w), lambda i: (i, 0))
    fmap = pl.BlockSpec((tm, F_WIDTH), lambda i: (i % tps, i // tps))
    out_specs = [fmap, fmap, tok(NA_WIDTH), tok(NA_WIDTH), tok(NA_WIDTH), tok(CONV_WIDTH), tok(CONV_WIDTH)]
    out_shape = [
        jax.ShapeDtypeStruct((seq_len, n_seq * F_WIDTH), dft_dtype),
        jax.ShapeDtypeStruct((seq_len, n_seq * F_WIDTH), dft_dtype),
        jax.ShapeDtypeStruct((n_tok, NA_WIDTH), BF16),
        jax.ShapeDtypeStruct((n_tok, NA_WIDTH), BF16),
        jax.ShapeDtypeStruct((n_tok, NA_WIDTH), BF16),
        jax.ShapeDtypeStruct((n_tok, CONV_WIDTH), F32),
        jax.ShapeDtypeStruct((n_tok, CONV_WIDTH), F32),
    ]
    return pl.pallas_call(
        functools.partial(_proj_kernel, tiles_per_seq=tps, fixed_row=fixed_row, rope=rope),
        grid=(n_tok // tm,),
        in_specs=in_specs,
        out_specs=out_specs,
        out_shape=out_shape,
        compiler_params=_cparams(("arbitrary",)),
        name="proj",
    )(*args)


def _fourier_kernel(c_ref, s_ref, a_ref, b_ref, o_ref):
    o = (jnp.dot(c_ref[...], a_ref[...], preferred_element_type=F32)
         - jnp.dot(s_ref[...], b_ref[...], preferred_element_type=F32))
    o_ref[...] = o.astype(BF16)


def _fourier(c_tab, s_tab, a, b):
    seq_len, width = a.shape
    tk = min(seq_len, 256)
    full = lambda i: (0, 0)
    return pl.pallas_call(
        _fourier_kernel,
        grid=(seq_len // tk,),
        in_specs=[
            pl.BlockSpec((tk, seq_len), lambda i: (i, 0)),
            pl.BlockSpec((tk, seq_len), lambda i: (i, 0)),
            pl.BlockSpec((seq_len, width), full, pipeline_mode=pl.Buffered(1)),
            pl.BlockSpec((seq_len, width), full, pipeline_mode=pl.Buffered(1)),
        ],
        out_specs=pl.BlockSpec((tk, width), lambda i: (i, 0)),
        out_shape=jax.ShapeDtypeStruct((seq_len, width), BF16),
        compiler_params=_cparams(("arbitrary",)),
        name="fourier",
    )(c_tab, s_tab, a, b)


def _fft1_kernel(a_ref, b_ref, k1_ref, tc_ref, ts_ref, zr_ref, zi_ref):
    n = FFT_RADIX * FFT_STEP
    width = a_ref.shape[2]
    k1 = k1_ref[...]
    r1 = jnp.dot(k1, a_ref[...].reshape(n, width).astype(BF16), preferred_element_type=F32)
    r2 = jnp.dot(k1, b_ref[...].reshape(n, width).astype(BF16), preferred_element_type=F32)
    yr = r1[0:n] - r2[n:2 * n]
    yi = -(r2[0:n] + r1[n:2 * n])
    tc = tc_ref[...][:, 0:1]
    ts = ts_ref[...][:, 0:1]
    zr_ref[...] = (yr * tc + yi * ts).reshape(zr_ref.shape)
    zi_ref[...] = (yi * tc - yr * ts).reshape(zi_ref.shape)


def _fft2_kernel(zr_ref, zi_ref, k2_ref, f_ref):
    n = FFT_RADIX * FFT_STEP
    width = zr_ref.shape[2]
    zz = jnp.concatenate([zr_ref[...].reshape(n, width), zi_ref[...].reshape(n, width)], axis=0).astype(BF16)
    f_ref[...] = jnp.dot(k2_ref[...], zz, preferred_element_type=F32).reshape(f_ref.shape)


def _fourier_two_stage(a, b, k1, k2, tc, ts):
    seq_len, width = a.shape
    r = FFT_RADIX
    n = r * FFT_STEP
    a3 = a.reshape(r, r, width)
    b3 = b.reshape(r, r, width)
    steps = r // FFT_STEP
    col_blk = pl.BlockSpec((r, FFT_STEP, width), lambda j: (0, j, 0))
    row_blk = pl.BlockSpec((FFT_STEP, r, width), lambda j: (j, 0, 0))
    tw_blk = pl.BlockSpec((n, V7X_LANES), lambda j: (j, 0))
    z_shape = jax.ShapeDtypeStruct((r, r, width), F32)
    zr, zi = pl.pallas_call(
        _fft1_kernel,
        grid=(steps,),
        in_specs=[col_blk, col_blk, pl.BlockSpec((2 * n, n), lambda j: (0, 0)), tw_blk, tw_blk],
        out_specs=[col_blk, col_blk],
        out_shape=[z_shape, z_shape],
        compiler_params=_cparams(("arbitrary",)),
        name="fft1",
    )(a3, b3, k1, tc, ts)
    f3 = pl.pallas_call(
        _fft2_kernel,
        grid=(steps,),
        in_specs=[row_blk, row_blk, pl.BlockSpec((n, 2 * n), lambda j: (0, 0))],
        out_specs=col_blk,
        out_shape=z_shape,
        compiler_params=_cparams(("arbitrary",)),
        name="fft2",
    )(zr, zi, k2)
    return f3.reshape(seq_len, width)


def _stack_heads(qg):
    lane_head = lax.broadcasted_iota(I32, qg.shape, 1) // HEAD_DIM
    zero = jnp.zeros_like(qg)
    return jnp.concatenate([jnp.where(lane_head == h, qg, zero) for h in range(HEADS_PER_GROUP)], axis=0)


def _unstack_heads(o, rows):
    lane_head = lax.broadcasted_iota(I32, (rows, o.shape[1]), 1) // HEAD_DIM
    acc = jnp.zeros((rows, o.shape[1]), F32)
    for h in range(HEADS_PER_GROUP):
        acc = acc + jnp.where(lane_head == h, o[h * rows:(h + 1) * rows, :], 0.0)
    return acc


_NT = (((1,), (1,)), ((), ()))


def _attn_kernel(q_ref, k_ref, v_ref, kc_ref, vc_ref, bias_tab_ref, o_ref, *, rows):
    n_loc = NA_WIN_R * GRID_W
    for j in range(ATTN_ROWS_PER_STEP):
        r = pl.program_id(1) * ATTN_ROWS_PER_STEP + j
        rs = jnp.clip(r - NA_WIN_R // 2, 0, rows - NA_WIN_R)
        start = pl.multiple_of(rs * GRID_W, GRID_W)
        kwin = k_ref[pl.ds(start, n_loc), :]
        vwin = v_ref[pl.ds(start, n_loc), :]
        q = q_ref[j * GRID_W:(j + 1) * GRID_W, :]
        bias_ref = bias_tab_ref.at[rs - r + (NA_WIN_R - 1)]
        outs = []
        for g in range(NA_HEADS // HEADS_PER_GROUP):
            sl = slice(g * V7X_MXU_DIM, (g + 1) * V7X_MXU_DIM)
            qs = _stack_heads(q[:, sl])
            s_loc = lax.dot_general(qs, kwin[:, sl], _NT, preferred_element_type=F32)
            bias = bias_ref[g * HEADS_PER_GROUP:(g + 1) * HEADS_PER_GROUP].reshape(HEADS_PER_GROUP * GRID_W, n_loc)
            s_ctx = lax.dot_general(qs, kc_ref[:, sl], _NT, preferred_element_type=F32)
            s = jnp.concatenate([s_loc + bias, s_ctx], axis=1)
            m = jnp.max(s, axis=-1, keepdims=True)
            p = jnp.exp2(s - m)
            denom = jnp.sum(p, axis=-1, keepdims=True)
            pb = p.astype(BF16)
            o = (jnp.dot(pb[:, :n_loc], vwin[:, sl], preferred_element_type=F32)
                 + jnp.dot(pb[:, n_loc:], vc_ref[:, sl], preferred_element_type=F32))
            outs.append(_unstack_heads(o / denom, GRID_W))
        o_ref[j * GRID_W:(j + 1) * GRID_W, :] = jnp.concatenate(outs, axis=1).astype(BF16)


def _attn(q, k, v, kc, vc, bias_tab, *, n_seq, seq_len, ctx_len):
    rows = seq_len // GRID_W
    rps = ATTN_ROWS_PER_STEP
    steps = rows // rps

    return pl.pallas_call(
        functools.partial(_attn_kernel, rows=rows),
        grid=(n_seq, steps),
        in_specs=[
            pl.BlockSpec((rps * GRID_W, NA_WIDTH), lambda b, s: (b * steps + s, 0)),
            pl.BlockSpec((seq_len, NA_WIDTH), lambda b, s: (b, 0)),
            pl.BlockSpec((seq_len, NA_WIDTH), lambda b, s: (b, 0)),
            pl.BlockSpec((ctx_len, NA_WIDTH), lambda b, s: (b, 0)),
            pl.BlockSpec((ctx_len, NA_WIDTH), lambda b, s: (b, 0)),
            pl.BlockSpec(bias_tab.shape, lambda b, s: (0, 0, 0, 0), pipeline_mode=pl.Buffered(1)),
        ],
        out_specs=pl.BlockSpec((rps * GRID_W, NA_WIDTH), lambda b, s: (b * steps + s, 0)),
        out_shape=jax.ShapeDtypeStruct((n_seq * seq_len, NA_WIDTH), BF16),
        compiler_params=_cparams(("arbitrary", "arbitrary")),
        name="attn",
    )(q, k, v, kc, vc, bias_tab)


def _ctx_attn_kernel(q_ref, k_ref, v_ref, o_ref):
    q = q_ref[...]
    n = q.shape[0]
    outs = []
    for g in range(NA_HEADS // HEADS_PER_GROUP):
        sl = slice(g * V7X_MXU_DIM, (g + 1) * V7X_MXU_DIM)
        qs = _stack_heads(q[:, sl])
        s = lax.dot_general(qs, k_ref[:, sl], _NT, preferred_element_type=F32)
        m = jnp.max(s, axis=-1, keepdims=True)
        p = jnp.exp2(s - m)
        denom = jnp.sum(p, axis=-1, keepdims=True)
        o = jnp.dot(p.astype(BF16), v_ref[:, sl], preferred_element_type=F32)
        outs.append(_unstack_heads(o / denom, n))
    o_ref[...] = jnp.concatenate(outs, axis=1).astype(BF16)


def _ctx_attn(q, k, v, *, n_seq, ctx_len):
    spec = pl.BlockSpec((ctx_len, NA_WIDTH), lambda b: (b, 0))
    return pl.pallas_call(
        _ctx_attn_kernel,
        grid=(n_seq,),
        in_specs=[spec, spec, spec],
        out_specs=spec,
        out_shape=jax.ShapeDtypeStruct((n_seq * ctx_len, NA_WIDTH), BF16),
        compiler_params=_cparams(("arbitrary",)),
        name="ctx_attn",
    )(q, k, v)


def _merge_kernel(x_ref, mods_ref, n1_ref, n2_ref, f_ref, at_ref, u_ref, up_ref, un_ref, bg_ref, cw_ref,
                  wg_ref, wf_ref, wna_ref, wcv_ref, wo_ref, rw_ref,
                  xo_ref, h2_ref, lg_ref, *, tiles_per_seq, fixed_row):
    i = pl.program_id(0)
    m = _mod_row(mods_ref, i, tiles_per_seq, fixed_row)
    dm = D_MODEL

    u = u_ref[...]
    t = u.shape[0]
    ti = i % tiles_per_seq
    row = lax.broadcasted_iota(I32, u.shape, 0)
    prev_row = jnp.where(ti == 0, 0.0, up_ref[V7X_SUBLANES - 1:V7X_SUBLANES, :])
    next_row = jnp.where(ti == tiles_per_seq - 1, 0.0, un_ref[0:1, :])
    u_prev = jnp.where(row == 0, prev_row, pltpu.roll(u, 1, 0))
    u_next = jnp.where(row == t - 1, next_row, pltpu.roll(u, t - 1, 0))
    conv = (bg_ref[...] * (cw_ref[0:1, :] * u_prev + cw_ref[1:2, :] * u + cw_ref[2:3, :] * u_next)).astype(BF16)

    chunk = min(MERGE_CHUNK_ROWS, t)
    for c in range(t // chunk):
        rows = slice(c * chunk, (c + 1) * chunk)
        x = x_ref[rows, :]
        h = _norm_mod(x, n1_ref[...], m[:, 0:dm], m[:, dm:2 * dm]).astype(BF16)
        gates = jax.nn.sigmoid(jnp.dot(h, wg_ref[...], preferred_element_type=F32))
        y_f = jnp.dot(f_ref[rows, :].astype(BF16), wf_ref[...], preferred_element_type=F32)
        y_na = jnp.dot(at_ref[rows, :], wna_ref[...], preferred_element_type=F32)
        y_cv = jnp.dot(conv[rows, :], wcv_ref[...], preferred_element_type=F32)
        merged = gates[:, 0:dm] * y_f + gates[:, dm:2 * dm] * y_na + gates[:, 2 * dm:3 * dm] * y_cv
        mixed = jnp.dot(merged.astype(BF16), wo_ref[...], preferred_element_type=F32)
        x_new = x + m[:, 2 * dm:3 * dm] * mixed
        xo_ref[rows, :] = x_new

        h2 = _norm_mod(x_new, n2_ref[...], m[:, 3 * dm:4 * dm], m[:, 4 * dm:5 * dm])
        h2_ref[rows, :] = h2.astype(BF16)
        hi = h2.astype(BF16)
        lo = (h2 - hi.astype(F32)).astype(BF16)
        p_hi = jnp.dot(hi, rw_ref[...], preferred_element_type=F32)
        p_lo = jnp.dot(lo, rw_ref[...], preferred_element_type=F32)
        lg_ref[rows, :] = p_hi + pltpu.roll(p_hi, V7X_LANES - N_EXPERTS, 1) + p_lo


def _merge(x2, mods_l, n1, n2, f_all, attn, u, bg, conv_w, w_gate, w_f, w_na, w_cv, w_o, rwt,
           *, seq_len, fixed_row):
    n_tok = x2.shape[0]
    tm = min(TOKEN_TILE, seq_len)
    tps = seq_len // tm
    const = lambda i: (0, 0)
    halo = tm // V7X_SUBLANES
    n_halo = n_tok // V7X_SUBLANES
    in_specs = [
        pl.BlockSpec((tm, D_MODEL), lambda i: (i, 0)),
        pl.BlockSpec((MODS_ROWS, 6 * D_MODEL), const),
        pl.BlockSpec((1, D_MODEL), const),
        pl.BlockSpec((1, D_MODEL), const),
        pl.BlockSpec((tm, F_WIDTH), lambda i: (i % tps, i // tps)),
        pl.BlockSpec((tm, NA_WIDTH), lambda i: (i, 0)),
        pl.BlockSpec((tm, CONV_WIDTH), lambda i: (i, 0)),
        pl.BlockSpec((V7X_SUBLANES, CONV_WIDTH), lambda i: (jnp.maximum(i * halo - 1, 0), 0)),
        pl.BlockSpec((V7X_SUBLANES, CONV_WIDTH), lambda i: (jnp.minimum((i + 1) * halo, n_halo - 1), 0)),
        pl.BlockSpec((tm, CONV_WIDTH), lambda i: (i, 0)),
        pl.BlockSpec((3, CONV_WIDTH), const),
        pl.BlockSpec((D_MODEL, 3 * D_MODEL), const),
        pl.BlockSpec((F_WIDTH, D_MODEL), const),
        pl.BlockSpec((NA_WIDTH, D_MODEL), const),
        pl.BlockSpec((CONV_WIDTH, D_MODEL), const),
        pl.BlockSpec((D_MODEL, D_MODEL), const),
        pl.BlockSpec((D_MODEL, V7X_LANES), const),
    ]
    out_specs = [
        pl.BlockSpec((tm, D_MODEL), lambda i: (i, 0)),
        pl.BlockSpec((tm, D_MODEL), lambda i: (i, 0)),
        pl.BlockSpec((tm, V7X_LANES), lambda i: (i, 0)),
    ]
    out_shape = [
        jax.ShapeDtypeStruct((n_tok, D_MODEL), F32),
        jax.ShapeDtypeStruct((n_tok, D_MODEL), BF16),
        jax.ShapeDtypeStruct((n_tok, V7X_LANES), F32),
    ]
    return pl.pallas_call(
        functools.partial(_merge_kernel, tiles_per_seq=tps, fixed_row=fixed_row),
        grid=(n_tok // tm,),
        in_specs=in_specs,
        out_specs=out_specs,
        out_shape=out_shape,
        compiler_params=_cparams(("arbitrary",)),
        name="merge",
    )(x2, mods_l, n1, n2, f_all, attn, u, u, u, bg, conv_w, w_gate, w_f, w_na, w_cv, w_o, rwt)


def _first_max(vals):
    best = vals[0]
    idx = jnp.zeros(best.shape, I32)
    for j in range(1, len(vals)):
        better = vals[j] > best
        idx = jnp.where(better, j, idx)
        best = jnp.where(better, vals[j], best)
    return best, idx


def _select(idx, vals):
    out = vals[-1]
    for j in range(len(vals) - 2, -1, -1):
        out = jnp.where(idx == j, vals[j], out)
    return out


def _route_kernel(lg_ref, rb_ref, ids_ref, wts_ref, cnt_ref, tot_ref, run_ref):
    step = pl.program_id(0)

    @pl.when(step == 0)
    def _():
        run_ref[...] = jnp.zeros_like(run_ref)

    s = jax.nn.sigmoid(lg_ref[...].T[0:N_EXPERTS, :])
    sb = s + rb_ref[...]
    t = s.shape[1]
    s_rows = [s[e:e + 1, :] for e in range(N_EXPERTS)]
    b_rows = [sb[e:e + 1, :] for e in range(N_EXPERTS)]
    epg = EXPERTS_PER_GROUP
    gscore = []
    for g in range(N_GROUPS):
        v = b_rows[g * epg:(g + 1) * epg]
        pair = None
        for a in range(epg):
            for b in range(a + 1, epg):
                pair = v[a] + v[b] if pair is None else jnp.maximum(pair, v[a] + v[b])
        gscore.append(pair)
    _, gi = _first_max(gscore)
    bv = [_select(gi, [b_rows[g * epg + j] for g in range(N_GROUPS)]) for j in range(epg)]
    sv = [_select(gi, [s_rows[g * epg + j] for g in range(N_GROUPS)]) for j in range(epg)]
    _, i1 = _first_max(bv)
    _, i2 = _first_max([jnp.where(i1 == j, -jnp.inf, bv[j]) for j in range(epg)])
    s1 = _select(i1, sv)
    s2 = _select(i2, sv)
    tot = s1 + s2
    e1 = gi * epg + i1
    e2 = gi * epg + i2

    eid = lax.broadcasted_iota(I32, (N_EXPERTS, t), 0)
    hit1 = eid == e1
    hit2 = eid == e2
    onehot = jnp.where(hit1 | hit2, 1.0, 0.0)
    before = (lax.broadcasted_iota(I32, (t, t), 0) < lax.broadcasted_iota(I32, (t, t), 1))
    prefix = jnp.dot(onehot.astype(BF16), jnp.where(before, 1.0, 0.0).astype(BF16),
                     preferred_element_type=F32)
    r1 = jnp.sum(jnp.where(hit1, prefix, 0.0), axis=0, keepdims=True)
    r2 = jnp.sum(jnp.where(hit2, prefix, 0.0), axis=0, keepdims=True)
    grp = float(SLOT_GROUP)
    cnt = jnp.sum(onehot, axis=1, keepdims=True)
    cnt = jnp.floor((cnt + (grp - 1.0)) / grp) * grp
    run = run_ref[...] + cnt
    run_ref[...] = run
    cnt_ref[...] = jnp.broadcast_to(cnt, cnt_ref.shape)
    tot_ref[...] = jnp.broadcast_to(run, tot_ref.shape)

    zi = jnp.zeros((V7X_SUBLANES - 4, t), I32)
    ids_ref[...] = jnp.concatenate([e1, e2, r1.astype(I32), r2.astype(I32), zi], axis=0)
    zf = jnp.zeros((V7X_SUBLANES - 2, t), F32)
    wts_ref[...] = jnp.concatenate([s1 / tot, s2 / tot, zf], axis=0)


def _route(logits, router_b):
    n_tok = logits.shape[0]
    tr = ROUTE_TILE
    return pl.pallas_call(
        _route_kernel,
        grid=(n_tok // tr,),
        in_specs=[
            pl.BlockSpec((tr, V7X_LANES), lambda i: (i, 0)),
            pl.BlockSpec((N_EXPERTS, 1), lambda i: (0, 0)),
        ],
        out_specs=[
            pl.BlockSpec((V7X_SUBLANES, tr), lambda i: (0, i)),
            pl.BlockSpec((V7X_SUBLANES, tr), lambda i: (0, i)),
            pl.BlockSpec((None, N_EXPERTS, V7X_LANES), lambda i: (i, 0, 0)),
            pl.BlockSpec((N_EXPERTS, V7X_LANES), lambda i: (0, 0)),
        ],
        out_shape=[
            jax.ShapeDtypeStruct((V7X_SUBLANES, n_tok), I32),
            jax.ShapeDtypeStruct((V7X_SUBLANES, n_tok), F32),
            jax.ShapeDtypeStruct((n_tok // tr, N_EXPERTS, V7X_LANES), F32),
            jax.ShapeDtypeStruct((N_EXPERTS, V7X_LANES), F32),
        ],
        scratch_shapes=[pltpu.VMEM((N_EXPERTS, 1), F32)],
        compiler_params=_cparams(("arbitrary",)),
        name="route",
    )(logits, router_b.reshape(N_EXPERTS, 1))


def _lane_table(vals, width):
    lane = lax.broadcasted_iota(I32, (1, width), 1)
    out = jnp.zeros((1, width), F32)
    for e, v in enumerate(vals):
        out = jnp.where(lane == e, v, out)
    return out


def _slots_kernel(ids_ref, cnt_ref, tot_ref, loc_ref, gmap_ref, blk_ref, off_ref):
    step = pl.program_id(0)

    @pl.when(step == 0)
    def _():
        off_ref[...] = jnp.zeros_like(off_ref)

    blk = float(EXPERT_BLOCK)
    grp = float(SLOT_GROUP)
    cnt = cnt_ref[...][:, 0:1]
    tot = tot_ref[...][:, 0:1]
    off = off_ref[...]
    region = jnp.floor((tot + (blk - 1.0)) / blk) * blk
    starts, ends, local = [], [], []
    run = jnp.zeros((1, 1), F32)
    lrun = jnp.zeros((1, 1), F32)
    for e in range(N_EXPERTS):
        starts.append(run)
        run = run + region[e:e + 1, :]
        ends.append(run)
        local.append(lrun)
        lrun = lrun + cnt[e:e + 1, :]

    ids = ids_ref[...]
    e1, e2 = ids[0:1, :], ids[1:2, :]
    t = ids.shape[1]
    l1 = jnp.zeros((1, t), F32)
    l2 = jnp.zeros((1, t), F32)
    for e in range(N_EXPERTS):
        l1 = jnp.where(e1 == e, local[e], l1)
        l2 = jnp.where(e2 == e, local[e], l2)
    zi = jnp.zeros((V7X_SUBLANES - 2, t), I32)
    loc_ref[...] = jnp.concatenate([l1.astype(I32) + ids[2:3, :], l2.astype(I32) + ids[3:4, :], zi], axis=0)

    wg = gmap_ref.shape[1]
    first = lax.broadcasted_iota(I32, (1, wg), 1).astype(F32) * grp
    dest = jnp.zeros((1, wg), F32)
    for e in range(N_EXPERTS):
        inside = (first >= local[e]) & (first < local[e] + cnt[e:e + 1, :])
        dest = jnp.where(inside, starts[e] + off[e:e + 1, :] + (first - local[e]), dest)
    n_groups = jnp.broadcast_to(lrun / grp, (1, wg))
    zg = jnp.zeros((V7X_SUBLANES - 2, wg), I32)
    gmap_ref[...] = jnp.concatenate([(dest / grp).astype(I32), n_groups.astype(I32), zg], axis=0)
    off_ref[...] = off + cnt

    w = blk_ref.shape[1]
    first_row = lax.broadcasted_iota(I32, (1, w), 1).astype(F32) * blk
    owner = jnp.zeros((1, w), F32)
    for e in range(N_EXPERTS):
        owner = owner + jnp.where(first_row >= ends[e], 1.0, 0.0)
    owner = jnp.minimum(owner, float(N_EXPERTS - 1))
    used = jnp.broadcast_to(ends[-1] / blk, (1, w))
    pad_first = _lane_table([(starts[e] + tot[e:e + 1, :]) / grp for e in range(N_EXPERTS)], w)
    pad_count = _lane_table([(region[e:e + 1, :] - tot[e:e + 1, :]) / grp for e in range(N_EXPERTS)], w)
    zb = jnp.zeros((V7X_SUBLANES - 4, w), I32)
    blk_ref[...] = jnp.concatenate([owner.astype(I32), used.astype(I32), pad_first.astype(I32),
                                    pad_count.astype(I32), zb], axis=0)


def _slots(ids, cnt, tot, n_blocks):
    n_tok = ids.shape[1]
    tr = ROUTE_TILE
    wblk = -(-n_blocks // V7X_LANES) * V7X_LANES
    return pl.pallas_call(
        _slots_kernel,
        grid=(n_tok // tr,),
        in_specs=[
            pl.BlockSpec((V7X_SUBLANES, tr), lambda i: (0, i)),
            pl.BlockSpec((None, N_EXPERTS, V7X_LANES), lambda i: (i, 0, 0)),
            pl.BlockSpec((N_EXPERTS, V7X_LANES), lambda i: (0, 0)),
        ],
        out_specs=[
            pl.BlockSpec((V7X_SUBLANES, tr), lambda i: (0, i)),
            pl.BlockSpec((None, V7X_SUBLANES, SORT_GROUPS_PAD), lambda i: (i, 0, 0)),
            pl.BlockSpec((V7X_SUBLANES, wblk), lambda i: (0, 0)),
        ],
        out_shape=[
            jax.ShapeDtypeStruct((V7X_SUBLANES, n_tok), I32),
            jax.ShapeDtypeStruct((n_tok // tr, V7X_SUBLANES, SORT_GROUPS_PAD), I32),
            jax.ShapeDtypeStruct((V7X_SUBLANES, wblk), I32),
        ],
        scratch_shapes=[pltpu.VMEM((N_EXPERTS, 1), F32)],
        compiler_params=_cparams(("arbitrary",)),
        name="slots",
    )(ids, cnt, tot)


def _group_rows(group):
    if isinstance(group, int):
        return pl.ds(group * SLOT_GROUP, SLOT_GROUP)
    return pl.ds(pl.multiple_of(group * SLOT_GROUP, SLOT_GROUP), SLOT_GROUP)


def _for_each_group(n, body):
    unroll = 4

    def chunk(q, c):
        for u in range(unroll):
            body(q * unroll + u)
        return c

    def single(g, c):
        body(g)
        return c

    whole = n // unroll
    lax.fori_loop(0, whole, chunk, 0)
    lax.fori_loop(whole * unroll, n, single, 0)


def _group_copy(src_ref, src_group, dst_ref, dst_group, sem):
    return pltpu.make_async_copy(src_ref.at[_group_rows(src_group)], dst_ref.at[_group_rows(dst_group)], sem)


def _dispatch_kernel(gmap_ref, gprev_ref, blk_ref, loc_ref, *refs, n_first):
    if n_first is None:
        h_ref, xb_ref, sorted_ref, zero_ref, sem = refs
        second_ref = None
    else:
        h_ref, second_ref, xb_ref, sorted_ref, zero_ref, sem = refs
    step = pl.program_id(0)
    last = pl.num_programs(0) - 1
    buf = step % 2
    loc = loc_ref[...]
    slot = lax.broadcasted_iota(I32, (SORT_ROWS, loc.shape[1]), 0)
    perm = jnp.where(slot == loc[0:1, :], 1.0, jnp.where(slot == loc[1:2, :], 1.0, 0.0)).astype(BF16)

    def sort_rows(src_ref):
        sorted_ref[buf] = jnp.dot(perm, src_ref[...].astype(BF16), preferred_element_type=F32).astype(BF16)

    if second_ref is None:
        sort_rows(h_ref)
    else:
        pl.when(step < n_first)(lambda: sort_rows(h_ref))
        pl.when(step >= n_first)(lambda: sort_rows(second_ref))

    def tile_copy(map_ref, which, g):
        return _group_copy(sorted_ref.at[which], g, xb_ref, map_ref[0, g], sem.at[which])

    def start(g):
        tile_copy(gmap_ref, buf, g).start()

    def wait_tile(map_ref, which):
        rows = pl.ds(0, map_ref[1, 0] * SLOT_GROUP)
        pltpu.make_async_copy(sorted_ref.at[which, rows], xb_ref.at[rows], sem.at[which]).wait()

    _for_each_group(gmap_ref[1, 0], start)

    @pl.when(step == last)
    def _():
        zero_ref[...] = jnp.zeros_like(zero_ref)
        for e in range(N_EXPERTS):
            first = blk_ref[2, e]

            def zstart(g, c, first=first):
                _group_copy(zero_ref, 0, xb_ref, first + g, sem.at[2]).start()
                return c

            def zwait(g, c, first=first):
                _group_copy(zero_ref, 0, xb_ref, first + g, sem.at[2]).wait()
                return c

            lax.fori_loop(0, blk_ref[3, e], zstart, 0)
            lax.fori_loop(0, blk_ref[3, e], zwait, 0)

        def block_copy(b):
            rows = pl.ds(pl.multiple_of(b * EXPERT_BLOCK, EXPERT_BLOCK), EXPERT_BLOCK)
            return pltpu.make_async_copy(zero_ref, xb_ref.at[rows], sem.at[2])

        def bstart(b, c):
            block_copy(b).start()
            return c

        def bwait(b, c):
            block_copy(b).wait()
            return c

        n_blocks = xb_ref.shape[0] // EXPERT_BLOCK
        lax.fori_loop(blk_ref[1, 0], n_blocks, bstart, 0)
        lax.fori_loop(blk_ref[1, 0], n_blocks, bwait, 0)

    @pl.when((step > 0) & (gprev_ref[1, 0] > 0))
    def _():
        wait_tile(gprev_ref, 1 - buf)

    @pl.when((step == last) & (gmap_ref[1, 0] > 0))
    def _():
        wait_tile(gmap_ref, buf)


def _dispatch(gmap, blk, loc, h2, h2_second, n_slots):
    tr = ROUTE_TILE
    n_first = h2.shape[0] // tr
    n_tiles = n_first
    in_specs = [
        pl.BlockSpec((None, V7X_SUBLANES, SORT_GROUPS_PAD), lambda i: (i, 0, 0), memory_space=pltpu.SMEM),
        pl.BlockSpec((None, V7X_SUBLANES, SORT_GROUPS_PAD), lambda i: (jnp.maximum(i - 1, 0), 0, 0),
                     memory_space=pltpu.SMEM),
        pl.BlockSpec(blk.shape, lambda i: (0, 0), memory_space=pltpu.SMEM),
        pl.BlockSpec((V7X_SUBLANES, tr), lambda i: (0, i)),
        pl.BlockSpec((tr, D_MODEL), lambda i: (jnp.minimum(i, n_first - 1), 0)),
    ]
    args = [gmap, gmap, blk, loc, h2]
    if h2_second is not None:
        n_tiles += h2_second.shape[0] // tr
        in_specs.append(pl.BlockSpec((tr, D_MODEL), lambda i: (jnp.maximum(i - n_first, 0), 0)))
        args.append(h2_second)
    return pl.pallas_call(
        functools.partial(_dispatch_kernel, n_first=None if h2_second is None else n_first),
        grid=(n_tiles,),
        in_specs=in_specs,
        out_specs=pl.BlockSpec(memory_space=pl.ANY),
        out_shape=jax.ShapeDtypeStruct((n_slots, D_MODEL), BF16),
        scratch_shapes=[pltpu.VMEM((2, SORT_ROWS, D_MODEL), BF16), pltpu.VMEM((EXPERT_BLOCK, D_MODEL), BF16),
                        pltpu.SemaphoreType.DMA((3,))],
        compiler_params=_cparams(("arbitrary",)),
        name="dispatch",
    )(*args)


def _experts_kernel(blk_ref, used_ref, x_ref, w1_ref, w3_ref, w2_ref, y_ref, w1b, w3b, w2b):
    i = pl.program_id(0)
    prev = blk_ref[jnp.maximum(i - 1, 0)]

    @pl.when((i == 0) | (blk_ref[i] != prev))
    def _():
        w1b[...] = w1_ref[...].astype(BF16)
        w3b[...] = w3_ref[...].astype(BF16)
        w2b[...] = w2_ref[...].astype(BF16)

    @pl.when(i < used_ref[0])
    def _():
        x = x_ref[...]
        a = jnp.dot(x, w1b[...], preferred_element_type=F32)
        b = jnp.dot(x, w3b[...], preferred_element_type=F32)
        hid = (a * jax.nn.sigmoid(a) * b).astype(BF16)
        y_ref[...] = jnp.dot(hid, w2b[...], preferred_element_type=F32).astype(BF16)

    @pl.when(i >= used_ref[0])
    def _():
        y_ref[...] = jnp.zeros_like(y_ref)


def _experts(blk_e, used, xb, w1, w3, w2, layer):
    n_slots = xb.shape[0]
    bm = EXPERT_BLOCK
    row_map = lambda i, be, nu: (jnp.minimum(i, nu[0] - 1), 0)
    w_map = lambda i, be, nu: (layer, be[i], 0, 0)
    grid_spec = pltpu.PrefetchScalarGridSpec(
        num_scalar_prefetch=2,
        grid=(n_slots // bm,),
        in_specs=[
            pl.BlockSpec((bm, D_MODEL), row_map),
            pl.BlockSpec((None, None, D_MODEL, D_EXPERT), w_map),
            pl.BlockSpec((None, None, D_MODEL, D_EXPERT), w_map),
            pl.BlockSpec((None, None, D_EXPERT, D_MODEL), w_map),
        ],
        out_specs=pl.BlockSpec((bm, D_MODEL), lambda i, be, nu: (i, 0)),
        scratch_shapes=[pltpu.VMEM((D_MODEL, D_EXPERT), BF16), pltpu.VMEM((D_MODEL, D_EXPERT), BF16),
                        pltpu.VMEM((D_EXPERT, D_MODEL), BF16)],
    )
    return pl.pallas_call(
        _experts_kernel,
        grid_spec=grid_spec,
        out_shape=jax.ShapeDtypeStruct((n_slots, D_MODEL), BF16),
        compiler_params=_cparams(("arbitrary",)),
        name="experts",
    )(blk_e, used, xb, w1, w3, w2)


_TN = (((0,), (0,)), ((), ()))


def _combine_kernel(gmap_ref, gnext_ref, loc_ref, wts_ref, x_ref, mods_ref, yb_ref, o_ref, ys_ref, sem,
                    *, tiles_per_seq, fixed_row):
    step = pl.program_id(0)
    buf = step % 2

    def fetch(map_ref, which, g):
        return _group_copy(yb_ref, map_ref[0, g], ys_ref.at[which], g, sem.at[which])

    def start_own(g):
        fetch(gmap_ref, buf, g).start()

    def start_next(g):
        fetch(gnext_ref, 1 - buf, g).start()

    def wait_own():
        rows = pl.ds(0, gmap_ref[1, 0] * SLOT_GROUP)
        pltpu.make_async_copy(yb_ref.at[rows], ys_ref.at[buf, rows], sem.at[buf]).wait()

    @pl.when(step == 0)
    def _():
        ys_ref[...] = jnp.zeros_like(ys_ref)
        _for_each_group(gmap_ref[1, 0], start_own)

    @pl.when(step + 1 < pl.num_programs(0))
    def _():
        _for_each_group(gnext_ref[1, 0], start_next)

    loc = loc_ref[...]
    wts = wts_ref[...]
    slot = lax.broadcasted_iota(I32, (SORT_ROWS, loc.shape[1]), 0)
    perm = jnp.where(slot == loc[0:1, :], wts[0:1, :], jnp.where(slot == loc[1:2, :], wts[1:2, :], 0.0))
    pl.when(gmap_ref[1, 0] > 0)(wait_own)
    y = lax.dot_general(perm.astype(BF16), ys_ref[buf], _TN, preferred_element_type=F32)
    m = _mod_row(mods_ref, step, tiles_per_seq, fixed_row)
    o_ref[...] = x_ref[...] + m[:, 5 * D_MODEL:6 * D_MODEL] * y


def _combine(gmap, loc, wts, x_new, mods_l, yb, *, seq_len, fixed_row, tile_offset):
    n_tok = x_new.shape[0]
    tr = ROUTE_TILE
    tps = max(seq_len // tr, 1)
    n_tiles = n_tok // tr
    return pl.pallas_call(
        functools.partial(_combine_kernel, tiles_per_seq=tps, fixed_row=fixed_row),
        grid=(n_tiles,),
        in_specs=[
            pl.BlockSpec((None, V7X_SUBLANES, SORT_GROUPS_PAD), lambda i: (i + tile_offset, 0, 0),
                         memory_space=pltpu.SMEM),
            pl.BlockSpec((None, V7X_SUBLANES, SORT_GROUPS_PAD),
                         lambda i: (jnp.minimum(i + 1, n_tiles - 1) + tile_offset, 0, 0), memory_space=pltpu.SMEM),
            pl.BlockSpec((V7X_SUBLANES, tr), lambda i: (0, i + tile_offset)),
            pl.BlockSpec((V7X_SUBLANES, tr), lambda i: (0, i + tile_offset)),
            pl.BlockSpec((tr, D_MODEL), lambda i: (i, 0)),
            pl.BlockSpec((MODS_ROWS, 6 * D_MODEL), lambda i: (0, 0)),
            pl.BlockSpec(memory_space=pl.ANY),
        ],
        out_specs=pl.BlockSpec((tr, D_MODEL), lambda i: (i, 0)),
        out_shape=jax.ShapeDtypeStruct((n_tok, D_MODEL), F32),
        scratch_shapes=[pltpu.VMEM((2, SORT_ROWS, D_MODEL), BF16), pltpu.SemaphoreType.DMA((2,))],
        compiler_params=_cparams(("arbitrary",)),
        name="combine",
    )(gmap, gmap, loc, wts, x_new, mods_l, yb)


def _channel_dft_tables():
    j = np.arange(F_GDIM)
    ang = 2.0 * np.pi * ((j[:, None] * j[None, :]) % F_GDIM) / F_GDIM
    eye = np.eye(F_GROUPS)
    return (jnp.asarray(np.kron(eye, np.cos(ang)), F32).astype(BF16),
            jnp.asarray(np.kron(eye, np.sin(ang)), F32).astype(BF16))


def _position_dft_tables(seq_len):
    scale = 1.0 / math.sqrt(seq_len * F_GDIM)
    k = np.arange(seq_len, dtype=np.int64)
    ang = 2.0 * np.pi * ((k[:, None] * k[None, :]) % seq_len) / seq_len
    return (jnp.asarray(np.cos(ang) * scale, F32).astype(BF16),
            jnp.asarray(np.sin(ang) * scale, F32).astype(BF16))


def _two_stage_dft_tables(seq_len):
    r = FFT_RADIX
    assert seq_len == r * r
    scale = 1.0 / math.sqrt(seq_len * F_GDIM)
    j = np.arange(r, dtype=np.int64)
    ang_r = 2.0 * np.pi * ((j[:, None] * j[None, :]) % r) / r
    cs, ss = np.cos(ang_r), np.sin(ang_r)
    s = FFT_STEP
    eye = np.eye(s)
    k1 = np.concatenate([np.kron(cs, eye), np.kron(ss, eye)], axis=0) * scale

    def spread(m):
        out = np.zeros((r, s, s, r))
        for i in range(s):
            out[:, i, i, :] = m
        return out.reshape(r * s, s * r)

    k2 = np.concatenate([spread(cs), spread(ss)], axis=1)
    ka = j[None, :, None]
    t0 = (np.arange(r // s)[:, None, None] * s + np.arange(s)[None, None, :])
    ang_t = (2.0 * np.pi * ka * t0 / seq_len).reshape(-1, 1)
    tc = jnp.asarray(np.repeat(np.cos(ang_t), V7X_LANES, axis=1), F32)
    ts = jnp.asarray(np.repeat(np.sin(ang_t), V7X_LANES, axis=1), F32)
    return jnp.asarray(k1, F32).astype(BF16), jnp.asarray(k2, F32).astype(BF16), tc, ts


def _rope_tables(seq_len):
    t = np.arange(seq_len)
    row = (t // GRID_W).astype(np.float64)
    col = (t % GRID_W).astype(np.float64)
    inv = np.power(ROPE_BASE, -np.arange(ROPE_PER_AXIS, dtype=np.float64) / ROPE_PER_AXIS)
    ang = np.concatenate([row[:, None] * inv, col[:, None] * inv], axis=-1)
    cos = np.cos(ang)
    sin = np.sin(ang)
    cos_h = np.concatenate([cos, cos], axis=-1)
    sin_h = np.concatenate([-sin, sin], axis=-1)
    return (jnp.asarray(np.tile(cos_h, (1, NA_HEADS)), F32), jnp.asarray(np.tile(sin_h, (1, NA_HEADS)), F32))


def _bias_table(rpb_l):
    col = np.arange(GRID_W)
    col_start = np.clip(col - NA_WIN_C // 2, 0, GRID_W - NA_WIN_C)
    col_mask = (col[None, :] >= col_start[:, None]) & (col[None, :] < col_start[:, None] + NA_WIN_C)
    dc = np.clip(col[None, :] - col[:, None] + (NA_WIN_C - 1), 0, 2 * NA_WIN_C - 2)
    n_dc = 2 * NA_WIN_C - 1
    pick = (dc.reshape(-1)[None, :] == np.arange(n_dc)[:, None]).astype(np.float32)
    e = jnp.dot(rpb_l.reshape(-1, n_dc), jnp.asarray(pick), precision=HIGHEST)
    e = e.reshape(NA_HEADS, 2 * NA_WIN_R - 1, GRID_W, GRID_W)
    e = jnp.where(jnp.asarray(col_mask)[None, None], e * LOG2E, NEG_BIG)
    b = jnp.stack([e[:, o:o + NA_WIN_R] for o in range(NA_WIN_R)], axis=0)
    b = b.transpose(0, 1, 3, 2, 4)
    return b.reshape(NA_WIN_R, NA_HEADS, GRID_W, NA_WIN_R * GRID_W)


def _moe(h2, h2_second, logits, w1, w3, w2, layer, router_b):
    n_tok = logits.shape[0]
    n_tiles = n_tok // ROUTE_TILE
    max_rows = 2 * n_tok + N_EXPERTS * n_tiles * (SLOT_GROUP - 1) + N_EXPERTS * (EXPERT_BLOCK - 1)
    n_blocks = -(-max_rows // EXPERT_BLOCK)
    n_slots = n_blocks * EXPERT_BLOCK
    ids, wts, cnt, tot = _route(logits, router_b)
    loc, gmap, blk = _slots(ids, cnt, tot, n_blocks)
    xb = _dispatch(gmap, blk, loc, h2, h2_second, n_slots)
    yb = _experts(blk[0, :n_blocks], blk[1, 0:1], xb, w1, w3, w2, layer)
    return yb, gmap, loc, wts


def kernel(x, c, ctx, c_ctx, ada_w, ada_b, norm1_g, w_in, qn_g, kn_g, rpb, conv_w, w_f, w_na, w_cv, w_o,
           norm2_g, router_w, router_b, w1, w3, w2):
    bsz, seq_len, d = x.shape
    ctx_len = ctx.shape[1]
    n_lat = bsz * seq_len
    n_ctx = bsz * ctx_len
    ctx_row = bsz

    c8 = jnp.concatenate([c, c_ctx[None, :], jnp.zeros((MODS_ROWS - bsz - 1, d), F32)], axis=0)
    mods = _mods(c8, ada_w, ada_b)

    cbd, sbd = _channel_dft_tables()
    fst, g_dft, tc3, ts3 = _two_stage_dft_tables(seq_len)
    c_ctx_t, s_ctx_t = _position_dft_tables(ctx_len)
    cos_t, sin_t = _rope_tables(seq_len)
    mavg = jnp.asarray(np.kron(np.eye(HEADS_PER_GROUP), np.full((HEAD_DIM, HEAD_DIM), 1.0 / HEAD_DIM)),
                       F32).astype(BF16)
    rw_hi = router_w.astype(BF16)
    rw_lo = (router_w - rw_hi.astype(F32)).astype(BF16)
    rwt = jnp.concatenate([rw_hi, rw_lo, jnp.zeros((d, V7X_LANES - 2 * N_EXPERTS), BF16)], axis=1)

    xl = x.reshape(n_lat, d)
    xc = ctx.reshape(n_ctx, d)
    for l in range(DEPTH):
        last = l == DEPTH - 1
        w_proj = w_in[l][:, :COL_G].astype(BF16)
        w_gate = w_in[l][:, COL_G:].astype(BF16)
        wf, wna, wcv, wo = (w_f[l].astype(BF16), w_na[l].astype(BF16), w_cv[l].astype(BF16), w_o[l].astype(BF16))
        n1 = norm1_g[l].reshape(1, d)
        n2 = norm2_g[l].reshape(1, d)
        qg = jnp.tile(qn_g[l], NA_HEADS).reshape(1, NA_WIDTH)
        kg = jnp.tile(kn_g[l], NA_HEADS).reshape(1, NA_WIDTH)
        bias_tab = _bias_table(rpb[l])
        mods_l = mods[l]

        a_c, b_c, q_c, k_c, v_c, u_c, bg_c = _proj(
            xc, mods_l, n1, w_proj, qg, kg, mavg, cbd, sbd, None, None,
            seq_len=ctx_len, n_seq=bsz, fixed_row=ctx_row, dft_dtype=BF16)
        a_l, b_l, q_l, k_l, v_l, u_l, bg_l = _proj(
            xl, mods_l, n1, w_proj, qg, kg, mavg, cbd, sbd, cos_t, sin_t,
            seq_len=seq_len, n_seq=bsz, fixed_row=None, dft_dtype=F32)

        f_l = _fourier_two_stage(a_l, b_l, fst, g_dft, tc3, ts3)
        attn_l = _attn(q_l, k_l, v_l, k_c, v_c, bias_tab, n_seq=bsz, seq_len=seq_len, ctx_len=ctx_len)
        xl_new, h2_l, lg_l = _merge(xl, mods_l, n1, n2, f_l, attn_l, u_l, bg_l, conv_w[l], w_gate,
                                    wf, wna, wcv, wo, rwt, seq_len=seq_len, fixed_row=None)
        if last:
            yb, gmap, loc, wts = _moe(h2_l, None, lg_l, w1, w3, w2, l, router_b)
            xl = _combine(gmap, loc, wts, xl_new, mods_l, yb, seq_len=seq_len, fixed_row=None, tile_offset=0)
        else:
            f_c = _fourier(c_ctx_t, s_ctx_t, a_c, b_c)
            attn_c = _ctx_attn(q_c, k_c, v_c, n_seq=bsz, ctx_len=ctx_len)
            xc_new, h2_c, lg_c = _merge(xc, mods_l, n1, n2, f_c, attn_c, u_c, bg_c, conv_w[l], w_gate,
                                        wf, wna, wcv, wo, rwt, seq_len=ctx_len, fixed_row=ctx_row)
            lg = jnp.concatenate([lg_l, lg_c], axis=0)
            yb, gmap, loc, wts = _moe(h2_l, h2_c, lg, w1, w3, w2, l, router_b)
            xl = _combine(gmap, loc, wts, xl_new, mods_l, yb, seq_len=seq_len, fixed_row=None, tile_offset=0)
            xc = _combine(gmap, loc, wts, xc_new, mods_l, yb, seq_len=ctx_len, fixed_row=ctx_row,
                          tile_offset=n_lat // ROUTE_TILE)
    return xl.reshape(bsz, seq_len, d)
```

```python
import functools
import math

import numpy as np
import jax
import jax.numpy as jnp
from jax import lax
from jax.experimental import pallas as pl
from jax.experimental.pallas import tpu as pltpu

F32 = jnp.float32
BF16 = jnp.bfloat16
I32 = jnp.int32
HIGHEST = lax.Precision.HIGHEST

D_MODEL = 1024
DEPTH = 2
GRID_W = 64
EPS = 1e-6
F_GROUPS = 4
F_GDIM = 64
F_WIDTH = 256
NA_HEADS = 8
HEAD_DIM = 64
NA_WIDTH = 512
NA_WIN_R = 8
NA_WIN_C = 16
ATTN_SCALE = HEAD_DIM ** -0.5
LOG2E = math.log2(math.e)
ROPE_BASE = 10000.0
ROPE_PER_AXIS = HEAD_DIM // 4
CONV_WIDTH = 256
COL_Q = 256
COL_K = 768
COL_V = 1280
COL_CX = 1792
COL_CB = 2048
COL_CC = 2304
COL_G = 2560
N_EXPERTS = 16
N_GROUPS = 4
EXPERTS_PER_GROUP = 4
D_EXPERT = 512

V7X_LANES = 128
V7X_SUBLANES = 8
V7X_MXU_DIM = 256

TOKEN_TILE = 1024
PROJ_TILE = 1024
PROJ_CHUNK_ROWS = 128
MERGE_CHUNK_ROWS = 512
ROUTE_TILE = 512
EXPERT_BLOCK = 1024
SLOT_GROUP = 2 * V7X_SUBLANES
SORT_ROWS = -(-(2 * ROUTE_TILE + N_EXPERTS * (SLOT_GROUP - 1)) // V7X_LANES) * V7X_LANES
SORT_GROUPS_PAD = -(-(SORT_ROWS // SLOT_GROUP) // V7X_LANES) * V7X_LANES
HEADS_PER_GROUP = V7X_MXU_DIM // HEAD_DIM
ATTN_ROWS_PER_STEP = 16
FFT_RADIX = 64
FFT_STEP = V7X_SUBLANES
NEG_BIG = -1e30
MODS_ROWS = 8
VMEM_LIMIT = 56 * 1024 * 1024


def _cparams(sem):
    return pltpu.CompilerParams(dimension_semantics=sem, vmem_limit_bytes=VMEM_LIMIT)


def _mods_kernel(c_ref, w_ref, b_ref, o_ref):
    c = c_ref[...]
    sc = c * jax.nn.sigmoid(c)
    w = w_ref[...]
    w_hi = w.astype(BF16)
    w_lo = (w - w_hi.astype(F32)).astype(BF16)
    s_hi = sc.astype(BF16)
    s_lo = (sc - s_hi.astype(F32)).astype(BF16)
    acc = jnp.dot(s_hi, w_hi, preferred_element_type=F32)
    acc = acc + jnp.dot(s_hi, w_lo, preferred_element_type=F32)
    acc = acc + jnp.dot(s_lo, w_hi, preferred_element_type=F32)
    o_ref[...] = acc + b_ref[...]


def _mods(c8, ada_w, ada_b):
    nb = 1536
    return pl.pallas_call(
        _mods_kernel,
        grid=(DEPTH, 6 * D_MODEL // nb),
        in_specs=[
            pl.BlockSpec((MODS_ROWS, D_MODEL), lambda l, j: (0, 0)),
            pl.BlockSpec((None, D_MODEL, nb), lambda l, j: (l, 0, j)),
            pl.BlockSpec((None, 1, nb), lambda l, j: (l, 0, j)),
        ],
        out_specs=pl.BlockSpec((None, MODS_ROWS, nb), lambda l, j: (l, 0, j)),
        out_shape=jax.ShapeDtypeStruct((DEPTH, MODS_ROWS, 6 * D_MODEL), F32),
        compiler_params=_cparams(("arbitrary", "arbitrary")),
        name="mods",
    )(c8, ada_w, ada_b.reshape(DEPTH, 1, 6 * D_MODEL))


def _norm_mod(x, g, shift, scale):
    ms = jnp.mean(x * x, axis=-1, keepdims=True)
    return (x * lax.rsqrt(ms + EPS) * g) * (1.0 + scale) + shift


def _mod_row(mods_ref, tile, tiles_per_seq, fixed_row):
    row = fixed_row if fixed_row is not None else tile // tiles_per_seq
    return mods_ref[pl.ds(row, 1), :]


def _proj_kernel(*refs, tiles_per_seq, fixed_row, rope):
    if rope:
        (x_ref, mods_ref, g_ref, w_ref, qg_ref, kg_ref, mavg_ref, cbd_ref, sbd_ref, cos_ref, sin_ref,
         a_ref, b_ref, q_ref, k_ref, v_ref, u_ref, bg_ref) = refs
    else:
        (x_ref, mods_ref, g_ref, w_ref, qg_ref, kg_ref, mavg_ref, cbd_ref, sbd_ref,
         a_ref, b_ref, q_ref, k_ref, v_ref, u_ref, bg_ref) = refs
    m = _mod_row(mods_ref, pl.program_id(0), tiles_per_seq, fixed_row)
    chunk = min(PROJ_CHUNK_ROWS, x_ref.shape[0])
    for c in range(x_ref.shape[0] // chunk):
        rows = slice(c * chunk, (c + 1) * chunk)
        _proj_rows(rows, m, rope, x_ref, g_ref, w_ref, qg_ref, kg_ref, mavg_ref, cbd_ref, sbd_ref,
                   cos_ref if rope else None, sin_ref if rope else None,
                   a_ref, b_ref, q_ref, k_ref, v_ref, u_ref, bg_ref)


def _proj_rows(rows, m, rope, x_ref, g_ref, w_ref, qg_ref, kg_ref, mavg_ref, cbd_ref, sbd_ref, cos_ref, sin_ref,
               a_ref, b_ref, q_ref, k_ref, v_ref, u_ref, bg_ref):
    h = _norm_mod(x_ref[rows, :], g_ref[...], m[:, 0:D_MODEL], m[:, D_MODEL:2 * D_MODEL])
    p = jnp.dot(h.astype(BF16), w_ref[...], preferred_element_type=F32)

    uf = p[:, 0:COL_Q].astype(BF16)
    a_ref[rows, :] = jnp.dot(uf, cbd_ref[...], preferred_element_type=F32).astype(a_ref.dtype)
    b_ref[rows, :] = jnp.dot(uf, sbd_ref[...], preferred_element_type=F32).astype(b_ref.dtype)

    def head_norm(t, g):
        sq = (t * t).astype(BF16)
        half = V7X_MXU_DIM
        ms = jnp.concatenate(
            [jnp.dot(sq[:, i:i + half], mavg_ref[...], preferred_element_type=F32)
             for i in range(0, t.shape[1], half)], axis=1)
        return t * lax.rsqrt(ms + EPS) * g

    def rotate(t):
        n = t.shape[-1]
        lane = lax.broadcasted_iota(I32, t.shape, 1)
        first_half = (lane % HEAD_DIM) < (HEAD_DIM // 2)
        swapped = jnp.where(first_half, pltpu.roll(t, n - HEAD_DIM // 2, 1), pltpu.roll(t, HEAD_DIM // 2, 1))
        return t * cos_ref[rows, :] + swapped * sin_ref[rows, :]

    q = head_norm(p[:, COL_Q:COL_K], qg_ref[...])
    k = head_norm(p[:, COL_K:COL_V], kg_ref[...])
    if rope:
        q = rotate(q)
        k = rotate(k)
    q_ref[rows, :] = (q * (ATTN_SCALE * LOG2E)).astype(BF16)
    k_ref[rows, :] = k.astype(BF16)
    v_ref[rows, :] = p[:, COL_V:COL_CX].astype(BF16)
    u_ref[rows, :] = p[:, COL_CC:COL_G] * p[:, COL_CX:COL_CB]
    bg_ref[rows, :] = p[:, COL_CB:COL_CC]


def _proj(x2, mods_l, norm_g, w_proj, qg, kg, mavg, cbd, sbd, cos_t, sin_t, *, seq_len, n_seq, fixed_row,
          dft_dtype):
    n_tok = x2.shape[0]
    tm = min(PROJ_TILE, seq_len)
    tps = seq_len // tm
    rope = cos_t is not None
    const = lambda i: (0, 0)
    in_specs = [
        pl.BlockSpec((tm, D_MODEL), lambda i: (i, 0)),
        pl.BlockSpec((MODS_ROWS, 6 * D_MODEL), const),
        pl.BlockSpec((1, D_MODEL), const),
        pl.BlockSpec((D_MODEL, COL_G), const),
        pl.BlockSpec((1, NA_WIDTH), const),
        pl.BlockSpec((1, NA_WIDTH), const),
        pl.BlockSpec((V7X_MXU_DIM, V7X_MXU_DIM), const),
        pl.BlockSpec((F_WIDTH, F_WIDTH), const),
        pl.BlockSpec((F_WIDTH, F_WIDTH), const),
    ]
    args = [x2, mods_l, norm_g, w_proj, qg, kg, mavg, cbd, sbd]
    if rope:
        in_specs += [pl.BlockSpec((tm, NA_WIDTH), lambda i: (i % tps, 0))] * 2
        args += [cos_t, sin_t]
    tok = lambda w: pl.BlockSpec((tm, w), lambda i: (i, 0))
    fmap = pl.BlockSpec((tm, F_WIDTH), lambda i: (i % tps, i // tps))
    out_specs = [fmap, fmap, tok(NA_WIDTH), tok(NA_WIDTH), tok(NA_WIDTH), tok(CONV_WIDTH), tok(CONV_WIDTH)]
    out_shape = [
        jax.ShapeDtypeStruct((seq_len, n_seq * F_WIDTH), dft_dtype),
        jax.ShapeDtypeStruct((seq_len, n_seq * F_WIDTH), dft_dtype),
        jax.ShapeDtypeStruct((n_tok, NA_WIDTH), BF16),
        jax.ShapeDtypeStruct((n_tok, NA_WIDTH), BF16),
        jax.ShapeDtypeStruct((n_tok, NA_WIDTH), BF16),
        jax.ShapeDtypeStruct((n_tok, CONV_WIDTH), F32),
        jax.ShapeDtypeStruct((n_tok, CONV_WIDTH), F32),
    ]
    return pl.pallas_call(
        functools.partial(_proj_kernel, tiles_per_seq=tps, fixed_row=fixed_row, rope=rope),
        grid=(n_tok // tm,),
        in_specs=in_specs,
        out_specs=out_specs,
        out_shape=out_shape,
        compiler_params=_cparams(("arbitrary",)),
        name="proj",
    )(*args)


def _fourier_kernel(c_ref, s_ref, a_ref, b_ref, o_ref):
    o = (jnp.dot(c_ref[...], a_ref[...], preferred_element_type=F32)
         - jnp.dot(s_ref[...], b_ref[...], preferred_element_type=F32))
    o_ref[...] = o.astype(BF16)


def _fourier(c_tab, s_tab, a, b):
    seq_len, width = a.shape
    tk = min(seq_len, 256)
    full = lambda i: (0, 0)
    return pl.pallas_call(
        _fourier_kernel,
        grid=(seq_len // tk,),
        in_specs=[
            pl.BlockSpec((tk, seq_len), lambda i: (i, 0)),
            pl.BlockSpec((tk, seq_len), lambda i: (i, 0)),
            pl.BlockSpec((seq_len, width), full, pipeline_mode=pl.Buffered(1)),
            pl.BlockSpec((seq_len, width), full, pipeline_mode=pl.Buffered(1)),
        ],
        out_specs=pl.BlockSpec((tk, width), lambda i: (i, 0)),
        out_shape=jax.ShapeDtypeStruct((seq_len, width), BF16),
        compiler_params=_cparams(("arbitrary",)),
        name="fourier",
    )(c_tab, s_tab, a, b)


def _fft1_kernel(a_ref, b_ref, k1_ref, tc_ref, ts_ref, zr_ref, zi_ref):
    n = FFT_RADIX * FFT_STEP
    width = a_ref.shape[2]
    k1 = k1_ref[...]
    r1 = jnp.dot(k1, a_ref[...].reshape(n, width).astype(BF16), preferred_element_type=F32)
    r2 = jnp.dot(k1, b_ref[...].reshape(n, width).astype(BF16), preferred_element_type=F32)
    yr = r1[0:n] - r2[n:2 * n]
    yi = -(r2[0:n] + r1[n:2 * n])
    tc = tc_ref[...][:, 0:1]
    ts = ts_ref[...][:, 0:1]
    zr_ref[...] = (yr * tc + yi * ts).reshape(zr_ref.shape)
    zi_ref[...] = (yi * tc - yr * ts).reshape(zi_ref.shape)


def _fft2_kernel(zr_ref, zi_ref, k2_ref, f_ref):
    n = FFT_RADIX * FFT_STEP
    width = zr_ref.shape[2]
    zz = jnp.concatenate([zr_ref[...].reshape(n, width), zi_ref[...].reshape(n, width)], axis=0).astype(BF16)
    f_ref[...] = jnp.dot(k2_ref[...], zz, preferred_element_type=F32).reshape(f_ref.shape)


def _fourier_two_stage(a, b, k1, k2, tc, ts):
    seq_len, width = a.shape
    r = FFT_RADIX
    n = r * FFT_STEP
    a3 = a.reshape(r, r, width)
    b3 = b.reshape(r, r, width)
    steps = r // FFT_STEP
    col_blk = pl.BlockSpec((r, FFT_STEP, width), lambda j: (0, j, 0))
    row_blk = pl.BlockSpec((FFT_STEP, r, width), lambda j: (j, 0, 0))
    tw_blk = pl.BlockSpec((n, V7X_LANES), lambda j: (j, 0))
    z_shape = jax.ShapeDtypeStruct((r, r, width), F32)
    zr, zi = pl.pallas_call(
        _fft1_kernel,
        grid=(steps,),
        in_specs=[col_blk, col_blk, pl.BlockSpec((2 * n, n), lambda j: (0, 0)), tw_blk, tw_blk],
        out_specs=[col_blk, col_blk],
        out_shape=[z_shape, z_shape],
        compiler_params=_cparams(("arbitrary",)),
        name="fft1",
    )(a3, b3, k1, tc, ts)
    f3 = pl.pallas_call(
        _fft2_kernel,
        grid=(steps,),
        in_specs=[row_blk, row_blk, pl.BlockSpec((n, 2 * n), lambda j: (0, 0))],
        out_specs=col_blk,
        out_shape=z_shape,
        compiler_params=_cparams(("arbitrary",)),
        name="fft2",
    )(zr, zi, k2)
    return f3.reshape(seq_len, width)


def _stack_heads(qg):
    lane_head = lax.broadcasted_iota(I32, qg.shape, 1) // HEAD_DIM
    zero = jnp.zeros_like(qg)
    return jnp.concatenate([jnp.where(lane_head == h, qg, zero) for h in range(HEADS_PER_GROUP)], axis=0)


def _unstack_heads(o, rows):
    lane_head = lax.broadcasted_iota(I32, (rows, o.shape[1]), 1) // HEAD_DIM
    acc = jnp.zeros((rows, o.shape[1]), F32)
    for h in range(HEADS_PER_GROUP):
        acc = acc + jnp.where(lane_head == h, o[h * rows:(h + 1) * rows, :], 0.0)
    return acc


_NT = (((1,), (1,)), ((), ()))


def _attn_kernel(q_ref, k_ref, v_ref, kc_ref, vc_ref, bias_tab_ref, o_ref, *, rows):
    n_loc = NA_WIN_R * GRID_W
    for j in range(ATTN_ROWS_PER_STEP):
        r = pl.program_id(1) * ATTN_ROWS_PER_STEP + j
        rs = jnp.clip(r - NA_WIN_R // 2, 0, rows - NA_WIN_R)
        start = pl.multiple_of(rs * GRID_W, GRID_W)
        kwin = k_ref[pl.ds(start, n_loc), :]
        vwin = v_ref[pl.ds(start, n_loc), :]
        q = q_ref[j * GRID_W:(j + 1) * GRID_W, :]
        bias_ref = bias_tab_ref.at[rs - r + (NA_WIN_R - 1)]
        outs = []
        for g in range(NA_HEADS // HEADS_PER_GROUP):
            sl = slice(g * V7X_MXU_DIM, (g + 1) * V7X_MXU_DIM)
            qs = _stack_heads(q[:, sl])
            s_loc = lax.dot_general(qs, kwin[:, sl], _NT, preferred_element_type=F32)
            bias = bias_ref[g * HEADS_PER_GROUP:(g + 1) * HEADS_PER_GROUP].reshape(HEADS_PER_GROUP * GRID_W, n_loc)
            s_ctx = lax.dot_general(qs, kc_ref[:, sl], _NT, preferred_element_type=F32)
            s = jnp.concatenate([s_loc + bias, s_ctx], axis=1)
            m = jnp.max(s, axis=-1, keepdims=True)
            p = jnp.exp2(s - m)
            denom = jnp.sum(p, axis=-1, keepdims=True)
            pb = p.astype(BF16)
            o = (jnp.dot(pb[:, :n_loc], vwin[:, sl], preferred_element_type=F32)
                 + jnp.dot(pb[:, n_loc:], vc_ref[:, sl], preferred_element_type=F32))
            outs.append(_unstack_heads(o / denom, GRID_W))
        o_ref[j * GRID_W:(j + 1) * GRID_W, :] = jnp.concatenate(outs, axis=1).astype(BF16)


def _attn(q, k, v, kc, vc, bias_tab, *, n_seq, seq_len, ctx_len):
    rows = seq_len // GRID_W
    rps = ATTN_ROWS_PER_STEP
    steps = rows // rps

    return pl.pallas_call(
        functools.partial(_attn_kernel, rows=rows),
        grid=(n_seq, steps),
        in_specs=[
            pl.BlockSpec((rps * GRID_W, NA_WIDTH), lambda b, s: (b * steps + s, 0)),
            pl.BlockSpec((seq_len, NA_WIDTH), lambda b, s: (b, 0)),
            pl.BlockSpec((seq_len, NA_WIDTH), lambda b, s: (b, 0)),
            pl.BlockSpec((ctx_len, NA_WIDTH), lambda b, s: (b, 0)),
            pl.BlockSpec((ctx_len, NA_WIDTH), lambda b, s: (b, 0)),
            pl.BlockSpec(bias_tab.shape, lambda b, s: (0, 0, 0, 0), pipeline_mode=pl.Buffered(1)),
        ],
        out_specs=pl.BlockSpec((rps * GRID_W, NA_WIDTH), lambda b, s: (b * steps + s, 0)),
        out_shape=jax.ShapeDtypeStruct((n_seq * seq_len, NA_WIDTH), BF16),
        compiler_params=_cparams(("arbitrary", "arbitrary")),
        name="attn",
    )(q, k, v, kc, vc, bias_tab)


def _ctx_attn_kernel(q_ref, k_ref, v_ref, o_ref):
    q = q_ref[...]
    n = q.shape[0]
    outs = []
    for g in range(NA_HEADS // HEADS_PER_GROUP):
        sl = slice(g * V7X_MXU_DIM, (g + 1) * V7X_MXU_DIM)
        qs = _stack_heads(q[:, sl])
        s = lax.dot_general(qs, k_ref[:, sl], _NT, preferred_element_type=F32)
        m = jnp.max(s, axis=-1, keepdims=True)
        p = jnp.exp2(s - m)
        denom = jnp.sum(p, axis=-1, keepdims=True)
        o = jnp.dot(p.astype(BF16), v_ref[:, sl], preferred_element_type=F32)
        outs.append(_unstack_heads(o / denom, n))
    o_ref[...] = jnp.concatenate(outs, axis=1).astype(BF16)


def _ctx_attn(q, k, v, *, n_seq, ctx_len):
    spec = pl.BlockSpec((ctx_len, NA_WIDTH), lambda b: (b, 0))
    return pl.pallas_call(
        _ctx_attn_kernel,
        grid=(n_seq,),
        in_specs=[spec, spec, spec],
        out_specs=spec,
        out_shape=jax.ShapeDtypeStruct((n_seq * ctx_len, NA_WIDTH), BF16),
        compiler_params=_cparams(("arbitrary",)),
        name="ctx_attn",
    )(q, k, v)


def _merge_kernel(x_ref, mods_ref, n1_ref, n2_ref, f_ref, at_ref, u_ref, up_ref, un_ref, bg_ref, cw_ref,
                  wg_ref, wf_ref, wna_ref, wcv_ref, wo_ref, rw_ref,
                  xo_ref, h2_ref, lg_ref, *, tiles_per_seq, fixed_row):
    i = pl.program_id(0)
    m = _mod_row(mods_ref, i, tiles_per_seq, fixed_row)
    dm = D_MODEL

    u = u_ref[...]
    t = u.shape[0]
    ti = i % tiles_per_seq
    row = lax.broadcasted_iota(I32, u.shape, 0)
    prev_row = jnp.where(ti == 0, 0.0, up_ref[V7X_SUBLANES - 1:V7X_SUBLANES, :])
    next_row = jnp.where(ti == tiles_per_seq - 1, 0.0, un_ref[0:1, :])
    u_prev = jnp.where(row == 0, prev_row, pltpu.roll(u, 1, 0))
    u_next = jnp.where(row == t - 1, next_row, pltpu.roll(u, t - 1, 0))
    conv = (bg_ref[...] * (cw_ref[0:1, :] * u_prev + cw_ref[1:2, :] * u + cw_ref[2:3, :] * u_next)).astype(BF16)

    chunk = min(MERGE_CHUNK_ROWS, t)
    for c in range(t // chunk):
        rows = slice(c * chunk, (c + 1) * chunk)
        x = x_ref[rows, :]
        h = _norm_mod(x, n1_ref[...], m[:, 0:dm], m[:, dm:2 * dm]).astype(BF16)
        gates = jax.nn.sigmoid(jnp.dot(h, wg_ref[...], preferred_element_type=F32))
        y_f = jnp.dot(f_ref[rows, :].astype(BF16), wf_ref[...], preferred_element_type=F32)
        y_na = jnp.dot(at_ref[rows, :], wna_ref[...], preferred_element_type=F32)
        y_cv = jnp.dot(conv[rows, :], wcv_ref[...], preferred_element_type=F32)
        merged = gates[:, 0:dm] * y_f + gates[:, dm:2 * dm] * y_na + gates[:, 2 * dm:3 * dm] * y_cv
        mixed = jnp.dot(merged.astype(BF16), wo_ref[...], preferred_element_type=F32)
        x_new = x + m[:, 2 * dm:3 * dm] * mixed
        xo_ref[rows, :] = x_new

        h2 = _norm_mod(x_new, n2_ref[...], m[:, 3 * dm:4 * dm], m[:, 4 * dm:5 * dm])
        h2_ref[rows, :] = h2.astype(BF16)
        hi = h2.astype(BF16)
        lo = (h2 - hi.astype(F32)).astype(BF16)
        p_hi = jnp.dot(hi, rw_ref[...], preferred_element_type=F32)
        p_lo = jnp.dot(lo, rw_ref[...], preferred_element_type=F32)
        lg_ref[rows, :] = p_hi + pltpu.roll(p_hi, V7X_LANES - N_EXPERTS, 1) + p_lo


def _merge(x2, mods_l, n1, n2, f_all, attn, u, bg, conv_w, w_gate, w_f, w_na, w_cv, w_o, rwt,
           *, seq_len, fixed_row):
    n_tok = x2.shape[0]
    tm = min(TOKEN_TILE, seq_len)
    tps = seq_len // tm
    const = lambda i: (0, 0)
    halo = tm // V7X_SUBLANES
    n_halo = n_tok // V7X_SUBLANES
    once = pl.Buffered(1)
    in_specs = [
        pl.BlockSpec((tm, D_MODEL), lambda i: (i, 0)),
        pl.BlockSpec((MODS_ROWS, 6 * D_MODEL), const),
        pl.BlockSpec((1, D_MODEL), const),
        pl.BlockSpec((1, D_MODEL), const),
        pl.BlockSpec((tm, F_WIDTH), lambda i: (i % tps, i // tps)),
        pl.BlockSpec((tm, NA_WIDTH), lambda i: (i, 0)),
        pl.BlockSpec((tm, CONV_WIDTH), lambda i: (i, 0)),
        pl.BlockSpec((V7X_SUBLANES, CONV_WIDTH), lambda i: (jnp.maximum(i * halo - 1, 0), 0)),
        pl.BlockSpec((V7X_SUBLANES, CONV_WIDTH), lambda i: (jnp.minimum((i + 1) * halo, n_halo - 1), 0)),
        pl.BlockSpec((tm, CONV_WIDTH), lambda i: (i, 0)),
        pl.BlockSpec((3, CONV_WIDTH), const),
        pl.BlockSpec((D_MODEL, 3 * D_MODEL), const, pipeline_mode=once),
        pl.BlockSpec((F_WIDTH, D_MODEL), const, pipeline_mode=once),
        pl.BlockSpec((NA_WIDTH, D_MODEL), const, pipeline_mode=once),
        pl.BlockSpec((CONV_WIDTH, D_MODEL), const, pipeline_mode=once),
        pl.BlockSpec((D_MODEL, D_MODEL), const, pipeline_mode=once),
        pl.BlockSpec((D_MODEL, V7X_LANES), const),
    ]
    out_specs = [
        pl.BlockSpec((tm, D_MODEL), lambda i: (i, 0)),
        pl.BlockSpec((tm, D_MODEL), lambda i: (i, 0)),
        pl.BlockSpec((tm, V7X_LANES), lambda i: (i, 0)),
    ]
    out_shape = [
        jax.ShapeDtypeStruct((n_tok, D_MODEL), F32),
        jax.ShapeDtypeStruct((n_tok, D_MODEL), BF16),
        jax.ShapeDtypeStruct((n_tok, V7X_LANES), F32),
    ]
    return pl.pallas_call(
        functools.partial(_merge_kernel, tiles_per_seq=tps, fixed_row=fixed_row),
        grid=(n_tok // tm,),
        in_specs=in_specs,
        out_specs=out_specs,
        out_shape=out_shape,
        compiler_params=_cparams(("arbitrary",)),
        name="merge",
    )(x2, mods_l, n1, n2, f_all, attn, u, u, u, bg, conv_w, w_gate, w_f, w_na, w_cv, w_o, rwt)


def _first_max(vals):
    best = vals[0]
    idx = jnp.zeros(best.shape, I32)
    for j in range(1, len(vals)):
        better = vals[j] > best
        idx = jnp.where(better, j, idx)
        best = jnp.where(better, vals[j], best)
    return best, idx


def _select(idx, vals):
    out = vals[-1]
    for j in range(len(vals) - 2, -1, -1):
        out = jnp.where(idx == j, vals[j], out)
    return out


def _route_kernel(lg_ref, rb_ref, ids_ref, wts_ref, cnt_ref, tot_ref, run_ref):
    step = pl.program_id(0)

    @pl.when(step == 0)
    def _():
        run_ref[...] = jnp.zeros_like(run_ref)

    s = jax.nn.sigmoid(lg_ref[...].T[0:N_EXPERTS, :])
    sb = s + rb_ref[...]
    t = s.shape[1]
    s_rows = [s[e:e + 1, :] for e in range(N_EXPERTS)]
    b_rows = [sb[e:e + 1, :] for e in range(N_EXPERTS)]
    epg = EXPERTS_PER_GROUP
    gscore = []
    for g in range(N_GROUPS):
        v = b_rows[g * epg:(g + 1) * epg]
        pair = None
        for a in range(epg):
            for b in range(a + 1, epg):
                pair = v[a] + v[b] if pair is None else jnp.maximum(pair, v[a] + v[b])
        gscore.append(pair)
    _, gi = _first_max(gscore)
    bv = [_select(gi, [b_rows[g * epg + j] for g in range(N_GROUPS)]) for j in range(epg)]
    sv = [_select(gi, [s_rows[g * epg + j] for g in range(N_GROUPS)]) for j in range(epg)]
    _, i1 = _first_max(bv)
    _, i2 = _first_max([jnp.where(i1 == j, -jnp.inf, bv[j]) for j in range(epg)])
    s1 = _select(i1, sv)
    s2 = _select(i2, sv)
    tot = s1 + s2
    e1 = gi * epg + i1
    e2 = gi * epg + i2

    eid = lax.broadcasted_iota(I32, (N_EXPERTS, t), 0)
    hit1 = eid == e1
    hit2 = eid == e2
    onehot = jnp.where(hit1 | hit2, 1.0, 0.0)
    before = (lax.broadcasted_iota(I32, (t, t), 0) < lax.broadcasted_iota(I32, (t, t), 1))
    prefix = jnp.dot(onehot.astype(BF16), jnp.where(before, 1.0, 0.0).astype(BF16),
                     preferred_element_type=F32)
    r1 = jnp.sum(jnp.where(hit1, prefix, 0.0), axis=0, keepdims=True)
    r2 = jnp.sum(jnp.where(hit2, prefix, 0.0), axis=0, keepdims=True)
    grp = float(SLOT_GROUP)
    cnt = jnp.sum(onehot, axis=1, keepdims=True)
    cnt = jnp.floor((cnt + (grp - 1.0)) / grp) * grp
    run = run_ref[...] + cnt
    run_ref[...] = run
    cnt_ref[...] = jnp.broadcast_to(cnt, cnt_ref.shape)
    tot_ref[...] = jnp.broadcast_to(run, tot_ref.shape)

    zi = jnp.zeros((V7X_SUBLANES - 4, t), I32)
    ids_ref[...] = jnp.concatenate([e1, e2, r1.astype(I32), r2.astype(I32), zi], axis=0)
    zf = jnp.zeros((V7X_SUBLANES - 2, t), F32)
    wts_ref[...] = jnp.concatenate([s1 / tot, s2 / tot, zf], axis=0)


def _route(logits, router_b):
    n_tok = logits.shape[0]
    tr = ROUTE_TILE
    return pl.pallas_call(
        _route_kernel,
        grid=(n_tok // tr,),
        in_specs=[
            pl.BlockSpec((tr, V7X_LANES), lambda i: (i, 0)),
            pl.BlockSpec((N_EXPERTS, 1), lambda i: (0, 0)),
        ],
        out_specs=[
            pl.BlockSpec((V7X_SUBLANES, tr), lambda i: (0, i)),
            pl.BlockSpec((V7X_SUBLANES, tr), lambda i: (0, i)),
            pl.BlockSpec((None, N_EXPERTS, V7X_LANES), lambda i: (i, 0, 0)),
            pl.BlockSpec((N_EXPERTS, V7X_LANES), lambda i: (0, 0)),
        ],
        out_shape=[
            jax.ShapeDtypeStruct((V7X_SUBLANES, n_tok), I32),
            jax.ShapeDtypeStruct((V7X_SUBLANES, n_tok), F32),
            jax.ShapeDtypeStruct((n_tok // tr, N_EXPERTS, V7X_LANES), F32),
            jax.ShapeDtypeStruct((N_EXPERTS, V7X_LANES), F32),
        ],
        scratch_shapes=[pltpu.VMEM((N_EXPERTS, 1), F32)],
        compiler_params=_cparams(("arbitrary",)),
        name="route",
    )(logits, router_b.reshape(N_EXPERTS, 1))


def _lane_table(vals, width):
    lane = lax.broadcasted_iota(I32, (1, width), 1)
    out = jnp.zeros((1, width), F32)
    for e, v in enumerate(vals):
        out = jnp.where(lane == e, v, out)
    return out


def _slots_kernel(ids_ref, cnt_ref, tot_ref, loc_ref, gmap_ref, blk_ref, off_ref):
    step = pl.program_id(0)

    @pl.when(step == 0)
    def _():
        off_ref[...] = jnp.zeros_like(off_ref)

    blk = float(EXPERT_BLOCK)
    grp = float(SLOT_GROUP)
    cnt = cnt_ref[...][:, 0:1]
    tot = tot_ref[...][:, 0:1]
    off = off_ref[...]
    region = jnp.floor((tot + (blk - 1.0)) / blk) * blk
    starts, ends, local = [], [], []
    run = jnp.zeros((1, 1), F32)
    lrun = jnp.zeros((1, 1), F32)
    for e in range(N_EXPERTS):
        starts.append(run)
        run = run + region[e:e + 1, :]
        ends.append(run)
        local.append(lrun)
        lrun = lrun + cnt[e:e + 1, :]

    ids = ids_ref[...]
    e1, e2 = ids[0:1, :], ids[1:2, :]
    t = ids.shape[1]
    l1 = jnp.zeros((1, t), F32)
    l2 = jnp.zeros((1, t), F32)
    for e in range(N_EXPERTS):
        l1 = jnp.where(e1 == e, local[e], l1)
        l2 = jnp.where(e2 == e, local[e], l2)
    zi = jnp.zeros((V7X_SUBLANES - 2, t), I32)
    loc_ref[...] = jnp.concatenate([l1.astype(I32) + ids[2:3, :], l2.astype(I32) + ids[3:4, :], zi], axis=0)

    wg = gmap_ref.shape[1]
    first = lax.broadcasted_iota(I32, (1, wg), 1).astype(F32) * grp
    dest = jnp.zeros((1, wg), F32)
    for e in range(N_EXPERTS):
        inside = (first >= local[e]) & (first < local[e] + cnt[e:e + 1, :])
        dest = jnp.where(inside, starts[e] + off[e:e + 1, :] + (first - local[e]), dest)
    n_groups = jnp.broadcast_to(lrun / grp, (1, wg))
    zg = jnp.zeros((V7X_SUBLANES - 2, wg), I32)
    gmap_ref[...] = jnp.concatenate([(dest / grp).astype(I32), n_groups.astype(I32), zg], axis=0)
    off_ref[...] = off + cnt

    w = blk_ref.shape[1]
    first_row = lax.broadcasted_iota(I32, (1, w), 1).astype(F32) * blk
    owner = jnp.zeros((1, w), F32)
    for e in range(N_EXPERTS):
        owner = owner + jnp.where(first_row >= ends[e], 1.0, 0.0)
    owner = jnp.minimum(owner, float(N_EXPERTS - 1))
    used = jnp.broadcast_to(ends[-1] / blk, (1, w))
    pad_first = _lane_table([(starts[e] + tot[e:e + 1, :]) / grp for e in range(N_EXPERTS)], w)
    pad_count = _lane_table([(region[e:e + 1, :] - tot[e:e + 1, :]) / grp for e in range(N_EXPERTS)], w)
    zb = jnp.zeros((V7X_SUBLANES - 4, w), I32)
    blk_ref[...] = jnp.concatenate([owner.astype(I32), used.astype(I32), pad_first.astype(I32),
                                    pad_count.astype(I32), zb], axis=0)


def _slots(ids, cnt, tot, n_blocks):
    n_tok = ids.shape[1]
    tr = ROUTE_TILE
    wblk = -(-n_blocks // V7X_LANES) * V7X_LANES
    return pl.pallas_call(
        _slots_kernel,
        grid=(n_tok // tr,),
        in_specs=[
            pl.BlockSpec((V7X_SUBLANES, tr), lambda i: (0, i)),
            pl.BlockSpec((None, N_EXPERTS, V7X_LANES), lambda i: (i, 0, 0)),
            pl.BlockSpec((N_EXPERTS, V7X_LANES), lambda i: (0, 0)),
        ],
        out_specs=[
            pl.BlockSpec((V7X_SUBLANES, tr), lambda i: (0, i)),
            pl.BlockSpec((None, V7X_SUBLANES, SORT_GROUPS_PAD), lambda i: (i, 0, 0)),
            pl.BlockSpec((V7X_SUBLANES, wblk), lambda i: (0, 0)),
        ],
        out_shape=[
            jax.ShapeDtypeStruct((V7X_SUBLANES, n_tok), I32),
            jax.ShapeDtypeStruct((n_tok // tr, V7X_SUBLANES, SORT_GROUPS_PAD), I32),
            jax.ShapeDtypeStruct((V7X_SUBLANES, wblk), I32),
        ],
        scratch_shapes=[pltpu.VMEM((N_EXPERTS, 1), F32)],
        compiler_params=_cparams(("arbitrary",)),
        name="slots",
    )(ids, cnt, tot)


def _group_rows(group):
    if isinstance(group, int):
        return pl.ds(group * SLOT_GROUP, SLOT_GROUP)
    return pl.ds(pl.multiple_of(group * SLOT_GROUP, SLOT_GROUP), SLOT_GROUP)


def _for_each_group(n, body):
    unroll = 4

    def chunk(q, c):
        for u in range(unroll):
            body(q * unroll + u)
        return c

    def single(g, c):
        body(g)
        return c

    whole = n // unroll
    lax.fori_loop(0, whole, chunk, 0)
    lax.fori_loop(whole * unroll, n, single, 0)


def _group_copy(src_ref, src_group, dst_ref, dst_group, sem):
    return pltpu.make_async_copy(src_ref.at[_group_rows(src_group)], dst_ref.at[_group_rows(dst_group)], sem)


def _dispatch_kernel(gmap_ref, gprev_ref, blk_ref, loc_ref, *refs, n_first):
    if n_first is None:
        h_ref, xb_ref, sorted_ref, zero_ref, sem = refs
        second_ref = None
    else:
        h_ref, second_ref, xb_ref, sorted_ref, zero_ref, sem = refs
    step = pl.program_id(0)
    last = pl.num_programs(0) - 1
    buf = step % 2
    loc = loc_ref[...]
    slot = lax.broadcasted_iota(I32, (SORT_ROWS, loc.shape[1]), 0)
    perm = jnp.where(slot == loc[0:1, :], 1.0, jnp.where(slot == loc[1:2, :], 1.0, 0.0)).astype(BF16)

    def sort_rows(src_ref):
        sorted_ref[buf] = jnp.dot(perm, src_ref[...].astype(BF16), preferred_element_type=F32).astype(BF16)

    if second_ref is None:
        sort_rows(h_ref)
    else:
        pl.when(step < n_first)(lambda: sort_rows(h_ref))
        pl.when(step >= n_first)(lambda: sort_rows(second_ref))

    def tile_copy(map_ref, which, g):
        return _group_copy(sorted_ref.at[which], g, xb_ref, map_ref[0, g], sem.at[which])

    def start(g):
        tile_copy(gmap_ref, buf, g).start()

    def wait_tile(map_ref, which):
        rows = pl.ds(0, map_ref[1, 0] * SLOT_GROUP)
        pltpu.make_async_copy(sorted_ref.at[which, rows], xb_ref.at[rows], sem.at[which]).wait()

    _for_each_group(gmap_ref[1, 0], start)

    @pl.when(step == last)
    def _():
        zero_ref[...] = jnp.zeros_like(zero_ref)
        for e in range(N_EXPERTS):
            first = blk_ref[2, e]

            def zstart(g, c, first=first):
                _group_copy(zero_ref, 0, xb_ref, first + g, sem.at[2]).start()
                return c

            def zwait(g, c, first=first):
                _group_copy(zero_ref, 0, xb_ref, first + g, sem.at[2]).wait()
                return c

            lax.fori_loop(0, blk_ref[3, e], zstart, 0)
            lax.fori_loop(0, blk_ref[3, e], zwait, 0)

        def block_copy(b):
            rows = pl.ds(pl.multiple_of(b * EXPERT_BLOCK, EXPERT_BLOCK), EXPERT_BLOCK)
            return pltpu.make_async_copy(zero_ref, xb_ref.at[rows], sem.at[2])

        def bstart(b, c):
            block_copy(b).start()
            return c

        def bwait(b, c):
            block_copy(b).wait()
            return c

        n_blocks = xb_ref.shape[0] // EXPERT_BLOCK
        lax.fori_loop(blk_ref[1, 0], n_blocks, bstart, 0)
        lax.fori_loop(blk_ref[1, 0], n_blocks, bwait, 0)

    @pl.when((step > 0) & (gprev_ref[1, 0] > 0))
    def _():
        wait_tile(gprev_ref, 1 - buf)

    @pl.when((step == last) & (gmap_ref[1, 0] > 0))
    def _():
        wait_tile(gmap_ref, buf)


def _dispatch(gmap, blk, loc, h2, h2_second, n_slots):
    tr = ROUTE_TILE
    n_first = h2.shape[0] // tr
    n_tiles = n_first
    in_specs = [
        pl.BlockSpec((None, V7X_SUBLANES, SORT_GROUPS_PAD), lambda i: (i, 0, 0), memory_space=pltpu.SMEM),
        pl.BlockSpec((None, V7X_SUBLANES, SORT_GROUPS_PAD), lambda i: (jnp.maximum(i - 1, 0), 0, 0),
                     memory_space=pltpu.SMEM),
        pl.BlockSpec(blk.shape, lambda i: (0, 0), memory_space=pltpu.SMEM),
        pl.BlockSpec((V7X_SUBLANES, tr), lambda i: (0, i)),
        pl.BlockSpec((tr, D_MODEL), lambda i: (jnp.minimum(i, n_first - 1), 0)),
    ]
    args = [gmap, gmap, blk, loc, h2]
    if h2_second is not None:
        n_tiles += h2_second.shape[0] // tr
        in_specs.append(pl.BlockSpec((tr, D_MODEL), lambda i: (jnp.maximum(i - n_first, 0), 0)))
        args.append(h2_second)
    return pl.pallas_call(
        functools.partial(_dispatch_kernel, n_first=None if h2_second is None else n_first),
        grid=(n_tiles,),
        in_specs=in_specs,
        out_specs=pl.BlockSpec(memory_space=pl.ANY),
        out_shape=jax.ShapeDtypeStruct((n_slots, D_MODEL), BF16),
        scratch_shapes=[pltpu.VMEM((2, SORT_ROWS, D_MODEL), BF16), pltpu.VMEM((EXPERT_BLOCK, D_MODEL), BF16),
                        pltpu.SemaphoreType.DMA((3,))],
        compiler_params=_cparams(("arbitrary",)),
        name="dispatch",
    )(*args)


def _experts_kernel(blk_ref, used_ref, x_ref, w1_ref, w3_ref, w2_ref, y_ref, w1b, w3b, w2b):
    i = pl.program_id(0)
    prev = blk_ref[jnp.maximum(i - 1, 0)]

    @pl.when((i == 0) | (blk_ref[i] != prev))
    def _():
        w1b[...] = w1_ref[...].astype(BF16)
        w3b[...] = w3_ref[...].astype(BF16)
        w2b[...] = w2_ref[...].astype(BF16)

    @pl.when(i < used_ref[0])
    def _():
        x = x_ref[...]
        a = jnp.dot(x, w1b[...], preferred_element_type=F32)
        b = jnp.dot(x, w3b[...], preferred_element_type=F32)
        hid = (a * jax.nn.sigmoid(a) * b).astype(BF16)
        y_ref[...] = jnp.dot(hid, w2b[...], preferred_element_type=F32).astype(BF16)

    @pl.when(i >= used_ref[0])
    def _():
        y_ref[...] = jnp.zeros_like(y_ref)


def _experts(blk_e, used, xb, w1, w3, w2, layer):
    n_slots = xb.shape[0]
    bm = EXPERT_BLOCK
    row_map = lambda i, be, nu: (jnp.minimum(i, nu[0] - 1), 0)
    w_map = lambda i, be, nu: (layer, be[i], 0, 0)
    grid_spec = pltpu.PrefetchScalarGridSpec(
        num_scalar_prefetch=2,
        grid=(n_slots // bm,),
        in_specs=[
            pl.BlockSpec((bm, D_MODEL), row_map),
            pl.BlockSpec((None, None, D_MODEL, D_EXPERT), w_map),
            pl.BlockSpec((None, None, D_MODEL, D_EXPERT), w_map),
            pl.BlockSpec((None, None, D_EXPERT, D_MODEL), w_map),
        ],
        out_specs=pl.BlockSpec((bm, D_MODEL), lambda i, be, nu: (i, 0)),
        scratch_shapes=[pltpu.VMEM((D_MODEL, D_EXPERT), BF16), pltpu.VMEM((D_MODEL, D_EXPERT), BF16),
                        pltpu.VMEM((D_EXPERT, D_MODEL), BF16)],
    )
    return pl.pallas_call(
        _experts_kernel,
        grid_spec=grid_spec,
        out_shape=jax.ShapeDtypeStruct((n_slots, D_MODEL), BF16),
        compiler_params=_cparams(("arbitrary",)),
        name="experts",
    )(blk_e, used, xb, w1, w3, w2)


_TN = (((0,), (0,)), ((), ()))


def _combine_kernel(gmap_ref, gnext_ref, loc_ref, wts_ref, x_ref, mods_ref, yb_ref, o_ref, ys_ref, sem,
                    *, tiles_per_seq, fixed_row):
    step = pl.program_id(0)
    buf = step % 2

    def fetch(map_ref, which, g):
        return _group_copy(yb_ref, map_ref[0, g], ys_ref.at[which], g, sem.at[which])

    def start_own(g):
        fetch(gmap_ref, buf, g).start()

    def start_next(g):
        fetch(gnext_ref, 1 - buf, g).start()

    def wait_own():
        rows = pl.ds(0, gmap_ref[1, 0] * SLOT_GROUP)
        pltpu.make_async_copy(yb_ref.at[rows], ys_ref.at[buf, rows], sem.at[buf]).wait()

    @pl.when(step == 0)
    def _():
        ys_ref[...] = jnp.zeros_like(ys_ref)
        _for_each_group(gmap_ref[1, 0], start_own)

    @pl.when(step + 1 < pl.num_programs(0))
    def _():
        _for_each_group(gnext_ref[1, 0], start_next)

    loc = loc_ref[...]
    wts = wts_ref[...]
    slot = lax.broadcasted_iota(I32, (SORT_ROWS, loc.shape[1]), 0)
    perm = jnp.where(slot == loc[0:1, :], wts[0:1, :], jnp.where(slot == loc[1:2, :], wts[1:2, :], 0.0))
    pl.when(gmap_ref[1, 0] > 0)(wait_own)
    y = lax.dot_general(perm.astype(BF16), ys_ref[buf], _TN, preferred_element_type=F32)
    m = _mod_row(mods_ref, step, tiles_per_seq, fixed_row)
    o_ref[...] = x_ref[...] + m[:, 5 * D_MODEL:6 * D_MODEL] * y


def _combine(gmap, loc, wts, x_new, mods_l, yb, *, seq_len, fixed_row, tile_offset):
    n_tok = x_new.shape[0]
    tr = ROUTE_TILE
    tps = max(seq_len // tr, 1)
    n_tiles = n_tok // tr
    return pl.pallas_call(
        functools.partial(_combine_kernel, tiles_per_seq=tps, fixed_row=fixed_row),
        grid=(n_tiles,),
        in_specs=[
            pl.BlockSpec((None, V7X_SUBLANES, SORT_GROUPS_PAD), lambda i: (i + tile_offset, 0, 0),
                         memory_space=pltpu.SMEM),
            pl.BlockSpec((None, V7X_SUBLANES, SORT_GROUPS_PAD),
                         lambda i: (jnp.minimum(i + 1, n_tiles - 1) + tile_offset, 0, 0), memory_space=pltpu.SMEM),
            pl.BlockSpec((V7X_SUBLANES, tr), lambda i: (0, i + tile_offset)),
            pl.BlockSpec((V7X_SUBLANES, tr), lambda i: (0, i + tile_offset)),
            pl.BlockSpec((tr, D_MODEL), lambda i: (i, 0)),
            pl.BlockSpec((MODS_ROWS, 6 * D_MODEL), lambda i: (0, 0)),
            pl.BlockSpec(memory_space=pl.ANY),
        ],
        out_specs=pl.BlockSpec((tr, D_MODEL), lambda i: (i, 0)),
        out_shape=jax.ShapeDtypeStruct((n_tok, D_MODEL), F32),
        scratch_shapes=[pltpu.VMEM((2, SORT_ROWS, D_MODEL), BF16), pltpu.SemaphoreType.DMA((2,))],
        compiler_params=_cparams(("arbitrary",)),
        name="combine",
    )(gmap, gmap, loc, wts, x_new, mods_l, yb)


def _channel_dft_tables():
    j = np.arange(F_GDIM)
    ang = 2.0 * np.pi * ((j[:, None] * j[None, :]) % F_GDIM) / F_GDIM
    eye = np.eye(F_GROUPS)
    return (jnp.asarray(np.kron(eye, np.cos(ang)), F32).astype(BF16),
            jnp.asarray(np.kron(eye, np.sin(ang)), F32).astype(BF16))


def _position_dft_tables(seq_len):
    scale = 1.0 / math.sqrt(seq_len * F_GDIM)
    k = np.arange(seq_len, dtype=np.int64)
    ang = 2.0 * np.pi * ((k[:, None] * k[None, :]) % seq_len) / seq_len
    return (jnp.asarray(np.cos(ang) * scale, F32).astype(BF16),
            jnp.asarray(np.sin(ang) * scale, F32).astype(BF16))


def _two_stage_dft_tables(seq_len):
    r = FFT_RADIX
    assert seq_len == r * r
    scale = 1.0 / math.sqrt(seq_len * F_GDIM)
    j = np.arange(r, dtype=np.int64)
    ang_r = 2.0 * np.pi * ((j[:, None] * j[None, :]) % r) / r
    cs, ss = np.cos(ang_r), np.sin(ang_r)
    s = FFT_STEP
    eye = np.eye(s)
    k1 = np.concatenate([np.kron(cs, eye), np.kron(ss, eye)], axis=0) * scale

    def spread(m):
        out = np.zeros((r, s, s, r))
        for i in range(s):
            out[:, i, i, :] = m
        return out.reshape(r * s, s * r)

    k2 = np.concatenate([spread(cs), spread(ss)], axis=1)
    ka = j[None, :, None]
    t0 = (np.arange(r // s)[:, None, None] * s + np.arange(s)[None, None, :])
    ang_t = (2.0 * np.pi * ka * t0 / seq_len).reshape(-1, 1)
    tc = jnp.asarray(np.repeat(np.cos(ang_t), V7X_LANES, axis=1), F32)
    ts = jnp.asarray(np.repeat(np.sin(ang_t), V7X_LANES, axis=1), F32)
    return jnp.asarray(k1, F32).astype(BF16), jnp.asarray(k2, F32).astype(BF16), tc, ts


def _rope_tables(seq_len):
    t = np.arange(seq_len)
    row = (t // GRID_W).astype(np.float64)
    col = (t % GRID_W).astype(np.float64)
    inv = np.power(ROPE_BASE, -np.arange(ROPE_PER_AXIS, dtype=np.float64) / ROPE_PER_AXIS)
    ang = np.concatenate([row[:, None] * inv, col[:, None] * inv], axis=-1)
    cos = np.cos(ang)
    sin = np.sin(ang)
    cos_h = np.concatenate([cos, cos], axis=-1)
    sin_h = np.concatenate([-sin, sin], axis=-1)
    return (jnp.asarray(np.tile(cos_h, (1, NA_HEADS)), F32), jnp.asarray(np.tile(sin_h, (1, NA_HEADS)), F32))


def _bias_table(rpb_l):
    col = np.arange(GRID_W)
    col_start = np.clip(col - NA_WIN_C // 2, 0, GRID_W - NA_WIN_C)
    col_mask = (col[None, :] >= col_start[:, None]) & (col[None, :] < col_start[:, None] + NA_WIN_C)
    dc = np.clip(col[None, :] - col[:, None] + (NA_WIN_C - 1), 0, 2 * NA_WIN_C - 2)
    n_dc = 2 * NA_WIN_C - 1
    pick = (dc.reshape(-1)[None, :] == np.arange(n_dc)[:, None]).astype(np.float32)
    e = jnp.dot(rpb_l.reshape(-1, n_dc), jnp.asarray(pick), precision=HIGHEST)
    e = e.reshape(NA_HEADS, 2 * NA_WIN_R - 1, GRID_W, GRID_W)
    e = jnp.where(jnp.asarray(col_mask)[None, None], e * LOG2E, NEG_BIG)
    e = e.transpose(0, 2, 1, 3)
    b = jnp.stack([e[:, :, o:o + NA_WIN_R] for o in range(NA_WIN_R)], axis=0)
    return b.reshape(NA_WIN_R, NA_HEADS, GRID_W, NA_WIN_R * GRID_W)


def _moe(h2, h2_second, logits, w1, w3, w2, layer, router_b):
    n_tok = logits.shape[0]
    n_tiles = n_tok // ROUTE_TILE
    max_rows = 2 * n_tok + N_EXPERTS * n_tiles * (SLOT_GROUP - 1) + N_EXPERTS * (EXPERT_BLOCK - 1)
    n_blocks = -(-max_rows // EXPERT_BLOCK)
    n_slots = n_blocks * EXPERT_BLOCK
    ids, wts, cnt, tot = _route(logits, router_b)
    loc, gmap, blk = _slots(ids, cnt, tot, n_blocks)
    xb = _dispatch(gmap, blk, loc, h2, h2_second, n_slots)
    yb = _experts(blk[0, :n_blocks], blk[1, 0:1], xb, w1, w3, w2, layer)
    return yb, gmap, loc, wts


def kernel(x, c, ctx, c_ctx, ada_w, ada_b, norm1_g, w_in, qn_g, kn_g, rpb, conv_w, w_f, w_na, w_cv, w_o,
           norm2_g, router_w, router_b, w1, w3, w2):
    bsz, seq_len, d = x.shape
    ctx_len = ctx.shape[1]
    n_lat = bsz * seq_len
    n_ctx = bsz * ctx_len
    ctx_row = bsz

    c8 = jnp.concatenate([c, c_ctx[None, :], jnp.zeros((MODS_ROWS - bsz - 1, d), F32)], axis=0)
    mods = _mods(c8, ada_w, ada_b)

    cbd, sbd = _channel_dft_tables()
    fst, g_dft, tc3, ts3 = _two_stage_dft_tables(seq_len)
    c_ctx_t, s_ctx_t = _position_dft_tables(ctx_len)
    cos_t, sin_t = _rope_tables(seq_len)
    mavg = jnp.asarray(np.kron(np.eye(HEADS_PER_GROUP), np.full((HEAD_DIM, HEAD_DIM), 1.0 / HEAD_DIM)),
                       F32).astype(BF16)
    rw_hi = router_w.astype(BF16)
    rw_lo = (router_w - rw_hi.astype(F32)).astype(BF16)
    rwt = jnp.concatenate([rw_hi, rw_lo, jnp.zeros((d, V7X_LANES - 2 * N_EXPERTS), BF16)], axis=1)

    xl = x.reshape(n_lat, d)
    xc = ctx.reshape(n_ctx, d)
    for l in range(DEPTH):
        last = l == DEPTH - 1
        w_proj = w_in[l][:, :COL_G].astype(BF16)
        w_gate = w_in[l][:, COL_G:].astype(BF16)
        wf, wna, wcv, wo = (w_f[l].astype(BF16), w_na[l].astype(BF16), w_cv[l].astype(BF16), w_o[l].astype(BF16))
        n1 = norm1_g[l].reshape(1, d)
        n2 = norm2_g[l].reshape(1, d)
        qg = jnp.tile(qn_g[l], NA_HEADS).reshape(1, NA_WIDTH)
        kg = jnp.tile(kn_g[l], NA_HEADS).reshape(1, NA_WIDTH)
        bias_tab = _bias_table(rpb[l])
        mods_l = mods[l]

        a_c, b_c, q_c, k_c, v_c, u_c, bg_c = _proj(
            xc, mods_l, n1, w_proj, qg, kg, mavg, cbd, sbd, None, None,
            seq_len=ctx_len, n_seq=bsz, fixed_row=ctx_row, dft_dtype=BF16)
        a_l, b_l, q_l, k_l, v_l, u_l, bg_l = _proj(
            xl, mods_l, n1, w_proj, qg, kg, mavg, cbd, sbd, cos_t, sin_t,
            seq_len=seq_len, n_seq=bsz, fixed_row=None, dft_dtype=F32)

        f_l = _fourier_two_stage(a_l, b_l, fst, g_dft, tc3, ts3)
        attn_l = _attn(q_l, k_l, v_l, k_c, v_c, bias_tab, n_seq=bsz, seq_len=seq_len, ctx_len=ctx_len)
        xl_new, h2_l, lg_l = _merge(xl, mods_l, n1, n2, f_l, attn_l, u_l, bg_l, conv_w[l], w_gate,
                                    wf, wna, wcv, wo, rwt, seq_len=seq_len, fixed_row=None)
        if last:
            yb, gmap, loc, wts = _moe(h2_l, None, lg_l, w1, w3, w2, l, router_b)
            xl = _combine(gmap, loc, wts, xl_new, mods_l, yb, seq_len=seq_len, fixed_row=None, tile_offset=0)
        else:
            f_c = _fourier(c_ctx_t, s_ctx_t, a_c, b_c)
            attn_c = _ctx_attn(q_c, k_c, v_c, n_seq=bsz, ctx_len=ctx_len)
            xc_new, h2_c, lg_c = _merge(xc, mods_l, n1, n2, f_c, attn_c, u_c, bg_c, conv_w[l], w_gate,
                                        wf, wna, wcv, wo, rwt, seq_len=ctx_len, fixed_row=ctx_row)
            lg = jnp.concatenate([lg_l, lg_c], axis=0)
            yb, gmap, loc, wts = _moe(h2_l, h2_c, lg, w1, w3, w2, l, router_b)
            xl = _combine(gmap, loc, wts, xl_new, mods_l, yb, seq_len=seq_len, fixed_row=None, tile_offset=0)
            xc = _combine(gmap, loc, wts, xc_new, mods_l, yb, seq_len=ctx_len, fixed_row=ctx_row,
                          tile_offset=n_lat // ROUTE_TILE)
    return xl.reshape(bsz, seq_len, d)
```

```python
import functools
import math

import numpy as np
import jax
import jax.numpy as jnp
from jax import lax
from jax.experimental import pallas as pl
from jax.experimental.pallas import tpu as pltpu

F32 = jnp.float32
BF16 = jnp.bfloat16
I32 = jnp.int32
HIGHEST = lax.Precision.HIGHEST

D_MODEL = 1024
DEPTH = 2
GRID_W = 64
EPS = 1e-6
F_GROUPS = 4
F_GDIM = 64
F_WIDTH = 256
NA_HEADS = 8
HEAD_DIM = 64
NA_WIDTH = 512
NA_WIN_R = 8
NA_WIN_C = 16
ATTN_SCALE = HEAD_DIM ** -0.5
LOG2E = math.log2(math.e)
ROPE_BASE = 10000.0
ROPE_PER_AXIS = HEAD_DIM // 4
CONV_WIDTH = 256
COL_Q = 256
COL_K = 768
COL_V = 1280
COL_CX = 1792
COL_CB = 2048
COL_CC = 2304
COL_G = 2560
N_EXPERTS = 16
N_GROUPS = 4
EXPERTS_PER_GROUP = 4
D_EXPERT = 512

V7X_LANES = 128
V7X_SUBLANES = 8
V7X_MXU_DIM = 256

TOKEN_TILE = 1024
PROJ_TILE = 1024
PROJ_CHUNK_ROWS = 128
MERGE_CHUNK_ROWS = 512
ROUTE_TILE = 512
EXPERT_BLOCK = 1024
SLOT_GROUP = 2 * V7X_SUBLANES
SORT_ROWS = -(-(2 * ROUTE_TILE + N_EXPERTS * (SLOT_GROUP - 1)) // V7X_LANES) * V7X_LANES
SORT_GROUPS_PAD = -(-(SORT_ROWS // SLOT_GROUP) // V7X_LANES) * V7X_LANES
HEADS_PER_GROUP = V7X_MXU_DIM // HEAD_DIM
ATTN_ROWS_PER_STEP = 16
FFT_RADIX = 64
FFT_STEP = V7X_SUBLANES
NEG_BIG = -1e30
MODS_ROWS = 8
VMEM_LIMIT = 56 * 1024 * 1024


VMEM_BUDGET_MIB = {
    "mods": 32, "proj": 56, "fourier": 16, "fft1": 40, "fft2": 32, "attn": 48, "ctx_attn": 16, "merge": 56,
    "route": 8, "slots": 8, "dispatch": 32, "experts": 48, "combine": 32,
}
assert max(VMEM_BUDGET_MIB.values()) * 1024 * 1024 <= VMEM_LIMIT


def _cparams(sem, name):
    return pltpu.CompilerParams(dimension_semantics=sem, vmem_limit_bytes=VMEM_BUDGET_MIB[name] * 1024 * 1024)


def _mods_kernel(c_ref, w_ref, b_ref, o_ref):
    c = c_ref[...]
    sc = c * jax.nn.sigmoid(c)
    w = w_ref[...]
    w_hi = w.astype(BF16)
    w_lo = (w - w_hi.astype(F32)).astype(BF16)
    s_hi = sc.astype(BF16)
    s_lo = (sc - s_hi.astype(F32)).astype(BF16)
    acc = jnp.dot(s_hi, w_hi, preferred_element_type=F32)
    acc = acc + jnp.dot(s_hi, w_lo, preferred_element_type=F32)
    acc = acc + jnp.dot(s_lo, w_hi, preferred_element_type=F32)
    o_ref[...] = acc + b_ref[...]


def _mods(c8, ada_w, ada_b):
    nb = 1536
    return pl.pallas_call(
        _mods_kernel,
        grid=(DEPTH, 6 * D_MODEL // nb),
        in_specs=[
            pl.BlockSpec((MODS_ROWS, D_MODEL), lambda l, j: (0, 0)),
            pl.BlockSpec((None, D_MODEL, nb), lambda l, j: (l, 0, j)),
            pl.BlockSpec((None, 1, nb), lambda l, j: (l, 0, j)),
        ],
        out_specs=pl.BlockSpec((None, MODS_ROWS, nb), lambda l, j: (l, 0, j)),
        out_shape=jax.ShapeDtypeStruct((DEPTH, MODS_ROWS, 6 * D_MODEL), F32),
        compiler_params=_cparams(("arbitrary", "arbitrary"), "mods"),
        name="mods",
    )(c8, ada_w, ada_b.reshape(DEPTH, 1, 6 * D_MODEL))


def _norm_mod(x, g, shift, scale):
    ms = jnp.mean(x * x, axis=-1, keepdims=True)
    return (x * lax.rsqrt(ms + EPS) * g) * (1.0 + scale) + shift


def _mod_row(mods_ref, tile, tiles_per_seq, fixed_row):
    row = fixed_row if fixed_row is not None else tile // tiles_per_seq
    return mods_ref[pl.ds(row, 1), :]


def _proj_kernel(*refs, tiles_per_seq, fixed_row, rope):
    if rope:
        (x_ref, mods_ref, g_ref, w_ref, qg_ref, kg_ref, mavg_ref, cbd_ref, sbd_ref, cos_ref, sin_ref,
         a_ref, b_ref, q_ref, k_ref, v_ref, u_ref, bg_ref) = refs
    else:
        (x_ref, mods_ref, g_ref, w_ref, qg_ref, kg_ref, mavg_ref, cbd_ref, sbd_ref,
         a_ref, b_ref, q_ref, k_ref, v_ref, u_ref, bg_ref) = refs
    m = _mod_row(mods_ref, pl.program_id(0), tiles_per_seq, fixed_row)
    chunk = min(PROJ_CHUNK_ROWS, x_ref.shape[0])
    for c in range(x_ref.shape[0] // chunk):
        rows = slice(c * chunk, (c + 1) * chunk)
        _proj_rows(rows, m, rope, x_ref, g_ref, w_ref, qg_ref, kg_ref, mavg_ref, cbd_ref, sbd_ref,
                   cos_ref if rope else None, sin_ref if rope else None,
                   a_ref, b_ref, q_ref, k_ref, v_ref, u_ref, bg_ref)


def _proj_rows(rows, m, rope, x_ref, g_ref, w_ref, qg_ref, kg_ref, mavg_ref, cbd_ref, sbd_ref, cos_ref, sin_ref,
               a_ref, b_ref, q_ref, k_ref, v_ref, u_ref, bg_ref):
    h = _norm_mod(x_ref[rows, :], g_ref[...], m[:, 0:D_MODEL], m[:, D_MODEL:2 * D_MODEL])
    p = jnp.dot(h.astype(BF16), w_ref[...], preferred_element_type=F32)

    uf = p[:, 0:COL_Q].astype(BF16)
    a_ref[rows, :] = jnp.dot(uf, cbd_ref[...], preferred_element_type=F32).astype(a_ref.dtype)
    b_ref[rows, :] = jnp.dot(uf, sbd_ref[...], preferred_element_type=F32).astype(b_ref.dtype)

    def head_norm(t, g):
        sq = (t * t).astype(BF16)
        half = V7X_MXU_DIM
        ms = jnp.concatenate(
            [jnp.dot(sq[:, i:i + half], mavg_ref[...], preferred_element_type=F32)
             for i in range(0, t.shape[1], half)], axis=1)
        return t * lax.rsqrt(ms + EPS) * g

    def rotate(t):
        n = t.shape[-1]
        lane = lax.broadcasted_iota(I32, t.shape, 1)
        first_half = (lane % HEAD_DIM) < (HEAD_DIM // 2)
        swapped = jnp.where(first_half, pltpu.roll(t, n - HEAD_DIM // 2, 1), pltpu.roll(t, HEAD_DIM // 2, 1))
        return t * cos_ref[rows, :] + swapped * sin_ref[rows, :]

    q = head_norm(p[:, COL_Q:COL_K], qg_ref[...])
    k = head_norm(p[:, COL_K:COL_V], kg_ref[...])
    if rope:
        q = rotate(q)
        k = rotate(k)
    q_ref[rows, :] = (q * (ATTN_SCALE * LOG2E)).astype(BF16)
    k_ref[rows, :] = k.astype(BF16)
    v_ref[rows, :] = p[:, COL_V:COL_CX].astype(BF16)
    u_ref[rows, :] = p[:, COL_CC:COL_G] * p[:, COL_CX:COL_CB]
    bg_ref[rows, :] = p[:, COL_CB:COL_CC]


def _proj(x2, mods_l, norm_g, w_proj, qg, kg, mavg, cbd, sbd, cos_t, sin_t, *, seq_len, n_seq, fixed_row,
          dft_dtype):
    n_tok = x2.shape[0]
    tm = min(PROJ_TILE, seq_len)
    tps = seq_len // tm
    rope = cos_t is not None
    const = lambda i: (0, 0)
    in_specs = [
        pl.BlockSpec((tm, D_MODEL), lambda i: (i, 0)),
        pl.BlockSpec((MODS_ROWS, 6 * D_MODEL), const),
        pl.BlockSpec((1, D_MODEL), const),
        pl.BlockSpec((D_MODEL, COL_G), const),
        pl.BlockSpec((1, NA_WIDTH), const),
        pl.BlockSpec((1, NA_WIDTH), const),
        pl.BlockSpec((V7X_MXU_DIM, V7X_MXU_DIM), const),
        pl.BlockSpec((F_WIDTH, F_WIDTH), const),
        pl.BlockSpec((F_WIDTH, F_WIDTH), const),
    ]
    args = [x2, mods_l, norm_g, w_proj, qg, kg, mavg, cbd, sbd]
    if rope:
        in_specs += [pl.BlockSpec((tm, NA_WIDTH), lambda i: (i % tps, 0))] * 2
        args += [cos_t, sin_t]
    tok = lambda w: pl.BlockSpec((tm, w), lambda i: (i, 0))
    fmap = pl.BlockSpec((tm, F_WIDTH), lambda i: (i % tps, i // tps))
    out_specs = [fmap, fmap, tok(NA_WIDTH), tok(NA_WIDTH), tok(NA_WIDTH), tok(CONV_WIDTH), tok(CONV_WIDTH)]
    out_shape = [
        jax.ShapeDtypeStruct((seq_len, n_seq * F_WIDTH), dft_dtype),
        jax.ShapeDtypeStruct((seq_len, n_seq * F_WIDTH), dft_dtype),
        jax.ShapeDtypeStruct((n_tok, NA_WIDTH), BF16),
        jax.ShapeDtypeStruct((n_tok, NA_WIDTH), BF16),
        jax.ShapeDtypeStruct((n_tok, NA_WIDTH), BF16),
        jax.ShapeDtypeStruct((n_tok, CONV_WIDTH), F32),
        jax.ShapeDtypeStruct((n_tok, CONV_WIDTH), F32),
    ]
    return pl.pallas_call(
        functools.partial(_proj_kernel, tiles_per_seq=tps, fixed_row=fixed_row, rope=rope),
        grid=(n_tok // tm,),
        in_specs=in_specs,
        out_specs=out_specs,
        out_shape=out_shape,
        compiler_params=_cparams(("arbitrary",), "proj"),
        name="proj",
    )(*args)


def _fourier_kernel(c_ref, s_ref, a_ref, b_ref, o_ref):
    o = (jnp.dot(c_ref[...], a_ref[...], preferred_element_type=F32)
         - jnp.dot(s_ref[...], b_ref[...], preferred_element_type=F32))
    o_ref[...] = o.astype(BF16)


def _fourier(c_tab, s_tab, a, b):
    seq_len, width = a.shape
    tk = min(seq_len, 256)
    full = lambda i: (0, 0)
    return pl.pallas_call(
        _fourier_kernel,
        grid=(seq_len // tk,),
        in_specs=[
            pl.BlockSpec((tk, seq_len), lambda i: (i, 0)),
            pl.BlockSpec((tk, seq_len), lambda i: (i, 0)),
            pl.BlockSpec((seq_len, width), full, pipeline_mode=pl.Buffered(1)),
            pl.BlockSpec((seq_len, width), full, pipeline_mode=pl.Buffered(1)),
        ],
        out_specs=pl.BlockSpec((tk, width), lambda i: (i, 0)),
        out_shape=jax.ShapeDtypeStruct((seq_len, width), BF16),
        compiler_params=_cparams(("arbitrary",), "fourier"),
        name="fourier",
    )(c_tab, s_tab, a, b)


def _fft1_kernel(a_ref, b_ref, k1_ref, tc_ref, ts_ref, zr_ref, zi_ref):
    n = FFT_RADIX * FFT_STEP
    width = a_ref.shape[2]
    k1 = k1_ref[...]
    r1 = jnp.dot(k1, a_ref[...].reshape(n, width).astype(BF16), preferred_element_type=F32)
    r2 = jnp.dot(k1, b_ref[...].reshape(n, width).astype(BF16), preferred_element_type=F32)
    yr = r1[0:n] - r2[n:2 * n]
    yi = -(r2[0:n] + r1[n:2 * n])
    tc = tc_ref[...][:, 0:1]
    ts = ts_ref[...][:, 0:1]
    zr_ref[...] = (yr * tc + yi * ts).reshape(zr_ref.shape)
    zi_ref[...] = (yi * tc - yr * ts).reshape(zi_ref.shape)


def _fft2_kernel(zr_ref, zi_ref, k2_ref, f_ref):
    n = FFT_RADIX * FFT_STEP
    width = zr_ref.shape[2]
    zz = jnp.concatenate([zr_ref[...].reshape(n, width), zi_ref[...].reshape(n, width)], axis=0).astype(BF16)
    f_ref[...] = jnp.dot(k2_ref[...], zz, preferred_element_type=F32).reshape(f_ref.shape)


def _fourier_two_stage(a, b, k1, k2, tc, ts):
    seq_len, width = a.shape
    r = FFT_RADIX
    n = r * FFT_STEP
    a3 = a.reshape(r, r, width)
    b3 = b.reshape(r, r, width)
    steps = r // FFT_STEP
    col_blk = pl.BlockSpec((r, FFT_STEP, width), lambda j: (0, j, 0))
    row_blk = pl.BlockSpec((FFT_STEP, r, width), lambda j: (j, 0, 0))
    tw_blk = pl.BlockSpec((n, V7X_LANES), lambda j: (j, 0))
    z_shape = jax.ShapeDtypeStruct((r, r, width), F32)
    zr, zi = pl.pallas_call(
        _fft1_kernel,
        grid=(steps,),
        in_specs=[col_blk, col_blk, pl.BlockSpec((2 * n, n), lambda j: (0, 0)), tw_blk, tw_blk],
        out_specs=[col_blk, col_blk],
        out_shape=[z_shape, z_shape],
        compiler_params=_cparams(("arbitrary",), "fft1"),
        name="fft1",
    )(a3, b3, k1, tc, ts)
    f3 = pl.pallas_call(
        _fft2_kernel,
        grid=(steps,),
        in_specs=[row_blk, row_blk, pl.BlockSpec((n, 2 * n), lambda j: (0, 0))],
        out_specs=col_blk,
        out_shape=z_shape,
        compiler_params=_cparams(("arbitrary",), "fft2"),
        name="fft2",
    )(zr, zi, k2)
    return f3.reshape(seq_len, width)


def _stack_heads(qg):
    lane_head = lax.broadcasted_iota(I32, qg.shape, 1) // HEAD_DIM
    zero = jnp.zeros_like(qg)
    return jnp.concatenate([jnp.where(lane_head == h, qg, zero) for h in range(HEADS_PER_GROUP)], axis=0)


def _unstack_heads(o, rows):
    lane_head = lax.broadcasted_iota(I32, (rows, o.shape[1]), 1) // HEAD_DIM
    acc = jnp.zeros((rows, o.shape[1]), F32)
    for h in range(HEADS_PER_GROUP):
        acc = acc + jnp.where(lane_head == h, o[h * rows:(h + 1) * rows, :], 0.0)
    return acc


_NT = (((1,), (1,)), ((), ()))


def _attn_kernel(q_ref, k_ref, v_ref, kc_ref, vc_ref, bias_tab_ref, o_ref, *, rows):
    n_loc = NA_WIN_R * GRID_W
    for j in range(ATTN_ROWS_PER_STEP):
        r = pl.program_id(1) * ATTN_ROWS_PER_STEP + j
        rs = jnp.clip(r - NA_WIN_R // 2, 0, rows - NA_WIN_R)
        start = pl.multiple_of(rs * GRID_W, GRID_W)
        kwin = k_ref[pl.ds(start, n_loc), :]
        vwin = v_ref[pl.ds(start, n_loc), :]
        q = q_ref[j * GRID_W:(j + 1) * GRID_W, :]
        bias_ref = bias_tab_ref.at[rs - r + (NA_WIN_R - 1)]
        outs = []
        for g in range(NA_HEADS // HEADS_PER_GROUP):
            sl = slice(g * V7X_MXU_DIM, (g + 1) * V7X_MXU_DIM)
            qs = _stack_heads(q[:, sl])
            s_loc = lax.dot_general(qs, kwin[:, sl], _NT, preferred_element_type=F32)
            bias = bias_ref[g * HEADS_PER_GROUP:(g + 1) * HEADS_PER_GROUP].reshape(HEADS_PER_GROUP * GRID_W, n_loc)
            s_ctx = lax.dot_general(qs, kc_ref[:, sl], _NT, preferred_element_type=F32)
            s = jnp.concatenate([s_loc + bias, s_ctx], axis=1)
            m = jnp.max(s, axis=-1, keepdims=True)
            p = jnp.exp2(s - m)
            denom = jnp.sum(p, axis=-1, keepdims=True)
            pb = p.astype(BF16)
            o = (jnp.dot(pb[:, :n_loc], vwin[:, sl], preferred_element_type=F32)
                 + jnp.dot(pb[:, n_loc:], vc_ref[:, sl], preferred_element_type=F32))
            outs.append(_unstack_heads(o / denom, GRID_W))
        o_ref[j * GRID_W:(j + 1) * GRID_W, :] = jnp.concatenate(outs, axis=1).astype(BF16)


def _attn(q, k, v, kc, vc, bias_tab, *, n_seq, seq_len, ctx_len):
    rows = seq_len // GRID_W
    rps = ATTN_ROWS_PER_STEP
    steps = rows // rps

    return pl.pallas_call(
        functools.partial(_attn_kernel, rows=rows),
        grid=(n_seq, steps),
        in_specs=[
            pl.BlockSpec((rps * GRID_W, NA_WIDTH), lambda b, s: (b * steps + s, 0)),
            pl.BlockSpec((seq_len, NA_WIDTH), lambda b, s: (b, 0)),
            pl.BlockSpec((seq_len, NA_WIDTH), lambda b, s: (b, 0)),
            pl.BlockSpec((ctx_len, NA_WIDTH), lambda b, s: (b, 0)),
            pl.BlockSpec((ctx_len, NA_WIDTH), lambda b, s: (b, 0)),
            pl.BlockSpec(bias_tab.shape, lambda b, s: (0, 0, 0, 0), pipeline_mode=pl.Buffered(1)),
        ],
        out_specs=pl.BlockSpec((rps * GRID_W, NA_WIDTH), lambda b, s: (b * steps + s, 0)),
        out_shape=jax.ShapeDtypeStruct((n_seq * seq_len, NA_WIDTH), BF16),
        compiler_params=_cparams(("arbitrary", "arbitrary"), "attn"),
        name="attn",
    )(q, k, v, kc, vc, bias_tab)


def _ctx_attn_kernel(q_ref, k_ref, v_ref, o_ref):
    q = q_ref[...]
    n = q.shape[0]
    outs = []
    for g in range(NA_HEADS // HEADS_PER_GROUP):
        sl = slice(g * V7X_MXU_DIM, (g + 1) * V7X_MXU_DIM)
        qs = _stack_heads(q[:, sl])
        s = lax.dot_general(qs, k_ref[:, sl], _NT, preferred_element_type=F32)
        m = jnp.max(s, axis=-1, keepdims=True)
        p = jnp.exp2(s - m)
        denom = jnp.sum(p, axis=-1, keepdims=True)
        o = jnp.dot(p.astype(BF16), v_ref[:, sl], preferred_element_type=F32)
        outs.append(_unstack_heads(o / denom, n))
    o_ref[...] = jnp.concatenate(outs, axis=1).astype(BF16)


def _ctx_attn(q, k, v, *, n_seq, ctx_len):
    spec = pl.BlockSpec((ctx_len, NA_WIDTH), lambda b: (b, 0))
    return pl.pallas_call(
        _ctx_attn_kernel,
        grid=(n_seq,),
        in_specs=[spec, spec, spec],
        out_specs=spec,
        out_shape=jax.ShapeDtypeStruct((n_seq * ctx_len, NA_WIDTH), BF16),
        compiler_params=_cparams(("arbitrary",), "ctx_attn"),
        name="ctx_attn",
    )(q, k, v)


def _merge_kernel(x_ref, mods_ref, n1_ref, n2_ref, f_ref, at_ref, u_ref, up_ref, un_ref, bg_ref, cw_ref,
                  wg_ref, wf_ref, wna_ref, wcv_ref, wo_ref, rw_ref,
                  xo_ref, h2_ref, lg_ref, *, tiles_per_seq, fixed_row):
    i = pl.program_id(0)
    m = _mod_row(mods_ref, i, tiles_per_seq, fixed_row)
    dm = D_MODEL

    u = u_ref[...]
    t = u.shape[0]
    ti = i % tiles_per_seq
    row = lax.broadcasted_iota(I32, u.shape, 0)
    prev_row = jnp.where(ti == 0, 0.0, up_ref[V7X_SUBLANES - 1:V7X_SUBLANES, :])
    next_row = jnp.where(ti == tiles_per_seq - 1, 0.0, un_ref[0:1, :])
    u_prev = jnp.where(row == 0, prev_row, pltpu.roll(u, 1, 0))
    u_next = jnp.where(row == t - 1, next_row, pltpu.roll(u, t - 1, 0))
    conv = (bg_ref[...] * (cw_ref[0:1, :] * u_prev + cw_ref[1:2, :] * u + cw_ref[2:3, :] * u_next)).astype(BF16)

    chunk = min(MERGE_CHUNK_ROWS, t)
    for c in range(t // chunk):
        rows = slice(c * chunk, (c + 1) * chunk)
        x = x_ref[rows, :]
        h = _norm_mod(x, n1_ref[...], m[:, 0:dm], m[:, dm:2 * dm]).astype(BF16)
        gates = jax.nn.sigmoid(jnp.dot(h, wg_ref[...], preferred_element_type=F32))
        y_f = jnp.dot(f_ref[rows, :].astype(BF16), wf_ref[...], preferred_element_type=F32)
        y_na = jnp.dot(at_ref[rows, :], wna_ref[...], preferred_element_type=F32)
        y_cv = jnp.dot(conv[rows, :], wcv_ref[...], preferred_element_type=F32)
        merged = gates[:, 0:dm] * y_f + gates[:, dm:2 * dm] * y_na + gates[:, 2 * dm:3 * dm] * y_cv
        mixed = jnp.dot(merged.astype(BF16), wo_ref[...], preferred_element_type=F32)
        x_new = x + m[:, 2 * dm:3 * dm] * mixed
        xo_ref[rows, :] = x_new

        h2 = _norm_mod(x_new, n2_ref[...], m[:, 3 * dm:4 * dm], m[:, 4 * dm:5 * dm])
        h2_ref[rows, :] = h2.astype(BF16)
        hi = h2.astype(BF16)
        lo = (h2 - hi.astype(F32)).astype(BF16)
        p_hi = jnp.dot(hi, rw_ref[...], preferred_element_type=F32)
        p_lo = jnp.dot(lo, rw_ref[...], preferred_element_type=F32)
        lg_ref[rows, :] = p_hi + pltpu.roll(p_hi, V7X_LANES - N_EXPERTS, 1) + p_lo


def _merge(x2, mods_l, n1, n2, f_all, attn, u, bg, conv_w, w_gate, w_f, w_na, w_cv, w_o, rwt,
           *, seq_len, fixed_row):
    n_tok = x2.shape[0]
    tm = min(TOKEN_TILE, seq_len)
    tps = seq_len // tm
    const = lambda i: (0, 0)
    halo = tm // V7X_SUBLANES
    n_halo = n_tok // V7X_SUBLANES
    once = pl.Buffered(1)
    in_specs = [
        pl.BlockSpec((tm, D_MODEL), lambda i: (i, 0)),
        pl.BlockSpec((MODS_ROWS, 6 * D_MODEL), const),
        pl.BlockSpec((1, D_MODEL), const),
        pl.BlockSpec((1, D_MODEL), const),
        pl.BlockSpec((tm, F_WIDTH), lambda i: (i % tps, i // tps)),
        pl.BlockSpec((tm, NA_WIDTH), lambda i: (i, 0)),
        pl.BlockSpec((tm, CONV_WIDTH), lambda i: (i, 0)),
        pl.BlockSpec((V7X_SUBLANES, CONV_WIDTH), lambda i: (jnp.maximum(i * halo - 1, 0), 0)),
        pl.BlockSpec((V7X_SUBLANES, CONV_WIDTH), lambda i: (jnp.minimum((i + 1) * halo, n_halo - 1), 0)),
        pl.BlockSpec((tm, CONV_WIDTH), lambda i: (i, 0)),
        pl.BlockSpec((3, CONV_WIDTH), const),
        pl.BlockSpec((D_MODEL, 3 * D_MODEL), const, pipeline_mode=once),
        pl.BlockSpec((F_WIDTH, D_MODEL), const, pipeline_mode=once),
        pl.BlockSpec((NA_WIDTH, D_MODEL), const, pipeline_mode=once),
        pl.BlockSpec((CONV_WIDTH, D_MODEL), const, pipeline_mode=once),
        pl.BlockSpec((D_MODEL, D_MODEL), const, pipeline_mode=once),
        pl.BlockSpec((D_MODEL, V7X_LANES), const),
    ]
    out_specs = [
        pl.BlockSpec((tm, D_MODEL), lambda i: (i, 0)),
        pl.BlockSpec((tm, D_MODEL), lambda i: (i, 0)),
        pl.BlockSpec((tm, V7X_LANES), lambda i: (i, 0)),
    ]
    out_shape = [
        jax.ShapeDtypeStruct((n_tok, D_MODEL), F32),
        jax.ShapeDtypeStruct((n_tok, D_MODEL), BF16),
        jax.ShapeDtypeStruct((n_tok, V7X_LANES), F32),
    ]
    return pl.pallas_call(
        functools.partial(_merge_kernel, tiles_per_seq=tps, fixed_row=fixed_row),
        grid=(n_tok // tm,),
        in_specs=in_specs,
        out_specs=out_specs,
        out_shape=out_shape,
        compiler_params=_cparams(("arbitrary",), "merge"),
        name="merge",
    )(x2, mods_l, n1, n2, f_all, attn, u, u, u, bg, conv_w, w_gate, w_f, w_na, w_cv, w_o, rwt)


def _first_max(vals):
    best = vals[0]
    idx = jnp.zeros(best.shape, I32)
    for j in range(1, len(vals)):
        better = vals[j] > best
        idx = jnp.where(better, j, idx)
        best = jnp.where(better, vals[j], best)
    return best, idx


def _select(idx, vals):
    out = vals[-1]
    for j in range(len(vals) - 2, -1, -1):
        out = jnp.where(idx == j, vals[j], out)
    return out


def _route_kernel(lg_ref, rb_ref, ids_ref, wts_ref, cnt_ref, tot_ref, run_ref):
    step = pl.program_id(0)

    @pl.when(step == 0)
    def _():
        run_ref[...] = jnp.zeros_like(run_ref)

    s = jax.nn.sigmoid(lg_ref[...].T[0:N_EXPERTS, :])
    sb = s + rb_ref[...]
    t = s.shape[1]
    s_rows = [s[e:e + 1, :] for e in range(N_EXPERTS)]
    b_rows = [sb[e:e + 1, :] for e in range(N_EXPERTS)]
    epg = EXPERTS_PER_GROUP
    gscore = []
    for g in range(N_GROUPS):
        v = b_rows[g * epg:(g + 1) * epg]
        pair = None
        for a in range(epg):
            for b in range(a + 1, epg):
                pair = v[a] + v[b] if pair is None else jnp.maximum(pair, v[a] + v[b])
        gscore.append(pair)
    _, gi = _first_max(gscore)
    bv = [_select(gi, [b_rows[g * epg + j] for g in range(N_GROUPS)]) for j in range(epg)]
    sv = [_select(gi, [s_rows[g * epg + j] for g in range(N_GROUPS)]) for j in range(epg)]
    _, i1 = _first_max(bv)
    _, i2 = _first_max([jnp.where(i1 == j, -jnp.inf, bv[j]) for j in range(epg)])
    s1 = _select(i1, sv)
    s2 = _select(i2, sv)
    tot = s1 + s2
    e1 = gi * epg + i1
    e2 = gi * epg + i2

    eid = lax.broadcasted_iota(I32, (N_EXPERTS, t), 0)
    hit1 = eid == e1
    hit2 = eid == e2
    onehot = jnp.where(hit1 | hit2, 1.0, 0.0)
    before = (lax.broadcasted_iota(I32, (t, t), 0) < lax.broadcasted_iota(I32, (t, t), 1))
    prefix = jnp.dot(onehot.astype(BF16), jnp.where(before, 1.0, 0.0).astype(BF16),
                     preferred_element_type=F32)
    r1 = jnp.sum(jnp.where(hit1, prefix, 0.0), axis=0, keepdims=True)
    r2 = jnp.sum(jnp.where(hit2, prefix, 0.0), axis=0, keepdims=True)
    grp = float(SLOT_GROUP)
    cnt = jnp.sum(onehot, axis=1, keepdims=True)
    cnt = jnp.floor((cnt + (grp - 1.0)) / grp) * grp
    run = run_ref[...] + cnt
    run_ref[...] = run
    cnt_ref[...] = jnp.broadcast_to(cnt, cnt_ref.shape)
    tot_ref[...] = jnp.broadcast_to(run, tot_ref.shape)

    zi = jnp.zeros((V7X_SUBLANES - 4, t), I32)
    ids_ref[...] = jnp.concatenate([e1, e2, r1.astype(I32), r2.astype(I32), zi], axis=0)
    zf = jnp.zeros((V7X_SUBLANES - 2, t), F32)
    wts_ref[...] = jnp.concatenate([s1 / tot, s2 / tot, zf], axis=0)


def _route(logits, router_b):
    n_tok = logits.shape[0]
    tr = ROUTE_TILE
    return pl.pallas_call(
        _route_kernel,
        grid=(n_tok // tr,),
        in_specs=[
            pl.BlockSpec((tr, V7X_LANES), lambda i: (i, 0)),
            pl.BlockSpec((N_EXPERTS, 1), lambda i: (0, 0)),
        ],
        out_specs=[
            pl.BlockSpec((V7X_SUBLANES, tr), lambda i: (0, i)),
            pl.BlockSpec((V7X_SUBLANES, tr), lambda i: (0, i)),
            pl.BlockSpec((None, N_EXPERTS, V7X_LANES), lambda i: (i, 0, 0)),
            pl.BlockSpec((N_EXPERTS, V7X_LANES), lambda i: (0, 0)),
        ],
        out_shape=[
            jax.ShapeDtypeStruct((V7X_SUBLANES, n_tok), I32),
            jax.ShapeDtypeStruct((V7X_SUBLANES, n_tok), F32),
            jax.ShapeDtypeStruct((n_tok // tr, N_EXPERTS, V7X_LANES), F32),
            jax.ShapeDtypeStruct((N_EXPERTS, V7X_LANES), F32),
        ],
        scratch_shapes=[pltpu.VMEM((N_EXPERTS, 1), F32)],
        compiler_params=_cparams(("arbitrary",), "route"),
        name="route",
    )(logits, router_b.reshape(N_EXPERTS, 1))


def _lane_table(vals, width):
    lane = lax.broadcasted_iota(I32, (1, width), 1)
    out = jnp.zeros((1, width), F32)
    for e, v in enumerate(vals):
        out = jnp.where(lane == e, v, out)
    return out


def _slots_kernel(ids_ref, cnt_ref, tot_ref, loc_ref, gmap_ref, blk_ref, off_ref):
    step = pl.program_id(0)

    @pl.when(step == 0)
    def _():
        off_ref[...] = jnp.zeros_like(off_ref)

    blk = float(EXPERT_BLOCK)
    grp = float(SLOT_GROUP)
    cnt = cnt_ref[...][:, 0:1]
    tot = tot_ref[...][:, 0:1]
    off = off_ref[...]
    region = jnp.floor((tot + (blk - 1.0)) / blk) * blk
    starts, ends, local = [], [], []
    run = jnp.zeros((1, 1), F32)
    lrun = jnp.zeros((1, 1), F32)
    for e in range(N_EXPERTS):
        starts.append(run)
        run = run + region[e:e + 1, :]
        ends.append(run)
        local.append(lrun)
        lrun = lrun + cnt[e:e + 1, :]

    ids = ids_ref[...]
    e1, e2 = ids[0:1, :], ids[1:2, :]
    t = ids.shape[1]
    l1 = jnp.zeros((1, t), F32)
    l2 = jnp.zeros((1, t), F32)
    for e in range(N_EXPERTS):
        l1 = jnp.where(e1 == e, local[e], l1)
        l2 = jnp.where(e2 == e, local[e], l2)
    zi = jnp.zeros((V7X_SUBLANES - 2, t), I32)
    loc_ref[...] = jnp.concatenate([l1.astype(I32) + ids[2:3, :], l2.astype(I32) + ids[3:4, :], zi], axis=0)

    wg = gmap_ref.shape[1]
    first = lax.broadcasted_iota(I32, (1, wg), 1).astype(F32) * grp
    dest = jnp.zeros((1, wg), F32)
    for e in range(N_EXPERTS):
        inside = (first >= local[e]) & (first < local[e] + cnt[e:e + 1, :])
        dest = jnp.where(inside, starts[e] + off[e:e + 1, :] + (first - local[e]), dest)
    n_groups = jnp.broadcast_to(lrun / grp, (1, wg))
    zg = jnp.zeros((V7X_SUBLANES - 2, wg), I32)
    gmap_ref[...] = jnp.concatenate([(dest / grp).astype(I32), n_groups.astype(I32), zg], axis=0)
    off_ref[...] = off + cnt

    w = blk_ref.shape[1]
    first_row = lax.broadcasted_iota(I32, (1, w), 1).astype(F32) * blk
    owner = jnp.zeros((1, w), F32)
    for e in range(N_EXPERTS):
        owner = owner + jnp.where(first_row >= ends[e], 1.0, 0.0)
    owner = jnp.minimum(owner, float(N_EXPERTS - 1))
    used = jnp.broadcast_to(ends[-1] / blk, (1, w))
    pad_first = _lane_table([(starts[e] + tot[e:e + 1, :]) / grp for e in range(N_EXPERTS)], w)
    pad_count = _lane_table([(region[e:e + 1, :] - tot[e:e + 1, :]) / grp for e in range(N_EXPERTS)], w)
    zb = jnp.zeros((V7X_SUBLANES - 4, w), I32)
    blk_ref[...] = jnp.concatenate([owner.astype(I32), used.astype(I32), pad_first.astype(I32),
                                    pad_count.astype(I32), zb], axis=0)


def _slots(ids, cnt, tot, n_blocks):
    n_tok = ids.shape[1]
    tr = ROUTE_TILE
    wblk = -(-n_blocks // V7X_LANES) * V7X_LANES
    return pl.pallas_call(
        _slots_kernel,
        grid=(n_tok // tr,),
        in_specs=[
            pl.BlockSpec((V7X_SUBLANES, tr), lambda i: (0, i)),
            pl.BlockSpec((None, N_EXPERTS, V7X_LANES), lambda i: (i, 0, 0)),
            pl.BlockSpec((N_EXPERTS, V7X_LANES), lambda i: (0, 0)),
        ],
        out_specs=[
            pl.BlockSpec((V7X_SUBLANES, tr), lambda i: (0, i)),
            pl.BlockSpec((None, V7X_SUBLANES, SORT_GROUPS_PAD), lambda i: (i, 0, 0)),
            pl.BlockSpec((V7X_SUBLANES, wblk), lambda i: (0, 0)),
        ],
        out_shape=[
            jax.ShapeDtypeStruct((V7X_SUBLANES, n_tok), I32),
            jax.ShapeDtypeStruct((n_tok // tr, V7X_SUBLANES, SORT_GROUPS_PAD), I32),
            jax.ShapeDtypeStruct((V7X_SUBLANES, wblk), I32),
        ],
        scratch_shapes=[pltpu.VMEM((N_EXPERTS, 1), F32)],
        compiler_params=_cparams(("arbitrary",), "slots"),
        name="slots",
    )(ids, cnt, tot)


def _group_rows(group):
    if isinstance(group, int):
        return pl.ds(group * SLOT_GROUP, SLOT_GROUP)
    return pl.ds(pl.multiple_of(group * SLOT_GROUP, SLOT_GROUP), SLOT_GROUP)


def _for_each_group(n, body):
    unroll = 4

    def chunk(q, c):
        for u in range(unroll):
            body(q * unroll + u)
        return c

    def single(g, c):
        body(g)
        return c

    whole = n // unroll
    lax.fori_loop(0, whole, chunk, 0)
    lax.fori_loop(whole * unroll, n, single, 0)


def _group_copy(src_ref, src_group, dst_ref, dst_group, sem):
    return pltpu.make_async_copy(src_ref.at[_group_rows(src_group)], dst_ref.at[_group_rows(dst_group)], sem)


def _dispatch_kernel(gmap_ref, gprev_ref, blk_ref, loc_ref, *refs, n_first):
    if n_first is None:
        h_ref, xb_ref, sorted_ref, zero_ref, sem = refs
        second_ref = None
    else:
        h_ref, second_ref, xb_ref, sorted_ref, zero_ref, sem = refs
    step = pl.program_id(0)
    last = pl.num_programs(0) - 1
    buf = step % 2
    loc = loc_ref[...]
    slot = lax.broadcasted_iota(I32, (SORT_ROWS, loc.shape[1]), 0)
    perm = jnp.where(slot == loc[0:1, :], 1.0, jnp.where(slot == loc[1:2, :], 1.0, 0.0)).astype(BF16)

    def sort_rows(src_ref):
        sorted_ref[buf] = jnp.dot(perm, src_ref[...].astype(BF16), preferred_element_type=F32).astype(BF16)

    if second_ref is None:
        sort_rows(h_ref)
    else:
        pl.when(step < n_first)(lambda: sort_rows(h_ref))
        pl.when(step >= n_first)(lambda: sort_rows(second_ref))

    def tile_copy(map_ref, which, g):
        return _group_copy(sorted_ref.at[which], g, xb_ref, map_ref[0, g], sem.at[which])

    def start(g):
        tile_copy(gmap_ref, buf, g).start()

    def wait_tile(map_ref, which):
        rows = pl.ds(0, map_ref[1, 0] * SLOT_GROUP)
        pltpu.make_async_copy(sorted_ref.at[which, rows], xb_ref.at[rows], sem.at[which]).wait()

    _for_each_group(gmap_ref[1, 0], start)

    @pl.when(step == last)
    def _():
        zero_ref[...] = jnp.zeros_like(zero_ref)
        for e in range(N_EXPERTS):
            first = blk_ref[2, e]

            def zstart(g, c, first=first):
                _group_copy(zero_ref, 0, xb_ref, first + g, sem.at[2]).start()
                return c

            def zwait(g, c, first=first):
                _group_copy(zero_ref, 0, xb_ref, first + g, sem.at[2]).wait()
                return c

            lax.fori_loop(0, blk_ref[3, e], zstart, 0)
            lax.fori_loop(0, blk_ref[3, e], zwait, 0)

        def block_copy(b):
            rows = pl.ds(pl.multiple_of(b * EXPERT_BLOCK, EXPERT_BLOCK), EXPERT_BLOCK)
            return pltpu.make_async_copy(zero_ref, xb_ref.at[rows], sem.at[2])

        def bstart(b, c):
            block_copy(b).start()
            return c

        def bwait(b, c):
            block_copy(b).wait()
            return c

        n_blocks = xb_ref.shape[0] // EXPERT_BLOCK
        lax.fori_loop(blk_ref[1, 0], n_blocks, bstart, 0)
        lax.fori_loop(blk_ref[1, 0], n_blocks, bwait, 0)

    @pl.when((step > 0) & (gprev_ref[1, 0] > 0))
    def _():
        wait_tile(gprev_ref, 1 - buf)

    @pl.when((step == last) & (gmap_ref[1, 0] > 0))
    def _():
        wait_tile(gmap_ref, buf)


def _dispatch(gmap, blk, loc, h2, h2_second, n_slots):
    tr = ROUTE_TILE
    n_first = h2.shape[0] // tr
    n_tiles = n_first
    in_specs = [
        pl.BlockSpec((None, V7X_SUBLANES, SORT_GROUPS_PAD), lambda i: (i, 0, 0), memory_space=pltpu.SMEM),
        pl.BlockSpec((None, V7X_SUBLANES, SORT_GROUPS_PAD), lambda i: (jnp.maximum(i - 1, 0), 0, 0),
                     memory_space=pltpu.SMEM),
        pl.BlockSpec(blk.shape, lambda i: (0, 0), memory_space=pltpu.SMEM),
        pl.BlockSpec((V7X_SUBLANES, tr), lambda i: (0, i)),
        pl.BlockSpec((tr, D_MODEL), lambda i: (jnp.minimum(i, n_first - 1), 0)),
    ]
    args = [gmap, gmap, blk, loc, h2]
    if h2_second is not None:
        n_tiles += h2_second.shape[0] // tr
        in_specs.append(pl.BlockSpec((tr, D_MODEL), lambda i: (jnp.maximum(i - n_first, 0), 0)))
        args.append(h2_second)
    return pl.pallas_call(
        functools.partial(_dispatch_kernel, n_first=None if h2_second is None else n_first),
        grid=(n_tiles,),
        in_specs=in_specs,
        out_specs=pl.BlockSpec(memory_space=pl.ANY),
        out_shape=jax.ShapeDtypeStruct((n_slots, D_MODEL), BF16),
        scratch_shapes=[pltpu.VMEM((2, SORT_ROWS, D_MODEL), BF16), pltpu.VMEM((EXPERT_BLOCK, D_MODEL), BF16),
                        pltpu.SemaphoreType.DMA((3,))],
        compiler_params=_cparams(("arbitrary",), "dispatch"),
        name="dispatch",
    )(*args)


def _experts_kernel(blk_ref, used_ref, x_ref, w1_ref, w3_ref, w2_ref, y_ref, w1b, w3b, w2b):
    i = pl.program_id(0)
    prev = blk_ref[jnp.maximum(i - 1, 0)]

    @pl.when((i == 0) | (blk_ref[i] != prev))
    def _():
        w1b[...] = w1_ref[...].astype(BF16)
        w3b[...] = w3_ref[...].astype(BF16)
        w2b[...] = w2_ref[...].astype(BF16)

    @pl.when(i < used_ref[0])
    def _():
        x = x_ref[...]
        a = jnp.dot(x, w1b[...], preferred_element_type=F32)
        b = jnp.dot(x, w3b[...], preferred_element_type=F32)
        hid = (a * jax.nn.sigmoid(a) * b).astype(BF16)
        y_ref[...] = jnp.dot(hid, w2b[...], preferred_element_type=F32).astype(BF16)

    @pl.when(i >= used_ref[0])
    def _():
        y_ref[...] = jnp.zeros_like(y_ref)


def _experts(blk_e, used, xb, w1, w3, w2, layer):
    n_slots = xb.shape[0]
    bm = EXPERT_BLOCK
    row_map = lambda i, be, nu: (jnp.minimum(i, nu[0] - 1), 0)
    w_map = lambda i, be, nu: (layer, be[i], 0, 0)
    grid_spec = pltpu.PrefetchScalarGridSpec(
        num_scalar_prefetch=2,
        grid=(n_slots // bm,),
        in_specs=[
            pl.BlockSpec((bm, D_MODEL), row_map),
            pl.BlockSpec((None, None, D_MODEL, D_EXPERT), w_map),
            pl.BlockSpec((None, None, D_MODEL, D_EXPERT), w_map),
            pl.BlockSpec((None, None, D_EXPERT, D_MODEL), w_map),
        ],
        out_specs=pl.BlockSpec((bm, D_MODEL), lambda i, be, nu: (i, 0)),
        scratch_shapes=[pltpu.VMEM((D_MODEL, D_EXPERT), BF16), pltpu.VMEM((D_MODEL, D_EXPERT), BF16),
                        pltpu.VMEM((D_EXPERT, D_MODEL), BF16)],
    )
    return pl.pallas_call(
        _experts_kernel,
        grid_spec=grid_spec,
        out_shape=jax.ShapeDtypeStruct((n_slots, D_MODEL), BF16),
        compiler_params=_cparams(("arbitrary",), "experts"),
        name="experts",
    )(blk_e, used, xb, w1, w3, w2)


_TN = (((0,), (0,)), ((), ()))


def _combine_kernel(gmap_ref, gnext_ref, loc_ref, wts_ref, x_ref, mods_ref, yb_ref, o_ref, ys_ref, sem,
                    *, tiles_per_seq, fixed_row):
    step = pl.program_id(0)
    buf = step % 2

    def fetch(map_ref, which, g):
        return _group_copy(yb_ref, map_ref[0, g], ys_ref.at[which], g, sem.at[which])

    def start_own(g):
        fetch(gmap_ref, buf, g).start()

    def start_next(g):
        fetch(gnext_ref, 1 - buf, g).start()

    def wait_own():
        rows = pl.ds(0, gmap_ref[1, 0] * SLOT_GROUP)
        pltpu.make_async_copy(yb_ref.at[rows], ys_ref.at[buf, rows], sem.at[buf]).wait()

    @pl.when(step == 0)
    def _():
        ys_ref[...] = jnp.zeros_like(ys_ref)
        _for_each_group(gmap_ref[1, 0], start_own)

    @pl.when(step + 1 < pl.num_programs(0))
    def _():
        _for_each_group(gnext_ref[1, 0], start_next)

    loc = loc_ref[...]
    wts = wts_ref[...]
    slot = lax.broadcasted_iota(I32, (SORT_ROWS, loc.shape[1]), 0)
    perm = jnp.where(slot == loc[0:1, :], wts[0:1, :], jnp.where(slot == loc[1:2, :], wts[1:2, :], 0.0))
    pl.when(gmap_ref[1, 0] > 0)(wait_own)
    y = lax.dot_general(perm.astype(BF16), ys_ref[buf], _TN, preferred_element_type=F32)
    m = _mod_row(mods_ref, step, tiles_per_seq, fixed_row)
    o_ref[...] = x_ref[...] + m[:, 5 * D_MODEL:6 * D_MODEL] * y


def _combine(gmap, loc, wts, x_new, mods_l, yb, *, seq_len, fixed_row, tile_offset):
    n_tok = x_new.shape[0]
    tr = ROUTE_TILE
    tps = max(seq_len // tr, 1)
    n_tiles = n_tok // tr
    return pl.pallas_call(
        functools.partial(_combine_kernel, tiles_per_seq=tps, fixed_row=fixed_row),
        grid=(n_tiles,),
        in_specs=[
            pl.BlockSpec((None, V7X_SUBLANES, SORT_GROUPS_PAD), lambda i: (i + tile_offset, 0, 0),
                         memory_space=pltpu.SMEM),
            pl.BlockSpec((None, V7X_SUBLANES, SORT_GROUPS_PAD),
                         lambda i: (jnp.minimum(i + 1, n_tiles - 1) + tile_offset, 0, 0), memory_space=pltpu.SMEM),
            pl.BlockSpec((V7X_SUBLANES, tr), lambda i: (0, i + tile_offset)),
            pl.BlockSpec((V7X_SUBLANES, tr), lambda i: (0, i + tile_offset)),
            pl.BlockSpec((tr, D_MODEL), lambda i: (i, 0)),
            pl.BlockSpec((MODS_ROWS, 6 * D_MODEL), lambda i: (0, 0)),
            pl.BlockSpec(memory_space=pl.ANY),
        ],
        out_specs=pl.BlockSpec((tr, D_MODEL), lambda i: (i, 0)),
        out_shape=jax.ShapeDtypeStruct((n_tok, D_MODEL), F32),
        scratch_shapes=[pltpu.VMEM((2, SORT_ROWS, D_MODEL), BF16), pltpu.SemaphoreType.DMA((2,))],
        compiler_params=_cparams(("arbitrary",), "combine"),
        name="combine",
    )(gmap, gmap, loc, wts, x_new, mods_l, yb)


def _channel_dft_tables():
    j = np.arange(F_GDIM)
    ang = 2.0 * np.pi * ((j[:, None] * j[None, :]) % F_GDIM) / F_GDIM
    eye = np.eye(F_GROUPS)
    return (jnp.asarray(np.kron(eye, np.cos(ang)), F32).astype(BF16),
            jnp.asarray(np.kron(eye, np.sin(ang)), F32).astype(BF16))


def _position_dft_tables(seq_len):
    scale = 1.0 / math.sqrt(seq_len * F_GDIM)
    k = np.arange(seq_len, dtype=np.int64)
    ang = 2.0 * np.pi * ((k[:, None] * k[None, :]) % seq_len) / seq_len
    return (jnp.asarray(np.cos(ang) * scale, F32).astype(BF16),
            jnp.asarray(np.sin(ang) * scale, F32).astype(BF16))


def _two_stage_dft_tables(seq_len):
    r = FFT_RADIX
    assert seq_len == r * r
    scale = 1.0 / math.sqrt(seq_len * F_GDIM)
    j = np.arange(r, dtype=np.int64)
    ang_r = 2.0 * np.pi * ((j[:, None] * j[None, :]) % r) / r
    cs, ss = np.cos(ang_r), np.sin(ang_r)
    s = FFT_STEP
    eye = np.eye(s)
    k1 = np.concatenate([np.kron(cs, eye), np.kron(ss, eye)], axis=0) * scale

    def spread(m):
        out = np.zeros((r, s, s, r))
        for i in range(s):
            out[:, i, i, :] = m
        return out.reshape(r * s, s * r)

    k2 = np.concatenate([spread(cs), spread(ss)], axis=1)
    ka = j[None, :, None]
    t0 = (np.arange(r // s)[:, None, None] * s + np.arange(s)[None, None, :])
    ang_t = (2.0 * np.pi * ka * t0 / seq_len).reshape(-1, 1)
    tc = jnp.asarray(np.repeat(np.cos(ang_t), V7X_LANES, axis=1), F32)
    ts = jnp.asarray(np.repeat(np.sin(ang_t), V7X_LANES, axis=1), F32)
    return jnp.asarray(k1, F32).astype(BF16), jnp.asarray(k2, F32).astype(BF16), tc, ts


def _rope_tables(seq_len):
    t = np.arange(seq_len)
    row = (t // GRID_W).astype(np.float64)
    col = (t % GRID_W).astype(np.float64)
    inv = np.power(ROPE_BASE, -np.arange(ROPE_PER_AXIS, dtype=np.float64) / ROPE_PER_AXIS)
    ang = np.concatenate([row[:, None] * inv, col[:, None] * inv], axis=-1)
    cos = np.cos(ang)
    sin = np.sin(ang)
    cos_h = np.concatenate([cos, cos], axis=-1)
    sin_h = np.concatenate([-sin, sin], axis=-1)
    return (jnp.asarray(np.tile(cos_h, (1, NA_HEADS)), F32), jnp.asarray(np.tile(sin_h, (1, NA_HEADS)), F32))


def _bias_table(rpb_l):
    col = np.arange(GRID_W)
    col_start = np.clip(col - NA_WIN_C // 2, 0, GRID_W - NA_WIN_C)
    col_mask = (col[None, :] >= col_start[:, None]) & (col[None, :] < col_start[:, None] + NA_WIN_C)
    dc = np.clip(col[None, :] - col[:, None] + (NA_WIN_C - 1), 0, 2 * NA_WIN_C - 2)
    n_dc = 2 * NA_WIN_C - 1
    pick = (dc.reshape(-1)[None, :] == np.arange(n_dc)[:, None]).astype(np.float32)
    e = jnp.dot(rpb_l.reshape(-1, n_dc), jnp.asarray(pick), precision=HIGHEST)
    e = e.reshape(NA_HEADS, 2 * NA_WIN_R - 1, GRID_W, GRID_W)
    e = jnp.where(jnp.asarray(col_mask)[None, None], e * LOG2E, NEG_BIG)
    e = e.transpose(0, 2, 1, 3)
    b = jnp.stack([e[:, :, o:o + NA_WIN_R] for o in range(NA_WIN_R)], axis=0)
    return b.reshape(NA_WIN_R, NA_HEADS, GRID_W, NA_WIN_R * GRID_W)


def _moe(h2, h2_second, logits, w1, w3, w2, layer, router_b):
    n_tok = logits.shape[0]
    n_tiles = n_tok // ROUTE_TILE
    max_rows = 2 * n_tok + N_EXPERTS * n_tiles * (SLOT_GROUP - 1) + N_EXPERTS * (EXPERT_BLOCK - 1)
    n_blocks = -(-max_rows // EXPERT_BLOCK)
    n_slots = n_blocks * EXPERT_BLOCK
    ids, wts, cnt, tot = _route(logits, router_b)
    loc, gmap, blk = _slots(ids, cnt, tot, n_blocks)
    xb = _dispatch(gmap, blk, loc, h2, h2_second, n_slots)
    yb = _experts(blk[0, :n_blocks], blk[1, 0:1], xb, w1, w3, w2, layer)
    return yb, gmap, loc, wts


def kernel(x, c, ctx, c_ctx, ada_w, ada_b, norm1_g, w_in, qn_g, kn_g, rpb, conv_w, w_f, w_na, w_cv, w_o,
           norm2_g, router_w, router_b, w1, w3, w2):
    bsz, seq_len, d = x.shape
    ctx_len = ctx.shape[1]
    n_lat = bsz * seq_len
    n_ctx = bsz * ctx_len
    ctx_row = bsz

    c8 = jnp.concatenate([c, c_ctx[None, :], jnp.zeros((MODS_ROWS - bsz - 1, d), F32)], axis=0)
    mods = _mods(c8, ada_w, ada_b)

    cbd, sbd = _channel_dft_tables()
    fst, g_dft, tc3, ts3 = _two_stage_dft_tables(seq_len)
    c_ctx_t, s_ctx_t = _position_dft_tables(ctx_len)
    cos_t, sin_t = _rope_tables(seq_len)
    mavg = jnp.asarray(np.kron(np.eye(HEADS_PER_GROUP), np.full((HEAD_DIM, HEAD_DIM), 1.0 / HEAD_DIM)),
                       F32).astype(BF16)
    rw_hi = router_w.astype(BF16)
    rw_lo = (router_w - rw_hi.astype(F32)).astype(BF16)
    rwt = jnp.concatenate([rw_hi, rw_lo, jnp.zeros((d, V7X_LANES - 2 * N_EXPERTS), BF16)], axis=1)

    xl = x.reshape(n_lat, d)
    xc = ctx.reshape(n_ctx, d)
    for l in range(DEPTH):
        last = l == DEPTH - 1
        w_proj = w_in[l][:, :COL_G].astype(BF16)
        w_gate = w_in[l][:, COL_G:].astype(BF16)
        wf, wna, wcv, wo = (w_f[l].astype(BF16), w_na[l].astype(BF16), w_cv[l].astype(BF16), w_o[l].astype(BF16))
        n1 = norm1_g[l].reshape(1, d)
        n2 = norm2_g[l].reshape(1, d)
        qg = jnp.tile(qn_g[l], NA_HEADS).reshape(1, NA_WIDTH)
        kg = jnp.tile(kn_g[l], NA_HEADS).reshape(1, NA_WIDTH)
        bias_tab = _bias_table(rpb[l])
        mods_l = mods[l]

        a_c, b_c, q_c, k_c, v_c, u_c, bg_c = _proj(
            xc, mods_l, n1, w_proj, qg, kg, mavg, cbd, sbd, None, None,
            seq_len=ctx_len, n_seq=bsz, fixed_row=ctx_row, dft_dtype=BF16)
        a_l, b_l, q_l, k_l, v_l, u_l, bg_l = _proj(
            xl, mods_l, n1, w_proj, qg, kg, mavg, cbd, sbd, cos_t, sin_t,
            seq_len=seq_len, n_seq=bsz, fixed_row=None, dft_dtype=F32)

        f_l = _fourier_two_stage(a_l, b_l, fst, g_dft, tc3, ts3)
        attn_l = _attn(q_l, k_l, v_l, k_c, v_c, bias_tab, n_seq=bsz, seq_len=seq_len, ctx_len=ctx_len)
        xl_new, h2_l, lg_l = _merge(xl, mods_l, n1, n2, f_l, attn_l, u_l, bg_l, conv_w[l], w_gate,
                                    wf, wna, wcv, wo, rwt, seq_len=seq_len, fixed_row=None)
        if last:
            yb, gmap, loc, wts = _moe(h2_l, None, lg_l, w1, w3, w2, l, router_b)
            xl = _combine(gmap, loc, wts, xl_new, mods_l, yb, seq_len=seq_len, fixed_row=None, tile_offset=0)
        else:
            f_c = _fourier(c_ctx_t, s_ctx_t, a_c, b_c)
            attn_c = _ctx_attn(q_c, k_c, v_c, n_seq=bsz, ctx_len=ctx_len)
            xc_new, h2_c, lg_c = _merge(xc, mods_l, n1, n2, f_c, attn_c, u_c, bg_c, conv_w[l], w_gate,
                                        wf, wna, wcv, wo, rwt, seq_len=ctx_len, fixed_row=ctx_row)
            lg = jnp.concatenate([lg_l, lg_c], axis=0)
            yb, gmap, loc, wts = _moe(h2_l, h2_c, lg, w1, w3, w2, l, router_b)
            xl = _combine(gmap, loc, wts, xl_new, mods_l, yb, seq_len=seq_len, fixed_row=None, tile_offset=0)
            xc = _combine(gmap, loc, wts, xc_new, mods_l, yb, seq_len=ctx_len, fixed_row=ctx_row,
                          tile_offset=n_lat // ROUTE_TILE)
    return xl.reshape(bsz, seq_len, d)
```

```python
import functools
import math

import numpy as np
import jax
import jax.numpy as jnp
from jax import lax
from jax.experimental import pallas as pl
from jax.experimental.pallas import tpu as pltpu

F32 = jnp.float32
BF16 = jnp.bfloat16
I32 = jnp.int32
HIGHEST = lax.Precision.HIGHEST

D_MODEL = 1024
DEPTH = 2
GRID_W = 64
EPS = 1e-6
F_GROUPS = 4
F_GDIM = 64
F_WIDTH = 256
NA_HEADS = 8
HEAD_DIM = 64
NA_WIDTH = 512
NA_WIN_R = 8
NA_WIN_C = 16
ATTN_SCALE = HEAD_DIM ** -0.5
LOG2E = math.log2(math.e)
ROPE_BASE = 10000.0
ROPE_PER_AXIS = HEAD_DIM // 4
CONV_WIDTH = 256
COL_Q = 256
COL_K = 768
COL_V = 1280
COL_CX = 1792
COL_CB = 2048
COL_CC = 2304
COL_G = 2560
N_EXPERTS = 16
N_GROUPS = 4
EXPERTS_PER_GROUP = 4
D_EXPERT = 512

V7X_LANES = 128
V7X_SUBLANES = 8
V7X_MXU_DIM = 256

TOKEN_TILE = 1024
PROJ_TILE = 1024
PROJ_CHUNK_ROWS = 128
MERGE_CHUNK_ROWS = 512
ROUTE_TILE = 512
EXPERT_BLOCK = 1024
SLOT_GROUP = 2 * V7X_SUBLANES
SORT_ROWS = -(-(2 * ROUTE_TILE + N_EXPERTS * (SLOT_GROUP - 1)) // V7X_LANES) * V7X_LANES
SORT_GROUPS_PAD = -(-(SORT_ROWS // SLOT_GROUP) // V7X_LANES) * V7X_LANES
HEADS_PER_GROUP = V7X_MXU_DIM // HEAD_DIM
ATTN_ROWS_PER_STEP = 16
FFT_RADIX = 64
FFT_STEP = V7X_SUBLANES
NEG_BIG = -1e30
MODS_ROWS = 8
VMEM_LIMIT = 56 * 1024 * 1024


VMEM_BUDGET_MIB = {
    "mods": 24, "proj": 56, "fourier": 8, "fft1": 32, "fft2": 24, "attn": 48, "ctx_attn": 8, "merge": 56,
    "route": 8, "slots": 8, "dispatch": 24, "experts": 40, "combine": 24,
}
assert max(VMEM_BUDGET_MIB.values()) * 1024 * 1024 <= VMEM_LIMIT


def _cparams(sem, name):
    return pltpu.CompilerParams(dimension_semantics=sem, vmem_limit_bytes=VMEM_BUDGET_MIB[name] * 1024 * 1024)


def _mods_kernel(c_ref, w_ref, b_ref, o_ref):
    c = c_ref[...]
    sc = c * jax.nn.sigmoid(c)
    w = w_ref[...]
    w_hi = w.astype(BF16)
    w_lo = (w - w_hi.astype(F32)).astype(BF16)
    s_hi = sc.astype(BF16)
    s_lo = (sc - s_hi.astype(F32)).astype(BF16)
    acc = jnp.dot(s_hi, w_hi, preferred_element_type=F32)
    acc = acc + jnp.dot(s_hi, w_lo, preferred_element_type=F32)
    acc = acc + jnp.dot(s_lo, w_hi, preferred_element_type=F32)
    o_ref[...] = acc + b_ref[...]


def _mods(c8, ada_w, ada_b):
    nb = 1536
    return pl.pallas_call(
        _mods_kernel,
        grid=(DEPTH, 6 * D_MODEL // nb),
        in_specs=[
            pl.BlockSpec((MODS_ROWS, D_MODEL), lambda l, j: (0, 0)),
            pl.BlockSpec((None, D_MODEL, nb), lambda l, j: (l, 0, j)),
            pl.BlockSpec((None, 1, nb), lambda l, j: (l, 0, j)),
        ],
        out_specs=pl.BlockSpec((None, MODS_ROWS, nb), lambda l, j: (l, 0, j)),
        out_shape=jax.ShapeDtypeStruct((DEPTH, MODS_ROWS, 6 * D_MODEL), F32),
        compiler_params=_cparams(("arbitrary", "arbitrary"), "mods"),
        name="mods",
    )(c8, ada_w, ada_b.reshape(DEPTH, 1, 6 * D_MODEL))


def _norm_mod(x, g, shift, scale):
    ms = jnp.mean(x * x, axis=-1, keepdims=True)
    return (x * lax.rsqrt(ms + EPS) * g) * (1.0 + scale) + shift


def _mod_row(mods_ref, tile, tiles_per_seq, fixed_row):
    row = fixed_row if fixed_row is not None else tile // tiles_per_seq
    return mods_ref[pl.ds(row, 1), :]


def _proj_kernel(*refs, tiles_per_seq, fixed_row, rope):
    if rope:
        (x_ref, mods_ref, g_ref, w_ref, qg_ref, kg_ref, mavg_ref, cbd_ref, sbd_ref, cs_ref,
         ab_ref, qkv_ref, ubg_ref) = refs
        cos_ref, sin_ref = cs_ref.at[0], cs_ref.at[1]
    else:
        (x_ref, mods_ref, g_ref, w_ref, qg_ref, kg_ref, mavg_ref, cbd_ref, sbd_ref,
         ab_ref, qkv_ref, ubg_ref) = refs
    a_ref, b_ref = ab_ref.at[0], ab_ref.at[1]
    q_ref, k_ref, v_ref = qkv_ref.at[0], qkv_ref.at[1], qkv_ref.at[2]
    u_ref, bg_ref = ubg_ref.at[0], ubg_ref.at[1]
    m = _mod_row(mods_ref, pl.program_id(0), tiles_per_seq, fixed_row)
    chunk = min(PROJ_CHUNK_ROWS, x_ref.shape[0])
    for c in range(x_ref.shape[0] // chunk):
        rows = slice(c * chunk, (c + 1) * chunk)
        _proj_rows(rows, m, rope, x_ref, g_ref, w_ref, qg_ref, kg_ref, mavg_ref, cbd_ref, sbd_ref,
                   cos_ref if rope else None, sin_ref if rope else None,
                   a_ref, b_ref, q_ref, k_ref, v_ref, u_ref, bg_ref)


def _proj_rows(rows, m, rope, x_ref, g_ref, w_ref, qg_ref, kg_ref, mavg_ref, cbd_ref, sbd_ref, cos_ref, sin_ref,
               a_ref, b_ref, q_ref, k_ref, v_ref, u_ref, bg_ref):
    h = _norm_mod(x_ref[rows, :], g_ref[...], m[:, 0:D_MODEL], m[:, D_MODEL:2 * D_MODEL])
    p = jnp.dot(h.astype(BF16), w_ref[...], preferred_element_type=F32)

    uf = p[:, 0:COL_Q].astype(BF16)
    a_ref[rows, :] = jnp.dot(uf, cbd_ref[...], preferred_element_type=F32).astype(a_ref.dtype)
    b_ref[rows, :] = jnp.dot(uf, sbd_ref[...], preferred_element_type=F32).astype(b_ref.dtype)

    def head_norm(t, g):
        sq = (t * t).astype(BF16)
        half = V7X_MXU_DIM
        ms = jnp.concatenate(
            [jnp.dot(sq[:, i:i + half], mavg_ref[...], preferred_element_type=F32)
             for i in range(0, t.shape[1], half)], axis=1)
        return t * lax.rsqrt(ms + EPS) * g

    def rotate(t):
        n = t.shape[-1]
        lane = lax.broadcasted_iota(I32, t.shape, 1)
        first_half = (lane % HEAD_DIM) < (HEAD_DIM // 2)
        swapped = jnp.where(first_half, pltpu.roll(t, n - HEAD_DIM // 2, 1), pltpu.roll(t, HEAD_DIM // 2, 1))
        return t * cos_ref[rows, :] + swapped * sin_ref[rows, :]

    q = head_norm(p[:, COL_Q:COL_K], qg_ref[...])
    k = head_norm(p[:, COL_K:COL_V], kg_ref[...])
    if rope:
        q = rotate(q)
        k = rotate(k)
    q_ref[rows, :] = (q * (ATTN_SCALE * LOG2E)).astype(BF16)
    k_ref[rows, :] = k.astype(BF16)
    v_ref[rows, :] = p[:, COL_V:COL_CX].astype(BF16)
    u_ref[rows, :] = p[:, COL_CC:COL_G] * p[:, COL_CX:COL_CB]
    bg_ref[rows, :] = p[:, COL_CB:COL_CC]


def _proj(x2, mods_l, norm_g, w_proj, qg, kg, mavg, cbd, sbd, cos_sin, *, seq_len, n_seq, fixed_row,
          dft_dtype):
    n_tok = x2.shape[0]
    tm = min(PROJ_TILE, seq_len)
    tps = seq_len // tm
    rope = cos_sin is not None
    const = lambda i: (0, 0)
    in_specs = [
        pl.BlockSpec((tm, D_MODEL), lambda i: (i, 0)),
        pl.BlockSpec((MODS_ROWS, 6 * D_MODEL), const),
        pl.BlockSpec((1, D_MODEL), const),
        pl.BlockSpec((D_MODEL, COL_G), const),
        pl.BlockSpec((1, NA_WIDTH), const),
        pl.BlockSpec((1, NA_WIDTH), const),
        pl.BlockSpec((V7X_MXU_DIM, V7X_MXU_DIM), const),
        pl.BlockSpec((F_WIDTH, F_WIDTH), const),
        pl.BlockSpec((F_WIDTH, F_WIDTH), const),
    ]
    args = [x2, mods_l, norm_g, w_proj, qg, kg, mavg, cbd, sbd]
    if rope:
        in_specs.append(pl.BlockSpec((2, tm, NA_WIDTH), lambda i: (0, i % tps, 0)))
        args.append(cos_sin)
    out_specs = [
        pl.BlockSpec((2, tm, F_WIDTH), lambda i: (0, i % tps, i // tps)),
        pl.BlockSpec((3, tm, NA_WIDTH), lambda i: (0, i, 0)),
        pl.BlockSpec((2, tm, CONV_WIDTH), lambda i: (0, i, 0)),
    ]
    out_shape = [
        jax.ShapeDtypeStruct((2, seq_len, n_seq * F_WIDTH), dft_dtype),
        jax.ShapeDtypeStruct((3, n_tok, NA_WIDTH), BF16),
        jax.ShapeDtypeStruct((2, n_tok, CONV_WIDTH), F32),
    ]
    return pl.pallas_call(
        functools.partial(_proj_kernel, tiles_per_seq=tps, fixed_row=fixed_row, rope=rope),
        grid=(n_tok // tm,),
        in_specs=in_specs,
        out_specs=out_specs,
        out_shape=out_shape,
        compiler_params=_cparams(("arbitrary",), "proj"),
        name="proj",
    )(*args)


def _fourier_kernel(c_ref, s_ref, a_ref, b_ref, o_ref):
    o = (jnp.dot(c_ref[...], a_ref[...], preferred_element_type=F32)
         - jnp.dot(s_ref[...], b_ref[...], preferred_element_type=F32))
    o_ref[...] = o.astype(BF16)


def _fourier(c_tab, s_tab, ab):
    _, seq_len, width = ab.shape
    tk = min(seq_len, 256)
    return pl.pallas_call(
        _fourier_kernel,
        grid=(seq_len // tk,),
        in_specs=[
            pl.BlockSpec((tk, seq_len), lambda i: (i, 0)),
            pl.BlockSpec((tk, seq_len), lambda i: (i, 0)),
            pl.BlockSpec((None, seq_len, width), lambda i: (0, 0, 0), pipeline_mode=pl.Buffered(1)),
            pl.BlockSpec((None, seq_len, width), lambda i: (1, 0, 0), pipeline_mode=pl.Buffered(1)),
        ],
        out_specs=pl.BlockSpec((tk, width), lambda i: (i, 0)),
        out_shape=jax.ShapeDtypeStruct((seq_len, width), BF16),
        compiler_params=_cparams(("arbitrary",), "fourier"),
        name="fourier",
    )(c_tab, s_tab, ab, ab)


def _fft1_kernel(a_ref, b_ref, k1_ref, tc_ref, ts_ref, zr_ref, zi_ref):
    n = FFT_RADIX * FFT_STEP
    width = a_ref.shape[2]
    k1 = k1_ref[...]
    r1 = jnp.dot(k1, a_ref[...].reshape(n, width).astype(BF16), preferred_element_type=F32)
    r2 = jnp.dot(k1, b_ref[...].reshape(n, width).astype(BF16), preferred_element_type=F32)
    yr = r1[0:n] - r2[n:2 * n]
    yi = -(r2[0:n] + r1[n:2 * n])
    tc = tc_ref[...][:, 0:1]
    ts = ts_ref[...][:, 0:1]
    zr_ref[...] = (yr * tc + yi * ts).reshape(zr_ref.shape)
    zi_ref[...] = (yi * tc - yr * ts).reshape(zi_ref.shape)


def _fft2_kernel(zr_ref, zi_ref, k2_ref, f_ref):
    n = FFT_RADIX * FFT_STEP
    width = zr_ref.shape[2]
    zz = jnp.concatenate([zr_ref[...].reshape(n, width), zi_ref[...].reshape(n, width)], axis=0).astype(BF16)
    f_ref[...] = jnp.dot(k2_ref[...], zz, preferred_element_type=F32).reshape(f_ref.shape)


def _fourier_two_stage(ab, k1, k2, tc, ts):
    _, seq_len, width = ab.shape
    r = FFT_RADIX
    n = r * FFT_STEP
    ab4 = ab.reshape(2, r, r, width)
    steps = r // FFT_STEP
    col_blk = pl.BlockSpec((r, FFT_STEP, width), lambda j: (0, j, 0))
    a_blk = pl.BlockSpec((None, r, FFT_STEP, width), lambda j: (0, 0, j, 0))
    b_blk = pl.BlockSpec((None, r, FFT_STEP, width), lambda j: (1, 0, j, 0))
    row_blk = pl.BlockSpec((FFT_STEP, r, width), lambda j: (j, 0, 0))
    tw_blk = pl.BlockSpec((n, V7X_LANES), lambda j: (j, 0))
    z_shape = jax.ShapeDtypeStruct((r, r, width), F32)
    zr, zi = pl.pallas_call(
        _fft1_kernel,
        grid=(steps,),
        in_specs=[a_blk, b_blk, pl.BlockSpec((2 * n, n), lambda j: (0, 0)), tw_blk, tw_blk],
        out_specs=[col_blk, col_blk],
        out_shape=[z_shape, z_shape],
        compiler_params=_cparams(("arbitrary",), "fft1"),
        name="fft1",
    )(ab4, ab4, k1, tc, ts)
    f3 = pl.pallas_call(
        _fft2_kernel,
        grid=(steps,),
        in_specs=[row_blk, row_blk, pl.BlockSpec((n, 2 * n), lambda j: (0, 0))],
        out_specs=col_blk,
        out_shape=z_shape,
        compiler_params=_cparams(("arbitrary",), "fft2"),
        name="fft2",
    )(zr, zi, k2)
    return f3.reshape(seq_len, width)


def _stack_heads(qg):
    lane_head = lax.broadcasted_iota(I32, qg.shape, 1) // HEAD_DIM
    zero = jnp.zeros_like(qg)
    return jnp.concatenate([jnp.where(lane_head == h, qg, zero) for h in range(HEADS_PER_GROUP)], axis=0)


def _unstack_heads(o, rows):
    lane_head = lax.broadcasted_iota(I32, (rows, o.shape[1]), 1) // HEAD_DIM
    acc = jnp.zeros((rows, o.shape[1]), F32)
    for h in range(HEADS_PER_GROUP):
        acc = acc + jnp.where(lane_head == h, o[h * rows:(h + 1) * rows, :], 0.0)
    return acc


_NT = (((1,), (1,)), ((), ()))


def _attn_kernel(q_ref, k_ref, v_ref, kc_ref, vc_ref, bias_tab_ref, o_ref, *, rows):
    n_loc = NA_WIN_R * GRID_W
    for j in range(ATTN_ROWS_PER_STEP):
        r = pl.program_id(1) * ATTN_ROWS_PER_STEP + j
        rs = jnp.clip(r - NA_WIN_R // 2, 0, rows - NA_WIN_R)
        start = pl.multiple_of(rs * GRID_W, GRID_W)
        kwin = k_ref[pl.ds(start, n_loc), :]
        vwin = v_ref[pl.ds(start, n_loc), :]
        q = q_ref[j * GRID_W:(j + 1) * GRID_W, :]
        bias_ref = bias_tab_ref.at[rs - r + (NA_WIN_R - 1)]
        outs = []
        for g in range(NA_HEADS // HEADS_PER_GROUP):
            sl = slice(g * V7X_MXU_DIM, (g + 1) * V7X_MXU_DIM)
            qs = _stack_heads(q[:, sl])
            s_loc = lax.dot_general(qs, kwin[:, sl], _NT, preferred_element_type=F32)
            bias = bias_ref[g * HEADS_PER_GROUP:(g + 1) * HEADS_PER_GROUP].reshape(HEADS_PER_GROUP * GRID_W, n_loc)
            s_ctx = lax.dot_general(qs, kc_ref[:, sl], _NT, preferred_element_type=F32)
            s = jnp.concatenate([s_loc + bias, s_ctx], axis=1)
            m = jnp.max(s, axis=-1, keepdims=True)
            p = jnp.exp2(s - m)
            denom = jnp.sum(p, axis=-1, keepdims=True)
            pb = p.astype(BF16)
            o = (jnp.dot(pb[:, :n_loc], vwin[:, sl], preferred_element_type=F32)
                 + jnp.dot(pb[:, n_loc:], vc_ref[:, sl], preferred_element_type=F32))
            outs.append(_unstack_heads(o / denom, GRID_W))
        o_ref[j * GRID_W:(j + 1) * GRID_W, :] = jnp.concatenate(outs, axis=1).astype(BF16)


def _attn(qkv, qkv_ctx, bias_tab, *, n_seq, seq_len, ctx_len):
    rows = seq_len // GRID_W
    rps = ATTN_ROWS_PER_STEP
    steps = rows // rps

    return pl.pallas_call(
        functools.partial(_attn_kernel, rows=rows),
        grid=(n_seq, steps),
        in_specs=[
            pl.BlockSpec((None, rps * GRID_W, NA_WIDTH), lambda b, s: (0, b * steps + s, 0)),
            pl.BlockSpec((None, seq_len, NA_WIDTH), lambda b, s: (1, b, 0)),
            pl.BlockSpec((None, seq_len, NA_WIDTH), lambda b, s: (2, b, 0)),
            pl.BlockSpec((None, ctx_len, NA_WIDTH), lambda b, s: (1, b, 0)),
            pl.BlockSpec((None, ctx_len, NA_WIDTH), lambda b, s: (2, b, 0)),
            pl.BlockSpec(bias_tab.shape, lambda b, s: (0, 0, 0, 0), pipeline_mode=pl.Buffered(1)),
        ],
        out_specs=pl.BlockSpec((rps * GRID_W, NA_WIDTH), lambda b, s: (b * steps + s, 0)),
        out_shape=jax.ShapeDtypeStruct((n_seq * seq_len, NA_WIDTH), BF16),
        compiler_params=_cparams(("arbitrary", "arbitrary"), "attn"),
        name="attn",
    )(qkv, qkv, qkv, qkv_ctx, qkv_ctx, bias_tab)


def _ctx_attn_kernel(q_ref, k_ref, v_ref, o_ref):
    q = q_ref[...]
    n = q.shape[0]
    outs = []
    for g in range(NA_HEADS // HEADS_PER_GROUP):
        sl = slice(g * V7X_MXU_DIM, (g + 1) * V7X_MXU_DIM)
        qs = _stack_heads(q[:, sl])
        s = lax.dot_general(qs, k_ref[:, sl], _NT, preferred_element_type=F32)
        m = jnp.max(s, axis=-1, keepdims=True)
        p = jnp.exp2(s - m)
        denom = jnp.sum(p, axis=-1, keepdims=True)
        o = jnp.dot(p.astype(BF16), v_ref[:, sl], preferred_element_type=F32)
        outs.append(_unstack_heads(o / denom, n))
    o_ref[...] = jnp.concatenate(outs, axis=1).astype(BF16)


def _ctx_attn(qkv, *, n_seq, ctx_len):
    part = lambda p: pl.BlockSpec((None, ctx_len, NA_WIDTH), lambda b: (p, b, 0))
    return pl.pallas_call(
        _ctx_attn_kernel,
        grid=(n_seq,),
        in_specs=[part(0), part(1), part(2)],
        out_specs=pl.BlockSpec((ctx_len, NA_WIDTH), lambda b: (b, 0)),
        out_shape=jax.ShapeDtypeStruct((n_seq * ctx_len, NA_WIDTH), BF16),
        compiler_params=_cparams(("arbitrary",), "ctx_attn"),
        name="ctx_attn",
    )(qkv, qkv, qkv)


def _merge_kernel(x_ref, mods_ref, n1_ref, n2_ref, f_ref, at_ref, u_ref, up_ref, un_ref, bg_ref, cw_ref,
                  wg_ref, wf_ref, wna_ref, wcv_ref, wo_ref, rw_ref,
                  xo_ref, h2_ref, lg_ref, *, tiles_per_seq, fixed_row):
    i = pl.program_id(0)
    m = _mod_row(mods_ref, i, tiles_per_seq, fixed_row)
    dm = D_MODEL

    u = u_ref[...]
    t = u.shape[0]
    ti = i % tiles_per_seq
    row = lax.broadcasted_iota(I32, u.shape, 0)
    prev_row = jnp.where(ti == 0, 0.0, up_ref[V7X_SUBLANES - 1:V7X_SUBLANES, :])
    next_row = jnp.where(ti == tiles_per_seq - 1, 0.0, un_ref[0:1, :])
    u_prev = jnp.where(row == 0, prev_row, pltpu.roll(u, 1, 0))
    u_next = jnp.where(row == t - 1, next_row, pltpu.roll(u, t - 1, 0))
    conv = (bg_ref[...] * (cw_ref[0:1, :] * u_prev + cw_ref[1:2, :] * u + cw_ref[2:3, :] * u_next)).astype(BF16)

    chunk = min(MERGE_CHUNK_ROWS, t)
    for c in range(t // chunk):
        rows = slice(c * chunk, (c + 1) * chunk)
        x = x_ref[rows, :]
        h = _norm_mod(x, n1_ref[...], m[:, 0:dm], m[:, dm:2 * dm]).astype(BF16)
        gates = jax.nn.sigmoid(jnp.dot(h, wg_ref[...], preferred_element_type=F32))
        y_f = jnp.dot(f_ref[rows, :].astype(BF16), wf_ref[...], preferred_element_type=F32)
        y_na = jnp.dot(at_ref[rows, :], wna_ref[...], preferred_element_type=F32)
        y_cv = jnp.dot(conv[rows, :], wcv_ref[...], preferred_element_type=F32)
        merged = gates[:, 0:dm] * y_f + gates[:, dm:2 * dm] * y_na + gates[:, 2 * dm:3 * dm] * y_cv
        mixed = jnp.dot(merged.astype(BF16), wo_ref[...], preferred_element_type=F32)
        x_new = x + m[:, 2 * dm:3 * dm] * mixed
        xo_ref[rows, :] = x_new

        h2 = _norm_mod(x_new, n2_ref[...], m[:, 3 * dm:4 * dm], m[:, 4 * dm:5 * dm])
        h2_ref[rows, :] = h2.astype(BF16)
        hi = h2.astype(BF16)
        lo = (h2 - hi.astype(F32)).astype(BF16)
        p_hi = jnp.dot(hi, rw_ref[...], preferred_element_type=F32)
        p_lo = jnp.dot(lo, rw_ref[...], preferred_element_type=F32)
        lg_ref[rows, :] = p_hi + pltpu.roll(p_hi, V7X_LANES - N_EXPERTS, 1) + p_lo


def _merge(x2, mods_l, n1, n2, f_all, attn, ubg, conv_w, w_gate, w_f, w_na, w_cv, w_o, rwt,
           *, seq_len, fixed_row):
    n_tok = x2.shape[0]
    tm = min(TOKEN_TILE, seq_len)
    tps = seq_len // tm
    const = lambda i: (0, 0)
    halo = tm // V7X_SUBLANES
    n_halo = n_tok // V7X_SUBLANES
    once = pl.Buffered(1)
    in_specs = [
        pl.BlockSpec((tm, D_MODEL), lambda i: (i, 0)),
        pl.BlockSpec((MODS_ROWS, 6 * D_MODEL), const),
        pl.BlockSpec((1, D_MODEL), const),
        pl.BlockSpec((1, D_MODEL), const),
        pl.BlockSpec((tm, F_WIDTH), lambda i: (i % tps, i // tps)),
        pl.BlockSpec((tm, NA_WIDTH), lambda i: (i, 0)),
        pl.BlockSpec((None, tm, CONV_WIDTH), lambda i: (0, i, 0)),
        pl.BlockSpec((None, V7X_SUBLANES, CONV_WIDTH), lambda i: (0, jnp.maximum(i * halo - 1, 0), 0)),
        pl.BlockSpec((None, V7X_SUBLANES, CONV_WIDTH), lambda i: (0, jnp.minimum((i + 1) * halo, n_halo - 1), 0)),
        pl.BlockSpec((None, tm, CONV_WIDTH), lambda i: (1, i, 0)),
        pl.BlockSpec((3, CONV_WIDTH), const),
        pl.BlockSpec((D_MODEL, 3 * D_MODEL), const, pipeline_mode=once),
        pl.BlockSpec((F_WIDTH, D_MODEL), const, pipeline_mode=once),
        pl.BlockSpec((NA_WIDTH, D_MODEL), const, pipeline_mode=once),
        pl.BlockSpec((CONV_WIDTH, D_MODEL), const, pipeline_mode=once),
        pl.BlockSpec((D_MODEL, D_MODEL), const, pipeline_mode=once),
        pl.BlockSpec((D_MODEL, V7X_LANES), const),
    ]
    out_specs = [
        pl.BlockSpec((tm, D_MODEL), lambda i: (i, 0)),
        pl.BlockSpec((tm, D_MODEL), lambda i: (i, 0)),
        pl.BlockSpec((tm, V7X_LANES), lambda i: (i, 0)),
    ]
    out_shape = [
        jax.ShapeDtypeStruct((n_tok, D_MODEL), F32),
        jax.ShapeDtypeStruct((n_tok, D_MODEL), BF16),
        jax.ShapeDtypeStruct((n_tok, V7X_LANES), F32),
    ]
    return pl.pallas_call(
        functools.partial(_merge_kernel, tiles_per_seq=tps, fixed_row=fixed_row),
        grid=(n_tok // tm,),
        in_specs=in_specs,
        out_specs=out_specs,
        out_shape=out_shape,
        compiler_params=_cparams(("arbitrary",), "merge"),
        name="merge",
    )(x2, mods_l, n1, n2, f_all, attn, ubg, ubg, ubg, ubg, conv_w, w_gate, w_f, w_na, w_cv, w_o, rwt)


def _first_max(vals):
    best = vals[0]
    idx = jnp.zeros(best.shape, I32)
    for j in range(1, len(vals)):
        better = vals[j] > best
        idx = jnp.where(better, j, idx)
        best = jnp.where(better, vals[j], best)
    return best, idx


def _select(idx, vals):
    out = vals[-1]
    for j in range(len(vals) - 2, -1, -1):
        out = jnp.where(idx == j, vals[j], out)
    return out


def _route_kernel(lg_ref, rb_ref, ids_ref, wts_ref, cnt_ref, tot_ref, run_ref):
    step = pl.program_id(0)

    @pl.when(step == 0)
    def _():
        run_ref[...] = jnp.zeros_like(run_ref)

    s = jax.nn.sigmoid(lg_ref[...].T[0:N_EXPERTS, :])
    sb = s + rb_ref[...]
    t = s.shape[1]
    s_rows = [s[e:e + 1, :] for e in range(N_EXPERTS)]
    b_rows = [sb[e:e + 1, :] for e in range(N_EXPERTS)]
    epg = EXPERTS_PER_GROUP
    gscore = []
    for g in range(N_GROUPS):
        v = b_rows[g * epg:(g + 1) * epg]
        pair = None
        for a in range(epg):
            for b in range(a + 1, epg):
                pair = v[a] + v[b] if pair is None else jnp.maximum(pair, v[a] + v[b])
        gscore.append(pair)
    _, gi = _first_max(gscore)
    bv = [_select(gi, [b_rows[g * epg + j] for g in range(N_GROUPS)]) for j in range(epg)]
    sv = [_select(gi, [s_rows[g * epg + j] for g in range(N_GROUPS)]) for j in range(epg)]
    _, i1 = _first_max(bv)
    _, i2 = _first_max([jnp.where(i1 == j, -jnp.inf, bv[j]) for j in range(epg)])
    s1 = _select(i1, sv)
    s2 = _select(i2, sv)
    tot = s1 + s2
    e1 = gi * epg + i1
    e2 = gi * epg + i2

    eid = lax.broadcasted_iota(I32, (N_EXPERTS, t), 0)
    hit1 = eid == e1
    hit2 = eid == e2
    onehot = jnp.where(hit1 | hit2, 1.0, 0.0)
    before = (lax.broadcasted_iota(I32, (t, t), 0) < lax.broadcasted_iota(I32, (t, t), 1))
    prefix = jnp.dot(onehot.astype(BF16), jnp.where(before, 1.0, 0.0).astype(BF16),
                     preferred_element_type=F32)
    r1 = jnp.sum(jnp.where(hit1, prefix, 0.0), axis=0, keepdims=True)
    r2 = jnp.sum(jnp.where(hit2, prefix, 0.0), axis=0, keepdims=True)
    grp = float(SLOT_GROUP)
    cnt = jnp.sum(onehot, axis=1, keepdims=True)
    cnt = jnp.floor((cnt + (grp - 1.0)) / grp) * grp
    run = run_ref[...] + cnt
    run_ref[...] = run
    cnt_ref[...] = jnp.broadcast_to(cnt, cnt_ref.shape)
    tot_ref[...] = jnp.broadcast_to(run, tot_ref.shape)

    zi = jnp.zeros((V7X_SUBLANES - 4, t), I32)
    ids_ref[...] = jnp.concatenate([e1, e2, r1.astype(I32), r2.astype(I32), zi], axis=0)
    zf = jnp.zeros((V7X_SUBLANES - 2, t), F32)
    wts_ref[...] = jnp.concatenate([s1 / tot, s2 / tot, zf], axis=0)


def _route(logits, router_b):
    n_tok = logits.shape[0]
    tr = ROUTE_TILE
    return pl.pallas_call(
        _route_kernel,
        grid=(n_tok // tr,),
        in_specs=[
            pl.BlockSpec((tr, V7X_LANES), lambda i: (i, 0)),
            pl.BlockSpec((N_EXPERTS, 1), lambda i: (0, 0)),
        ],
        out_specs=[
            pl.BlockSpec((V7X_SUBLANES, tr), lambda i: (0, i)),
            pl.BlockSpec((V7X_SUBLANES, tr), lambda i: (0, i)),
            pl.BlockSpec((None, N_EXPERTS, V7X_LANES), lambda i: (i, 0, 0)),
            pl.BlockSpec((N_EXPERTS, V7X_LANES), lambda i: (0, 0)),
        ],
        out_shape=[
            jax.ShapeDtypeStruct((V7X_SUBLANES, n_tok), I32),
            jax.ShapeDtypeStruct((V7X_SUBLANES, n_tok), F32),
            jax.ShapeDtypeStruct((n_tok // tr, N_EXPERTS, V7X_LANES), F32),
            jax.ShapeDtypeStruct((N_EXPERTS, V7X_LANES), F32),
        ],
        scratch_shapes=[pltpu.VMEM((N_EXPERTS, 1), F32)],
        compiler_params=_cparams(("arbitrary",), "route"),
        name="route",
    )(logits, router_b.reshape(N_EXPERTS, 1))


def _lane_table(vals, width):
    lane = lax.broadcasted_iota(I32, (1, width), 1)
    out = jnp.zeros((1, width), F32)
    for e, v in enumerate(vals):
        out = jnp.where(lane == e, v, out)
    return out


def _slots_kernel(ids_ref, cnt_ref, tot_ref, loc_ref, gmap_ref, blk_ref, off_ref):
    step = pl.program_id(0)

    @pl.when(step == 0)
    def _():
        off_ref[...] = jnp.zeros_like(off_ref)

    blk = float(EXPERT_BLOCK)
    grp = float(SLOT_GROUP)
    cnt = cnt_ref[...][:, 0:1]
    tot = tot_ref[...][:, 0:1]
    off = off_ref[...]
    region = jnp.floor((tot + (blk - 1.0)) / blk) * blk
    starts, ends, local = [], [], []
    run = jnp.zeros((1, 1), F32)
    lrun = jnp.zeros((1, 1), F32)
    for e in range(N_EXPERTS):
        starts.append(run)
        run = run + region[e:e + 1, :]
        ends.append(run)
        local.append(lrun)
        lrun = lrun + cnt[e:e + 1, :]

    ids = ids_ref[...]
    e1, e2 = ids[0:1, :], ids[1:2, :]
    t = ids.shape[1]
    l1 = jnp.zeros((1, t), F32)
    l2 = jnp.zeros((1, t), F32)
    for e in range(N_EXPERTS):
        l1 = jnp.where(e1 == e, local[e], l1)
        l2 = jnp.where(e2 == e, local[e], l2)
    zi = jnp.zeros((V7X_SUBLANES - 2, t), I32)
    loc_ref[...] = jnp.concatenate([l1.astype(I32) + ids[2:3, :], l2.astype(I32) + ids[3:4, :], zi], axis=0)

    wg = gmap_ref.shape[1]
    first = lax.broadcasted_iota(I32, (1, wg), 1).astype(F32) * grp
    dest = jnp.zeros((1, wg), F32)
    for e in range(N_EXPERTS):
        inside = (first >= local[e]) & (first < local[e] + cnt[e:e + 1, :])
        dest = jnp.where(inside, starts[e] + off[e:e + 1, :] + (first - local[e]), dest)
    n_groups = jnp.broadcast_to(lrun / grp, (1, wg))
    zg = jnp.zeros((V7X_SUBLANES - 2, wg), I32)
    gmap_ref[...] = jnp.concatenate([(dest / grp).astype(I32), n_groups.astype(I32), zg], axis=0)
    off_ref[...] = off + cnt

    w = blk_ref.shape[1]
    first_row = lax.broadcasted_iota(I32, (1, w), 1).astype(F32) * blk
    owner = jnp.zeros((1, w), F32)
    for e in range(N_EXPERTS):
        owner = owner + jnp.where(first_row >= ends[e], 1.0, 0.0)
    owner = jnp.minimum(owner, float(N_EXPERTS - 1))
    used = jnp.broadcast_to(ends[-1] / blk, (1, w))
    pad_first = _lane_table([(starts[e] + tot[e:e + 1, :]) / grp for e in range(N_EXPERTS)], w)
    pad_count = _lane_table([(region[e:e + 1, :] - tot[e:e + 1, :]) / grp for e in range(N_EXPERTS)], w)
    zb = jnp.zeros((V7X_SUBLANES - 4, w), I32)
    blk_ref[...] = jnp.concatenate([owner.astype(I32), used.astype(I32), pad_first.astype(I32),
                                    pad_count.astype(I32), zb], axis=0)


def _slots(ids, cnt, tot, n_blocks):
    n_tok = ids.shape[1]
    tr = ROUTE_TILE
    wblk = -(-n_blocks // V7X_LANES) * V7X_LANES
    return pl.pallas_call(
        _slots_kernel,
        grid=(n_tok // tr,),
        in_specs=[
            pl.BlockSpec((V7X_SUBLANES, tr), lambda i: (0, i)),
            pl.BlockSpec((None, N_EXPERTS, V7X_LANES), lambda i: (i, 0, 0)),
            pl.BlockSpec((N_EXPERTS, V7X_LANES), lambda i: (0, 0)),
        ],
        out_specs=[
            pl.BlockSpec((V7X_SUBLANES, tr), lambda i: (0, i)),
            pl.BlockSpec((None, V7X_SUBLANES, SORT_GROUPS_PAD), lambda i: (i, 0, 0)),
            pl.BlockSpec((V7X_SUBLANES, wblk), lambda i: (0, 0)),
        ],
        out_shape=[
            jax.ShapeDtypeStruct((V7X_SUBLANES, n_tok), I32),
            jax.ShapeDtypeStruct((n_tok // tr, V7X_SUBLANES, SORT_GROUPS_PAD), I32),
            jax.ShapeDtypeStruct((V7X_SUBLANES, wblk), I32),
        ],
        scratch_shapes=[pltpu.VMEM((N_EXPERTS, 1), F32)],
        compiler_params=_cparams(("arbitrary",), "slots"),
        name="slots",
    )(ids, cnt, tot)


def _group_rows(group):
    if isinstance(group, int):
        return pl.ds(group * SLOT_GROUP, SLOT_GROUP)
    return pl.ds(pl.multiple_of(group * SLOT_GROUP, SLOT_GROUP), SLOT_GROUP)


def _for_each_group(n, body):
    unroll = 4

    def chunk(q, c):
        for u in range(unroll):
            body(q * unroll + u)
        return c

    def single(g, c):
        body(g)
        return c

    whole = n // unroll
    lax.fori_loop(0, whole, chunk, 0)
    lax.fori_loop(whole * unroll, n, single, 0)


def _group_copy(src_ref, src_group, dst_ref, dst_group, sem):
    return pltpu.make_async_copy(src_ref.at[_group_rows(src_group)], dst_ref.at[_group_rows(dst_group)], sem)


def _dispatch_kernel(gmap_ref, gprev_ref, blk_ref, loc_ref, *refs, n_first):
    if n_first is None:
        h_ref, xb_ref, sorted_ref, zero_ref, sem = refs
        second_ref = None
    else:
        h_ref, second_ref, xb_ref, sorted_ref, zero_ref, sem = refs
    step = pl.program_id(0)
    last = pl.num_programs(0) - 1
    buf = step % 2
    loc = loc_ref[...]
    slot = lax.broadcasted_iota(I32, (SORT_ROWS, loc.shape[1]), 0)
    perm = jnp.where(slot == loc[0:1, :], 1.0, jnp.where(slot == loc[1:2, :], 1.0, 0.0)).astype(BF16)

    def sort_rows(src_ref):
        sorted_ref[buf] = jnp.dot(perm, src_ref[...].astype(BF16), preferred_element_type=F32).astype(BF16)

    if second_ref is None:
        sort_rows(h_ref)
    else:
        pl.when(step < n_first)(lambda: sort_rows(h_ref))
        pl.when(step >= n_first)(lambda: sort_rows(second_ref))

    def tile_copy(map_ref, which, g):
        return _group_copy(sorted_ref.at[which], g, xb_ref, map_ref[0, g], sem.at[which])

    def start(g):
        tile_copy(gmap_ref, buf, g).start()

    def wait_tile(map_ref, which):
        rows = pl.ds(0, map_ref[1, 0] * SLOT_GROUP)
        pltpu.make_async_copy(sorted_ref.at[which, rows], xb_ref.at[rows], sem.at[which]).wait()

    _for_each_group(gmap_ref[1, 0], start)

    @pl.when(step == last)
    def _():
        zero_ref[...] = jnp.zeros_like(zero_ref)
        for e in range(N_EXPERTS):
            first = blk_ref[2, e]

            def zstart(g, c, first=first):
                _group_copy(zero_ref, 0, xb_ref, first + g, sem.at[2]).start()
                return c

            def zwait(g, c, first=first):
                _group_copy(zero_ref, 0, xb_ref, first + g, sem.at[2]).wait()
                return c

            lax.fori_loop(0, blk_ref[3, e], zstart, 0)
            lax.fori_loop(0, blk_ref[3, e], zwait, 0)

        def block_copy(b):
            rows = pl.ds(pl.multiple_of(b * EXPERT_BLOCK, EXPERT_BLOCK), EXPERT_BLOCK)
            return pltpu.make_async_copy(zero_ref, xb_ref.at[rows], sem.at[2])

        def bstart(b, c):
            block_copy(b).start()
            return c

        def bwait(b, c):
            block_copy(b).wait()
            return c

        n_blocks = xb_ref.shape[0] // EXPERT_BLOCK
        lax.fori_loop(blk_ref[1, 0], n_blocks, bstart, 0)
        lax.fori_loop(blk_ref[1, 0], n_blocks, bwait, 0)

    @pl.when((step > 0) & (gprev_ref[1, 0] > 0))
    def _():
        wait_tile(gprev_ref, 1 - buf)

    @pl.when((step == last) & (gmap_ref[1, 0] > 0))
    def _():
        wait_tile(gmap_ref, buf)


def _dispatch(gmap, blk, loc, h2, h2_second, n_slots):
    tr = ROUTE_TILE
    n_first = h2.shape[0] // tr
    n_tiles = n_first
    in_specs = [
        pl.BlockSpec((None, V7X_SUBLANES, SORT_GROUPS_PAD), lambda i: (i, 0, 0), memory_space=pltpu.SMEM),
        pl.BlockSpec((None, V7X_SUBLANES, SORT_GROUPS_PAD), lambda i: (jnp.maximum(i - 1, 0), 0, 0),
                     memory_space=pltpu.SMEM),
        pl.BlockSpec(blk.shape, lambda i: (0, 0), memory_space=pltpu.SMEM),
        pl.BlockSpec((V7X_SUBLANES, tr), lambda i: (0, i)),
        pl.BlockSpec((tr, D_MODEL), lambda i: (jnp.minimum(i, n_first - 1), 0)),
    ]
    args = [gmap, gmap, blk, loc, h2]
    if h2_second is not None:
        n_tiles += h2_second.shape[0] // tr
        in_specs.append(pl.BlockSpec((tr, D_MODEL), lambda i: (jnp.maximum(i - n_first, 0), 0)))
        args.append(h2_second)
    return pl.pallas_call(
        functools.partial(_dispatch_kernel, n_first=None if h2_second is None else n_first),
        grid=(n_tiles,),
        in_specs=in_specs,
        out_specs=pl.BlockSpec(memory_space=pl.ANY),
        out_shape=jax.ShapeDtypeStruct((n_slots, D_MODEL), BF16),
        scratch_shapes=[pltpu.VMEM((2, SORT_ROWS, D_MODEL), BF16), pltpu.VMEM((EXPERT_BLOCK, D_MODEL), BF16),
                        pltpu.SemaphoreType.DMA((3,))],
        compiler_params=_cparams(("arbitrary",), "dispatch"),
        name="dispatch",
    )(*args)


def _experts_kernel(blk_ref, used_ref, x_ref, w1_ref, w3_ref, w2_ref, y_ref, w1b, w3b, w2b):
    i = pl.program_id(0)
    prev = blk_ref[jnp.maximum(i - 1, 0)]

    @pl.when((i == 0) | (blk_ref[i] != prev))
    def _():
        w1b[...] = w1_ref[...].astype(BF16)
        w3b[...] = w3_ref[...].astype(BF16)
        w2b[...] = w2_ref[...].astype(BF16)

    @pl.when(i < used_ref[0])
    def _():
        x = x_ref[...]
        a = jnp.dot(x, w1b[...], preferred_element_type=F32)
        b = jnp.dot(x, w3b[...], preferred_element_type=F32)
        hid = (a * jax.nn.sigmoid(a) * b).astype(BF16)
        y_ref[...] = jnp.dot(hid, w2b[...], preferred_element_type=F32).astype(BF16)

    @pl.when(i >= used_ref[0])
    def _():
        y_ref[...] = jnp.zeros_like(y_ref)


def _experts(blk_e, used, xb, w1, w3, w2, layer):
    n_slots = xb.shape[0]
    bm = EXPERT_BLOCK
    row_map = lambda i, be, nu: (jnp.minimum(i, nu[0] - 1), 0)
    w_map = lambda i, be, nu: (layer, be[i], 0, 0)
    grid_spec = pltpu.PrefetchScalarGridSpec(
        num_scalar_prefetch=2,
        grid=(n_slots // bm,),
        in_specs=[
            pl.BlockSpec((bm, D_MODEL), row_map),
            pl.BlockSpec((None, None, D_MODEL, D_EXPERT), w_map),
            pl.BlockSpec((None, None, D_MODEL, D_EXPERT), w_map),
            pl.BlockSpec((None, None, D_EXPERT, D_MODEL), w_map),
        ],
        out_specs=pl.BlockSpec((bm, D_MODEL), lambda i, be, nu: (i, 0)),
        scratch_shapes=[pltpu.VMEM((D_MODEL, D_EXPERT), BF16), pltpu.VMEM((D_MODEL, D_EXPERT), BF16),
                        pltpu.VMEM((D_EXPERT, D_MODEL), BF16)],
    )
    return pl.pallas_call(
        _experts_kernel,
        grid_spec=grid_spec,
        out_shape=jax.ShapeDtypeStruct((n_slots, D_MODEL), BF16),
        compiler_params=_cparams(("arbitrary",), "experts"),
        name="experts",
    )(blk_e, used, xb, w1, w3, w2)


_TN = (((0,), (0,)), ((), ()))


def _combine_kernel(gmap_ref, gnext_ref, loc_ref, wts_ref, x_ref, mods_ref, yb_ref, o_ref, ys_ref, sem,
                    *, tiles_per_seq, fixed_row):
    step = pl.program_id(0)
    buf = step % 2

    def fetch(map_ref, which, g):
        return _group_copy(yb_ref, map_ref[0, g], ys_ref.at[which], g, sem.at[which])

    def start_own(g):
        fetch(gmap_ref, buf, g).start()

    def start_next(g):
        fetch(gnext_ref, 1 - buf, g).start()

    def wait_own():
        rows = pl.ds(0, gmap_ref[1, 0] * SLOT_GROUP)
        pltpu.make_async_copy(yb_ref.at[rows], ys_ref.at[buf, rows], sem.at[buf]).wait()

    @pl.when(step == 0)
    def _():
        ys_ref[...] = jnp.zeros_like(ys_ref)
        _for_each_group(gmap_ref[1, 0], start_own)

    @pl.when(step + 1 < pl.num_programs(0))
    def _():
        _for_each_group(gnext_ref[1, 0], start_next)

    loc = loc_ref[...]
    wts = wts_ref[...]
    slot = lax.broadcasted_iota(I32, (SORT_ROWS, loc.shape[1]), 0)
    perm = jnp.where(slot == loc[0:1, :], wts[0:1, :], jnp.where(slot == loc[1:2, :], wts[1:2, :], 0.0))
    pl.when(gmap_ref[1, 0] > 0)(wait_own)
    y = lax.dot_general(perm.astype(BF16), ys_ref[buf], _TN, preferred_element_type=F32)
    m = _mod_row(mods_ref, step, tiles_per_seq, fixed_row)
    o_ref[...] = x_ref[...] + m[:, 5 * D_MODEL:6 * D_MODEL] * y


def _combine(gmap, loc, wts, x_new, mods_l, yb, *, seq_len, fixed_row, tile_offset):
    n_tok = x_new.shape[0]
    tr = ROUTE_TILE
    tps = max(seq_len // tr, 1)
    n_tiles = n_tok // tr
    return pl.pallas_call(
        functools.partial(_combine_kernel, tiles_per_seq=tps, fixed_row=fixed_row),
        grid=(n_tiles,),
        in_specs=[
            pl.BlockSpec((None, V7X_SUBLANES, SORT_GROUPS_PAD), lambda i: (i + tile_offset, 0, 0),
                         memory_space=pltpu.SMEM),
            pl.BlockSpec((None, V7X_SUBLANES, SORT_GROUPS_PAD),
                         lambda i: (jnp.minimum(i + 1, n_tiles - 1) + tile_offset, 0, 0), memory_space=pltpu.SMEM),
            pl.BlockSpec((V7X_SUBLANES, tr), lambda i: (0, i + tile_offset)),
            pl.BlockSpec((V7X_SUBLANES, tr), lambda i: (0, i + tile_offset)),
            pl.BlockSpec((tr, D_MODEL), lambda i: (i, 0)),
            pl.BlockSpec((MODS_ROWS, 6 * D_MODEL), lambda i: (0, 0)),
            pl.BlockSpec(memory_space=pl.ANY),
        ],
        out_specs=pl.BlockSpec((tr, D_MODEL), lambda i: (i, 0)),
        out_shape=jax.ShapeDtypeStruct((n_tok, D_MODEL), F32),
        scratch_shapes=[pltpu.VMEM((2, SORT_ROWS, D_MODEL), BF16), pltpu.SemaphoreType.DMA((2,))],
        compiler_params=_cparams(("arbitrary",), "combine"),
        name="combine",
    )(gmap, gmap, loc, wts, x_new, mods_l, yb)


def _channel_dft_tables():
    j = np.arange(F_GDIM)
    ang = 2.0 * np.pi * ((j[:, None] * j[None, :]) % F_GDIM) / F_GDIM
    eye = np.eye(F_GROUPS)
    return (jnp.asarray(np.kron(eye, np.cos(ang)), F32).astype(BF16),
            jnp.asarray(np.kron(eye, np.sin(ang)), F32).astype(BF16))


def _position_dft_tables(seq_len):
    scale = 1.0 / math.sqrt(seq_len * F_GDIM)
    k = np.arange(seq_len, dtype=np.int64)
    ang = 2.0 * np.pi * ((k[:, None] * k[None, :]) % seq_len) / seq_len
    return (jnp.asarray(np.cos(ang) * scale, F32).astype(BF16),
            jnp.asarray(np.sin(ang) * scale, F32).astype(BF16))


def _two_stage_dft_tables(seq_len):
    r = FFT_RADIX
    assert seq_len == r * r
    scale = 1.0 / math.sqrt(seq_len * F_GDIM)
    j = np.arange(r, dtype=np.int64)
    ang_r = 2.0 * np.pi * ((j[:, None] * j[None, :]) % r) / r
    cs, ss = np.cos(ang_r), np.sin(ang_r)
    s = FFT_STEP
    eye = np.eye(s)
    k1 = np.concatenate([np.kron(cs, eye), np.kron(ss, eye)], axis=0) * scale

    def spread(m):
        out = np.zeros((r, s, s, r))
        for i in range(s):
            out[:, i, i, :] = m
        return out.reshape(r * s, s * r)

    k2 = np.concatenate([spread(cs), spread(ss)], axis=1)
    ka = j[None, :, None]
    t0 = (np.arange(r // s)[:, None, None] * s + np.arange(s)[None, None, :])
    ang_t = (2.0 * np.pi * ka * t0 / seq_len).reshape(-1, 1)
    tc = jnp.asarray(np.repeat(np.cos(ang_t), V7X_LANES, axis=1), F32)
    ts = jnp.asarray(np.repeat(np.sin(ang_t), V7X_LANES, axis=1), F32)
    return jnp.asarray(k1, F32).astype(BF16), jnp.asarray(k2, F32).astype(BF16), tc, ts


def _rope_tables(seq_len):
    t = np.arange(seq_len)
    row = (t // GRID_W).astype(np.float64)
    col = (t % GRID_W).astype(np.float64)
    inv = np.power(ROPE_BASE, -np.arange(ROPE_PER_AXIS, dtype=np.float64) / ROPE_PER_AXIS)
    ang = np.concatenate([row[:, None] * inv, col[:, None] * inv], axis=-1)
    cos = np.cos(ang)
    sin = np.sin(ang)
    cos_h = np.concatenate([cos, cos], axis=-1)
    sin_h = np.concatenate([-sin, sin], axis=-1)
    return jnp.asarray(np.stack([np.tile(cos_h, (1, NA_HEADS)), np.tile(sin_h, (1, NA_HEADS))]), F32)


def _bias_table(rpb_l):
    col = np.arange(GRID_W)
    col_start = np.clip(col - NA_WIN_C // 2, 0, GRID_W - NA_WIN_C)
    col_mask = (col[None, :] >= col_start[:, None]) & (col[None, :] < col_start[:, None] + NA_WIN_C)
    dc = np.clip(col[None, :] - col[:, None] + (NA_WIN_C - 1), 0, 2 * NA_WIN_C - 2)
    n_dc = 2 * NA_WIN_C - 1
    pick = (dc.reshape(-1)[None, :] == np.arange(n_dc)[:, None]).astype(np.float32)
    e = jnp.dot(rpb_l.reshape(-1, n_dc), jnp.asarray(pick), precision=HIGHEST)
    e = e.reshape(NA_HEADS, 2 * NA_WIN_R - 1, GRID_W, GRID_W)
    e = jnp.where(jnp.asarray(col_mask)[None, None], e * LOG2E, NEG_BIG)
    e = e.transpose(0, 2, 1, 3)
    b = jnp.stack([e[:, :, o:o + NA_WIN_R] for o in range(NA_WIN_R)], axis=0)
    return b.reshape(NA_WIN_R, NA_HEADS, GRID_W, NA_WIN_R * GRID_W)


def _moe(h2, h2_second, logits, w1, w3, w2, layer, router_b):
    n_tok = logits.shape[0]
    n_tiles = n_tok // ROUTE_TILE
    max_rows = 2 * n_tok + N_EXPERTS * n_tiles * (SLOT_GROUP - 1) + N_EXPERTS * (EXPERT_BLOCK - 1)
    n_blocks = -(-max_rows // EXPERT_BLOCK)
    n_slots = n_blocks * EXPERT_BLOCK
    ids, wts, cnt, tot = _route(logits, router_b)
    loc, gmap, blk = _slots(ids, cnt, tot, n_blocks)
    xb = _dispatch(gmap, blk, loc, h2, h2_second, n_slots)
    yb = _experts(blk[0, :n_blocks], blk[1, 0:1], xb, w1, w3, w2, layer)
    return yb, gmap, loc, wts


def kernel(x, c, ctx, c_ctx, ada_w, ada_b, norm1_g, w_in, qn_g, kn_g, rpb, conv_w, w_f, w_na, w_cv, w_o,
           norm2_g, router_w, router_b, w1, w3, w2):
    bsz, seq_len, d = x.shape
    ctx_len = ctx.shape[1]
    n_lat = bsz * seq_len
    n_ctx = bsz * ctx_len
    ctx_row = bsz

    c8 = jnp.concatenate([c, c_ctx[None, :], jnp.zeros((MODS_ROWS - bsz - 1, d), F32)], axis=0)
    mods = _mods(c8, ada_w, ada_b)

    cbd, sbd = _channel_dft_tables()
    fst, g_dft, tc3, ts3 = _two_stage_dft_tables(seq_len)
    c_ctx_t, s_ctx_t = _position_dft_tables(ctx_len)
    cos_sin = _rope_tables(seq_len)
    mavg = jnp.asarray(np.kron(np.eye(HEADS_PER_GROUP), np.full((HEAD_DIM, HEAD_DIM), 1.0 / HEAD_DIM)),
                       F32).astype(BF16)
    rw_hi = router_w.astype(BF16)
    rw_lo = (router_w - rw_hi.astype(F32)).astype(BF16)
    rwt = jnp.concatenate([rw_hi, rw_lo, jnp.zeros((d, V7X_LANES - 2 * N_EXPERTS), BF16)], axis=1)

    xl = x.reshape(n_lat, d)
    xc = ctx.reshape(n_ctx, d)
    for l in range(DEPTH):
        last = l == DEPTH - 1
        w_proj = w_in[l][:, :COL_G].astype(BF16)
        w_gate = w_in[l][:, COL_G:].astype(BF16)
        wf, wna, wcv, wo = (w_f[l].astype(BF16), w_na[l].astype(BF16), w_cv[l].astype(BF16), w_o[l].astype(BF16))
        n1 = norm1_g[l].reshape(1, d)
        n2 = norm2_g[l].reshape(1, d)
        qg = jnp.tile(qn_g[l], NA_HEADS).reshape(1, NA_WIDTH)
        kg = jnp.tile(kn_g[l], NA_HEADS).reshape(1, NA_WIDTH)
        bias_tab = _bias_table(rpb[l])
        mods_l = mods[l]

        ab_c, qkv_c, ubg_c = _proj(
            xc, mods_l, n1, w_proj, qg, kg, mavg, cbd, sbd, None,
            seq_len=ctx_len, n_seq=bsz, fixed_row=ctx_row, dft_dtype=BF16)
        ab_l, qkv_l, ubg_l = _proj(
            xl, mods_l, n1, w_proj, qg, kg, mavg, cbd, sbd, cos_sin,
            seq_len=seq_len, n_seq=bsz, fixed_row=None, dft_dtype=F32)

        f_l = _fourier_two_stage(ab_l, fst, g_dft, tc3, ts3)
        attn_l = _attn(qkv_l, qkv_c, bias_tab, n_seq=bsz, seq_len=seq_len, ctx_len=ctx_len)
        xl_new, h2_l, lg_l = _merge(xl, mods_l, n1, n2, f_l, attn_l, ubg_l, conv_w[l], w_gate,
                                    wf, wna, wcv, wo, rwt, seq_len=seq_len, fixed_row=None)
        if last:
            yb, gmap, loc, wts = _moe(h2_l, None, lg_l, w1, w3, w2, l, router_b)
            xl = _combine(gmap, loc, wts, xl_new, mods_l, yb, seq_len=seq_len, fixed_row=None, tile_offset=0)
        else:
            f_c = _fourier(c_ctx_t, s_ctx_t, ab_c)
            attn_c = _ctx_attn(qkv_c, n_seq=bsz, ctx_len=ctx_len)
            xc_new, h2_c, lg_c = _merge(xc, mods_l, n1, n2, f_c, attn_c, ubg_c, conv_w[l], w_gate,
                                        wf, wna, wcv, wo, rwt, seq_len=ctx_len, fixed_row=ctx_row)
            lg = jnp.concatenate([lg_l, lg_c], axis=0)
            yb, gmap, loc, wts = _moe(h2_l, h2_c, lg, w1, w3, w2, l, router_b)
            xl = _combine(gmap, loc, wts, xl_new, mods_l, yb, seq_len=seq_len, fixed_row=None, tile_offset=0)
            xc = _combine(gmap, loc, wts, xc_new, mods_l, yb, seq_len=ctx_len, fixed_row=ctx_row,
                          tile_offset=n_lat // ROUTE_TILE)
    return xl.reshape(bsz, seq_len, d)
```
